```python
import jax, jax.numpy as jnp
from jax import lax
import numpy as np

D_MODEL = 1024
BATCH = 8
SEQ = 2048
DEPTH = 1
DEC_BATCH = 128
DEC_SEQ = 4
PAST_LEN = 16384
PAGE_SIZE = 128

MIX_A = D_MODEL // 2
A_HEADS = 4
A_HEAD_DIM = MIX_A // A_HEADS
CHUNK = 128
CONV_DIM = D_MODEL - MIX_A
CONV_W = 3
PROJ_DIM = 2 * MIX_A + 3 * CONV_DIM
SPLITS = (MIX_A, 2 * MIX_A, 2 * MIX_A + CONV_DIM, 2 * MIX_A + 2 * CONV_DIM)
N_EXPERTS = 256
TOP_K = 8
N_GROUPS = 8
TOPK_GROUPS = 4
D_EXPERT = 256
D_SHARED = 256
ROUTED_SCALE = 2.5
MOE_BLOCK = 128
N_MOD = 6
RMS_EPS = 1e-6

kernel_name = 'hymba_chunkgmlp_shortconv_moe_adaln_step'


def rms_norm(x, g):
    xf = x.astype(jnp.float32)
    y = xf * lax.rsqrt(jnp.mean(xf * xf, axis=-1, keepdims=True) + RMS_EPS)
    return (y * g.astype(jnp.float32)).astype(x.dtype)


def ada_modulation(c, w, b, n):
    m = jax.nn.silu(c) @ w + b
    return m.reshape(c.shape[0], n, -1)


def modulate(x, shift, scale):
    return x * (1 + scale[:, None, :]) + shift[:, None, :]


def chunk_spatial_gate(u, v, w_s, b_s):
    bsz, L, H, P = v.shape
    c = CHUNK if L >= CHUNK else L
    n = -(-L // c)
    lp = n * c
    mask = jnp.tril(jnp.ones((c, c), dtype=bool))
    w = jnp.where(mask, w_s[:, :c, :c], jnp.zeros((), w_s.dtype))
    vc = jnp.pad(v, ((0, 0), (0, lp - L), (0, 0), (0, 0))).reshape(bsz, n, c, H, P)
    mix = jnp.einsum('hts,bnshp->bnthp', w, vc) + b_s[:, :c].T[:, :, None]
    return u * mix.reshape(bsz, lp, H, P)[:, :L]


def short_conv(z, conv_state, w_conv):
    L = z.shape[1]
    full = jnp.concatenate([conv_state.astype(z.dtype), z], axis=1)
    y = sum(full[:, k:k + L] * w_conv[k] for k in range(CONV_W))
    return y, full[:, full.shape[1] - (CONV_W - 1):]


def token_mixers(h, conv_state, w_in, w_s, b_s, g_v, w_conv, g_out_a, g_out_b, w_out):
    bsz, L, _ = h.shape
    p = h @ w_in
    u, v, gate_b, gate_c, z = jnp.split(p, SPLITS, axis=-1)
    u = jax.nn.gelu(u).reshape(bsz, L, A_HEADS, A_HEAD_DIM)
    v = rms_norm(jax.nn.gelu(v).reshape(bsz, L, A_HEADS, A_HEAD_DIM), g_v)
    y_a = chunk_spatial_gate(u, v, w_s, b_s).reshape(bsz, L, MIX_A)
    y_c, new_conv = short_conv(gate_c * z, conv_state, w_conv)
    y_b = gate_b * y_c
    y = jnp.concatenate([rms_norm(y_a, g_out_a), rms_norm(y_b, g_out_b)], axis=-1) @ w_out
    return y, new_conv, v


def swiglu(x, wg, wu, wd):
    return (jax.nn.silu(x @ wg) * (x @ wu)) @ wd


def moe_ffn(x2, w_router, b_router, w_e_gate, w_e_up, w_e_down, w_sh_gate, w_sh_up, w_sh_down):
    T, D = x2.shape
    scores = jax.nn.sigmoid(x2.astype(jnp.float32) @ w_router.astype(jnp.float32))
    sel = scores + b_router.astype(jnp.float32)
    per_group = N_EXPERTS // N_GROUPS
    grp_score = lax.top_k(sel.reshape(T, N_GROUPS, per_group), 2)[0].sum(-1)
    _, gidx = lax.top_k(grp_score, TOPK_GROUPS)
    gmask = jax.nn.one_hot(gidx, N_GROUPS, dtype=jnp.float32).sum(1) > 0
    sel = jnp.where(jnp.repeat(gmask, per_group, axis=1), sel, -jnp.inf)
    _, eidx = lax.top_k(sel, TOP_K)
    wts = jnp.take_along_axis(scores, eidx, axis=1)
    wts = wts / jnp.sum(wts, axis=-1, keepdims=True) * ROUTED_SCALE
    n_slots = T * TOP_K
    flat_e = eidx.reshape(-1).astype(jnp.int32)
    flat_t = jnp.arange(n_slots, dtype=jnp.int32) // TOP_K
    flat_w = wts.reshape(-1).astype(x2.dtype)
    order = jnp.argsort(flat_e)
    se = flat_e[order]
    counts = jnp.bincount(flat_e, length=N_EXPERTS)
    starts = jnp.cumsum(counts) - counts
    padded = (counts + MOE_BLOCK - 1) // MOE_BLOCK * MOE_BLOCK
    pends = jnp.cumsum(padded)
    pstarts = pends - padded
    dest = pstarts[se] + jnp.arange(n_slots, dtype=jnp.int32) - starts[se]
    n_blocks = -(-n_slots // MOE_BLOCK) + N_EXPERTS
    n_rows = n_blocks * MOE_BLOCK
    row_tok = jnp.full((n_rows,), T, dtype=jnp.int32).at[dest].set(flat_t[order])
    row_w = jnp.zeros((n_rows,), x2.dtype).at[dest].set(flat_w[order])
    blk_e = jnp.minimum(jnp.searchsorted(pends, jnp.arange(n_blocks) * MOE_BLOCK, side='right'),
                        N_EXPERTS - 1).astype(jnp.int32)
    x_pad = jnp.concatenate([x2, jnp.zeros((1, D), x2.dtype)], axis=0)

    def expert_block(args):
        toks, wb, e = args
        xb = x_pad[toks]
        hb = jax.nn.silu(xb @ w_e_gate[e]) * (xb @ w_e_up[e])
        return (hb @ w_e_down[e]) * wb[:, None]

    yb = lax.map(expert_block, (row_tok.reshape(n_blocks, MOE_BLOCK),
                                row_w.reshape(n_blocks, MOE_BLOCK), blk_e))
    routed = jax.ops.segment_sum(yb.reshape(n_rows, D), row_tok, num_segments=T + 1)[:T]
    return routed + swiglu(x2, w_sh_gate, w_sh_up, w_sh_down)


def setup_inputs(seed: int = 0) -> dict:
    key = jax.random.key(seed)
    ks = jax.random.split(key, 32)
    nrm = jax.random.normal
    f32 = jnp.float32
    D = D_MODEL
    return {
        'x_prompt': nrm(ks[0], (BATCH, SEQ, D), f32),
        'x_sample': nrm(ks[1], (DEC_BATCH, DEC_SEQ, D), f32),
        'state_conv': nrm(ks[2], (DEPTH, DEC_BATCH, CONV_W - 1, CONV_DIM), f32),
        'c_prompt': nrm(ks[3], (BATCH, D), f32),
        'c_sample': nrm(ks[4], (DEC_BATCH, D), f32),
        'w_ada': nrm(ks[5], (DEPTH, D, N_MOD * D), f32) * (0.5 * D ** -0.5),
        'b_ada': nrm(ks[6], (DEPTH, N_MOD * D), f32) * 0.02,
        'g_norm1': 1.0 + 0.05 * nrm(ks[7], (DEPTH, D), f32),
        'w_in': nrm(ks[8], (DEPTH, D, PROJ_DIM), f32) * D ** -0.5,
        'w_spatial': nrm(ks[9], (DEPTH, A_HEADS, CHUNK, CHUNK), f32) * CHUNK ** -0.5,
        'b_spatial': 1.0 + 0.1 * nrm(ks[10], (DEPTH, A_HEADS, CHUNK), f32),
        'g_v': 1.0 + 0.05 * nrm(ks[11], (DEPTH, A_HEADS, A_HEAD_DIM), f32),
        'w_conv': nrm(ks[12], (DEPTH, CONV_W, CONV_DIM), f32) * CONV_W ** -0.5,
        'g_out_a': 1.0 + 0.05 * nrm(ks[13], (DEPTH, MIX_A), f32),
        'g_out_b': 1.0 + 0.05 * nrm(ks[14], (DEPTH, CONV_DIM), f32),
        'w_out': nrm(ks[15], (DEPTH, MIX_A + CONV_DIM, D), f32) * (MIX_A + CONV_DIM) ** -0.5,
        'g_norm2': 1.0 + 0.05 * nrm(ks[16], (DEPTH, D), f32),
        'w_router': nrm(ks[17], (DEPTH, D, N_EXPERTS), f32) * D ** -0.5,
        'b_router': nrm(ks[18], (DEPTH, N_EXPERTS), f32) * 0.01,
        'w_exp_gate': nrm(ks[19], (DEPTH, N_EXPERTS, D, D_EXPERT), f32) * D ** -0.5,
        'w_exp_up': nrm(ks[20], (DEPTH, N_EXPERTS, D, D_EXPERT), f32) * D ** -0.5,
        'w_exp_down': nrm(ks[21], (DEPTH, N_EXPERTS, D_EXPERT, D), f32) * D_EXPERT ** -0.5,
        'w_sh_gate': nrm(ks[22], (DEPTH, D, D_SHARED), f32) * D ** -0.5,
        'w_sh_up': nrm(ks[23], (DEPTH, D, D_SHARED), f32) * D ** -0.5,
        'w_sh_down': nrm(ks[24], (DEPTH, D_SHARED, D), f32) * D_SHARED ** -0.5,
        'w_ada_final': nrm(ks[25], (D, 2 * D), f32) * (0.5 * D ** -0.5),
        'b_ada_final': nrm(ks[26], (2 * D,), f32) * 0.02,
        'g_final': 1.0 + 0.05 * nrm(ks[27], (D,), f32),
    }


def reference(x_prompt, x_sample, state_conv, c_prompt, c_sample, w_ada, b_ada, g_norm1, w_in,
              w_spatial, b_spatial, g_v, w_conv, g_out_a, g_out_b, w_out, g_norm2, w_router, b_router,
              w_exp_gate, w_exp_up, w_exp_down, w_sh_gate, w_sh_up, w_sh_down, w_ada_final,
              b_ada_final, g_final):
    x_p, x_s = x_prompt, x_sample
    n_p = x_p.shape[0] * x_p.shape[1]
    conv_p_rows, conv_s_rows, v_s_rows = [], [], []
    for l in range(DEPTH):
        m_p = ada_modulation(c_prompt, w_ada[l], b_ada[l], N_MOD)
        m_s = ada_modulation(c_sample, w_ada[l], b_ada[l], N_MOD)
        mix_w = (w_in[l], w_spatial[l], b_spatial[l], g_v[l], w_conv[l], g_out_a[l], g_out_b[l], w_out[l])
        h_p = modulate(rms_norm(x_p, g_norm1[l]), m_p[:, 0], m_p[:, 1])
        zero_state = jnp.zeros((x_p.shape[0], CONV_W - 1, CONV_DIM), x_p.dtype)
        y_p, conv_p, _ = token_mixers(h_p, zero_state, *mix_w)
        x_p = x_p + m_p[:, 2][:, None, :] * y_p
        h_s = modulate(rms_norm(x_s, g_norm1[l]), m_s[:, 0], m_s[:, 1])
        y_s, conv_s, v_s = token_mixers(h_s, state_conv[l], *mix_w)
        x_s = x_s + m_s[:, 2][:, None, :] * y_s
        conv_p_rows.append(conv_p)
        conv_s_rows.append(conv_s)
        v_s_rows.append(v_s)
        h2_p = modulate(rms_norm(x_p, g_norm2[l]), m_p[:, 3], m_p[:, 4])
        h2_s = modulate(rms_norm(x_s, g_norm2[l]), m_s[:, 3], m_s[:, 4])
        flat = jnp.concatenate([h2_p.reshape(-1, D_MODEL), h2_s.reshape(-1, D_MODEL)], axis=0)
        f = moe_ffn(flat, w_router[l], b_router[l], w_exp_gate[l], w_exp_up[l], w_exp_down[l],
                    w_sh_gate[l], w_sh_up[l], w_sh_down[l])
        x_p = x_p + m_p[:, 5][:, None, :] * f[:n_p].reshape(x_p.shape)
        x_s = x_s + m_s[:, 5][:, None, :] * f[n_p:].reshape(x_s.shape)
    mf_p = ada_modulation(c_prompt, w_ada_final, b_ada_final, 2)
    mf_s = ada_modulation(c_sample, w_ada_final, b_ada_final, 2)
    y_prompt = modulate(rms_norm(x_p, g_final), mf_p[:, 0], mf_p[:, 1])
    y_sample = modulate(rms_norm(x_s, g_final), mf_s[:, 0], mf_s[:, 1])
    new_state_conv_prompt = jnp.stack(conv_p_rows, axis=0)
    new_state_conv_sample = jnp.stack(conv_s_rows, axis=0)
    new_chunk_v_sample = jnp.stack(v_s_rows, axis=0)
    return (y_prompt, y_sample, new_state_conv_prompt, new_state_conv_sample, new_chunk_v_sample)
```

```python
import functools

import jax
import jax.numpy as jnp
from jax import lax
from jax.experimental import pallas as pl
from jax.experimental.pallas import tpu as pltpu

F32 = jnp.float32
BF16 = jnp.bfloat16
I32 = jnp.int32

D_MODEL = 1024
MIX_A = 512
A_HEADS = 4
A_HEAD_DIM = 128
CHUNK = 128
CONV_DIM = 512
CONV_W = 3
PROJ_DIM = 2 * MIX_A + 3 * CONV_DIM
N_EXPERTS = 256
TOP_K = 8
N_GROUPS = 8
TOPK_GROUPS = 4
GROUP_SIZE = N_EXPERTS // N_GROUPS
D_EXPERT = 256
D_SHARED = 256
ROUTED_SCALE = 2.5
N_MOD = 6
RMS_EPS = 1e-6
DEC_SEQ = 4

LANES = 128
SUBLANES = 8
TM = 512
BM = 256
TC = 128
VMEM_LIMIT_BYTES = 56 * 1024 * 1024


def _rms(x, g):
    return x * lax.rsqrt(jnp.mean(x * x, axis=-1, keepdims=True) + RMS_EPS) * g


def _const_spec(shape):
    nd = len(shape)
    return pl.BlockSpec(shape, lambda *_: (0,) * nd, pipeline_mode=pl.Buffered(1))


def _ada_kernel(c_ref, w_ref, b_ref, o_ref):
    a = jax.nn.silu(c_ref[...]).astype(BF16)
    o_ref[...] = jnp.dot(a, w_ref[...].astype(BF16), preferred_element_type=F32) + b_ref[...]


def _ada(c_all, w, b):
    rows, n = c_all.shape[0], w.shape[1]
    return pl.pallas_call(
        _ada_kernel,
        out_shape=jax.ShapeDtypeStruct((rows, n), F32),
        grid=(n // D_MODEL,),
        in_specs=[
            pl.BlockSpec((rows, D_MODEL), lambda j: (0, 0)),
            pl.BlockSpec((D_MODEL, D_MODEL), lambda j: (0, j)),
            pl.BlockSpec((1, D_MODEL), lambda j: (0, j)),
        ],
        out_specs=pl.BlockSpec((rows, D_MODEL), lambda j: (0, j)),
        compiler_params=pltpu.CompilerParams(dimension_semantics=("arbitrary",)),
        name="ada",
    )(c_all, w, b.reshape(1, n))


def _mixer_kernel(xp_ref, xsm_ref, mp_ref, ms_ref, g1_ref, win_ref, wt_ref, bsp_ref, gv_ref, wconv_ref,
                  goa_ref, gob_ref, wout_ref, g2_ref, wrt_ref, wshgu_ref, wshd_ref, csp_ref, bsps_ref, st_ref,
                  xs_out, h2_out, lt_out, cztail_out, vs_out, cz23_out, carry_ref, *, n_ptiles, tiles_per_seq):
    i = pl.program_id(0)
    d = D_MODEL

    def proj(x, sh1, sc1):
        h = _rms(x, g1_ref[...]) * (1.0 + sc1) + sh1
        p = jnp.dot(h.astype(BF16), win_ref[...], preferred_element_type=F32)
        u = jax.nn.gelu(p[:, :MIX_A])
        v = jax.nn.gelu(p[:, MIX_A:2 * MIX_A])
        vn = jnp.concatenate(
            [_rms(v[:, h * LANES:(h + 1) * LANES], gv_ref[:, h * LANES:(h + 1) * LANES]) for h in range(A_HEADS)],
            axis=1)
        o = 2 * MIX_A
        return u, vn, p[:, o:o + CONV_DIM], p[:, o + CONV_DIM:o + 2 * CONV_DIM], p[:, o + 2 * CONV_DIM:]

    def tail(x, ya, yb, gate1, sh2, sc2, gate2):
        cat = jnp.concatenate([_rms(ya, goa_ref[...]), _rms(yb, gob_ref[...])], axis=1).astype(BF16)
        x1 = x + gate1 * jnp.dot(cat, wout_ref[...], preferred_element_type=F32)
        h2 = _rms(x1, g2_ref[...]) * (1.0 + sc2) + sh2
        h2b = h2.astype(BF16)
        lt_out[...] = lax.dot_general(wrt_ref[...], h2b, (((1,), (1,)), ((), ())), preferred_element_type=F32)
        gu = jnp.dot(h2b, wshgu_ref[...], preferred_element_type=F32)
        hs = (jax.nn.silu(gu[:, :D_SHARED]) * gu[:, D_SHARED:]).astype(BF16)
        xs_out[...] = x1 + gate2 * jnp.dot(hs, wshd_ref[...], preferred_element_type=F32)
        h2_out[...] = h2

    @pl.when(i < n_ptiles)
    def _prompt():
        b = i // tiles_per_seq
        m = mp_ref[pl.ds(b, 1), :]
        mod = [m[:, k * d:(k + 1) * d] for k in range(N_MOD)]
        x = xp_ref[...]
        u, vn, bg, cg, z = proj(x, mod[0], mod[1])
        n_chunks = TM // CHUNK
        ya_cols = []
        for h in range(A_HEADS):
            vh = vn[:, h * LANES:(h + 1) * LANES].astype(BF16)
            rhs = jnp.concatenate([vh[c * CHUNK:(c + 1) * CHUNK] for c in range(n_chunks)], axis=1)
            mix = jnp.dot(wt_ref[h], rhs, preferred_element_type=F32)
            bias = bsp_ref[:, h * LANES:(h + 1) * LANES]
            mix = jnp.concatenate([mix[:, c * LANES:(c + 1) * LANES] + bias for c in range(n_chunks)], axis=0)
            ya_cols.append(u[:, h * LANES:(h + 1) * LANES] * mix)
        ya = jnp.concatenate(ya_cols, axis=1)
        cz = cg * z

        @pl.when(i % tiles_per_seq == 0)
        def _():
            carry_ref[...] = jnp.zeros_like(carry_ref)

        c6 = carry_ref[SUBLANES - 2:SUBLANES - 1, :]
        c7 = carry_ref[SUBLANES - 1:SUBLANES, :]
        r = lax.broadcasted_iota(I32, (TM, 1), 0)
        p1 = jnp.where(r == 0, c7, pltpu.roll(cz, 1, 0))
        p2 = jnp.where(r == 0, c6, jnp.where(r == 1, c7, pltpu.roll(cz, 2, 0)))
        wc = wconv_ref[...]
        yb = bg * (p2 * wc[0:1] + p1 * wc[1:2] + cz * wc[2:3])
        carry_ref[...] = cz[TM - SUBLANES:]
        cztail_out[0] = cz[TM - SUBLANES:]
        tail(x, ya, yb, mod[2], mod[3], mod[4], mod[5])

    @pl.when(i == n_ptiles)
    def _sample():
        ms = ms_ref[...]
        nb = TM // DEC_SEQ

        def mod(k):
            return jnp.concatenate([ms[:, k * d:(k + 1) * d]] * DEC_SEQ, axis=0)

        x = xsm_ref[...]
        u, vn, bg, cg, z = proj(x, mod(0), mod(1))
        vt = [vn[t * nb:(t + 1) * nb] for t in range(DEC_SEQ)]
        mixes = []
        for t in range(DEC_SEQ):
            acc = csp_ref[DEC_SEQ * t:DEC_SEQ * t + 1, :] * vt[0]
            for s in range(1, t + 1):
                acc = acc + csp_ref[DEC_SEQ * t + s:DEC_SEQ * t + s + 1, :] * vt[s]
            mixes.append(acc + bsps_ref[t:t + 1, :])
        ya = u * jnp.concatenate(mixes, axis=0)
        cz = cg * z
        czt = [cz[t * nb:(t + 1) * nb] for t in range(DEC_SEQ)]
        full = [st_ref[0], st_ref[1]] + czt
        wc = wconv_ref[...]
        yc = jnp.concatenate(
            [full[t] * wc[0:1] + full[t + 1] * wc[1:2] + full[t + 2] * wc[2:3] for t in range(DEC_SEQ)], axis=0)
        yb = bg * yc
        vs_out[...] = vn
        cz23_out[0] = czt[DEC_SEQ - 2]
        cz23_out[1] = czt[DEC_SEQ - 1]
        cztail_out[0] = cz[TM - SUBLANES:]
        tail(x, ya, yb, mod(2), mod(3), mod(4), mod(5))


def _mixer(xp, xsm, m_all, g1, win, wt, bsp, gv, wconv, goa, gob, wout, g2, wrt, wshgu, wshd, csp, bsps, st,
           *, n_batch, seq):
    t_p = xp.shape[0]
    n_ptiles = t_p // TM
    n_tiles = n_ptiles + 1
    t_all = n_tiles * TM
    dec_batch = xsm.shape[0] // DEC_SEQ
    d = D_MODEL
    kern = functools.partial(_mixer_kernel, n_ptiles=n_ptiles, tiles_per_seq=seq // TM)
    in_specs = [
        pl.BlockSpec((TM, d), lambda i: (jnp.minimum(i, n_ptiles - 1), 0)),
        _const_spec((TM, d)),
        pl.BlockSpec((n_batch, N_MOD * d), lambda i: (dec_batch // n_batch, 0), pipeline_mode=pl.Buffered(1)),
        pl.BlockSpec((dec_batch, N_MOD * d), lambda i: (0, 0), pipeline_mode=pl.Buffered(1)),
        _const_spec(g1.shape), _const_spec(win.shape), _const_spec(wt.shape), _const_spec(bsp.shape),
        _const_spec(gv.shape), _const_spec(wconv.shape), _const_spec(goa.shape), _const_spec(gob.shape),
        _const_spec(wout.shape), _const_spec(g2.shape), _const_spec(wrt.shape), _const_spec(wshgu.shape),
        _const_spec(wshd.shape), _const_spec(csp.shape), _const_spec(bsps.shape), _const_spec(st.shape),
    ]
    out_shape = [
        jax.ShapeDtypeStruct((t_all, d), F32),
        jax.ShapeDtypeStruct((t_all, d), F32),
        jax.ShapeDtypeStruct((N_EXPERTS, t_all), F32),
        jax.ShapeDtypeStruct((n_tiles, SUBLANES, CONV_DIM), F32),
        jax.ShapeDtypeStruct((TM, MIX_A), F32),
        jax.ShapeDtypeStruct((2, dec_batch, CONV_DIM), F32),
    ]
    out_specs = [
        pl.BlockSpec((TM, d), lambda i: (i, 0)),
        pl.BlockSpec((TM, d), lambda i: (i, 0)),
        pl.BlockSpec((N_EXPERTS, TM), lambda i: (0, i)),
        pl.BlockSpec((1, SUBLANES, CONV_DIM), lambda i: (i, 0, 0)),
        pl.BlockSpec((TM, MIX_A), lambda i: (0, 0)),
        pl.BlockSpec((2, dec_batch, CONV_DIM), lambda i: (0, 0, 0)),
    ]
    return pl.pallas_call(
        kern,
        out_shape=out_shape,
        grid=(n_tiles,),
        in_specs=in_specs,
        out_specs=out_specs,
        scratch_shapes=[pltpu.VMEM((SUBLANES, CONV_DIM), F32)],
        compiler_params=pltpu.CompilerParams(dimension_semantics=("arbitrary",),
                                             vmem_limit_bytes=VMEM_LIMIT_BYTES),
        name="mixer",
    )(xp, xsm, m_all, m_all, g1, win, wt, bsp, gv, wconv, goa, gob, wout, g2, wrt, wshgu, wshd, csp, bsps, st)


def _route_kernel(lt_ref, br_ref, e_out, w_out, r_out, cnt_out, carry_ref, tri_ref):
    i = pl.program_id(0)
    neg = -jnp.inf

    @pl.when(i == 0)
    def _():
        carry_ref[...] = jnp.zeros_like(carry_ref)
        a = lax.broadcasted_iota(I32, (TM, TM), 0)
        b = lax.broadcasted_iota(I32, (TM, TM), 1)
        tri_ref[...] = jnp.where(a < b, 1.0, 0.0).astype(BF16)

    scores = jax.nn.sigmoid(lt_ref[...])
    sel = scores + br_ref[...]
    iog = lax.broadcasted_iota(I32, (GROUP_SIZE, TM), 0)
    gs_rows = []
    for g in range(N_GROUPS):
        sg = sel[g * GROUP_SIZE:(g + 1) * GROUP_SIZE]
        m1 = jnp.max(sg, axis=0, keepdims=True)
        i1 = jnp.min(jnp.where(sg == m1, iog, GROUP_SIZE), axis=0, keepdims=True)
        m2 = jnp.max(jnp.where(iog == i1, neg, sg), axis=0, keepdims=True)
        gs_rows.append(m1 + m2)
    gs = jnp.concatenate(gs_rows, axis=0)
    io8 = lax.broadcasted_iota(I32, (N_GROUPS, TM), 0)
    keep = jnp.zeros((N_GROUPS, TM), F32)
    for _ in range(TOPK_GROUPS):
        m = jnp.max(gs, axis=0, keepdims=True)
        idx = jnp.min(jnp.where(gs == m, io8, N_GROUPS), axis=0, keepdims=True)
        hit = io8 == idx
        keep = jnp.where(hit, 1.0, keep)
        gs = jnp.where(hit, neg, gs)
    sel = jnp.concatenate(
        [jnp.where(keep[g:g + 1] > 0.0, sel[g * GROUP_SIZE:(g + 1) * GROUP_SIZE], neg) for g in range(N_GROUPS)],
        axis=0)
    ioe = lax.broadcasted_iota(I32, (N_EXPERTS, TM), 0)
    picked = jnp.zeros((N_EXPERTS, TM), F32)
    e_rows, w_rows = [], []
    for _ in range(TOP_K):
        m = jnp.max(sel, axis=0, keepdims=True)
        idx = jnp.min(jnp.where(sel == m, ioe, N_EXPERTS), axis=0, keepdims=True)
        hit = ioe == idx
        e_rows.append(idx)
        w_rows.append(jnp.sum(jnp.where(hit, scores, 0.0), axis=0, keepdims=True))
        picked = jnp.where(hit, 1.0, picked)
        sel = jnp.where(hit, neg, sel)
    wk = jnp.concatenate(w_rows, axis=0)
    w_out[...] = wk / jnp.sum(wk, axis=0, keepdims=True) * ROUTED_SCALE
    e_out[...] = jnp.concatenate(e_rows, axis=0)
    before = jnp.dot(picked.astype(BF16), tri_ref[...], preferred_element_type=F32) + carry_ref[...]
    r_rows = [jnp.sum(jnp.where(ioe == e_rows[k], before, 0.0), axis=0, keepdims=True) for k in range(TOP_K)]
    r_out[...] = jnp.concatenate(r_rows, axis=0).astype(I32)
    carry_ref[...] = carry_ref[...] + jnp.sum(picked, axis=1, keepdims=True)
    cnt_out[...] = carry_ref[...]


def _route(lt, br):
    t_all = lt.shape[1]
    n_tiles = t_all // TM
    return pl.pallas_call(
        _route_kernel,
        out_shape=[
            jax.ShapeDtypeStruct((TOP_K, t_all), I32),
            jax.ShapeDtypeStruct((TOP_K, t_all), F32),
            jax.ShapeDtypeStruct((TOP_K, t_all), I32),
            jax.ShapeDtypeStruct((N_EXPERTS, 1), F32),
        ],
        grid=(n_tiles,),
        in_specs=[pl.BlockSpec((N_EXPERTS, TM), lambda i: (0, i)), _const_spec((N_EXPERTS, 1))],
        out_specs=[
            pl.BlockSpec((TOP_K, TM), lambda i: (0, i)),
            pl.BlockSpec((TOP_K, TM), lambda i: (0, i)),
            pl.BlockSpec((TOP_K, TM), lambda i: (0, i)),
            pl.BlockSpec((N_EXPERTS, 1), lambda i: (0, 0)),
        ],
        scratch_shapes=[pltpu.VMEM((N_EXPERTS, 1), F32), pltpu.VMEM((TM, TM), BF16)],
        compiler_params=pltpu.CompilerParams(dimension_semantics=("arbitrary",)),
        name="route",
    )(lt, br)


def _dispatch_kernel(dest_ref, h2_ref, xs_hbm, sem):
    def row_copy(t, k):
        return pltpu.make_async_copy(h2_ref.at[pl.ds(t, 1)], xs_hbm.at[pl.ds(dest_ref[k, t], 1)], sem)

    def issue(t, carry):
        for k in range(TOP_K):
            row_copy(t, k).start()
        return carry

    lax.fori_loop(0, TM, issue, 0)

    def drain(t, carry):
        for k in range(TOP_K):
            row_copy(t, k).wait()
        return carry

    lax.fori_loop(0, TM, drain, 0)


def _dispatch(dest, h2, n_rows):
    t_all, d = h2.shape
    return pl.pallas_call(
        _dispatch_kernel,
        out_shape=jax.ShapeDtypeStruct((n_rows, d), F32),
        grid=(t_all // TM,),
        in_specs=[
            pl.BlockSpec((TOP_K, TM), lambda i: (0, i), memory_space=pltpu.SMEM),
            pl.BlockSpec((TM, d), lambda i: (i, 0)),
        ],
        out_specs=pl.BlockSpec(memory_space=pl.ANY),
        scratch_shapes=[pltpu.SemaphoreType.DMA(())],
        compiler_params=pltpu.CompilerParams(dimension_semantics=("arbitrary",)),
        name="dispatch",
    )(dest, h2)


def _moe_kernel(blk_e_ref, nblk_ref, x_ref, wg_ref, wu_ref, wd_ref, y_ref, wgu_s, wd_s):
    b = pl.program_id(0)

    @pl.when(b < nblk_ref[0])
    def _():
        e = blk_e_ref[b]
        prev = blk_e_ref[jnp.maximum(b - 1, 0)]

        @pl.when((b == 0) | (e != prev))
        def _():
            wgu_s[:, :D_EXPERT] = wg_ref[0].astype(BF16)
            wgu_s[:, D_EXPERT:] = wu_ref[0].astype(BF16)
            wd_s[...] = wd_ref[0].astype(BF16)

        gu = jnp.dot(x_ref[...].astype(BF16), wgu_s[...], preferred_element_type=F32)
        h = (jax.nn.silu(gu[:, :D_EXPERT]) * gu[:, D_EXPERT:]).astype(BF16)
        y_ref[...] = jnp.dot(h, wd_s[...], preferred_element_type=F32)


def _moe(blk_e, nblk, xs, wg, wu, wd):
    n_rows, d = xs.shape
    n_blocks = n_rows // BM

    def row_map(b, be, nb):
        return (jnp.minimum(b, nb[0] - 1), 0)

    def w_map(b, be, nb):
        return (be[jnp.minimum(b, nb[0] - 1)], 0, 0)

    grid_spec = pltpu.PrefetchScalarGridSpec(
        num_scalar_prefetch=2,
        grid=(n_blocks,),
        in_specs=[
            pl.BlockSpec((BM, d), row_map),
            pl.BlockSpec((1, d, D_EXPERT), w_map),
            pl.BlockSpec((1, d, D_EXPERT), w_map),
            pl.BlockSpec((1, D_EXPERT, d), w_map),
        ],
        out_specs=pl.BlockSpec((BM, d), row_map),
        scratch_shapes=[pltpu.VMEM((d, 2 * D_EXPERT), BF16), pltpu.VMEM((D_EXPERT, d), BF16)],
    )
    return pl.pallas_call(
        _moe_kernel,
        out_shape=jax.ShapeDtypeStruct((n_rows, d), F32),
        grid_spec=grid_spec,
        compiler_params=pltpu.CompilerParams(dimension_semantics=("arbitrary",),
                                             vmem_limit_bytes=VMEM_LIMIT_BYTES),
        name="moe",
    )(blk_e, nblk, xs, wg, wu, wd)


def _final_kernel(dest_ref, xs_ref, w_ref, gp_ref, gs_ref, fp_ref, fs_ref, gf_ref, y_hbm, op_ref, os_ref,
                  buf, sem, *, n_ptiles, tiles_per_seq):
    i = pl.program_id(0)
    d = D_MODEL

    def row_copy(t, k):
        return pltpu.make_async_copy(y_hbm.at[pl.ds(dest_ref[k, t], 1)], buf.at[k, pl.ds(t, 1)], sem)

    def issue(t, carry):
        for k in range(TOP_K):
            row_copy(t, k).start()
        return carry

    lax.fori_loop(0, TC, issue, 0)

    def drain(t, carry):
        for k in range(TOP_K):
            row_copy(t, k).wait()
        return carry

    lax.fori_loop(0, TC, drain, 0)

    w = w_ref[...]
    acc = w[:, 0:1] * buf[0]
    for k in range(1, TOP_K):
        acc = acc + w[:, k:k + 1] * buf[k]

    def finish(gate2, shift, scale, o_ref):
        x2 = xs_ref[...] + gate2 * acc
        o_ref[...] = _rms(x2, gf_ref[...]) * (1.0 + scale) + shift

    @pl.when(i < n_ptiles)
    def _():
        b = i // tiles_per_seq
        f = fp_ref[pl.ds(b, 1), :]
        finish(gp_ref[pl.ds(b, 1), :], f[:, :d], f[:, d:], op_ref)

    @pl.when(i >= n_ptiles)
    def _():
        f = fs_ref[...]
        finish(gs_ref[...], f[:, :d], f[:, d:], os_ref)


def _final(dest, xs, wts, m_all, mf_all, gf, y_sorted, *, n_batch, seq, t_p):
    t_all, d = xs.shape
    n_tiles = t_all // TC
    n_ptiles = t_p // TC
    dec_batch = m_all.shape[0] - n_batch
    kern = functools.partial(_final_kernel, n_ptiles=n_ptiles, tiles_per_seq=seq // TC)
    pb = dec_batch // n_batch
    return pl.pallas_call(
        kern,
        out_shape=[jax.ShapeDtypeStruct((t_p, d), F32), jax.ShapeDtypeStruct((t_all - t_p, d), F32)],
        grid=(n_tiles,),
        in_specs=[
            pl.BlockSpec((TOP_K, TC), lambda i: (0, i), memory_space=pltpu.SMEM),
            pl.BlockSpec((TC, d), lambda i: (i, 0)),
            pl.BlockSpec((TC, TOP_K), lambda i: (i, 0)),
            pl.BlockSpec((n_batch, d), lambda i: (pb, N_MOD - 1)),
            pl.BlockSpec((dec_batch, d), lambda i: (0, N_MOD - 1)),
            pl.BlockSpec((n_batch, 2 * d), lambda i: (pb, 0)),
            pl.BlockSpec((dec_batch, 2 * d), lambda i: (0, 0)),
            _const_spec(gf.shape),
            pl.BlockSpec(memory_space=pl.ANY),
        ],
        out_specs=[
            pl.BlockSpec((TC, d), lambda i: (jnp.minimum(i, n_ptiles - 1), 0)),
            pl.BlockSpec((TC, d), lambda i: (jnp.maximum(i - n_ptiles, 0), 0)),
        ],
        scratch_shapes=[pltpu.VMEM((TOP_K, TC, d), F32), pltpu.SemaphoreType.DMA(())],
        compiler_params=pltpu.CompilerParams(dimension_semantics=("arbitrary",),
                                             vmem_limit_bytes=VMEM_LIMIT_BYTES),
        name="final",
    )(dest, xs, wts, m_all, m_all, mf_all, mf_all, gf, y_sorted)


def kernel(x_prompt, x_sample, state_conv, c_prompt, c_sample, w_ada, b_ada, g_norm1, w_in, w_spatial, b_spatial, g_v, w_conv, g_out_a, g_out_b, w_out, g_norm2, w_router, b_router, w_exp_gate, w_exp_up, w_exp_down, w_sh_gate, w_sh_up, w_sh_down, w_ada_final, b_ada_final, g_final):
    n_batch, seq, d = x_prompt.shape
    dec_batch, dec_seq, _ = x_sample.shape
    assert w_ada.shape[0] == 1, "one layer only"
    assert d == D_MODEL and dec_seq == DEC_SEQ and dec_batch * dec_seq == TM and seq % TM == 0
    assert n_batch % SUBLANES == 0 and dec_batch % n_batch == 0
    t_p = n_batch * seq
    t_s = dec_batch * dec_seq
    t_all = t_p + t_s

    c_all = jnp.concatenate([c_sample, c_prompt], axis=0)
    m_all = _ada(c_all, w_ada[0], b_ada[0])
    mf_all = _ada(c_all, w_ada_final, b_ada_final)

    rep = functools.partial(jnp.repeat, repeats=A_HEAD_DIM, axis=1)
    tril = jnp.tril(jnp.ones((CHUNK, CHUNK), dtype=bool))
    wt = jnp.where(tril, w_spatial[0], 0.0).astype(BF16)
    bsp = rep(b_spatial[0].T)
    csp = rep(jnp.transpose(w_spatial[0][:, :DEC_SEQ, :DEC_SEQ], (1, 2, 0)).reshape(DEC_SEQ * DEC_SEQ, A_HEADS))
    bsps = rep(b_spatial[0][:, :DEC_SEQ].T)
    st = jnp.transpose(state_conv[0], (1, 0, 2))
    wshgu = jnp.concatenate([w_sh_gate[0], w_sh_up[0]], axis=1).astype(BF16)
    xsm = jnp.transpose(x_sample, (1, 0, 2)).reshape(t_s, d)

    xs, h2, lt, cztail, vs, cz23 = _mixer(
        x_prompt.reshape(t_p, d), xsm, m_all, g_norm1, w_in[0].astype(BF16), wt, bsp,
        g_v.reshape(1, MIX_A), w_conv[0], g_out_a, g_out_b, w_out[0].astype(BF16), g_norm2,
        w_router[0].T.astype(BF16), wshgu, w_sh_down[0].astype(BF16), csp, bsps, st,
        n_batch=n_batch, seq=seq)

    eidx, wts, rank, cnt = _route(lt, b_router[0].reshape(N_EXPERTS, 1))

    n_blocks = (t_all * TOP_K) // BM + N_EXPERTS
    counts = cnt[:, 0].astype(I32)
    padded = (counts + BM - 1) // BM * BM
    pends = jnp.cumsum(padded)
    pstart = pends - padded
    dest = pstart[eidx] + rank
    nblk = (pends[-1:] // BM).astype(I32)
    blk_e = jnp.minimum(jnp.searchsorted(pends, jnp.arange(n_blocks, dtype=I32) * BM, side="right"),
                        N_EXPERTS - 1).astype(I32)

    x_sorted = _dispatch(dest, h2, n_blocks * BM)
    y_sorted = _moe(blk_e, nblk, x_sorted, w_exp_gate[0], w_exp_up[0], w_exp_down[0])
    y_p, y_s = _final(dest, xs, wts.T, m_all, mf_all, g_final.reshape(1, d), y_sorted,
                      n_batch=n_batch, seq=seq, t_p=t_p)

    tiles_per_seq = seq // TM
    y_prompt = y_p.reshape(n_batch, seq, d)
    y_sample = jnp.transpose(y_s.reshape(dec_seq, dec_batch, d), (1, 0, 2))
    conv_p = cztail[tiles_per_seq - 1:n_batch * tiles_per_seq:tiles_per_seq, SUBLANES - (CONV_W - 1):, :][None]
    conv_s = jnp.transpose(cz23, (1, 0, 2))[None]
    v_s = jnp.transpose(vs.reshape(dec_seq, dec_batch, A_HEADS, A_HEAD_DIM), (1, 0, 2, 3))[None]
    return (y_prompt, y_sample, conv_p, conv_s, v_s)
```

```python
import functools

import jax
import jax.numpy as jnp
from jax import lax
from jax.experimental import pallas as pl
from jax.experimental.pallas import tpu as pltpu

F32 = jnp.float32
BF16 = jnp.bfloat16
I32 = jnp.int32

D_MODEL = 1024
MIX_A = 512
A_HEADS = 4
A_HEAD_DIM = 128
CHUNK = 128
CONV_DIM = 512
CONV_W = 3
PROJ_DIM = 2 * MIX_A + 3 * CONV_DIM
N_EXPERTS = 256
TOP_K = 8
N_GROUPS = 8
TOPK_GROUPS = 4
GROUP_SIZE = N_EXPERTS // N_GROUPS
D_EXPERT = 256
D_SHARED = 256
ROUTED_SCALE = 2.5
N_MOD = 6
RMS_EPS = 1e-6
DEC_SEQ = 4

LANES = 128
SUBLANES = 8
TM = 512
BM = 256
TC = 128
VMEM_LIMIT_BYTES = 56 * 1024 * 1024


def _rms(x, g):
    return x * lax.rsqrt(jnp.mean(x * x, axis=-1, keepdims=True) + RMS_EPS) * g


def _const_spec(shape):
    nd = len(shape)
    return pl.BlockSpec(shape, lambda *_: (0,) * nd, pipeline_mode=pl.Buffered(1))


def _ada_kernel(c_ref, w_ref, b_ref, o_ref):
    a = jax.nn.silu(c_ref[...]).astype(BF16)
    o_ref[...] = jnp.dot(a, w_ref[...].astype(BF16), preferred_element_type=F32) + b_ref[...]


def _ada(c_all, w, b):
    rows, n = c_all.shape[0], w.shape[1]
    return pl.pallas_call(
        _ada_kernel,
        out_shape=jax.ShapeDtypeStruct((rows, n), F32),
        grid=(n // D_MODEL,),
        in_specs=[
            pl.BlockSpec((rows, D_MODEL), lambda j: (0, 0)),
            pl.BlockSpec((D_MODEL, D_MODEL), lambda j: (0, j)),
            pl.BlockSpec((1, D_MODEL), lambda j: (0, j)),
        ],
        out_specs=pl.BlockSpec((rows, D_MODEL), lambda j: (0, j)),
        compiler_params=pltpu.CompilerParams(dimension_semantics=("arbitrary",)),
        name="ada",
    )(c_all, w, b.reshape(1, n))


def _mixer_kernel(xp_ref, xsm_ref, mp_ref, ms_ref, g1_ref, win_ref, wt_ref, bsp_ref, gv_ref, wconv_ref,
                  goa_ref, gob_ref, wout_ref, g2_ref, wrt_ref, wshgu_ref, wshd_ref, csp_ref, bsps_ref, st_ref,
                  xs_out, h2_out, lt_out, cztail_out, vs_out, cz23_out, carry_ref, *, n_ptiles, tiles_per_seq):
    i = pl.program_id(0)
    d = D_MODEL

    def proj(x, sh1, sc1):
        h = _rms(x, g1_ref[...]) * (1.0 + sc1) + sh1
        p = jnp.dot(h.astype(BF16), win_ref[...], preferred_element_type=F32)
        u = jax.nn.gelu(p[:, :MIX_A])
        v = jax.nn.gelu(p[:, MIX_A:2 * MIX_A])
        vn = jnp.concatenate(
            [_rms(v[:, h * LANES:(h + 1) * LANES], gv_ref[:, h * LANES:(h + 1) * LANES]) for h in range(A_HEADS)],
            axis=1)
        o = 2 * MIX_A
        return u, vn, p[:, o:o + CONV_DIM], p[:, o + CONV_DIM:o + 2 * CONV_DIM], p[:, o + 2 * CONV_DIM:]

    def tail(x, ya, yb, gate1, sh2, sc2, gate2):
        cat = jnp.concatenate([_rms(ya, goa_ref[...]), _rms(yb, gob_ref[...])], axis=1).astype(BF16)
        x1 = x + gate1 * jnp.dot(cat, wout_ref[...], preferred_element_type=F32)
        h2 = _rms(x1, g2_ref[...]) * (1.0 + sc2) + sh2
        h2b = h2.astype(BF16)
        lt_out[...] = lax.dot_general(wrt_ref[...], h2b, (((1,), (1,)), ((), ())), preferred_element_type=F32)
        gu = jnp.dot(h2b, wshgu_ref[...], preferred_element_type=F32)
        hs = (jax.nn.silu(gu[:, :D_SHARED]) * gu[:, D_SHARED:]).astype(BF16)
        xs_out[...] = x1 + gate2 * jnp.dot(hs, wshd_ref[...], preferred_element_type=F32)
        h2_out[...] = h2

    @pl.when(i < n_ptiles)
    def _prompt():
        b = i // tiles_per_seq
        m = mp_ref[pl.ds(b, 1), :]
        mod = [m[:, k * d:(k + 1) * d] for k in range(N_MOD)]
        x = xp_ref[...]
        u, vn, bg, cg, z = proj(x, mod[0], mod[1])
        n_chunks = TM // CHUNK
        ya_cols = []
        for h in range(A_HEADS):
            vh = vn[:, h * LANES:(h + 1) * LANES].astype(BF16)
            rhs = jnp.concatenate([vh[c * CHUNK:(c + 1) * CHUNK] for c in range(n_chunks)], axis=1)
            mix = jnp.dot(wt_ref[h], rhs, preferred_element_type=F32)
            bias = bsp_ref[:, h * LANES:(h + 1) * LANES]
            mix = jnp.concatenate([mix[:, c * LANES:(c + 1) * LANES] + bias for c in range(n_chunks)], axis=0)
            ya_cols.append(u[:, h * LANES:(h + 1) * LANES] * mix)
        ya = jnp.concatenate(ya_cols, axis=1)
        cz = cg * z

        @pl.when(i % tiles_per_seq == 0)
        def _():
            carry_ref[...] = jnp.zeros_like(carry_ref)

        c6 = carry_ref[SUBLANES - 2:SUBLANES - 1, :]
        c7 = carry_ref[SUBLANES - 1:SUBLANES, :]
        r = lax.broadcasted_iota(I32, (TM, 1), 0)
        p1 = jnp.where(r == 0, c7, pltpu.roll(cz, 1, 0))
        p2 = jnp.where(r == 0, c6, jnp.where(r == 1, c7, pltpu.roll(cz, 2, 0)))
        wc = wconv_ref[...]
        yb = bg * (p2 * wc[0:1] + p1 * wc[1:2] + cz * wc[2:3])
        carry_ref[...] = cz[TM - SUBLANES:]
        cztail_out[0] = cz[TM - SUBLANES:]
        tail(x, ya, yb, mod[2], mod[3], mod[4], mod[5])

    @pl.when(i == n_ptiles)
    def _sample():
        ms = ms_ref[...]
        nb = TM // DEC_SEQ

        def mod(k):
            return jnp.concatenate([ms[:, k * d:(k + 1) * d]] * DEC_SEQ, axis=0)

        x = xsm_ref[...]
        u, vn, bg, cg, z = proj(x, mod(0), mod(1))
        vt = [vn[t * nb:(t + 1) * nb] for t in range(DEC_SEQ)]
        mixes = []
        for t in range(DEC_SEQ):
            acc = csp_ref[DEC_SEQ * t:DEC_SEQ * t + 1, :] * vt[0]
            for s in range(1, t + 1):
                acc = acc + csp_ref[DEC_SEQ * t + s:DEC_SEQ * t + s + 1, :] * vt[s]
            mixes.append(acc + bsps_ref[t:t + 1, :])
        ya = u * jnp.concatenate(mixes, axis=0)
        cz = cg * z
        czt = [cz[t * nb:(t + 1) * nb] for t in range(DEC_SEQ)]
        full = [st_ref[0], st_ref[1]] + czt
        wc = wconv_ref[...]
        yc = jnp.concatenate(
            [full[t] * wc[0:1] + full[t + 1] * wc[1:2] + full[t + 2] * wc[2:3] for t in range(DEC_SEQ)], axis=0)
        yb = bg * yc
        vs_out[...] = vn
        cz23_out[0] = czt[DEC_SEQ - 2]
        cz23_out[1] = czt[DEC_SEQ - 1]
        cztail_out[0] = cz[TM - SUBLANES:]
        tail(x, ya, yb, mod(2), mod(3), mod(4), mod(5))


def _mixer(xp, xsm, m_all, g1, win, wt, bsp, gv, wconv, goa, gob, wout, g2, wrt, wshgu, wshd, csp, bsps, st,
           *, n_batch, seq):
    t_p = xp.shape[0]
    n_ptiles = t_p // TM
    n_tiles = n_ptiles + 1
    t_all = n_tiles * TM
    dec_batch = xsm.shape[0] // DEC_SEQ
    d = D_MODEL
    kern = functools.partial(_mixer_kernel, n_ptiles=n_ptiles, tiles_per_seq=seq // TM)
    in_specs = [
        pl.BlockSpec((TM, d), lambda i: (jnp.minimum(i, n_ptiles - 1), 0)),
        _const_spec((TM, d)),
        pl.BlockSpec((n_batch, N_MOD * d), lambda i: (dec_batch // n_batch, 0), pipeline_mode=pl.Buffered(1)),
        pl.BlockSpec((dec_batch, N_MOD * d), lambda i: (0, 0), pipeline_mode=pl.Buffered(1)),
        _const_spec(g1.shape), _const_spec(win.shape), _const_spec(wt.shape), _const_spec(bsp.shape),
        _const_spec(gv.shape), _const_spec(wconv.shape), _const_spec(goa.shape), _const_spec(gob.shape),
        _const_spec(wout.shape), _const_spec(g2.shape), _const_spec(wrt.shape), _const_spec(wshgu.shape),
        _const_spec(wshd.shape), _const_spec(csp.shape), _const_spec(bsps.shape), _const_spec(st.shape),
    ]
    out_shape = [
        jax.ShapeDtypeStruct((t_all, d), F32),
        jax.ShapeDtypeStruct((t_all, d), F32),
        jax.ShapeDtypeStruct((N_EXPERTS, t_all), F32),
        jax.ShapeDtypeStruct((n_tiles, SUBLANES, CONV_DIM), F32),
        jax.ShapeDtypeStruct((TM, MIX_A), F32),
        jax.ShapeDtypeStruct((2, dec_batch, CONV_DIM), F32),
    ]
    out_specs = [
        pl.BlockSpec((TM, d), lambda i: (i, 0)),
        pl.BlockSpec((TM, d), lambda i: (i, 0)),
        pl.BlockSpec((N_EXPERTS, TM), lambda i: (0, i)),
        pl.BlockSpec((1, SUBLANES, CONV_DIM), lambda i: (i, 0, 0)),
        pl.BlockSpec((TM, MIX_A), lambda i: (0, 0)),
        pl.BlockSpec((2, dec_batch, CONV_DIM), lambda i: (0, 0, 0)),
    ]
    return pl.pallas_call(
        kern,
        out_shape=out_shape,
        grid=(n_tiles,),
        in_specs=in_specs,
        out_specs=out_specs,
        scratch_shapes=[pltpu.VMEM((SUBLANES, CONV_DIM), F32)],
        compiler_params=pltpu.CompilerParams(dimension_semantics=("arbitrary",),
                                             vmem_limit_bytes=VMEM_LIMIT_BYTES),
        name="mixer",
    )(xp, xsm, m_all, m_all, g1, win, wt, bsp, gv, wconv, goa, gob, wout, g2, wrt, wshgu, wshd, csp, bsps, st)


def _route_kernel(lt_ref, br_ref, e_out, w_out, r_out, cnt_out, carry_ref, tri_ref):
    i = pl.program_id(0)
    neg = -jnp.inf

    @pl.when(i == 0)
    def _():
        carry_ref[...] = jnp.zeros_like(carry_ref)
        a = lax.broadcasted_iota(I32, (TM, TM), 0)
        b = lax.broadcasted_iota(I32, (TM, TM), 1)
        tri_ref[...] = jnp.where(a < b, 1.0, 0.0).astype(BF16)

    scores = jax.nn.sigmoid(lt_ref[...])
    sel = scores + br_ref[...]
    iog = lax.broadcasted_iota(I32, (GROUP_SIZE, TM), 0)
    gs_rows = []
    for g in range(N_GROUPS):
        sg = sel[g * GROUP_SIZE:(g + 1) * GROUP_SIZE]
        m1 = jnp.max(sg, axis=0, keepdims=True)
        i1 = jnp.min(jnp.where(sg == m1, iog, GROUP_SIZE), axis=0, keepdims=True)
        m2 = jnp.max(jnp.where(iog == i1, neg, sg), axis=0, keepdims=True)
        gs_rows.append(m1 + m2)
    gs = jnp.concatenate(gs_rows, axis=0)
    io8 = lax.broadcasted_iota(I32, (N_GROUPS, TM), 0)
    keep = jnp.zeros((N_GROUPS, TM), F32)
    for _ in range(TOPK_GROUPS):
        m = jnp.max(gs, axis=0, keepdims=True)
        idx = jnp.min(jnp.where(gs == m, io8, N_GROUPS), axis=0, keepdims=True)
        hit = io8 == idx
        keep = jnp.where(hit, 1.0, keep)
        gs = jnp.where(hit, neg, gs)
    sel = jnp.concatenate(
        [jnp.where(keep[g:g + 1] > 0.0, sel[g * GROUP_SIZE:(g + 1) * GROUP_SIZE], neg) for g in range(N_GROUPS)],
        axis=0)
    ioe = lax.broadcasted_iota(I32, (N_EXPERTS, TM), 0)
    picked = jnp.zeros((N_EXPERTS, TM), F32)
    e_rows, w_rows = [], []
    for _ in range(TOP_K):
        m = jnp.max(sel, axis=0, keepdims=True)
        idx = jnp.min(jnp.where(sel == m, ioe, N_EXPERTS), axis=0, keepdims=True)
        hit = ioe == idx
        e_rows.append(idx)
        w_rows.append(jnp.sum(jnp.where(hit, scores, 0.0), axis=0, keepdims=True))
        picked = jnp.where(hit, 1.0, picked)
        sel = jnp.where(hit, neg, sel)
    wk = jnp.concatenate(w_rows, axis=0)
    w_out[...] = wk / jnp.sum(wk, axis=0, keepdims=True) * ROUTED_SCALE
    e_out[...] = jnp.concatenate(e_rows, axis=0)
    before = jnp.dot(picked.astype(BF16), tri_ref[...], preferred_element_type=F32) + carry_ref[...]
    r_rows = [jnp.sum(jnp.where(ioe == e_rows[k], before, 0.0), axis=0, keepdims=True) for k in range(TOP_K)]
    r_out[...] = jnp.concatenate(r_rows, axis=0).astype(I32)
    carry_ref[...] = carry_ref[...] + jnp.sum(picked, axis=1, keepdims=True)
    cnt_out[...] = carry_ref[...]


def _route(lt, br):
    t_all = lt.shape[1]
    n_tiles = t_all // TM
    return pl.pallas_call(
        _route_kernel,
        out_shape=[
            jax.ShapeDtypeStruct((TOP_K, t_all), I32),
            jax.ShapeDtypeStruct((TOP_K, t_all), F32),
            jax.ShapeDtypeStruct((TOP_K, t_all), I32),
            jax.ShapeDtypeStruct((N_EXPERTS, 1), F32),
        ],
        grid=(n_tiles,),
        in_specs=[pl.BlockSpec((N_EXPERTS, TM), lambda i: (0, i)), _const_spec((N_EXPERTS, 1))],
        out_specs=[
            pl.BlockSpec((TOP_K, TM), lambda i: (0, i)),
            pl.BlockSpec((TOP_K, TM), lambda i: (0, i)),
            pl.BlockSpec((TOP_K, TM), lambda i: (0, i)),
            pl.BlockSpec((N_EXPERTS, 1), lambda i: (0, 0)),
        ],
        scratch_shapes=[pltpu.VMEM((N_EXPERTS, 1), F32), pltpu.VMEM((TM, TM), BF16)],
        compiler_params=pltpu.CompilerParams(dimension_semantics=("arbitrary",)),
        name="route",
    )(lt, br)


def _plan_kernel(cnt_ref, e_ref, r_ref, dest_out, blk_e_out, nxt_e_out, nblk_out, cnt_row_out, pstart_row_out,
                 pstart_ref):
    i = pl.program_id(0)

    @pl.when(i == 0)
    def _():
        nb = jnp.floor((cnt_ref[...] + (BM - 1)) * (1.0 / BM))
        a = lax.broadcasted_iota(I32, (N_EXPERTS, N_EXPERTS), 0)
        b = lax.broadcasted_iota(I32, (N_EXPERTS, N_EXPERTS), 1)
        lower = jnp.where(b < a, 1.0, 0.0).astype(BF16)
        nb_l = jnp.broadcast_to(nb, (N_EXPERTS, LANES)).astype(BF16)
        first_blk = jnp.dot(lower, nb_l, preferred_element_type=F32)[:, 0:1]
        end_blk = first_blk + nb
        pstart_ref[...] = first_blk * BM
        cnt_row_out[...] = jnp.sum(jnp.where(a == b, cnt_ref[...], 0.0), axis=0, keepdims=True).astype(I32)
        pstart_row_out[...] = jnp.sum(jnp.where(a == b, first_blk * BM, 0.0), axis=0, keepdims=True).astype(I32)
        n_lanes = blk_e_out.shape[1]
        blk = lax.broadcasted_iota(I32, (N_EXPERTS, n_lanes), 1).astype(F32)
        owner = jnp.sum(jnp.where(end_blk <= blk, 1.0, 0.0), axis=0, keepdims=True)
        blk_e_out[...] = jnp.minimum(owner, N_EXPERTS - 1.0).astype(I32)
        total = jnp.max(end_blk, axis=0, keepdims=True)
        nblk_out[...] = jnp.broadcast_to(total, nblk_out.shape).astype(I32)
        group_end = jnp.min(jnp.where(end_blk > blk, end_blk, 2.0 * n_lanes), axis=0, keepdims=True)
        nxt = jnp.sum(jnp.where(end_blk <= group_end, 1.0, 0.0), axis=0, keepdims=True)
        nxt_e_out[...] = jnp.where(group_end < total, nxt, -1.0).astype(I32)

    ioe = lax.broadcasted_iota(I32, (N_EXPERTS, TM), 0)
    e = e_ref[...]
    rows = [jnp.sum(jnp.where(ioe == e[k:k + 1], pstart_ref[...], 0.0), axis=0, keepdims=True)
            for k in range(TOP_K)]
    dest_out[...] = jnp.concatenate(rows, axis=0).astype(I32) + r_ref[...]


def _plan(cnt, eidx, rank, n_blocks):
    t_all = eidx.shape[1]
    n_lanes = pl.cdiv(n_blocks, LANES) * LANES
    return pl.pallas_call(
        _plan_kernel,
        out_shape=[
            jax.ShapeDtypeStruct((TOP_K, t_all), I32),
            jax.ShapeDtypeStruct((1, n_lanes), I32),
            jax.ShapeDtypeStruct((1, n_lanes), I32),
            jax.ShapeDtypeStruct((1, LANES), I32),
            jax.ShapeDtypeStruct((1, N_EXPERTS), I32),
            jax.ShapeDtypeStruct((1, N_EXPERTS), I32),
        ],
        grid=(t_all // TM,),
        in_specs=[
            _const_spec((N_EXPERTS, 1)),
            pl.BlockSpec((TOP_K, TM), lambda i: (0, i)),
            pl.BlockSpec((TOP_K, TM), lambda i: (0, i)),
        ],
        out_specs=[
            pl.BlockSpec((TOP_K, TM), lambda i: (0, i)),
            pl.BlockSpec((1, n_lanes), lambda i: (0, 0)),
            pl.BlockSpec((1, n_lanes), lambda i: (0, 0)),
            pl.BlockSpec((1, LANES), lambda i: (0, 0)),
            pl.BlockSpec((1, N_EXPERTS), lambda i: (0, 0)),
            pl.BlockSpec((1, N_EXPERTS), lambda i: (0, 0)),
        ],
        scratch_shapes=[pltpu.VMEM((N_EXPERTS, 1), F32)],
        compiler_params=pltpu.CompilerParams(dimension_semantics=("arbitrary",)),
        name="plan",
    )(cnt, eidx, rank)


def _dispatch_kernel(cnt_ref, pst_ref, nblk_ref, dest_ref, h2_ref, xs_hbm, zeros, sem, fill_sem,
                     *, n_steps, n_blocks):
    i = pl.program_id(0)
    per_step = pl.cdiv(N_EXPERTS, n_steps)

    @pl.when(i == 0)
    def _():
        zeros[...] = jnp.zeros_like(zeros)

    def issue(t, carry):
        for k in range(TOP_K):
            pltpu.make_async_copy(h2_ref.at[pl.ds(t, 1)], xs_hbm.at[pl.ds(dest_ref[k, t], 1)], sem).start()
        return carry

    lax.fori_loop(0, TM, issue, 0)

    def fills(do):
        def per_expert(j, carry):
            e = i * per_step + j

            @pl.when(e < N_EXPERTS)
            def _():
                cnt = cnt_ref[e]
                base = pst_ref[e]
                padded = (cnt + (BM - 1)) // BM * BM
                mid = jnp.minimum((cnt + (SUBLANES - 1)) // SUBLANES * SUBLANES, padded)

                def one(r, c):
                    do(pltpu.make_async_copy(zeros.at[pl.ds(0, 1)], xs_hbm.at[pl.ds(base + r, 1)], fill_sem))
                    return c

                lax.fori_loop(cnt, mid, one, 0)

                def eight(q, c):
                    r = pl.multiple_of(base + mid + q * SUBLANES, SUBLANES)
                    do(pltpu.make_async_copy(zeros.at[pl.ds(0, SUBLANES)], xs_hbm.at[pl.ds(r, SUBLANES)],
                                             fill_sem))
                    return c

                lax.fori_loop(0, (padded - mid) // SUBLANES, eight, 0)

            return carry

        lax.fori_loop(0, per_step, per_expert, 0)

        def per_block(j, carry):
            b = nblk_ref[0] + i * per_step + j

            @pl.when(b < n_blocks)
            def _():
                r = pl.multiple_of(b * BM, BM)
                do(pltpu.make_async_copy(zeros, xs_hbm.at[pl.ds(r, BM)], fill_sem))

            return carry

        lax.fori_loop(0, per_step, per_block, 0)

    fills(lambda c: c.start())
    for k in range(TOP_K):
        pltpu.make_async_copy(h2_ref, xs_hbm.at[pl.ds(0, TM)], sem).wait()
    fills(lambda c: c.wait())


def _dispatch(cnt_row, pstart_row, nblk, dest, h2, n_blocks):
    t_all, d = h2.shape
    n_steps = t_all // TM
    kern = functools.partial(_dispatch_kernel, n_steps=n_steps, n_blocks=n_blocks)
    grid_spec = pltpu.PrefetchScalarGridSpec(
        num_scalar_prefetch=3,
        grid=(n_steps,),
        in_specs=[
            pl.BlockSpec((TOP_K, TM), lambda i, *_: (0, i), memory_space=pltpu.SMEM),
            pl.BlockSpec((TM, d), lambda i, *_: (i, 0)),
        ],
        out_specs=pl.BlockSpec(memory_space=pl.ANY),
        scratch_shapes=[pltpu.VMEM((BM, d), F32), pltpu.SemaphoreType.DMA(()), pltpu.SemaphoreType.DMA(())],
    )
    return pl.pallas_call(
        kern,
        out_shape=jax.ShapeDtypeStruct((n_blocks * BM, d), F32),
        grid_spec=grid_spec,
        compiler_params=pltpu.CompilerParams(dimension_semantics=("arbitrary",)),
        name="dispatch",
    )(cnt_row, pstart_row, nblk, dest, h2)


def _moe_kernel(blk_e_ref, nxt_e_ref, nblk_ref, x_ref, wg_hbm, wu_hbm, wd_hbm, y_ref,
                wg_l, wu_l, wd_l, wgu_s, wd_s, sems, cur_ref):
    b = pl.program_id(0)

    def weight_copies(e, slot):
        return (pltpu.make_async_copy(wg_hbm.at[e], wg_l.at[slot], sems.at[slot, 0]),
                pltpu.make_async_copy(wu_hbm.at[e], wu_l.at[slot], sems.at[slot, 1]),
                pltpu.make_async_copy(wd_hbm.at[e], wd_l.at[slot], sems.at[slot, 2]))

    @pl.when(b < nblk_ref[0])
    def _():
        e = blk_e_ref[b]

        @pl.when(b == 0)
        def _():
            cur_ref[0] = 0
            for c in weight_copies(e, 0):
                c.start()

        @pl.when((b == 0) | (e != blk_e_ref[jnp.maximum(b - 1, 0)]))
        def _():
            @pl.when(b > 0)
            def _():
                cur_ref[0] = 1 - cur_ref[0]

            slot = cur_ref[0]
            for c in weight_copies(e, slot):
                c.wait()
            nxt = nxt_e_ref[b]

            @pl.when(nxt >= 0)
            def _():
                for c in weight_copies(nxt, 1 - slot):
                    c.start()

            wgu_s[:, :D_EXPERT] = wg_l[slot].astype(BF16)
            wgu_s[:, D_EXPERT:] = wu_l[slot].astype(BF16)
            wd_s[...] = wd_l[slot].astype(BF16)

        gu = jnp.dot(x_ref[...].astype(BF16), wgu_s[...], preferred_element_type=F32)
        h = (jax.nn.silu(gu[:, :D_EXPERT]) * gu[:, D_EXPERT:]).astype(BF16)
        y_ref[...] = jnp.dot(h, wd_s[...], preferred_element_type=F32)

    @pl.when(b >= nblk_ref[0])
    def _():
        y_ref[...] = jnp.zeros_like(y_ref)


def _moe(blk_e, nxt_e, nblk, xs, wg, wu, wd):
    n_rows, d = xs.shape
    n_blocks = n_rows // BM

    def row_map(b, be, nx, nb):
        return (jnp.minimum(b, nb[0] - 1), 0)

    def out_map(b, be, nx, nb):
        return (b, 0)

    grid_spec = pltpu.PrefetchScalarGridSpec(
        num_scalar_prefetch=3,
        grid=(n_blocks,),
        in_specs=[
            pl.BlockSpec((BM, d), row_map),
            pl.BlockSpec(memory_space=pl.ANY),
            pl.BlockSpec(memory_space=pl.ANY),
            pl.BlockSpec(memory_space=pl.ANY),
        ],
        out_specs=pl.BlockSpec((BM, d), out_map),
        scratch_shapes=[
            pltpu.VMEM((2, d, D_EXPERT), F32), pltpu.VMEM((2, d, D_EXPERT), F32), pltpu.VMEM((2, D_EXPERT, d), F32),
            pltpu.VMEM((d, 2 * D_EXPERT), BF16), pltpu.VMEM((D_EXPERT, d), BF16),
            pltpu.SemaphoreType.DMA((2, 3)), pltpu.SMEM((1,), I32),
        ],
    )
    return pl.pallas_call(
        _moe_kernel,
        out_shape=jax.ShapeDtypeStruct((n_rows, d), F32),
        grid_spec=grid_spec,
        compiler_params=pltpu.CompilerParams(dimension_semantics=("arbitrary",),
                                             vmem_limit_bytes=VMEM_LIMIT_BYTES),
        name="moe",
    )(blk_e, nxt_e, nblk, xs, wg, wu, wd)


def _final_kernel(dest_ref, dnext_ref, xs_ref, w_ref, gp_ref, gs_ref, fp_ref, fs_ref, gf_ref, y_hbm,
                  op_ref, os_ref, buf, sems, *, n_tiles, n_ptiles, tiles_per_seq):
    i = pl.program_id(0)
    d = D_MODEL
    slot = i % 2

    def gather(d_ref, s):
        def issue(t, carry):
            for k in range(TOP_K):
                pltpu.make_async_copy(y_hbm.at[pl.ds(d_ref[k, t], 1)], buf.at[s, k, pl.ds(t, 1)],
                                      sems.at[s]).start()
            return carry

        lax.fori_loop(0, TC, issue, 0)

    @pl.when(i == 0)
    def _():
        gather(dest_ref, 0)

    @pl.when(i + 1 < n_tiles)
    def _():
        gather(dnext_ref, 1 - slot)

    for k in range(TOP_K):
        pltpu.make_async_copy(y_hbm.at[pl.ds(0, TC)], buf.at[slot, k], sems.at[slot]).wait()

    w = w_ref[...]
    acc = w[:, 0:1] * buf[slot, 0]
    for k in range(1, TOP_K):
        acc = acc + w[:, k:k + 1] * buf[slot, k]

    def finish(gate2, shift, scale, o_ref):
        x2 = xs_ref[...] + gate2 * acc
        o_ref[...] = _rms(x2, gf_ref[...]) * (1.0 + scale) + shift

    @pl.when(i < n_ptiles)
    def _():
        b = i // tiles_per_seq
        f = fp_ref[pl.ds(b, 1), :]
        finish(gp_ref[pl.ds(b, 1), :], f[:, :d], f[:, d:], op_ref)

    @pl.when(i >= n_ptiles)
    def _():
        f = fs_ref[...]
        finish(gs_ref[...], f[:, :d], f[:, d:], os_ref)


def _final(dest, xs, wts, m_all, mf_all, gf, y_sorted, *, n_batch, seq, t_p):
    t_all, d = xs.shape
    n_tiles = t_all // TC
    n_ptiles = t_p // TC
    dec_batch = m_all.shape[0] - n_batch
    kern = functools.partial(_final_kernel, n_tiles=n_tiles, n_ptiles=n_ptiles, tiles_per_seq=seq // TC)
    pb = dec_batch // n_batch
    return pl.pallas_call(
        kern,
        out_shape=[jax.ShapeDtypeStruct((t_p, d), F32), jax.ShapeDtypeStruct((t_all - t_p, d), F32)],
        grid=(n_tiles,),
        in_specs=[
            pl.BlockSpec((TOP_K, TC), lambda i: (0, i), memory_space=pltpu.SMEM),
            pl.BlockSpec((TOP_K, TC), lambda i: (0, jnp.minimum(i + 1, n_tiles - 1)), memory_space=pltpu.SMEM),
            pl.BlockSpec((TC, d), lambda i: (i, 0)),
            pl.BlockSpec((TC, TOP_K), lambda i: (i, 0)),
            pl.BlockSpec((n_batch, d), lambda i: (pb, N_MOD - 1)),
            pl.BlockSpec((dec_batch, d), lambda i: (0, N_MOD - 1)),
            pl.BlockSpec((n_batch, 2 * d), lambda i: (pb, 0)),
            pl.BlockSpec((dec_batch, 2 * d), lambda i: (0, 0)),
            _const_spec(gf.shape),
            pl.BlockSpec(memory_space=pl.ANY),
        ],
        out_specs=[
            pl.BlockSpec((TC, d), lambda i: (jnp.minimum(i, n_ptiles - 1), 0)),
            pl.BlockSpec((TC, d), lambda i: (jnp.maximum(i - n_ptiles, 0), 0)),
        ],
        scratch_shapes=[pltpu.VMEM((2, TOP_K, TC, d), F32), pltpu.SemaphoreType.DMA((2,))],
        compiler_params=pltpu.CompilerParams(dimension_semantics=("arbitrary",),
                                             vmem_limit_bytes=VMEM_LIMIT_BYTES),
        name="final",
    )(dest, dest, xs, wts, m_all, m_all, mf_all, mf_all, gf, y_sorted)


def kernel(x_prompt, x_sample, state_conv, c_prompt, c_sample, w_ada, b_ada, g_norm1, w_in, w_spatial, b_spatial, g_v, w_conv, g_out_a, g_out_b, w_out, g_norm2, w_router, b_router, w_exp_gate, w_exp_up, w_exp_down, w_sh_gate, w_sh_up, w_sh_down, w_ada_final, b_ada_final, g_final):
    n_batch, seq, d = x_prompt.shape
    dec_batch, dec_seq, _ = x_sample.shape
    assert w_ada.shape[0] == 1, "one layer only"
    assert d == D_MODEL and dec_seq == DEC_SEQ and dec_batch * dec_seq == TM and seq % TM == 0
    assert n_batch % SUBLANES == 0 and dec_batch % n_batch == 0
    t_p = n_batch * seq
    t_s = dec_batch * dec_seq
    t_all = t_p + t_s

    c_all = jnp.concatenate([c_sample, c_prompt], axis=0)
    m_all = _ada(c_all, w_ada[0], b_ada[0])
    mf_all = _ada(c_all, w_ada_final, b_ada_final)

    rep = functools.partial(jnp.repeat, repeats=A_HEAD_DIM, axis=1)
    tril = jnp.tril(jnp.ones((CHUNK, CHUNK), dtype=bool))
    wt = jnp.where(tril, w_spatial[0], 0.0).astype(BF16)
    bsp = rep(b_spatial[0].T)
    csp = rep(jnp.transpose(w_spatial[0][:, :DEC_SEQ, :DEC_SEQ], (1, 2, 0)).reshape(DEC_SEQ * DEC_SEQ, A_HEADS))
    bsps = rep(b_spatial[0][:, :DEC_SEQ].T)
    st = jnp.transpose(state_conv[0], (1, 0, 2))
    wshgu = jnp.concatenate([w_sh_gate[0], w_sh_up[0]], axis=1).astype(BF16)
    xsm = jnp.transpose(x_sample, (1, 0, 2)).reshape(t_s, d)

    xs, h2, lt, cztail, vs, cz23 = _mixer(
        x_prompt.reshape(t_p, d), xsm, m_all, g_norm1, w_in[0].astype(BF16), wt, bsp,
        g_v.reshape(1, MIX_A), w_conv[0], g_out_a, g_out_b, w_out[0].astype(BF16), g_norm2,
        w_router[0].T.astype(BF16), wshgu, w_sh_down[0].astype(BF16), csp, bsps, st,
        n_batch=n_batch, seq=seq)

    eidx, wts, rank, cnt = _route(lt, b_router[0].reshape(N_EXPERTS, 1))

    n_blocks = (t_all * TOP_K) // BM + N_EXPERTS
    dest, blk_e, nxt_e, nblk, cnt_row, pstart_row = _plan(cnt, eidx, rank, n_blocks)

    x_sorted = _dispatch(cnt_row[0], pstart_row[0], nblk[0, :1], dest, h2, n_blocks)
    y_sorted = _moe(blk_e[0, :n_blocks], nxt_e[0, :n_blocks], nblk[0, :1], x_sorted,
                    w_exp_gate[0], w_exp_up[0], w_exp_down[0])
    y_p, y_s = _final(dest, xs, wts.T, m_all, mf_all, g_final.reshape(1, d), y_sorted,
                      n_batch=n_batch, seq=seq, t_p=t_p)

    tiles_per_seq = seq // TM
    y_prompt = y_p.reshape(n_batch, seq, d)
    y_sample = jnp.transpose(y_s.reshape(dec_seq, dec_batch, d), (1, 0, 2))
    conv_p = cztail[tiles_per_seq - 1:n_batch * tiles_per_seq:tiles_per_seq, SUBLANES - (CONV_W - 1):, :][None]
    conv_s = jnp.transpose(cz23, (1, 0, 2))[None]
    v_s = jnp.transpose(vs.reshape(dec_seq, dec_batch, A_HEADS, A_HEAD_DIM), (1, 0, 2, 3))[None]
    return (y_prompt, y_sample, conv_p, conv_s, v_s)
```

```python
import functools

import jax
import jax.numpy as jnp
from jax import lax
from jax.experimental import pallas as pl
from jax.experimental.pallas import tpu as pltpu

F32 = jnp.float32
BF16 = jnp.bfloat16
I32 = jnp.int32

D_MODEL = 1024
MIX_A = 512
A_HEADS = 4
A_HEAD_DIM = 128
CHUNK = 128
CONV_DIM = 512
CONV_W = 3
PROJ_DIM = 2 * MIX_A + 3 * CONV_DIM
N_EXPERTS = 256
TOP_K = 8
N_GROUPS = 8
TOPK_GROUPS = 4
GROUP_SIZE = N_EXPERTS // N_GROUPS
D_EXPERT = 256
D_SHARED = 256
ROUTED_SCALE = 2.5
N_MOD = 6
RMS_EPS = 1e-6
DEC_SEQ = 4

LANES = 128
SUBLANES = 8
TM = 512
BM = 256
TC = 128
X_AHEAD = 2
X_RING = X_AHEAD + 1
VMEM_LIMIT_BYTES = 56 * 1024 * 1024


def _rms(x, g):
    return x * lax.rsqrt(jnp.mean(x * x, axis=-1, keepdims=True) + RMS_EPS) * g


def _const_spec(shape):
    nd = len(shape)
    return pl.BlockSpec(shape, lambda *_: (0,) * nd, pipeline_mode=pl.Buffered(1))


def _ada_kernel(c_ref, w_ref, b_ref, o_ref):
    a = jax.nn.silu(c_ref[...]).astype(BF16)
    o_ref[...] = jnp.dot(a, w_ref[...].astype(BF16), preferred_element_type=F32) + b_ref[...]


def _ada(c_all, w, b):
    rows, n = c_all.shape[0], w.shape[1]
    return pl.pallas_call(
        _ada_kernel,
        out_shape=jax.ShapeDtypeStruct((rows, n), F32),
        grid=(n // D_MODEL,),
        in_specs=[
            pl.BlockSpec((rows, D_MODEL), lambda j: (0, 0)),
            pl.BlockSpec((D_MODEL, D_MODEL), lambda j: (0, j)),
            pl.BlockSpec((1, D_MODEL), lambda j: (0, j)),
        ],
        out_specs=pl.BlockSpec((rows, D_MODEL), lambda j: (0, j)),
        compiler_params=pltpu.CompilerParams(dimension_semantics=("arbitrary",)),
        name="ada",
    )(c_all, w, b.reshape(1, n))


def _mixer_kernel(xp_ref, xsm_ref, mp_ref, ms_ref, g1_ref, win_ref, wt_ref, bsp_ref, gv_ref, wconv_ref,
                  goa_ref, gob_ref, wout_ref, g2_ref, wrt_ref, wshgu_ref, wshd_ref, csp_ref, bsps_ref, st_ref,
                  xs_out, h2_out, lt_out, cztail_out, vs_out, cz23_out, carry_ref, *, n_ptiles, tiles_per_seq):
    i = pl.program_id(0)
    d = D_MODEL

    def proj(x, sh1, sc1):
        h = _rms(x, g1_ref[...]) * (1.0 + sc1) + sh1
        p = jnp.dot(h.astype(BF16), win_ref[...], preferred_element_type=F32)
        u = jax.nn.gelu(p[:, :MIX_A])
        v = jax.nn.gelu(p[:, MIX_A:2 * MIX_A])
        vn = jnp.concatenate(
            [_rms(v[:, h * LANES:(h + 1) * LANES], gv_ref[:, h * LANES:(h + 1) * LANES]) for h in range(A_HEADS)],
            axis=1)
        o = 2 * MIX_A
        return u, vn, p[:, o:o + CONV_DIM], p[:, o + CONV_DIM:o + 2 * CONV_DIM], p[:, o + 2 * CONV_DIM:]

    def tail(x, ya, yb, gate1, sh2, sc2, gate2):
        cat = jnp.concatenate([_rms(ya, goa_ref[...]), _rms(yb, gob_ref[...])], axis=1).astype(BF16)
        x1 = x + gate1 * jnp.dot(cat, wout_ref[...], preferred_element_type=F32)
        h2 = _rms(x1, g2_ref[...]) * (1.0 + sc2) + sh2
        h2b = h2.astype(BF16)
        lt_out[...] = lax.dot_general(wrt_ref[...], h2b, (((1,), (1,)), ((), ())), preferred_element_type=F32)
        gu = jnp.dot(h2b, wshgu_ref[...], preferred_element_type=F32)
        hs = (jax.nn.silu(gu[:, :D_SHARED]) * gu[:, D_SHARED:]).astype(BF16)
        xs_out[...] = x1 + gate2 * jnp.dot(hs, wshd_ref[...], preferred_element_type=F32)
        h2_out[...] = h2

    @pl.when(i < n_ptiles)
    def _prompt():
        b = i // tiles_per_seq
        m = mp_ref[pl.ds(b, 1), :]
        mod = [m[:, k * d:(k + 1) * d] for k in range(N_MOD)]
        x = xp_ref[...]
        u, vn, bg, cg, z = proj(x, mod[0], mod[1])
        n_chunks = TM // CHUNK
        ya_cols = []
        for h in range(A_HEADS):
            vh = vn[:, h * LANES:(h + 1) * LANES].astype(BF16)
            rhs = jnp.concatenate([vh[c * CHUNK:(c + 1) * CHUNK] for c in range(n_chunks)], axis=1)
            mix = jnp.dot(wt_ref[h], rhs, preferred_element_type=F32)
            bias = bsp_ref[:, h * LANES:(h + 1) * LANES]
            mix = jnp.concatenate([mix[:, c * LANES:(c + 1) * LANES] + bias for c in range(n_chunks)], axis=0)
            ya_cols.append(u[:, h * LANES:(h + 1) * LANES] * mix)
        ya = jnp.concatenate(ya_cols, axis=1)
        cz = cg * z

        @pl.when(i % tiles_per_seq == 0)
        def _():
            carry_ref[...] = jnp.zeros_like(carry_ref)

        c6 = carry_ref[SUBLANES - 2:SUBLANES - 1, :]
        c7 = carry_ref[SUBLANES - 1:SUBLANES, :]
        r = lax.broadcasted_iota(I32, (TM, 1), 0)
        p1 = jnp.where(r == 0, c7, pltpu.roll(cz, 1, 0))
        p2 = jnp.where(r == 0, c6, jnp.where(r == 1, c7, pltpu.roll(cz, 2, 0)))
        wc = wconv_ref[...]
        yb = bg * (p2 * wc[0:1] + p1 * wc[1:2] + cz * wc[2:3])
        carry_ref[...] = cz[TM - SUBLANES:]
        cztail_out[0] = cz[TM - SUBLANES:]
        tail(x, ya, yb, mod[2], mod[3], mod[4], mod[5])

    @pl.when(i == n_ptiles)
    def _sample():
        ms = ms_ref[...]
        nb = TM // DEC_SEQ

        def mod(k):
            return jnp.concatenate([ms[:, k * d:(k + 1) * d]] * DEC_SEQ, axis=0)

        x = xsm_ref[...]
        u, vn, bg, cg, z = proj(x, mod(0), mod(1))
        vt = [vn[t * nb:(t + 1) * nb] for t in range(DEC_SEQ)]
        mixes = []
        for t in range(DEC_SEQ):
            acc = csp_ref[DEC_SEQ * t:DEC_SEQ * t + 1, :] * vt[0]
            for s in range(1, t + 1):
                acc = acc + csp_ref[DEC_SEQ * t + s:DEC_SEQ * t + s + 1, :] * vt[s]
            mixes.append(acc + bsps_ref[t:t + 1, :])
        ya = u * jnp.concatenate(mixes, axis=0)
        cz = cg * z
        czt = [cz[t * nb:(t + 1) * nb] for t in range(DEC_SEQ)]
        full = [st_ref[0], st_ref[1]] + czt
        wc = wconv_ref[...]
        yc = jnp.concatenate(
            [full[t] * wc[0:1] + full[t + 1] * wc[1:2] + full[t + 2] * wc[2:3] for t in range(DEC_SEQ)], axis=0)
        yb = bg * yc
        vs_out[...] = vn
        cz23_out[0] = czt[DEC_SEQ - 2]
        cz23_out[1] = czt[DEC_SEQ - 1]
        cztail_out[0] = cz[TM - SUBLANES:]
        tail(x, ya, yb, mod(2), mod(3), mod(4), mod(5))


def _mixer(xp, xsm, m_all, g1, win, wt, bsp, gv, wconv, goa, gob, wout, g2, wrt, wshgu, wshd, csp, bsps, st,
           *, n_batch, seq):
    t_p = xp.shape[0]
    n_ptiles = t_p // TM
    n_tiles = n_ptiles + 1
    t_all = n_tiles * TM
    dec_batch = xsm.shape[0] // DEC_SEQ
    d = D_MODEL
    kern = functools.partial(_mixer_kernel, n_ptiles=n_ptiles, tiles_per_seq=seq // TM)
    in_specs = [
        pl.BlockSpec((TM, d), lambda i: (jnp.minimum(i, n_ptiles - 1), 0)),
        _const_spec((TM, d)),
        pl.BlockSpec((n_batch, N_MOD * d), lambda i: (dec_batch // n_batch, 0), pipeline_mode=pl.Buffered(1)),
        pl.BlockSpec((dec_batch, N_MOD * d), lambda i: (0, 0), pipeline_mode=pl.Buffered(1)),
        _const_spec(g1.shape), _const_spec(win.shape), _const_spec(wt.shape), _const_spec(bsp.shape),
        _const_spec(gv.shape), _const_spec(wconv.shape), _const_spec(goa.shape), _const_spec(gob.shape),
        _const_spec(wout.shape), _const_spec(g2.shape), _const_spec(wrt.shape), _const_spec(wshgu.shape),
        _const_spec(wshd.shape), _const_spec(csp.shape), _const_spec(bsps.shape), _const_spec(st.shape),
    ]
    out_shape = [
        jax.ShapeDtypeStruct((t_all, d), F32),
        jax.ShapeDtypeStruct((t_all, d), F32),
        jax.ShapeDtypeStruct((N_EXPERTS, t_all), F32),
        jax.ShapeDtypeStruct((n_tiles, SUBLANES, CONV_DIM), F32),
        jax.ShapeDtypeStruct((TM, MIX_A), F32),
        jax.ShapeDtypeStruct((2, dec_batch, CONV_DIM), F32),
    ]
    out_specs = [
        pl.BlockSpec((TM, d), lambda i: (i, 0)),
        pl.BlockSpec((TM, d), lambda i: (i, 0)),
        pl.BlockSpec((N_EXPERTS, TM), lambda i: (0, i)),
        pl.BlockSpec((1, SUBLANES, CONV_DIM), lambda i: (i, 0, 0)),
        pl.BlockSpec((TM, MIX_A), lambda i: (0, 0)),
        pl.BlockSpec((2, dec_batch, CONV_DIM), lambda i: (0, 0, 0)),
    ]
    return pl.pallas_call(
        kern,
        out_shape=out_shape,
        grid=(n_tiles,),
        in_specs=in_specs,
        out_specs=out_specs,
        scratch_shapes=[pltpu.VMEM((SUBLANES, CONV_DIM), F32)],
        compiler_params=pltpu.CompilerParams(dimension_semantics=("arbitrary",),
                                             vmem_limit_bytes=VMEM_LIMIT_BYTES),
        name="mixer",
    )(xp, xsm, m_all, m_all, g1, win, wt, bsp, gv, wconv, goa, gob, wout, g2, wrt, wshgu, wshd, csp, bsps, st)


def _route_kernel(lt_ref, br_ref, e_out, w_out, r_out, cnt_out, carry_ref, tri_ref):
    i = pl.program_id(0)
    neg = -jnp.inf

    @pl.when(i == 0)
    def _():
        carry_ref[...] = jnp.zeros_like(carry_ref)
        a = lax.broadcasted_iota(I32, (TM, TM), 0)
        b = lax.broadcasted_iota(I32, (TM, TM), 1)
        tri_ref[...] = jnp.where(a < b, 1.0, 0.0).astype(BF16)

    scores = jax.nn.sigmoid(lt_ref[...])
    sel = scores + br_ref[...]
    iog = lax.broadcasted_iota(I32, (GROUP_SIZE, TM), 0)
    gs_rows = []
    for g in range(N_GROUPS):
        sg = sel[g * GROUP_SIZE:(g + 1) * GROUP_SIZE]
        m1 = jnp.max(sg, axis=0, keepdims=True)
        i1 = jnp.min(jnp.where(sg == m1, iog, GROUP_SIZE), axis=0, keepdims=True)
        m2 = jnp.max(jnp.where(iog == i1, neg, sg), axis=0, keepdims=True)
        gs_rows.append(m1 + m2)
    gs = jnp.concatenate(gs_rows, axis=0)
    io8 = lax.broadcasted_iota(I32, (N_GROUPS, TM), 0)
    keep = jnp.zeros((N_GROUPS, TM), F32)
    for _ in range(TOPK_GROUPS):
        m = jnp.max(gs, axis=0, keepdims=True)
        idx = jnp.min(jnp.where(gs == m, io8, N_GROUPS), axis=0, keepdims=True)
        hit = io8 == idx
        keep = jnp.where(hit, 1.0, keep)
        gs = jnp.where(hit, neg, gs)
    sel = jnp.concatenate(
        [jnp.where(keep[g:g + 1] > 0.0, sel[g * GROUP_SIZE:(g + 1) * GROUP_SIZE], neg) for g in range(N_GROUPS)],
        axis=0)
    ioe = lax.broadcasted_iota(I32, (N_EXPERTS, TM), 0)
    picked = jnp.zeros((N_EXPERTS, TM), F32)
    e_rows, w_rows = [], []
    for _ in range(TOP_K):
        m = jnp.max(sel, axis=0, keepdims=True)
        idx = jnp.min(jnp.where(sel == m, ioe, N_EXPERTS), axis=0, keepdims=True)
        hit = ioe == idx
        e_rows.append(idx)
        w_rows.append(jnp.sum(jnp.where(hit, scores, 0.0), axis=0, keepdims=True))
        picked = jnp.where(hit, 1.0, picked)
        sel = jnp.where(hit, neg, sel)
    wk = jnp.concatenate(w_rows, axis=0)
    w_out[...] = wk / jnp.sum(wk, axis=0, keepdims=True) * ROUTED_SCALE
    e_out[...] = jnp.concatenate(e_rows, axis=0)
    before = jnp.dot(picked.astype(BF16), tri_ref[...], preferred_element_type=F32) + carry_ref[...]
    r_rows = [jnp.sum(jnp.where(ioe == e_rows[k], before, 0.0), axis=0, keepdims=True) for k in range(TOP_K)]
    r_out[...] = jnp.concatenate(r_rows, axis=0).astype(I32)
    carry_ref[...] = carry_ref[...] + jnp.sum(picked, axis=1, keepdims=True)
    cnt_out[...] = carry_ref[...]


def _route(lt, br):
    t_all = lt.shape[1]
    n_tiles = t_all // TM
    return pl.pallas_call(
        _route_kernel,
        out_shape=[
            jax.ShapeDtypeStruct((TOP_K, t_all), I32),
            jax.ShapeDtypeStruct((TOP_K, t_all), F32),
            jax.ShapeDtypeStruct((TOP_K, t_all), I32),
            jax.ShapeDtypeStruct((N_EXPERTS, 1), F32),
        ],
        grid=(n_tiles,),
        in_specs=[pl.BlockSpec((N_EXPERTS, TM), lambda i: (0, i)), _const_spec((N_EXPERTS, 1))],
        out_specs=[
            pl.BlockSpec((TOP_K, TM), lambda i: (0, i)),
            pl.BlockSpec((TOP_K, TM), lambda i: (0, i)),
            pl.BlockSpec((TOP_K, TM), lambda i: (0, i)),
            pl.BlockSpec((N_EXPERTS, 1), lambda i: (0, 0)),
        ],
        scratch_shapes=[pltpu.VMEM((N_EXPERTS, 1), F32), pltpu.VMEM((TM, TM), BF16)],
        compiler_params=pltpu.CompilerParams(dimension_semantics=("arbitrary",)),
        name="route",
    )(lt, br)


def _plan_kernel(cnt_ref, e_ref, r_ref, dest_out, blk_e_out, nxt_e_out, nblk_out, cnt_row_out, pstart_row_out,
                 pstart_ref):
    i = pl.program_id(0)

    @pl.when(i == 0)
    def _():
        nb = jnp.floor((cnt_ref[...] + (BM - 1)) * (1.0 / BM))
        a = lax.broadcasted_iota(I32, (N_EXPERTS, N_EXPERTS), 0)
        b = lax.broadcasted_iota(I32, (N_EXPERTS, N_EXPERTS), 1)
        lower = jnp.where(b < a, 1.0, 0.0).astype(BF16)
        nb_l = jnp.broadcast_to(nb, (N_EXPERTS, LANES)).astype(BF16)
        first_blk = jnp.dot(lower, nb_l, preferred_element_type=F32)[:, 0:1]
        end_blk = first_blk + nb
        pstart_ref[...] = first_blk * BM
        cnt_row_out[...] = jnp.sum(jnp.where(a == b, cnt_ref[...], 0.0), axis=0, keepdims=True).astype(I32)
        pstart_row_out[...] = jnp.sum(jnp.where(a == b, first_blk * BM, 0.0), axis=0, keepdims=True).astype(I32)
        n_lanes = blk_e_out.shape[1]
        blk = lax.broadcasted_iota(I32, (N_EXPERTS, n_lanes), 1).astype(F32)
        owner = jnp.sum(jnp.where(end_blk <= blk, 1.0, 0.0), axis=0, keepdims=True)
        blk_e_out[...] = jnp.minimum(owner, N_EXPERTS - 1.0).astype(I32)
        total = jnp.max(end_blk, axis=0, keepdims=True)
        nblk_out[...] = jnp.broadcast_to(total, nblk_out.shape).astype(I32)
        group_end = jnp.min(jnp.where(end_blk > blk, end_blk, 2.0 * n_lanes), axis=0, keepdims=True)
        nxt = jnp.sum(jnp.where(end_blk <= group_end, 1.0, 0.0), axis=0, keepdims=True)
        nxt_e_out[...] = jnp.where(group_end < total, nxt, -1.0).astype(I32)

    ioe = lax.broadcasted_iota(I32, (N_EXPERTS, TM), 0)
    e = e_ref[...]
    rows = [jnp.sum(jnp.where(ioe == e[k:k + 1], pstart_ref[...], 0.0), axis=0, keepdims=True)
            for k in range(TOP_K)]
    dest_out[...] = jnp.concatenate(rows, axis=0).astype(I32) + r_ref[...]


def _plan(cnt, eidx, rank, n_blocks):
    t_all = eidx.shape[1]
    n_lanes = pl.cdiv(n_blocks, LANES) * LANES
    return pl.pallas_call(
        _plan_kernel,
        out_shape=[
            jax.ShapeDtypeStruct((TOP_K, t_all), I32),
            jax.ShapeDtypeStruct((1, n_lanes), I32),
            jax.ShapeDtypeStruct((1, n_lanes), I32),
            jax.ShapeDtypeStruct((1, LANES), I32),
            jax.ShapeDtypeStruct((1, N_EXPERTS), I32),
            jax.ShapeDtypeStruct((1, N_EXPERTS), I32),
        ],
        grid=(t_all // TM,),
        in_specs=[
            _const_spec((N_EXPERTS, 1)),
            pl.BlockSpec((TOP_K, TM), lambda i: (0, i)),
            pl.BlockSpec((TOP_K, TM), lambda i: (0, i)),
        ],
        out_specs=[
            pl.BlockSpec((TOP_K, TM), lambda i: (0, i)),
            pl.BlockSpec((1, n_lanes), lambda i: (0, 0)),
            pl.BlockSpec((1, n_lanes), lambda i: (0, 0)),
            pl.BlockSpec((1, LANES), lambda i: (0, 0)),
            pl.BlockSpec((1, N_EXPERTS), lambda i: (0, 0)),
            pl.BlockSpec((1, N_EXPERTS), lambda i: (0, 0)),
        ],
        scratch_shapes=[pltpu.VMEM((N_EXPERTS, 1), F32)],
        compiler_params=pltpu.CompilerParams(dimension_semantics=("arbitrary",)),
        name="plan",
    )(cnt, eidx, rank)


def _dispatch_kernel(cnt_ref, pst_ref, nblk_ref, dest_ref, h2_ref, xs_hbm, zeros, sem, fill_sem,
                     *, n_steps, n_blocks):
    i = pl.program_id(0)
    per_step = pl.cdiv(N_EXPERTS, n_steps)

    @pl.when(i == 0)
    def _():
        zeros[...] = jnp.zeros_like(zeros)

    def issue(t, carry):
        for k in range(TOP_K):
            pltpu.make_async_copy(h2_ref.at[pl.ds(t, 1)], xs_hbm.at[pl.ds(dest_ref[k, t], 1)], sem).start()
        return carry

    lax.fori_loop(0, TM, issue, 0)

    def fills(do):
        def per_expert(j, carry):
            e = i * per_step + j

            @pl.when(e < N_EXPERTS)
            def _():
                cnt = cnt_ref[e]
                base = pst_ref[e]
                padded = (cnt + (BM - 1)) // BM * BM
                mid = jnp.minimum((cnt + (SUBLANES - 1)) // SUBLANES * SUBLANES, padded)

                def one(r, c):
                    do(pltpu.make_async_copy(zeros.at[pl.ds(0, 1)], xs_hbm.at[pl.ds(base + r, 1)], fill_sem))
                    return c

                lax.fori_loop(cnt, mid, one, 0)

                def eight(q, c):
                    r = pl.multiple_of(base + mid + q * SUBLANES, SUBLANES)
                    do(pltpu.make_async_copy(zeros.at[pl.ds(0, SUBLANES)], xs_hbm.at[pl.ds(r, SUBLANES)],
                                             fill_sem))
                    return c

                lax.fori_loop(0, (padded - mid) // SUBLANES, eight, 0)

            return carry

        lax.fori_loop(0, per_step, per_expert, 0)

        def per_block(j, carry):
            b = nblk_ref[0] + i * per_step + j

            @pl.when(b < n_blocks)
            def _():
                r = pl.multiple_of(b * BM, BM)
                do(pltpu.make_async_copy(zeros, xs_hbm.at[pl.ds(r, BM)], fill_sem))

            return carry

        lax.fori_loop(0, per_step, per_block, 0)

    fills(lambda c: c.start())
    for k in range(TOP_K):
        pltpu.make_async_copy(h2_ref, xs_hbm.at[pl.ds(0, TM)], sem).wait()
    fills(lambda c: c.wait())


def _dispatch(cnt_row, pstart_row, nblk, dest, h2, n_blocks):
    t_all, d = h2.shape
    n_steps = t_all // TM
    kern = functools.partial(_dispatch_kernel, n_steps=n_steps, n_blocks=n_blocks)
    grid_spec = pltpu.PrefetchScalarGridSpec(
        num_scalar_prefetch=3,
        grid=(n_steps,),
        in_specs=[
            pl.BlockSpec((TOP_K, TM), lambda i, *_: (0, i), memory_space=pltpu.SMEM),
            pl.BlockSpec((TM, d), lambda i, *_: (i, 0)),
        ],
        out_specs=pl.BlockSpec(memory_space=pl.ANY),
        scratch_shapes=[pltpu.VMEM((BM, d), F32), pltpu.SemaphoreType.DMA(()), pltpu.SemaphoreType.DMA(())],
    )
    return pl.pallas_call(
        kern,
        out_shape=jax.ShapeDtypeStruct((n_blocks * BM, d), F32),
        grid_spec=grid_spec,
        compiler_params=pltpu.CompilerParams(dimension_semantics=("arbitrary",)),
        name="dispatch",
    )(cnt_row, pstart_row, nblk, dest, h2)


def _moe_kernel(blk_e_ref, nxt_e_ref, nblk_ref, x_hbm, wg_hbm, wu_hbm, wd_hbm, y_ref,
                x_ring, wg_l, wu_l, wd_l, wgu_s, wd_s, xsems, sems, cur_ref):
    b = pl.program_id(0)
    nblk = nblk_ref[0]

    def weight_copies(e, slot):
        return (pltpu.make_async_copy(wg_hbm.at[e], wg_l.at[slot], sems.at[slot, 0]),
                pltpu.make_async_copy(wu_hbm.at[e], wu_l.at[slot], sems.at[slot, 1]),
                pltpu.make_async_copy(wd_hbm.at[e], wd_l.at[slot], sems.at[slot, 2]))

    def row_copy(blk):
        slot = blk % X_RING
        r = pl.multiple_of(blk * BM, BM)
        return pltpu.make_async_copy(x_hbm.at[pl.ds(r, BM)], x_ring.at[slot], xsems.at[slot])

    @pl.when(b < nblk)
    def _():
        e = blk_e_ref[b]

        @pl.when(b == 0)
        def _():
            cur_ref[0] = 0
            for c in weight_copies(e, 0):
                c.start()
            for j in range(X_AHEAD):
                @pl.when(j < nblk)
                def _():
                    row_copy(j).start()

        @pl.when(b + X_AHEAD < nblk)
        def _():
            row_copy(b + X_AHEAD).start()

        @pl.when((b == 0) | (e != blk_e_ref[jnp.maximum(b - 1, 0)]))
        def _():
            @pl.when(b > 0)
            def _():
                cur_ref[0] = 1 - cur_ref[0]

            slot = cur_ref[0]
            for c in weight_copies(e, slot):
                c.wait()
            nxt = nxt_e_ref[b]

            @pl.when(nxt >= 0)
            def _():
                for c in weight_copies(nxt, 1 - slot):
                    c.start()

            wgu_s[:, :D_EXPERT] = wg_l[slot].astype(BF16)
            wgu_s[:, D_EXPERT:] = wu_l[slot].astype(BF16)
            wd_s[...] = wd_l[slot].astype(BF16)

        row_copy(b).wait()
        x = x_ring[b % X_RING].astype(BF16)
        gu = jnp.dot(x, wgu_s[...], preferred_element_type=F32)
        h = (jax.nn.silu(gu[:, :D_EXPERT]) * gu[:, D_EXPERT:]).astype(BF16)
        y_ref[...] = jnp.dot(h, wd_s[...], preferred_element_type=F32)

    @pl.when(b >= nblk)
    def _():
        y_ref[...] = jnp.zeros_like(y_ref)


def _moe(blk_e, nxt_e, nblk, xs, wg, wu, wd):
    n_rows, d = xs.shape
    n_blocks = n_rows // BM
    grid_spec = pltpu.PrefetchScalarGridSpec(
        num_scalar_prefetch=3,
        grid=(n_blocks,),
        in_specs=[
            pl.BlockSpec(memory_space=pl.ANY),
            pl.BlockSpec(memory_space=pl.ANY),
            pl.BlockSpec(memory_space=pl.ANY),
            pl.BlockSpec(memory_space=pl.ANY),
        ],
        out_specs=pl.BlockSpec((BM, d), lambda b, *_: (b, 0)),
        scratch_shapes=[
            pltpu.VMEM((X_RING, BM, d), F32),
            pltpu.VMEM((2, d, D_EXPERT), F32), pltpu.VMEM((2, d, D_EXPERT), F32), pltpu.VMEM((2, D_EXPERT, d), F32),
            pltpu.VMEM((d, 2 * D_EXPERT), BF16), pltpu.VMEM((D_EXPERT, d), BF16),
            pltpu.SemaphoreType.DMA((X_RING,)), pltpu.SemaphoreType.DMA((2, 3)), pltpu.SMEM((1,), I32),
        ],
    )
    return pl.pallas_call(
        _moe_kernel,
        out_shape=jax.ShapeDtypeStruct((n_rows, d), F32),
        grid_spec=grid_spec,
        compiler_params=pltpu.CompilerParams(dimension_semantics=("arbitrary",),
                                             vmem_limit_bytes=VMEM_LIMIT_BYTES),
        name="moe",
    )(blk_e, nxt_e, nblk, xs, wg, wu, wd)


def _final_kernel(dest_ref, dnext_ref, xs_ref, w_ref, gp_ref, gs_ref, fp_ref, fs_ref, gf_ref, y_hbm,
                  op_ref, os_ref, buf, sems, *, n_tiles, n_ptiles, tiles_per_seq):
    i = pl.program_id(0)
    d = D_MODEL
    slot = i % 2

    def gather(d_ref, s):
        def issue(t, carry):
            for k in range(TOP_K):
                pltpu.make_async_copy(y_hbm.at[pl.ds(d_ref[k, t], 1)], buf.at[s, k, pl.ds(t, 1)],
                                      sems.at[s]).start()
            return carry

        lax.fori_loop(0, TC, issue, 0)

    @pl.when(i == 0)
    def _():
        gather(dest_ref, 0)

    @pl.when(i + 1 < n_tiles)
    def _():
        gather(dnext_ref, 1 - slot)

    for k in range(TOP_K):
        pltpu.make_async_copy(y_hbm.at[pl.ds(0, TC)], buf.at[slot, k], sems.at[slot]).wait()

    w = w_ref[...]
    acc = w[:, 0:1] * buf[slot, 0]
    for k in range(1, TOP_K):
        acc = acc + w[:, k:k + 1] * buf[slot, k]

    def finish(gate2, shift, scale, o_ref):
        x2 = xs_ref[...] + gate2 * acc
        o_ref[...] = _rms(x2, gf_ref[...]) * (1.0 + scale) + shift

    @pl.when(i < n_ptiles)
    def _():
        b = i // tiles_per_seq
        f = fp_ref[pl.ds(b, 1), :]
        finish(gp_ref[pl.ds(b, 1), :], f[:, :d], f[:, d:], op_ref)

    @pl.when(i >= n_ptiles)
    def _():
        f = fs_ref[...]
        finish(gs_ref[...], f[:, :d], f[:, d:], os_ref)


def _final(dest, xs, wts, m_all, mf_all, gf, y_sorted, *, n_batch, seq, t_p):
    t_all, d = xs.shape
    n_tiles = t_all // TC
    n_ptiles = t_p // TC
    dec_batch = m_all.shape[0] - n_batch
    kern = functools.partial(_final_kernel, n_tiles=n_tiles, n_ptiles=n_ptiles, tiles_per_seq=seq // TC)
    pb = dec_batch // n_batch
    return pl.pallas_call(
        kern,
        out_shape=[jax.ShapeDtypeStruct((t_p, d), F32), jax.ShapeDtypeStruct((t_all - t_p, d), F32)],
        grid=(n_tiles,),
        in_specs=[
            pl.BlockSpec((TOP_K, TC), lambda i: (0, i), memory_space=pltpu.SMEM),
            pl.BlockSpec((TOP_K, TC), lambda i: (0, jnp.minimum(i + 1, n_tiles - 1)), memory_space=pltpu.SMEM),
            pl.BlockSpec((TC, d), lambda i: (i, 0)),
            pl.BlockSpec((TC, TOP_K), lambda i: (i, 0)),
            pl.BlockSpec((n_batch, d), lambda i: (pb, N_MOD - 1)),
            pl.BlockSpec((dec_batch, d), lambda i: (0, N_MOD - 1)),
            pl.BlockSpec((n_batch, 2 * d), lambda i: (pb, 0)),
            pl.BlockSpec((dec_batch, 2 * d), lambda i: (0, 0)),
            _const_spec(gf.shape),
            pl.BlockSpec(memory_space=pl.ANY),
        ],
        out_specs=[
            pl.BlockSpec((TC, d), lambda i: (jnp.minimum(i, n_ptiles - 1), 0)),
            pl.BlockSpec((TC, d), lambda i: (jnp.maximum(i - n_ptiles, 0), 0)),
        ],
        scratch_shapes=[pltpu.VMEM((2, TOP_K, TC, d), F32), pltpu.SemaphoreType.DMA((2,))],
        compiler_params=pltpu.CompilerParams(dimension_semantics=("arbitrary",),
                                             vmem_limit_bytes=VMEM_LIMIT_BYTES),
        name="final",
    )(dest, dest, xs, wts, m_all, m_all, mf_all, mf_all, gf, y_sorted)


def kernel(x_prompt, x_sample, state_conv, c_prompt, c_sample, w_ada, b_ada, g_norm1, w_in, w_spatial, b_spatial, g_v, w_conv, g_out_a, g_out_b, w_out, g_norm2, w_router, b_router, w_exp_gate, w_exp_up, w_exp_down, w_sh_gate, w_sh_up, w_sh_down, w_ada_final, b_ada_final, g_final):
    n_batch, seq, d = x_prompt.shape
    dec_batch, dec_seq, _ = x_sample.shape
    assert w_ada.shape[0] == 1, "one layer only"
    assert d == D_MODEL and dec_seq == DEC_SEQ and dec_batch * dec_seq == TM and seq % TM == 0
    assert n_batch % SUBLANES == 0 and dec_batch % n_batch == 0
    t_p = n_batch * seq
    t_s = dec_batch * dec_seq
    t_all = t_p + t_s

    c_all = jnp.concatenate([c_sample, c_prompt], axis=0)
    m_all = _ada(c_all, w_ada[0], b_ada[0])
    mf_all = _ada(c_all, w_ada_final, b_ada_final)

    rep = functools.partial(jnp.repeat, repeats=A_HEAD_DIM, axis=1)
    tril = jnp.tril(jnp.ones((CHUNK, CHUNK), dtype=bool))
    wt = jnp.where(tril, w_spatial[0], 0.0).astype(BF16)
    bsp = rep(b_spatial[0].T)
    csp = rep(jnp.transpose(w_spatial[0][:, :DEC_SEQ, :DEC_SEQ], (1, 2, 0)).reshape(DEC_SEQ * DEC_SEQ, A_HEADS))
    bsps = rep(b_spatial[0][:, :DEC_SEQ].T)
    st = jnp.transpose(state_conv[0], (1, 0, 2))
    wshgu = jnp.concatenate([w_sh_gate[0], w_sh_up[0]], axis=1).astype(BF16)
    xsm = jnp.transpose(x_sample, (1, 0, 2)).reshape(t_s, d)

    xs, h2, lt, cztail, vs, cz23 = _mixer(
        x_prompt.reshape(t_p, d), xsm, m_all, g_norm1, w_in[0].astype(BF16), wt, bsp,
        g_v.reshape(1, MIX_A), w_conv[0], g_out_a, g_out_b, w_out[0].astype(BF16), g_norm2,
        w_router[0].T.astype(BF16), wshgu, w_sh_down[0].astype(BF16), csp, bsps, st,
        n_batch=n_batch, seq=seq)

    eidx, wts, rank, cnt = _route(lt, b_router[0].reshape(N_EXPERTS, 1))

    n_blocks = (t_all * TOP_K) // BM + N_EXPERTS
    dest, blk_e, nxt_e, nblk, cnt_row, pstart_row = _plan(cnt, eidx, rank, n_blocks)

    x_sorted = _dispatch(cnt_row[0], pstart_row[0], nblk[0, :1], dest, h2, n_blocks)
    y_sorted = _moe(blk_e[0, :n_blocks], nxt_e[0, :n_blocks], nblk[0, :1], x_sorted,
                    w_exp_gate[0], w_exp_up[0], w_exp_down[0])
    y_p, y_s = _final(dest, xs, wts.T, m_all, mf_all, g_final.reshape(1, d), y_sorted,
                      n_batch=n_batch, seq=seq, t_p=t_p)

    tiles_per_seq = seq // TM
    y_prompt = y_p.reshape(n_batch, seq, d)
    y_sample = jnp.transpose(y_s.reshape(dec_seq, dec_batch, d), (1, 0, 2))
    conv_p = cztail[tiles_per_seq - 1:n_batch * tiles_per_seq:tiles_per_seq, SUBLANES - (CONV_W - 1):, :][None]
    conv_s = jnp.transpose(cz23, (1, 0, 2))[None]
    v_s = jnp.transpose(vs.reshape(dec_seq, dec_batch, A_HEADS, A_HEAD_DIM), (1, 0, 2, 3))[None]
    return (y_prompt, y_sample, conv_p, conv_s, v_s)
```

```python
import functools

import jax
import jax.numpy as jnp
from jax import lax
from jax.experimental import pallas as pl
from jax.experimental.pallas import tpu as pltpu

F32 = jnp.float32
BF16 = jnp.bfloat16
I32 = jnp.int32

D_MODEL = 1024
MIX_A = 512
A_HEADS = 4
A_HEAD_DIM = 128
CHUNK = 128
CONV_DIM = 512
CONV_W = 3
PROJ_DIM = 2 * MIX_A + 3 * CONV_DIM
N_EXPERTS = 256
TOP_K = 8
N_GROUPS = 8
TOPK_GROUPS = 4
GROUP_SIZE = N_EXPERTS // N_GROUPS
D_EXPERT = 256
D_SHARED = 256
ROUTED_SCALE = 2.5
N_MOD = 6
RMS_EPS = 1e-6
DEC_SEQ = 4

LANES = 128
SUBLANES = 8
TM = 512
BM = 256
TC = 128
DMA_THREADS = 2
X_AHEAD = 2
X_RING = X_AHEAD + 1
VMEM_LIMIT_BYTES = 56 * 1024 * 1024


def _rms(x, g):
    return x * lax.rsqrt(jnp.mean(x * x, axis=-1, keepdims=True) + RMS_EPS) * g


def _rows(ref, row, n):
    return ref.at[pl.ds(pl.multiple_of(row * SUBLANES, SUBLANES), n * SUBLANES)]


def _row_major(ref, n):
    return jnp.concatenate([ref[pl.ds(j, n, stride=SUBLANES), :] for j in range(SUBLANES)], axis=1)


def _const_spec(shape):
    nd = len(shape)
    return pl.BlockSpec(shape, lambda *_: (0,) * nd, pipeline_mode=pl.Buffered(1))


def _ada_kernel(c_ref, w_ref, b_ref, o_ref):
    a = jax.nn.silu(c_ref[...]).astype(BF16)
    o_ref[...] = jnp.dot(a, w_ref[...].astype(BF16), preferred_element_type=F32) + b_ref[...]


def _ada(c_all, w, b):
    rows, n = c_all.shape[0], w.shape[1]
    return pl.pallas_call(
        _ada_kernel,
        out_shape=jax.ShapeDtypeStruct((rows, n), F32),
        grid=(n // D_MODEL,),
        in_specs=[
            pl.BlockSpec((rows, D_MODEL), lambda j: (0, 0)),
            pl.BlockSpec((D_MODEL, D_MODEL), lambda j: (0, j)),
            pl.BlockSpec((1, D_MODEL), lambda j: (0, j)),
        ],
        out_specs=pl.BlockSpec((rows, D_MODEL), lambda j: (0, j)),
        compiler_params=pltpu.CompilerParams(dimension_semantics=("arbitrary",)),
        name="ada",
    )(c_all, w, b.reshape(1, n))


def _mixer_kernel(xp_ref, xsm_ref, mp_ref, ms_ref, g1_ref, win_ref, wt_ref, bsp_ref, gv_ref, wconv_ref,
                  goa_ref, gob_ref, wout_ref, g2_ref, wrt_ref, wshgu_ref, wshd_ref, csp_ref, bsps_ref, st_ref,
                  xs_out, h2_out, lt_out, cztail_out, vs_out, cz23_out, carry_ref, *, n_ptiles, tiles_per_seq):
    i = pl.program_id(0)
    d = D_MODEL

    def proj(x, sh1, sc1):
        h = _rms(x, g1_ref[...]) * (1.0 + sc1) + sh1
        p = jnp.dot(h.astype(BF16), win_ref[...], preferred_element_type=F32)
        u = jax.nn.gelu(p[:, :MIX_A])
        v = jax.nn.gelu(p[:, MIX_A:2 * MIX_A])
        vn = jnp.concatenate(
            [_rms(v[:, h * LANES:(h + 1) * LANES], gv_ref[:, h * LANES:(h + 1) * LANES]) for h in range(A_HEADS)],
            axis=1)
        o = 2 * MIX_A
        return u, vn, p[:, o:o + CONV_DIM], p[:, o + CONV_DIM:o + 2 * CONV_DIM], p[:, o + 2 * CONV_DIM:]

    def tail(x, ya, yb, gate1, sh2, sc2, gate2):
        cat = jnp.concatenate([_rms(ya, goa_ref[...]), _rms(yb, gob_ref[...])], axis=1).astype(BF16)
        x1 = x + gate1 * jnp.dot(cat, wout_ref[...], preferred_element_type=F32)
        h2 = _rms(x1, g2_ref[...]) * (1.0 + sc2) + sh2
        h2b = h2.astype(BF16)
        lt_out[...] = lax.dot_general(wrt_ref[...], h2b, (((1,), (1,)), ((), ())), preferred_element_type=F32)
        gu = jnp.dot(h2b, wshgu_ref[...], preferred_element_type=F32)
        hs = (jax.nn.silu(gu[:, :D_SHARED]) * gu[:, D_SHARED:]).astype(BF16)
        xs_out[...] = x1 + gate2 * jnp.dot(hs, wshd_ref[...], preferred_element_type=F32)
        for j in range(SUBLANES):
            h2_out[pl.ds(j, TM, stride=SUBLANES), :] = h2[:, j * LANES:(j + 1) * LANES]

    @pl.when(i < n_ptiles)
    def _prompt():
        b = i // tiles_per_seq
        m = mp_ref[pl.ds(b, 1), :]
        mod = [m[:, k * d:(k + 1) * d] for k in range(N_MOD)]
        x = xp_ref[...]
        u, vn, bg, cg, z = proj(x, mod[0], mod[1])
        n_chunks = TM // CHUNK
        ya_cols = []
        for h in range(A_HEADS):
            vh = vn[:, h * LANES:(h + 1) * LANES].astype(BF16)
            rhs = jnp.concatenate([vh[c * CHUNK:(c + 1) * CHUNK] for c in range(n_chunks)], axis=1)
            mix = jnp.dot(wt_ref[h], rhs, preferred_element_type=F32)
            bias = bsp_ref[:, h * LANES:(h + 1) * LANES]
            mix = jnp.concatenate([mix[:, c * LANES:(c + 1) * LANES] + bias for c in range(n_chunks)], axis=0)
            ya_cols.append(u[:, h * LANES:(h + 1) * LANES] * mix)
        ya = jnp.concatenate(ya_cols, axis=1)
        cz = cg * z

        @pl.when(i % tiles_per_seq == 0)
        def _():
            carry_ref[...] = jnp.zeros_like(carry_ref)

        c6 = carry_ref[SUBLANES - 2:SUBLANES - 1, :]
        c7 = carry_ref[SUBLANES - 1:SUBLANES, :]
        r = lax.broadcasted_iota(I32, (TM, 1), 0)
        p1 = jnp.where(r == 0, c7, pltpu.roll(cz, 1, 0))
        p2 = jnp.where(r == 0, c6, jnp.where(r == 1, c7, pltpu.roll(cz, 2, 0)))
        wc = wconv_ref[...]
        yb = bg * (p2 * wc[0:1] + p1 * wc[1:2] + cz * wc[2:3])
        carry_ref[...] = cz[TM - SUBLANES:]
        cztail_out[0] = cz[TM - SUBLANES:]
        tail(x, ya, yb, mod[2], mod[3], mod[4], mod[5])

    @pl.when(i == n_ptiles)
    def _sample():
        ms = ms_ref[...]
        nb = TM // DEC_SEQ

        def mod(k):
            return jnp.concatenate([ms[:, k * d:(k + 1) * d]] * DEC_SEQ, axis=0)

        x = xsm_ref[...]
        u, vn, bg, cg, z = proj(x, mod(0), mod(1))
        vt = [vn[t * nb:(t + 1) * nb] for t in range(DEC_SEQ)]
        mixes = []
        for t in range(DEC_SEQ):
            acc = csp_ref[DEC_SEQ * t:DEC_SEQ * t + 1, :] * vt[0]
            for s in range(1, t + 1):
                acc = acc + csp_ref[DEC_SEQ * t + s:DEC_SEQ * t + s + 1, :] * vt[s]
            mixes.append(acc + bsps_ref[t:t + 1, :])
        ya = u * jnp.concatenate(mixes, axis=0)
        cz = cg * z
        czt = [cz[t * nb:(t + 1) * nb] for t in range(DEC_SEQ)]
        full = [st_ref[0], st_ref[1]] + czt
        wc = wconv_ref[...]
        yc = jnp.concatenate(
            [full[t] * wc[0:1] + full[t + 1] * wc[1:2] + full[t + 2] * wc[2:3] for t in range(DEC_SEQ)], axis=0)
        yb = bg * yc
        vs_out[...] = vn
        cz23_out[0] = czt[DEC_SEQ - 2]
        cz23_out[1] = czt[DEC_SEQ - 1]
        cztail_out[0] = cz[TM - SUBLANES:]
        tail(x, ya, yb, mod(2), mod(3), mod(4), mod(5))


def _mixer(xp, xsm, m_all, g1, win, wt, bsp, gv, wconv, goa, gob, wout, g2, wrt, wshgu, wshd, csp, bsps, st,
           *, n_batch, seq):
    t_p = xp.shape[0]
    n_ptiles = t_p // TM
    n_tiles = n_ptiles + 1
    t_all = n_tiles * TM
    dec_batch = xsm.shape[0] // DEC_SEQ
    d = D_MODEL
    kern = functools.partial(_mixer_kernel, n_ptiles=n_ptiles, tiles_per_seq=seq // TM)
    in_specs = [
        pl.BlockSpec((TM, d), lambda i: (jnp.minimum(i, n_ptiles - 1), 0)),
        _const_spec((TM, d)),
        pl.BlockSpec((n_batch, N_MOD * d), lambda i: (dec_batch // n_batch, 0), pipeline_mode=pl.Buffered(1)),
        pl.BlockSpec((dec_batch, N_MOD * d), lambda i: (0, 0), pipeline_mode=pl.Buffered(1)),
        _const_spec(g1.shape), _const_spec(win.shape), _const_spec(wt.shape), _const_spec(bsp.shape),
        _const_spec(gv.shape), _const_spec(wconv.shape), _const_spec(goa.shape), _const_spec(gob.shape),
        _const_spec(wout.shape), _const_spec(g2.shape), _const_spec(wrt.shape), _const_spec(wshgu.shape),
        _const_spec(wshd.shape), _const_spec(csp.shape), _const_spec(bsps.shape), _const_spec(st.shape),
    ]
    out_shape = [
        jax.ShapeDtypeStruct((t_all, d), F32),
        jax.ShapeDtypeStruct((t_all * SUBLANES, LANES), F32),
        jax.ShapeDtypeStruct((N_EXPERTS, t_all), F32),
        jax.ShapeDtypeStruct((n_tiles, SUBLANES, CONV_DIM), F32),
        jax.ShapeDtypeStruct((TM, MIX_A), F32),
        jax.ShapeDtypeStruct((2, dec_batch, CONV_DIM), F32),
    ]
    out_specs = [
        pl.BlockSpec((TM, d), lambda i: (i, 0)),
        pl.BlockSpec((TM * SUBLANES, LANES), lambda i: (i, 0)),
        pl.BlockSpec((N_EXPERTS, TM), lambda i: (0, i)),
        pl.BlockSpec((1, SUBLANES, CONV_DIM), lambda i: (i, 0, 0)),
        pl.BlockSpec((TM, MIX_A), lambda i: (0, 0)),
        pl.BlockSpec((2, dec_batch, CONV_DIM), lambda i: (0, 0, 0)),
    ]
    return pl.pallas_call(
        kern,
        out_shape=out_shape,
        grid=(n_tiles,),
        in_specs=in_specs,
        out_specs=out_specs,
        scratch_shapes=[pltpu.VMEM((SUBLANES, CONV_DIM), F32)],
        compiler_params=pltpu.CompilerParams(dimension_semantics=("arbitrary",),
                                             vmem_limit_bytes=VMEM_LIMIT_BYTES),
        name="mixer",
    )(xp, xsm, m_all, m_all, g1, win, wt, bsp, gv, wconv, goa, gob, wout, g2, wrt, wshgu, wshd, csp, bsps, st)


def _route_kernel(lt_ref, br_ref, e_out, w_out, r_out, cnt_out, carry_ref, tri_ref):
    i = pl.program_id(0)
    neg = -jnp.inf

    @pl.when(i == 0)
    def _():
        carry_ref[...] = jnp.zeros_like(carry_ref)
        a = lax.broadcasted_iota(I32, (TM, TM), 0)
        b = lax.broadcasted_iota(I32, (TM, TM), 1)
        tri_ref[...] = jnp.where(a < b, 1.0, 0.0).astype(BF16)

    scores = jax.nn.sigmoid(lt_ref[...])
    sel = scores + br_ref[...]
    iog = lax.broadcasted_iota(I32, (GROUP_SIZE, TM), 0)
    gs_rows = []
    for g in range(N_GROUPS):
        sg = sel[g * GROUP_SIZE:(g + 1) * GROUP_SIZE]
        m1 = jnp.max(sg, axis=0, keepdims=True)
        i1 = jnp.min(jnp.where(sg == m1, iog, GROUP_SIZE), axis=0, keepdims=True)
        m2 = jnp.max(jnp.where(iog == i1, neg, sg), axis=0, keepdims=True)
        gs_rows.append(m1 + m2)
    gs = jnp.concatenate(gs_rows, axis=0)
    io8 = lax.broadcasted_iota(I32, (N_GROUPS, TM), 0)
    keep = jnp.zeros((N_GROUPS, TM), F32)
    for _ in range(TOPK_GROUPS):
        m = jnp.max(gs, axis=0, keepdims=True)
        idx = jnp.min(jnp.where(gs == m, io8, N_GROUPS), axis=0, keepdims=True)
        hit = io8 == idx
        keep = jnp.where(hit, 1.0, keep)
        gs = jnp.where(hit, neg, gs)
    sel = jnp.concatenate(
        [jnp.where(keep[g:g + 1] > 0.0, sel[g * GROUP_SIZE:(g + 1) * GROUP_SIZE], neg) for g in range(N_GROUPS)],
        axis=0)
    ioe = lax.broadcasted_iota(I32, (N_EXPERTS, TM), 0)
    picked = jnp.zeros((N_EXPERTS, TM), F32)
    e_rows, w_rows = [], []
    for _ in range(TOP_K):
        m = jnp.max(sel, axis=0, keepdims=True)
        idx = jnp.min(jnp.where(sel == m, ioe, N_EXPERTS), axis=0, keepdims=True)
        hit = ioe == idx
        e_rows.append(idx)
        w_rows.append(jnp.sum(jnp.where(hit, scores, 0.0), axis=0, keepdims=True))
        picked = jnp.where(hit, 1.0, picked)
        sel = jnp.where(hit, neg, sel)
    wk = jnp.concatenate(w_rows, axis=0)
    w_out[...] = wk / jnp.sum(wk, axis=0, keepdims=True) * ROUTED_SCALE
    e_out[...] = jnp.concatenate(e_rows, axis=0)
    before = jnp.dot(picked.astype(BF16), tri_ref[...], preferred_element_type=F32) + carry_ref[...]
    r_rows = [jnp.sum(jnp.where(ioe == e_rows[k], before, 0.0), axis=0, keepdims=True) for k in range(TOP_K)]
    r_out[...] = jnp.concatenate(r_rows, axis=0).astype(I32)
    carry_ref[...] = carry_ref[...] + jnp.sum(picked, axis=1, keepdims=True)
    cnt_out[...] = carry_ref[...]


def _route(lt, br):
    t_all = lt.shape[1]
    n_tiles = t_all // TM
    return pl.pallas_call(
        _route_kernel,
        out_shape=[
            jax.ShapeDtypeStruct((TOP_K, t_all), I32),
            jax.ShapeDtypeStruct((TOP_K, t_all), F32),
            jax.ShapeDtypeStruct((TOP_K, t_all), I32),
            jax.ShapeDtypeStruct((N_EXPERTS, 1), F32),
        ],
        grid=(n_tiles,),
        in_specs=[pl.BlockSpec((N_EXPERTS, TM), lambda i: (0, i)), _const_spec((N_EXPERTS, 1))],
        out_specs=[
            pl.BlockSpec((TOP_K, TM), lambda i: (0, i)),
            pl.BlockSpec((TOP_K, TM), lambda i: (0, i)),
            pl.BlockSpec((TOP_K, TM), lambda i: (0, i)),
            pl.BlockSpec((N_EXPERTS, 1), lambda i: (0, 0)),
        ],
        scratch_shapes=[pltpu.VMEM((N_EXPERTS, 1), F32), pltpu.VMEM((TM, TM), BF16)],
        compiler_params=pltpu.CompilerParams(dimension_semantics=("arbitrary",)),
        name="route",
    )(lt, br)


def _plan_kernel(cnt_ref, e_ref, r_ref, dest_out, blk_e_out, nxt_e_out, nblk_out, cnt_row_out, pstart_row_out,
                 pstart_ref):
    i = pl.program_id(0)

    @pl.when(i == 0)
    def _():
        nb = jnp.floor((cnt_ref[...] + (BM - 1)) * (1.0 / BM))
        a = lax.broadcasted_iota(I32, (N_EXPERTS, N_EXPERTS), 0)
        b = lax.broadcasted_iota(I32, (N_EXPERTS, N_EXPERTS), 1)
        lower = jnp.where(b < a, 1.0, 0.0).astype(BF16)
        nb_l = jnp.broadcast_to(nb, (N_EXPERTS, LANES)).astype(BF16)
        first_blk = jnp.dot(lower, nb_l, preferred_element_type=F32)[:, 0:1]
        end_blk = first_blk + nb
        pstart_ref[...] = first_blk * BM
        cnt_row_out[...] = jnp.sum(jnp.where(a == b, cnt_ref[...], 0.0), axis=0, keepdims=True).astype(I32)
        pstart_row_out[...] = jnp.sum(jnp.where(a == b, first_blk * BM, 0.0), axis=0, keepdims=True).astype(I32)
        n_lanes = blk_e_out.shape[1]
        blk = lax.broadcasted_iota(I32, (N_EXPERTS, n_lanes), 1).astype(F32)
        owner = jnp.sum(jnp.where(end_blk <= blk, 1.0, 0.0), axis=0, keepdims=True)
        blk_e_out[...] = jnp.minimum(owner, N_EXPERTS - 1.0).astype(I32)
        total = jnp.max(end_blk, axis=0, keepdims=True)
        nblk_out[...] = jnp.broadcast_to(total, nblk_out.shape).astype(I32)
        group_end = jnp.min(jnp.where(end_blk > blk, end_blk, 2.0 * n_lanes), axis=0, keepdims=True)
        nxt = jnp.sum(jnp.where(end_blk <= group_end, 1.0, 0.0), axis=0, keepdims=True)
        nxt_e_out[...] = jnp.where(group_end < total, nxt, -1.0).astype(I32)

    ioe = lax.broadcasted_iota(I32, (N_EXPERTS, TM), 0)
    e = e_ref[...]
    rows = [jnp.sum(jnp.where(ioe == e[k:k + 1], pstart_ref[...], 0.0), axis=0, keepdims=True)
            for k in range(TOP_K)]
    dest_out[...] = jnp.concatenate(rows, axis=0).astype(I32) + r_ref[...]


def _plan(cnt, eidx, rank, n_blocks):
    t_all = eidx.shape[1]
    n_lanes = pl.cdiv(n_blocks, LANES) * LANES
    return pl.pallas_call(
        _plan_kernel,
        out_shape=[
            jax.ShapeDtypeStruct((TOP_K, t_all), I32),
            jax.ShapeDtypeStruct((1, n_lanes), I32),
            jax.ShapeDtypeStruct((1, n_lanes), I32),
            jax.ShapeDtypeStruct((1, LANES), I32),
            jax.ShapeDtypeStruct((1, N_EXPERTS), I32),
            jax.ShapeDtypeStruct((1, N_EXPERTS), I32),
        ],
        grid=(t_all // TM,),
        in_specs=[
            _const_spec((N_EXPERTS, 1)),
            pl.BlockSpec((TOP_K, TM), lambda i: (0, i)),
            pl.BlockSpec((TOP_K, TM), lambda i: (0, i)),
        ],
        out_specs=[
            pl.BlockSpec((TOP_K, TM), lambda i: (0, i)),
            pl.BlockSpec((1, n_lanes), lambda i: (0, 0)),
            pl.BlockSpec((1, n_lanes), lambda i: (0, 0)),
            pl.BlockSpec((1, LANES), lambda i: (0, 0)),
            pl.BlockSpec((1, N_EXPERTS), lambda i: (0, 0)),
            pl.BlockSpec((1, N_EXPERTS), lambda i: (0, 0)),
        ],
        scratch_shapes=[pltpu.VMEM((N_EXPERTS, 1), F32)],
        compiler_params=pltpu.CompilerParams(dimension_semantics=("arbitrary",)),
        name="plan",
    )(cnt, eidx, rank)


def _dispatch_kernel(cnt_ref, pst_ref, nblk_ref, dest_ref, h2_ref, xs_hbm, zeros, sem, fill_sem,
                     *, n_steps, n_blocks):
    i = pl.program_id(0)
    per_step = pl.cdiv(N_EXPERTS, n_steps)

    @pl.when(i == 0)
    def _():
        zeros[...] = jnp.zeros_like(zeros)

    def issue(t, carry):
        for k in range(TOP_K):
            pltpu.make_async_copy(_rows(h2_ref, t, 1), _rows(xs_hbm, dest_ref[k, t], 1), sem).start(
                priority=k % DMA_THREADS)
        return carry

    lax.fori_loop(0, TM, issue, 0)

    def fills(do):
        def per_expert(j, carry):
            e = i * per_step + j

            @pl.when(e < N_EXPERTS)
            def _():
                cnt = cnt_ref[e]
                base = pst_ref[e]
                padded = (cnt + (BM - 1)) // BM * BM
                mid = jnp.minimum((cnt + (SUBLANES - 1)) // SUBLANES * SUBLANES, padded)

                def one(r, c):
                    do(pltpu.make_async_copy(_rows(zeros, 0, 1), _rows(xs_hbm, base + r, 1), fill_sem))
                    return c

                lax.fori_loop(cnt, mid, one, 0)

                def eight(q, c):
                    r = base + mid + q * SUBLANES
                    do(pltpu.make_async_copy(_rows(zeros, 0, SUBLANES), _rows(xs_hbm, r, SUBLANES), fill_sem))
                    return c

                lax.fori_loop(0, (padded - mid) // SUBLANES, eight, 0)

            return carry

        lax.fori_loop(0, per_step, per_expert, 0)

        def per_block(j, carry):
            b = nblk_ref[0] + i * per_step + j

            @pl.when(b < n_blocks)
            def _():
                do(pltpu.make_async_copy(zeros, _rows(xs_hbm, b * BM, BM), fill_sem))

            return carry

        lax.fori_loop(0, per_step, per_block, 0)

    fills(lambda c: c.start())
    for k in range(TOP_K):
        pltpu.make_async_copy(h2_ref, _rows(xs_hbm, 0, TM), sem).wait()
    fills(lambda c: c.wait())


def _dispatch(cnt_row, pstart_row, nblk, dest, h2, n_blocks):
    t_all = h2.shape[0] // SUBLANES
    n_steps = t_all // TM
    kern = functools.partial(_dispatch_kernel, n_steps=n_steps, n_blocks=n_blocks)
    grid_spec = pltpu.PrefetchScalarGridSpec(
        num_scalar_prefetch=3,
        grid=(n_steps,),
        in_specs=[
            pl.BlockSpec((TOP_K, TM), lambda i, *_: (0, i), memory_space=pltpu.SMEM),
            pl.BlockSpec((TM * SUBLANES, LANES), lambda i, *_: (i, 0)),
        ],
        out_specs=pl.BlockSpec(memory_space=pl.ANY),
        scratch_shapes=[pltpu.VMEM((BM * SUBLANES, LANES), F32), pltpu.SemaphoreType.DMA(()),
                        pltpu.SemaphoreType.DMA(())],
    )
    return pl.pallas_call(
        kern,
        out_shape=jax.ShapeDtypeStruct((n_blocks * BM * SUBLANES, LANES), F32),
        grid_spec=grid_spec,
        compiler_params=pltpu.CompilerParams(dimension_semantics=("arbitrary",)),
        name="dispatch",
    )(cnt_row, pstart_row, nblk, dest, h2)


def _moe_kernel(blk_e_ref, nxt_e_ref, nblk_ref, x_hbm, wg_hbm, wu_hbm, wd_hbm, y_ref,
                x_ring, wg_l, wu_l, wd_l, wgu_s, wd_s, xsems, sems, cur_ref):
    b = pl.program_id(0)
    nblk = nblk_ref[0]

    def weight_copies(e, slot):
        return (pltpu.make_async_copy(wg_hbm.at[e], wg_l.at[slot], sems.at[slot, 0]),
                pltpu.make_async_copy(wu_hbm.at[e], wu_l.at[slot], sems.at[slot, 1]),
                pltpu.make_async_copy(wd_hbm.at[e], wd_l.at[slot], sems.at[slot, 2]))

    def row_copy(blk):
        slot = blk % X_RING
        return pltpu.make_async_copy(_rows(x_hbm, blk * BM, BM), x_ring.at[slot], xsems.at[slot])

    @pl.when(b < nblk)
    def _():
        e = blk_e_ref[b]

        @pl.when(b == 0)
        def _():
            cur_ref[0] = 0
            for c in weight_copies(e, 0):
                c.start()
            for j in range(X_AHEAD):
                @pl.when(j < nblk)
                def _():
                    row_copy(j).start()

        @pl.when(b + X_AHEAD < nblk)
        def _():
            row_copy(b + X_AHEAD).start()

        @pl.when((b == 0) | (e != blk_e_ref[jnp.maximum(b - 1, 0)]))
        def _():
            @pl.when(b > 0)
            def _():
                cur_ref[0] = 1 - cur_ref[0]

            slot = cur_ref[0]
            for c in weight_copies(e, slot):
                c.wait()
            nxt = nxt_e_ref[b]

            @pl.when(nxt >= 0)
            def _():
                for c in weight_copies(nxt, 1 - slot):
                    c.start()

            wgu_s[:, :D_EXPERT] = wg_l[slot].astype(BF16)
            wgu_s[:, D_EXPERT:] = wu_l[slot].astype(BF16)
            wd_s[...] = wd_l[slot].astype(BF16)

        row_copy(b).wait()
        x = _row_major(x_ring.at[b % X_RING], BM).astype(BF16)
        gu = jnp.dot(x, wgu_s[...], preferred_element_type=F32)
        h = (jax.nn.silu(gu[:, :D_EXPERT]) * gu[:, D_EXPERT:]).astype(BF16)
        y = jnp.dot(h, wd_s[...], preferred_element_type=F32)
        for j in range(SUBLANES):
            y_ref[pl.ds(j, BM, stride=SUBLANES), :] = y[:, j * LANES:(j + 1) * LANES]

    @pl.when(b >= nblk)
    def _():
        y_ref[...] = jnp.zeros_like(y_ref)


def _moe(blk_e, nxt_e, nblk, xs, wg, wu, wd):
    d = D_MODEL
    n_blocks = xs.shape[0] // (BM * SUBLANES)
    grid_spec = pltpu.PrefetchScalarGridSpec(
        num_scalar_prefetch=3,
        grid=(n_blocks,),
        in_specs=[
            pl.BlockSpec(memory_space=pl.ANY),
            pl.BlockSpec(memory_space=pl.ANY),
            pl.BlockSpec(memory_space=pl.ANY),
            pl.BlockSpec(memory_space=pl.ANY),
        ],
        out_specs=pl.BlockSpec((BM * SUBLANES, LANES), lambda b, *_: (b, 0)),
        scratch_shapes=[
            pltpu.VMEM((X_RING, BM * SUBLANES, LANES), F32),
            pltpu.VMEM((2, d, D_EXPERT), F32), pltpu.VMEM((2, d, D_EXPERT), F32), pltpu.VMEM((2, D_EXPERT, d), F32),
            pltpu.VMEM((d, 2 * D_EXPERT), BF16), pltpu.VMEM((D_EXPERT, d), BF16),
            pltpu.SemaphoreType.DMA((X_RING,)), pltpu.SemaphoreType.DMA((2, 3)), pltpu.SMEM((1,), I32),
        ],
    )
    return pl.pallas_call(
        _moe_kernel,
        out_shape=jax.ShapeDtypeStruct(xs.shape, F32),
        grid_spec=grid_spec,
        compiler_params=pltpu.CompilerParams(dimension_semantics=("arbitrary",),
                                             vmem_limit_bytes=VMEM_LIMIT_BYTES),
        name="moe",
    )(blk_e, nxt_e, nblk, xs, wg, wu, wd)


def _final_kernel(dest_ref, dnext_ref, xs_ref, w_ref, gp_ref, gs_ref, fp_ref, fs_ref, gf_ref, y_hbm,
                  op_ref, os_ref, buf, sems, *, n_tiles, n_ptiles, tiles_per_seq):
    i = pl.program_id(0)
    d = D_MODEL
    slot = i % 2

    def gather(d_ref, s):
        def issue(t, carry):
            for k in range(TOP_K):
                pltpu.make_async_copy(_rows(y_hbm, d_ref[k, t], 1), _rows(buf.at[s, k], t, 1),
                                      sems.at[s]).start(priority=k % DMA_THREADS)
            return carry

        lax.fori_loop(0, TC, issue, 0)

    @pl.when(i == 0)
    def _():
        gather(dest_ref, 0)

    @pl.when(i + 1 < n_tiles)
    def _():
        gather(dnext_ref, 1 - slot)

    for k in range(TOP_K):
        pltpu.make_async_copy(_rows(y_hbm, 0, TC), buf.at[slot, k], sems.at[slot]).wait()

    w = w_ref[...]
    acc = w[:, 0:1] * _row_major(buf.at[slot, 0], TC)
    for k in range(1, TOP_K):
        acc = acc + w[:, k:k + 1] * _row_major(buf.at[slot, k], TC)

    def finish(gate2, shift, scale, o_ref):
        x2 = xs_ref[...] + gate2 * acc
        o_ref[...] = _rms(x2, gf_ref[...]) * (1.0 + scale) + shift

    @pl.when(i < n_ptiles)
    def _():
        b = i // tiles_per_seq
        f = fp_ref[pl.ds(b, 1), :]
        finish(gp_ref[pl.ds(b, 1), :], f[:, :d], f[:, d:], op_ref)

    @pl.when(i >= n_ptiles)
    def _():
        f = fs_ref[...]
        finish(gs_ref[...], f[:, :d], f[:, d:], os_ref)


def _final(dest, xs, wts, m_all, mf_all, gf, y_sorted, *, n_batch, seq, t_p):
    t_all, d = xs.shape
    n_tiles = t_all // TC
    n_ptiles = t_p // TC
    dec_batch = m_all.shape[0] - n_batch
    kern = functools.partial(_final_kernel, n_tiles=n_tiles, n_ptiles=n_ptiles, tiles_per_seq=seq // TC)
    pb = dec_batch // n_batch
    return pl.pallas_call(
        kern,
        out_shape=[jax.ShapeDtypeStruct((t_p, d), F32), jax.ShapeDtypeStruct((t_all - t_p, d), F32)],
        grid=(n_tiles,),
        in_specs=[
            pl.BlockSpec((TOP_K, TC), lambda i: (0, i), memory_space=pltpu.SMEM),
            pl.BlockSpec((TOP_K, TC), lambda i: (0, jnp.minimum(i + 1, n_tiles - 1)), memory_space=pltpu.SMEM),
            pl.BlockSpec((TC, d), lambda i: (i, 0)),
            pl.BlockSpec((TC, TOP_K), lambda i: (i, 0)),
            pl.BlockSpec((n_batch, d), lambda i: (pb, N_MOD - 1)),
            pl.BlockSpec((dec_batch, d), lambda i: (0, N_MOD - 1)),
            pl.BlockSpec((n_batch, 2 * d), lambda i: (pb, 0)),
            pl.BlockSpec((dec_batch, 2 * d), lambda i: (0, 0)),
            _const_spec(gf.shape),
            pl.BlockSpec(memory_space=pl.ANY),
        ],
        out_specs=[
            pl.BlockSpec((TC, d), lambda i: (jnp.minimum(i, n_ptiles - 1), 0)),
            pl.BlockSpec((TC, d), lambda i: (jnp.maximum(i - n_ptiles, 0), 0)),
        ],
        scratch_shapes=[pltpu.VMEM((2, TOP_K, TC * SUBLANES, LANES), F32), pltpu.SemaphoreType.DMA((2,))],
        compiler_params=pltpu.CompilerParams(dimension_semantics=("arbitrary",),
                                             vmem_limit_bytes=VMEM_LIMIT_BYTES),
        name="final",
    )(dest, dest, xs, wts, m_all, m_all, mf_all, mf_all, gf, y_sorted)


def kernel(x_prompt, x_sample, state_conv, c_prompt, c_sample, w_ada, b_ada, g_norm1, w_in, w_spatial, b_spatial, g_v, w_conv, g_out_a, g_out_b, w_out, g_norm2, w_router, b_router, w_exp_gate, w_exp_up, w_exp_down, w_sh_gate, w_sh_up, w_sh_down, w_ada_final, b_ada_final, g_final):
    n_batch, seq, d = x_prompt.shape
    dec_batch, dec_seq, _ = x_sample.shape
    assert w_ada.shape[0] == 1, "one layer only"
    assert d == D_MODEL and dec_seq == DEC_SEQ and dec_batch * dec_seq == TM and seq % TM == 0
    assert n_batch % SUBLANES == 0 and dec_batch % n_batch == 0
    t_p = n_batch * seq
    t_s = dec_batch * dec_seq
    t_all = t_p + t_s

    c_all = jnp.concatenate([c_sample, c_prompt], axis=0)
    m_all = _ada(c_all, w_ada[0], b_ada[0])
    mf_all = _ada(c_all, w_ada_final, b_ada_final)

    rep = functools.partial(jnp.repeat, repeats=A_HEAD_DIM, axis=1)
    tril = jnp.tril(jnp.ones((CHUNK, CHUNK), dtype=bool))
    wt = jnp.where(tril, w_spatial[0], 0.0).astype(BF16)
    bsp = rep(b_spatial[0].T)
    csp = rep(jnp.transpose(w_spatial[0][:, :DEC_SEQ, :DEC_SEQ], (1, 2, 0)).reshape(DEC_SEQ * DEC_SEQ, A_HEADS))
    bsps = rep(b_spatial[0][:, :DEC_SEQ].T)
    st = jnp.transpose(state_conv[0], (1, 0, 2))
    wshgu = jnp.concatenate([w_sh_gate[0], w_sh_up[0]], axis=1).astype(BF16)
    xsm = jnp.transpose(x_sample, (1, 0, 2)).reshape(t_s, d)

    xs, h2, lt, cztail, vs, cz23 = _mixer(
        x_prompt.reshape(t_p, d), xsm, m_all, g_norm1, w_in[0].astype(BF16), wt, bsp,
        g_v.reshape(1, MIX_A), w_conv[0], g_out_a, g_out_b, w_out[0].astype(BF16), g_norm2,
        w_router[0].T.astype(BF16), wshgu, w_sh_down[0].astype(BF16), csp, bsps, st,
        n_batch=n_batch, seq=seq)

    eidx, wts, rank, cnt = _route(lt, b_router[0].reshape(N_EXPERTS, 1))

    n_blocks = (t_all * TOP_K) // BM + N_EXPERTS
    dest, blk_e, nxt_e, nblk, cnt_row, pstart_row = _plan(cnt, eidx, rank, n_blocks)

    x_sorted = _dispatch(cnt_row[0], pstart_row[0], nblk[0, :1], dest, h2, n_blocks)
    y_sorted = _moe(blk_e[0, :n_blocks], nxt_e[0, :n_blocks], nblk[0, :1], x_sorted,
                    w_exp_gate[0], w_exp_up[0], w_exp_down[0])
    y_p, y_s = _final(dest, xs, wts.T, m_all, mf_all, g_final.reshape(1, d), y_sorted,
                      n_batch=n_batch, seq=seq, t_p=t_p)

    tiles_per_seq = seq // TM
    y_prompt = y_p.reshape(n_batch, seq, d)
    y_sample = jnp.transpose(y_s.reshape(dec_seq, dec_batch, d), (1, 0, 2))
    conv_p = cztail[tiles_per_seq - 1:n_batch * tiles_per_seq:tiles_per_seq, SUBLANES - (CONV_W - 1):, :][None]
    conv_s = jnp.transpose(cz23, (1, 0, 2))[None]
    v_s = jnp.transpose(vs.reshape(dec_seq, dec_batch, A_HEADS, A_HEAD_DIM), (1, 0, 2, 3))[None]
    return (y_prompt, y_sample, conv_p, conv_s, v_s)
```

```python
import functools

import jax
import jax.numpy as jnp
from jax import lax
from jax.experimental import pallas as pl
from jax.experimental.pallas import tpu as pltpu

F32 = jnp.float32
BF16 = jnp.bfloat16
I32 = jnp.int32

D_MODEL = 1024
MIX_A = 512
A_HEADS = 4
A_HEAD_DIM = 128
CHUNK = 128
CONV_DIM = 512
CONV_W = 3
PROJ_DIM = 2 * MIX_A + 3 * CONV_DIM
N_EXPERTS = 256
TOP_K = 8
N_GROUPS = 8
TOPK_GROUPS = 4
GROUP_SIZE = N_EXPERTS // N_GROUPS
D_EXPERT = 256
D_SHARED = 256
ROUTED_SCALE = 2.5
N_MOD = 6
RMS_EPS = 1e-6
DEC_SEQ = 4

LANES = 128
SUBLANES = 8
TM = 512
BM = 256
TC = 128
FINAL_GROUP = SUBLANES
DMA_THREADS = 2
X_AHEAD = 2
X_RING = X_AHEAD + 1
VMEM_LIMIT_BYTES = 56 * 1024 * 1024


def _rms(x, g):
    return x * lax.rsqrt(jnp.mean(x * x, axis=-1, keepdims=True) + RMS_EPS) * g


def _rows(ref, row, n):
    return ref.at[pl.ds(pl.multiple_of(row * SUBLANES, SUBLANES), n * SUBLANES)]


def _row_major(ref, n):
    return jnp.concatenate([ref[pl.ds(j, n, stride=SUBLANES), :] for j in range(SUBLANES)], axis=1)


def _const_spec(shape):
    nd = len(shape)
    return pl.BlockSpec(shape, lambda *_: (0,) * nd, pipeline_mode=pl.Buffered(1))


def _ada_kernel(c_ref, w_ref, b_ref, o_ref):
    a = jax.nn.silu(c_ref[...]).astype(BF16)
    o_ref[...] = jnp.dot(a, w_ref[...].astype(BF16), preferred_element_type=F32) + b_ref[...]


def _ada(c_all, w, b):
    rows, n = c_all.shape[0], w.shape[1]
    return pl.pallas_call(
        _ada_kernel,
        out_shape=jax.ShapeDtypeStruct((rows, n), F32),
        grid=(n // D_MODEL,),
        in_specs=[
            pl.BlockSpec((rows, D_MODEL), lambda j: (0, 0)),
            pl.BlockSpec((D_MODEL, D_MODEL), lambda j: (0, j)),
            pl.BlockSpec((1, D_MODEL), lambda j: (0, j)),
        ],
        out_specs=pl.BlockSpec((rows, D_MODEL), lambda j: (0, j)),
        compiler_params=pltpu.CompilerParams(dimension_semantics=("arbitrary",)),
        name="ada",
    )(c_all, w, b.reshape(1, n))


def _mixer_kernel(xp_ref, xsm_ref, mp_ref, ms_ref, g1_ref, win_ref, wt_ref, bsp_ref, gv_ref, wconv_ref,
                  goa_ref, gob_ref, wout_ref, g2_ref, wrt_ref, wshgu_ref, wshd_ref, csp_ref, bsps_ref, st_ref,
                  xs_out, h2_out, lt_out, cztail_out, vs_out, cz23_out, carry_ref, *, n_ptiles, tiles_per_seq):
    i = pl.program_id(0)
    d = D_MODEL

    def proj(x, sh1, sc1):
        h = _rms(x, g1_ref[...]) * (1.0 + sc1) + sh1
        p = jnp.dot(h.astype(BF16), win_ref[...], preferred_element_type=F32)
        u = jax.nn.gelu(p[:, :MIX_A])
        v = jax.nn.gelu(p[:, MIX_A:2 * MIX_A])
        vn = jnp.concatenate(
            [_rms(v[:, h * LANES:(h + 1) * LANES], gv_ref[:, h * LANES:(h + 1) * LANES]) for h in range(A_HEADS)],
            axis=1)
        o = 2 * MIX_A
        return u, vn, p[:, o:o + CONV_DIM], p[:, o + CONV_DIM:o + 2 * CONV_DIM], p[:, o + 2 * CONV_DIM:]

    def tail(x, ya, yb, gate1, sh2, sc2, gate2):
        cat = jnp.concatenate([_rms(ya, goa_ref[...]), _rms(yb, gob_ref[...])], axis=1).astype(BF16)
        x1 = x + gate1 * jnp.dot(cat, wout_ref[...], preferred_element_type=F32)
        h2 = _rms(x1, g2_ref[...]) * (1.0 + sc2) + sh2
        h2b = h2.astype(BF16)
        lt_out[...] = lax.dot_general(wrt_ref[...], h2b, (((1,), (1,)), ((), ())), preferred_element_type=F32)
        gu = jnp.dot(h2b, wshgu_ref[...], preferred_element_type=F32)
        hs = (jax.nn.silu(gu[:, :D_SHARED]) * gu[:, D_SHARED:]).astype(BF16)
        xs_out[...] = x1 + gate2 * jnp.dot(hs, wshd_ref[...], preferred_element_type=F32)
        for j in range(SUBLANES):
            h2_out[pl.ds(j, TM, stride=SUBLANES), :] = h2[:, j * LANES:(j + 1) * LANES]

    @pl.when(i < n_ptiles)
    def _prompt():
        b = i // tiles_per_seq
        m = mp_ref[pl.ds(b, 1), :]
        mod = [m[:, k * d:(k + 1) * d] for k in range(N_MOD)]
        x = xp_ref[...]
        u, vn, bg, cg, z = proj(x, mod[0], mod[1])
        n_chunks = TM // CHUNK
        ya_cols = []
        for h in range(A_HEADS):
            vh = vn[:, h * LANES:(h + 1) * LANES].astype(BF16)
            rhs = jnp.concatenate([vh[c * CHUNK:(c + 1) * CHUNK] for c in range(n_chunks)], axis=1)
            mix = jnp.dot(wt_ref[h], rhs, preferred_element_type=F32)
            bias = bsp_ref[:, h * LANES:(h + 1) * LANES]
            mix = jnp.concatenate([mix[:, c * LANES:(c + 1) * LANES] + bias for c in range(n_chunks)], axis=0)
            ya_cols.append(u[:, h * LANES:(h + 1) * LANES] * mix)
        ya = jnp.concatenate(ya_cols, axis=1)
        cz = cg * z

        @pl.when(i % tiles_per_seq == 0)
        def _():
            carry_ref[...] = jnp.zeros_like(carry_ref)

        c6 = carry_ref[SUBLANES - 2:SUBLANES - 1, :]
        c7 = carry_ref[SUBLANES - 1:SUBLANES, :]
        r = lax.broadcasted_iota(I32, (TM, 1), 0)
        p1 = jnp.where(r == 0, c7, pltpu.roll(cz, 1, 0))
        p2 = jnp.where(r == 0, c6, jnp.where(r == 1, c7, pltpu.roll(cz, 2, 0)))
        wc = wconv_ref[...]
        yb = bg * (p2 * wc[0:1] + p1 * wc[1:2] + cz * wc[2:3])
        carry_ref[...] = cz[TM - SUBLANES:]
        cztail_out[0] = cz[TM - SUBLANES:]
        tail(x, ya, yb, mod[2], mod[3], mod[4], mod[5])

    @pl.when(i == n_ptiles)
    def _sample():
        ms = ms_ref[...]
        nb = TM // DEC_SEQ

        def mod(k):
            return jnp.concatenate([ms[:, k * d:(k + 1) * d]] * DEC_SEQ, axis=0)

        x = xsm_ref[...]
        u, vn, bg, cg, z = proj(x, mod(0), mod(1))
        vt = [vn[t * nb:(t + 1) * nb] for t in range(DEC_SEQ)]
        mixes = []
        for t in range(DEC_SEQ):
            acc = csp_ref[DEC_SEQ * t:DEC_SEQ * t + 1, :] * vt[0]
            for s in range(1, t + 1):
                acc = acc + csp_ref[DEC_SEQ * t + s:DEC_SEQ * t + s + 1, :] * vt[s]
            mixes.append(acc + bsps_ref[t:t + 1, :])
        ya = u * jnp.concatenate(mixes, axis=0)
        cz = cg * z
        czt = [cz[t * nb:(t + 1) * nb] for t in range(DEC_SEQ)]
        full = [st_ref[0], st_ref[1]] + czt
        wc = wconv_ref[...]
        yc = jnp.concatenate(
            [full[t] * wc[0:1] + full[t + 1] * wc[1:2] + full[t + 2] * wc[2:3] for t in range(DEC_SEQ)], axis=0)
        yb = bg * yc
        vs_out[...] = vn
        cz23_out[0] = czt[DEC_SEQ - 2]
        cz23_out[1] = czt[DEC_SEQ - 1]
        cztail_out[0] = cz[TM - SUBLANES:]
        tail(x, ya, yb, mod(2), mod(3), mod(4), mod(5))


def _mixer(xp, xsm, m_all, g1, win, wt, bsp, gv, wconv, goa, gob, wout, g2, wrt, wshgu, wshd, csp, bsps, st,
           *, n_batch, seq):
    t_p = xp.shape[0]
    n_ptiles = t_p // TM
    n_tiles = n_ptiles + 1
    t_all = n_tiles * TM
    dec_batch = xsm.shape[0] // DEC_SEQ
    d = D_MODEL
    kern = functools.partial(_mixer_kernel, n_ptiles=n_ptiles, tiles_per_seq=seq // TM)
    in_specs = [
        pl.BlockSpec((TM, d), lambda i: (jnp.minimum(i, n_ptiles - 1), 0)),
        _const_spec((TM, d)),
        pl.BlockSpec((n_batch, N_MOD * d), lambda i: (dec_batch // n_batch, 0), pipeline_mode=pl.Buffered(1)),
        pl.BlockSpec((dec_batch, N_MOD * d), lambda i: (0, 0), pipeline_mode=pl.Buffered(1)),
        _const_spec(g1.shape), _const_spec(win.shape), _const_spec(wt.shape), _const_spec(bsp.shape),
        _const_spec(gv.shape), _const_spec(wconv.shape), _const_spec(goa.shape), _const_spec(gob.shape),
        _const_spec(wout.shape), _const_spec(g2.shape), _const_spec(wrt.shape), _const_spec(wshgu.shape),
        _const_spec(wshd.shape), _const_spec(csp.shape), _const_spec(bsps.shape), _const_spec(st.shape),
    ]
    out_shape = [
        jax.ShapeDtypeStruct((t_all, d), F32),
        jax.ShapeDtypeStruct((t_all * SUBLANES, LANES), F32),
        jax.ShapeDtypeStruct((N_EXPERTS, t_all), F32),
        jax.ShapeDtypeStruct((n_tiles, SUBLANES, CONV_DIM), F32),
        jax.ShapeDtypeStruct((TM, MIX_A), F32),
        jax.ShapeDtypeStruct((2, dec_batch, CONV_DIM), F32),
    ]
    out_specs = [
        pl.BlockSpec((TM, d), lambda i: (i, 0)),
        pl.BlockSpec((TM * SUBLANES, LANES), lambda i: (i, 0)),
        pl.BlockSpec((N_EXPERTS, TM), lambda i: (0, i)),
        pl.BlockSpec((1, SUBLANES, CONV_DIM), lambda i: (i, 0, 0)),
        pl.BlockSpec((TM, MIX_A), lambda i: (0, 0)),
        pl.BlockSpec((2, dec_batch, CONV_DIM), lambda i: (0, 0, 0)),
    ]
    return pl.pallas_call(
        kern,
        out_shape=out_shape,
        grid=(n_tiles,),
        in_specs=in_specs,
        out_specs=out_specs,
        scratch_shapes=[pltpu.VMEM((SUBLANES, CONV_DIM), F32)],
        compiler_params=pltpu.CompilerParams(dimension_semantics=("arbitrary",),
                                             vmem_limit_bytes=VMEM_LIMIT_BYTES),
        name="mixer",
    )(xp, xsm, m_all, m_all, g1, win, wt, bsp, gv, wconv, goa, gob, wout, g2, wrt, wshgu, wshd, csp, bsps, st)


def _route_kernel(lt_ref, br_ref, e_out, w_out, r_out, cnt_out, carry_ref, tri_ref):
    i = pl.program_id(0)
    neg = -jnp.inf

    @pl.when(i == 0)
    def _():
        carry_ref[...] = jnp.zeros_like(carry_ref)
        a = lax.broadcasted_iota(I32, (TM, TM), 0)
        b = lax.broadcasted_iota(I32, (TM, TM), 1)
        tri_ref[...] = jnp.where(a < b, 1.0, 0.0).astype(BF16)

    scores = jax.nn.sigmoid(lt_ref[...])
    sel = scores + br_ref[...]
    iog = lax.broadcasted_iota(I32, (GROUP_SIZE, TM), 0)
    gs_rows = []
    for g in range(N_GROUPS):
        sg = sel[g * GROUP_SIZE:(g + 1) * GROUP_SIZE]
        m1 = jnp.max(sg, axis=0, keepdims=True)
        i1 = jnp.min(jnp.where(sg == m1, iog, GROUP_SIZE), axis=0, keepdims=True)
        m2 = jnp.max(jnp.where(iog == i1, neg, sg), axis=0, keepdims=True)
        gs_rows.append(m1 + m2)
    gs = jnp.concatenate(gs_rows, axis=0)
    io8 = lax.broadcasted_iota(I32, (N_GROUPS, TM), 0)
    keep = jnp.zeros((N_GROUPS, TM), F32)
    for _ in range(TOPK_GROUPS):
        m = jnp.max(gs, axis=0, keepdims=True)
        idx = jnp.min(jnp.where(gs == m, io8, N_GROUPS), axis=0, keepdims=True)
        hit = io8 == idx
        keep = jnp.where(hit, 1.0, keep)
        gs = jnp.where(hit, neg, gs)
    sel = jnp.concatenate(
        [jnp.where(keep[g:g + 1] > 0.0, sel[g * GROUP_SIZE:(g + 1) * GROUP_SIZE], neg) for g in range(N_GROUPS)],
        axis=0)
    ioe = lax.broadcasted_iota(I32, (N_EXPERTS, TM), 0)
    picked = jnp.zeros((N_EXPERTS, TM), F32)
    e_rows, w_rows = [], []
    for _ in range(TOP_K):
        m = jnp.max(sel, axis=0, keepdims=True)
        idx = jnp.min(jnp.where(sel == m, ioe, N_EXPERTS), axis=0, keepdims=True)
        hit = ioe == idx
        e_rows.append(idx)
        w_rows.append(jnp.sum(jnp.where(hit, scores, 0.0), axis=0, keepdims=True))
        picked = jnp.where(hit, 1.0, picked)
        sel = jnp.where(hit, neg, sel)
    wk = jnp.concatenate(w_rows, axis=0)
    w_out[...] = wk / jnp.sum(wk, axis=0, keepdims=True) * ROUTED_SCALE
    e_out[...] = jnp.concatenate(e_rows, axis=0)
    before = jnp.dot(picked.astype(BF16), tri_ref[...], preferred_element_type=F32) + carry_ref[...]
    r_rows = [jnp.sum(jnp.where(ioe == e_rows[k], before, 0.0), axis=0, keepdims=True) for k in range(TOP_K)]
    r_out[...] = jnp.concatenate(r_rows, axis=0).astype(I32)
    carry_ref[...] = carry_ref[...] + jnp.sum(picked, axis=1, keepdims=True)
    cnt_out[...] = carry_ref[...]


def _route(lt, br):
    t_all = lt.shape[1]
    n_tiles = t_all // TM
    return pl.pallas_call(
        _route_kernel,
        out_shape=[
            jax.ShapeDtypeStruct((TOP_K, t_all), I32),
            jax.ShapeDtypeStruct((TOP_K, t_all), F32),
            jax.ShapeDtypeStruct((TOP_K, t_all), I32),
            jax.ShapeDtypeStruct((N_EXPERTS, 1), F32),
        ],
        grid=(n_tiles,),
        in_specs=[pl.BlockSpec((N_EXPERTS, TM), lambda i: (0, i)), _const_spec((N_EXPERTS, 1))],
        out_specs=[
            pl.BlockSpec((TOP_K, TM), lambda i: (0, i)),
            pl.BlockSpec((TOP_K, TM), lambda i: (0, i)),
            pl.BlockSpec((TOP_K, TM), lambda i: (0, i)),
            pl.BlockSpec((N_EXPERTS, 1), lambda i: (0, 0)),
        ],
        scratch_shapes=[pltpu.VMEM((N_EXPERTS, 1), F32), pltpu.VMEM((TM, TM), BF16)],
        compiler_params=pltpu.CompilerParams(dimension_semantics=("arbitrary",)),
        name="route",
    )(lt, br)


def _plan_kernel(cnt_ref, e_ref, r_ref, dest_out, blk_e_out, nxt_e_out, nblk_out, cnt_row_out, pstart_row_out,
                 pstart_ref):
    i = pl.program_id(0)

    @pl.when(i == 0)
    def _():
        nb = jnp.floor((cnt_ref[...] + (BM - 1)) * (1.0 / BM))
        a = lax.broadcasted_iota(I32, (N_EXPERTS, N_EXPERTS), 0)
        b = lax.broadcasted_iota(I32, (N_EXPERTS, N_EXPERTS), 1)
        lower = jnp.where(b < a, 1.0, 0.0).astype(BF16)
        nb_l = jnp.broadcast_to(nb, (N_EXPERTS, LANES)).astype(BF16)
        first_blk = jnp.dot(lower, nb_l, preferred_element_type=F32)[:, 0:1]
        end_blk = first_blk + nb
        pstart_ref[...] = first_blk * BM
        cnt_row_out[...] = jnp.sum(jnp.where(a == b, cnt_ref[...], 0.0), axis=0, keepdims=True).astype(I32)
        pstart_row_out[...] = jnp.sum(jnp.where(a == b, first_blk * BM, 0.0), axis=0, keepdims=True).astype(I32)
        n_lanes = blk_e_out.shape[1]
        blk = lax.broadcasted_iota(I32, (N_EXPERTS, n_lanes), 1).astype(F32)
        owner = jnp.sum(jnp.where(end_blk <= blk, 1.0, 0.0), axis=0, keepdims=True)
        blk_e_out[...] = jnp.minimum(owner, N_EXPERTS - 1.0).astype(I32)
        total = jnp.max(end_blk, axis=0, keepdims=True)
        nblk_out[...] = jnp.broadcast_to(total, nblk_out.shape).astype(I32)
        group_end = jnp.min(jnp.where(end_blk > blk, end_blk, 2.0 * n_lanes), axis=0, keepdims=True)
        nxt = jnp.sum(jnp.where(end_blk <= group_end, 1.0, 0.0), axis=0, keepdims=True)
        nxt_e_out[...] = jnp.where(group_end < total, nxt, -1.0).astype(I32)

    ioe = lax.broadcasted_iota(I32, (N_EXPERTS, TM), 0)
    e = e_ref[...]
    rows = [jnp.sum(jnp.where(ioe == e[k:k + 1], pstart_ref[...], 0.0), axis=0, keepdims=True)
            for k in range(TOP_K)]
    dest_out[...] = jnp.concatenate(rows, axis=0).astype(I32) + r_ref[...]


def _plan(cnt, eidx, rank, n_blocks):
    t_all = eidx.shape[1]
    n_lanes = pl.cdiv(n_blocks, LANES) * LANES
    return pl.pallas_call(
        _plan_kernel,
        out_shape=[
            jax.ShapeDtypeStruct((TOP_K, t_all), I32),
            jax.ShapeDtypeStruct((1, n_lanes), I32),
            jax.ShapeDtypeStruct((1, n_lanes), I32),
            jax.ShapeDtypeStruct((1, LANES), I32),
            jax.ShapeDtypeStruct((1, N_EXPERTS), I32),
            jax.ShapeDtypeStruct((1, N_EXPERTS), I32),
        ],
        grid=(t_all // TM,),
        in_specs=[
            _const_spec((N_EXPERTS, 1)),
            pl.BlockSpec((TOP_K, TM), lambda i: (0, i)),
            pl.BlockSpec((TOP_K, TM), lambda i: (0, i)),
        ],
        out_specs=[
            pl.BlockSpec((TOP_K, TM), lambda i: (0, i)),
            pl.BlockSpec((1, n_lanes), lambda i: (0, 0)),
            pl.BlockSpec((1, n_lanes), lambda i: (0, 0)),
            pl.BlockSpec((1, LANES), lambda i: (0, 0)),
            pl.BlockSpec((1, N_EXPERTS), lambda i: (0, 0)),
            pl.BlockSpec((1, N_EXPERTS), lambda i: (0, 0)),
        ],
        scratch_shapes=[pltpu.VMEM((N_EXPERTS, 1), F32)],
        compiler_params=pltpu.CompilerParams(dimension_semantics=("arbitrary",)),
        name="plan",
    )(cnt, eidx, rank)


def _dispatch_kernel(cnt_ref, pst_ref, nblk_ref, dest_ref, h2_ref, xs_hbm, zeros, sem, fill_sem,
                     *, n_steps, n_blocks):
    i = pl.program_id(0)
    per_step = pl.cdiv(N_EXPERTS, n_steps)

    @pl.when(i == 0)
    def _():
        zeros[...] = jnp.zeros_like(zeros)

    def issue(t, carry):
        for k in range(TOP_K):
            pltpu.make_async_copy(_rows(h2_ref, t, 1), _rows(xs_hbm, dest_ref[k, t], 1), sem).start(
                priority=k % DMA_THREADS)
        return carry

    lax.fori_loop(0, TM, issue, 0)

    def fills(do):
        def per_expert(j, carry):
            e = i * per_step + j

            @pl.when(e < N_EXPERTS)
            def _():
                cnt = cnt_ref[e]
                base = pst_ref[e]
                padded = (cnt + (BM - 1)) // BM * BM
                mid = jnp.minimum((cnt + (SUBLANES - 1)) // SUBLANES * SUBLANES, padded)

                def one(r, c):
                    do(pltpu.make_async_copy(_rows(zeros, 0, 1), _rows(xs_hbm, base + r, 1), fill_sem))
                    return c

                lax.fori_loop(cnt, mid, one, 0)

                def eight(q, c):
                    r = base + mid + q * SUBLANES
                    do(pltpu.make_async_copy(_rows(zeros, 0, SUBLANES), _rows(xs_hbm, r, SUBLANES), fill_sem))
                    return c

                lax.fori_loop(0, (padded - mid) // SUBLANES, eight, 0)

            return carry

        lax.fori_loop(0, per_step, per_expert, 0)

        def per_block(j, carry):
            b = nblk_ref[0] + i * per_step + j

            @pl.when(b < n_blocks)
            def _():
                do(pltpu.make_async_copy(zeros, _rows(xs_hbm, b * BM, BM), fill_sem))

            return carry

        lax.fori_loop(0, per_step, per_block, 0)

    fills(lambda c: c.start())
    for k in range(TOP_K):
        pltpu.make_async_copy(h2_ref, _rows(xs_hbm, 0, TM), sem).wait()
    fills(lambda c: c.wait())


def _dispatch(cnt_row, pstart_row, nblk, dest, h2, n_blocks):
    t_all = h2.shape[0] // SUBLANES
    n_steps = t_all // TM
    kern = functools.partial(_dispatch_kernel, n_steps=n_steps, n_blocks=n_blocks)
    grid_spec = pltpu.PrefetchScalarGridSpec(
        num_scalar_prefetch=3,
        grid=(n_steps,),
        in_specs=[
            pl.BlockSpec((TOP_K, TM), lambda i, *_: (0, i), memory_space=pltpu.SMEM),
            pl.BlockSpec((TM * SUBLANES, LANES), lambda i, *_: (i, 0)),
        ],
        out_specs=pl.BlockSpec(memory_space=pl.ANY),
        scratch_shapes=[pltpu.VMEM((BM * SUBLANES, LANES), F32), pltpu.SemaphoreType.DMA(()),
                        pltpu.SemaphoreType.DMA(())],
    )
    return pl.pallas_call(
        kern,
        out_shape=jax.ShapeDtypeStruct((n_blocks * BM * SUBLANES, LANES), F32),
        grid_spec=grid_spec,
        compiler_params=pltpu.CompilerParams(dimension_semantics=("arbitrary",)),
        name="dispatch",
    )(cnt_row, pstart_row, nblk, dest, h2)


def _moe_kernel(blk_e_ref, nxt_e_ref, nblk_ref, x_hbm, wg_hbm, wu_hbm, wd_hbm, y_ref,
                x_ring, wg_l, wu_l, wd_l, wgu_s, wd_s, xsems, sems, cur_ref):
    b = pl.program_id(0)
    nblk = nblk_ref[0]

    def weight_copies(e, slot):
        return (pltpu.make_async_copy(wg_hbm.at[e], wg_l.at[slot], sems.at[slot, 0]),
                pltpu.make_async_copy(wu_hbm.at[e], wu_l.at[slot], sems.at[slot, 1]),
                pltpu.make_async_copy(wd_hbm.at[e], wd_l.at[slot], sems.at[slot, 2]))

    def row_copy(blk):
        slot = blk % X_RING
        return pltpu.make_async_copy(_rows(x_hbm, blk * BM, BM), x_ring.at[slot], xsems.at[slot])

    @pl.when(b < nblk)
    def _():
        e = blk_e_ref[b]

        @pl.when(b == 0)
        def _():
            cur_ref[0] = 0
            for c in weight_copies(e, 0):
                c.start()
            for j in range(X_AHEAD):
                @pl.when(j < nblk)
                def _():
                    row_copy(j).start()

        @pl.when(b + X_AHEAD < nblk)
        def _():
            row_copy(b + X_AHEAD).start()

        @pl.when((b == 0) | (e != blk_e_ref[jnp.maximum(b - 1, 0)]))
        def _():
            @pl.when(b > 0)
            def _():
                cur_ref[0] = 1 - cur_ref[0]

            slot = cur_ref[0]
            for c in weight_copies(e, slot):
                c.wait()
            nxt = nxt_e_ref[b]

            @pl.when(nxt >= 0)
            def _():
                for c in weight_copies(nxt, 1 - slot):
                    c.start()

            wgu_s[:, :D_EXPERT] = wg_l[slot].astype(BF16)
            wgu_s[:, D_EXPERT:] = wu_l[slot].astype(BF16)
            wd_s[...] = wd_l[slot].astype(BF16)

        row_copy(b).wait()
        x = _row_major(x_ring.at[b % X_RING], BM).astype(BF16)
        gu = jnp.dot(x, wgu_s[...], preferred_element_type=F32)
        h = (jax.nn.silu(gu[:, :D_EXPERT]) * gu[:, D_EXPERT:]).astype(BF16)
        y = jnp.dot(h, wd_s[...], preferred_element_type=F32)
        for j in range(SUBLANES):
            y_ref[pl.ds(j, BM, stride=SUBLANES), :] = y[:, j * LANES:(j + 1) * LANES]

    @pl.when(b >= nblk)
    def _():
        y_ref[...] = jnp.zeros_like(y_ref)


def _moe(blk_e, nxt_e, nblk, xs, wg, wu, wd):
    d = D_MODEL
    n_blocks = xs.shape[0] // (BM * SUBLANES)
    grid_spec = pltpu.PrefetchScalarGridSpec(
        num_scalar_prefetch=3,
        grid=(n_blocks,),
        in_specs=[
            pl.BlockSpec(memory_space=pl.ANY),
            pl.BlockSpec(memory_space=pl.ANY),
            pl.BlockSpec(memory_space=pl.ANY),
            pl.BlockSpec(memory_space=pl.ANY),
        ],
        out_specs=pl.BlockSpec((BM * SUBLANES, LANES), lambda b, *_: (b, 0)),
        scratch_shapes=[
            pltpu.VMEM((X_RING, BM * SUBLANES, LANES), F32),
            pltpu.VMEM((2, d, D_EXPERT), F32), pltpu.VMEM((2, d, D_EXPERT), F32), pltpu.VMEM((2, D_EXPERT, d), F32),
            pltpu.VMEM((d, 2 * D_EXPERT), BF16), pltpu.VMEM((D_EXPERT, d), BF16),
            pltpu.SemaphoreType.DMA((X_RING,)), pltpu.SemaphoreType.DMA((2, 3)), pltpu.SMEM((1,), I32),
        ],
    )
    return pl.pallas_call(
        _moe_kernel,
        out_shape=jax.ShapeDtypeStruct(xs.shape, F32),
        grid_spec=grid_spec,
        compiler_params=pltpu.CompilerParams(dimension_semantics=("arbitrary",),
                                             vmem_limit_bytes=VMEM_LIMIT_BYTES),
        name="moe",
    )(blk_e, nxt_e, nblk, xs, wg, wu, wd)


def _final_kernel(dest_ref, dnext_ref, xs_ref, w_ref, gp_ref, gs_ref, fp_ref, fs_ref, gf_ref, y_hbm,
                  op_ref, os_ref, buf, sems, *, n_tiles, n_ptiles, tiles_per_seq):
    i = pl.program_id(0)
    d = D_MODEL
    slot = i % 2
    is_sample = i >= n_ptiles
    b = jnp.minimum(i // tiles_per_seq, gp_ref.shape[0] - 1)

    def issue(d_ref, s, t):
        for k in range(TOP_K):
            pltpu.make_async_copy(_rows(y_hbm, d_ref[k, t], 1), _rows(buf.at[s, k], t, 1),
                                  sems.at[s]).start(priority=k % DMA_THREADS)

    def drain(s):
        for k in range(TOP_K):
            pltpu.make_async_copy(_rows(y_hbm, 0, TC), buf.at[s, k], sems.at[s]).wait()

    @pl.when(i == 0)
    def _():
        def first(t, carry):
            issue(dest_ref, 0, t)
            return carry

        lax.fori_loop(0, TC, first, 0)

    drain(slot)
    gate_p = gp_ref[pl.ds(b, 1), :]
    f_p = fp_ref[pl.ds(b, 1), :]
    gf = gf_ref[...]

    def group(g, carry):
        r0 = pl.multiple_of(g * FINAL_GROUP, FINAL_GROUP)
        rows = pl.ds(r0, FINAL_GROUP)
        w = w_ref[rows, :]
        acc = None
        for k in range(TOP_K):
            plane = buf.at[slot, k]
            yk = jnp.concatenate(
                [plane[pl.ds(r0 * SUBLANES + j, FINAL_GROUP, stride=SUBLANES), :] for j in range(SUBLANES)], axis=1)
            term = w[:, k:k + 1] * yk
            acc = term if acc is None else acc + term
        gate2 = jnp.where(is_sample, gs_ref[rows, :], gate_p)
        f = jnp.where(is_sample, fs_ref[rows, :], f_p)
        x2 = xs_ref[rows, :] + gate2 * acc
        out = _rms(x2, gf) * (1.0 + f[:, d:]) + f[:, :d]
        for u in range(FINAL_GROUP):
            issue(dnext_ref, 1 - slot, r0 + u)

        @pl.when(is_sample)
        def _():
            os_ref[rows, :] = out

        @pl.when(jnp.logical_not(is_sample))
        def _():
            op_ref[rows, :] = out

        return carry

    lax.fori_loop(0, TC // FINAL_GROUP, group, 0)

    @pl.when(i == n_tiles - 1)
    def _():
        drain(1 - slot)


def _final(dest, xs, wts, m_all, mf_all, gf, y_sorted, *, n_batch, seq, t_p):
    t_all, d = xs.shape
    n_tiles = t_all // TC
    n_ptiles = t_p // TC
    dec_batch = m_all.shape[0] - n_batch
    kern = functools.partial(_final_kernel, n_tiles=n_tiles, n_ptiles=n_ptiles, tiles_per_seq=seq // TC)
    pb = dec_batch // n_batch
    return pl.pallas_call(
        kern,
        out_shape=[jax.ShapeDtypeStruct((t_p, d), F32), jax.ShapeDtypeStruct((t_all - t_p, d), F32)],
        grid=(n_tiles,),
        in_specs=[
            pl.BlockSpec((TOP_K, TC), lambda i: (0, i), memory_space=pltpu.SMEM),
            pl.BlockSpec((TOP_K, TC), lambda i: (0, jnp.minimum(i + 1, n_tiles - 1)), memory_space=pltpu.SMEM),
            pl.BlockSpec((TC, d), lambda i: (i, 0)),
            pl.BlockSpec((TC, TOP_K), lambda i: (i, 0)),
            pl.BlockSpec((n_batch, d), lambda i: (pb, N_MOD - 1)),
            pl.BlockSpec((dec_batch, d), lambda i: (0, N_MOD - 1)),
            pl.BlockSpec((n_batch, 2 * d), lambda i: (pb, 0)),
            pl.BlockSpec((dec_batch, 2 * d), lambda i: (0, 0)),
            _const_spec(gf.shape),
            pl.BlockSpec(memory_space=pl.ANY),
        ],
        out_specs=[
            pl.BlockSpec((TC, d), lambda i: (jnp.minimum(i, n_ptiles - 1), 0)),
            pl.BlockSpec((TC, d), lambda i: (jnp.maximum(i - n_ptiles, 0), 0)),
        ],
        scratch_shapes=[pltpu.VMEM((2, TOP_K, TC * SUBLANES, LANES), F32), pltpu.SemaphoreType.DMA((2,))],
        compiler_params=pltpu.CompilerParams(dimension_semantics=("arbitrary",),
                                             vmem_limit_bytes=VMEM_LIMIT_BYTES),
        name="final",
    )(dest, dest, xs, wts, m_all, m_all, mf_all, mf_all, gf, y_sorted)


def kernel(x_prompt, x_sample, state_conv, c_prompt, c_sample, w_ada, b_ada, g_norm1, w_in, w_spatial, b_spatial, g_v, w_conv, g_out_a, g_out_b, w_out, g_norm2, w_router, b_router, w_exp_gate, w_exp_up, w_exp_down, w_sh_gate, w_sh_up, w_sh_down, w_ada_final, b_ada_final, g_final):
    n_batch, seq, d = x_prompt.shape
    dec_batch, dec_seq, _ = x_sample.shape
    assert w_ada.shape[0] == 1, "one layer only"
    assert d == D_MODEL and dec_seq == DEC_SEQ and dec_batch * dec_seq == TM and seq % TM == 0
    assert n_batch % SUBLANES == 0 and dec_batch % n_batch == 0
    t_p = n_batch * seq
    t_s = dec_batch * dec_seq
    t_all = t_p + t_s

    c_all = jnp.concatenate([c_sample, c_prompt], axis=0)
    m_all = _ada(c_all, w_ada[0], b_ada[0])
    mf_all = _ada(c_all, w_ada_final, b_ada_final)

    rep = functools.partial(jnp.repeat, repeats=A_HEAD_DIM, axis=1)
    tril = jnp.tril(jnp.ones((CHUNK, CHUNK), dtype=bool))
    wt = jnp.where(tril, w_spatial[0], 0.0).astype(BF16)
    bsp = rep(b_spatial[0].T)
    csp = rep(jnp.transpose(w_spatial[0][:, :DEC_SEQ, :DEC_SEQ], (1, 2, 0)).reshape(DEC_SEQ * DEC_SEQ, A_HEADS))
    bsps = rep(b_spatial[0][:, :DEC_SEQ].T)
    st = jnp.transpose(state_conv[0], (1, 0, 2))
    wshgu = jnp.concatenate([w_sh_gate[0], w_sh_up[0]], axis=1).astype(BF16)
    xsm = jnp.transpose(x_sample, (1, 0, 2)).reshape(t_s, d)

    xs, h2, lt, cztail, vs, cz23 = _mixer(
        x_prompt.reshape(t_p, d), xsm, m_all, g_norm1, w_in[0].astype(BF16), wt, bsp,
        g_v.reshape(1, MIX_A), w_conv[0], g_out_a, g_out_b, w_out[0].astype(BF16), g_norm2,
        w_router[0].T.astype(BF16), wshgu, w_sh_down[0].astype(BF16), csp, bsps, st,
        n_batch=n_batch, seq=seq)

    eidx, wts, rank, cnt = _route(lt, b_router[0].reshape(N_EXPERTS, 1))

    n_blocks = (t_all * TOP_K) // BM + N_EXPERTS
    dest, blk_e, nxt_e, nblk, cnt_row, pstart_row = _plan(cnt, eidx, rank, n_blocks)

    x_sorted = _dispatch(cnt_row[0], pstart_row[0], nblk[0, :1], dest, h2, n_blocks)
    y_sorted = _moe(blk_e[0, :n_blocks], nxt_e[0, :n_blocks], nblk[0, :1], x_sorted,
                    w_exp_gate[0], w_exp_up[0], w_exp_down[0])
    y_p, y_s = _final(dest, xs, wts.T, m_all, mf_all, g_final.reshape(1, d), y_sorted,
                      n_batch=n_batch, seq=seq, t_p=t_p)

    tiles_per_seq = seq // TM
    y_prompt = y_p.reshape(n_batch, seq, d)
    y_sample = jnp.transpose(y_s.reshape(dec_seq, dec_batch, d), (1, 0, 2))
    conv_p = cztail[tiles_per_seq - 1:n_batch * tiles_per_seq:tiles_per_seq, SUBLANES - (CONV_W - 1):, :][None]
    conv_s = jnp.transpose(cz23, (1, 0, 2))[None]
    v_s = jnp.transpose(vs.reshape(dec_seq, dec_batch, A_HEADS, A_HEAD_DIM), (1, 0, 2, 3))[None]
    return (y_prompt, y_sample, conv_p, conv_s, v_s)
```

```python
import functools

import jax
import jax.numpy as jnp
from jax import lax
from jax.experimental import pallas as pl
from jax.experimental.pallas import tpu as pltpu
from jax.experimental.pallas import tpu_sc as plsc

F32 = jnp.float32
BF16 = jnp.bfloat16
I32 = jnp.int32

D_MODEL = 1024
MIX_A = 512
A_HEADS = 4
A_HEAD_DIM = 128
CHUNK = 128
CONV_DIM = 512
CONV_W = 3
PROJ_DIM = 2 * MIX_A + 3 * CONV_DIM
N_EXPERTS = 256
TOP_K = 8
N_GROUPS = 8
TOPK_GROUPS = 4
GROUP_SIZE = N_EXPERTS // N_GROUPS
D_EXPERT = 256
D_SHARED = 256
ROUTED_SCALE = 2.5
N_MOD = 6
RMS_EPS = 1e-6
DEC_SEQ = 4

LANES = 128
SUBLANES = 8
SC_CORES = 2
SC_SUBCORES = 16
SC_LANES = 16
TM = 512
BM = 256
TC = 128
DMA_THREADS = 2
X_AHEAD = 2
X_RING = X_AHEAD + 1
VMEM_LIMIT_BYTES = 56 * 1024 * 1024


def _rms(x, g):
    return x * lax.rsqrt(jnp.mean(x * x, axis=-1, keepdims=True) + RMS_EPS) * g


def _rows(ref, row, n):
    return ref.at[pl.ds(pl.multiple_of(row * SUBLANES, SUBLANES), n * SUBLANES)]


def _row_major(ref, n):
    return jnp.concatenate([ref[pl.ds(j, n, stride=SUBLANES), :] for j in range(SUBLANES)], axis=1)


def _const_spec(shape):
    nd = len(shape)
    return pl.BlockSpec(shape, lambda *_: (0,) * nd, pipeline_mode=pl.Buffered(1))


def _ada_kernel(c_ref, w_ref, b_ref, o_ref):
    a = jax.nn.silu(c_ref[...]).astype(BF16)
    o_ref[...] = jnp.dot(a, w_ref[...].astype(BF16), preferred_element_type=F32) + b_ref[...]


def _ada(c_all, w, b):
    rows, n = c_all.shape[0], w.shape[1]
    return pl.pallas_call(
        _ada_kernel,
        out_shape=jax.ShapeDtypeStruct((rows, n), F32),
        grid=(n // D_MODEL,),
        in_specs=[
            pl.BlockSpec((rows, D_MODEL), lambda j: (0, 0)),
            pl.BlockSpec((D_MODEL, D_MODEL), lambda j: (0, j)),
            pl.BlockSpec((1, D_MODEL), lambda j: (0, j)),
        ],
        out_specs=pl.BlockSpec((rows, D_MODEL), lambda j: (0, j)),
        compiler_params=pltpu.CompilerParams(dimension_semantics=("arbitrary",)),
        name="ada",
    )(c_all, w, b.reshape(1, n))


def _mixer_kernel(xp_ref, xsm_ref, mp_ref, ms_ref, g1_ref, win_ref, wt_ref, bsp_ref, gv_ref, wconv_ref,
                  goa_ref, gob_ref, wout_ref, g2_ref, wrt_ref, wshgu_ref, wshd_ref, csp_ref, bsps_ref, st_ref,
                  xs_out, h2_out, lt_out, cztail_out, vs_out, cz23_out, carry_ref, *, n_ptiles, tiles_per_seq):
    i = pl.program_id(0)
    d = D_MODEL

    def proj(x, sh1, sc1):
        h = _rms(x, g1_ref[...]) * (1.0 + sc1) + sh1
        p = jnp.dot(h.astype(BF16), win_ref[...], preferred_element_type=F32)
        u = jax.nn.gelu(p[:, :MIX_A])
        v = jax.nn.gelu(p[:, MIX_A:2 * MIX_A])
        vn = jnp.concatenate(
            [_rms(v[:, h * LANES:(h + 1) * LANES], gv_ref[:, h * LANES:(h + 1) * LANES]) for h in range(A_HEADS)],
            axis=1)
        o = 2 * MIX_A
        return u, vn, p[:, o:o + CONV_DIM], p[:, o + CONV_DIM:o + 2 * CONV_DIM], p[:, o + 2 * CONV_DIM:]

    def tail(x, ya, yb, gate1, sh2, sc2, gate2):
        cat = jnp.concatenate([_rms(ya, goa_ref[...]), _rms(yb, gob_ref[...])], axis=1).astype(BF16)
        x1 = x + gate1 * jnp.dot(cat, wout_ref[...], preferred_element_type=F32)
        h2 = _rms(x1, g2_ref[...]) * (1.0 + sc2) + sh2
        h2b = h2.astype(BF16)
        lt_out[...] = lax.dot_general(wrt_ref[...], h2b, (((1,), (1,)), ((), ())), preferred_element_type=F32)
        gu = jnp.dot(h2b, wshgu_ref[...], preferred_element_type=F32)
        hs = (jax.nn.silu(gu[:, :D_SHARED]) * gu[:, D_SHARED:]).astype(BF16)
        xs_out[...] = x1 + gate2 * jnp.dot(hs, wshd_ref[...], preferred_element_type=F32)
        for j in range(SUBLANES):
            h2_out[pl.ds(j, TM, stride=SUBLANES), :] = h2[:, j * LANES:(j + 1) * LANES]

    @pl.when(i < n_ptiles)
    def _prompt():
        b = i // tiles_per_seq
        m = mp_ref[pl.ds(b, 1), :]
        mod = [m[:, k * d:(k + 1) * d] for k in range(N_MOD)]
        x = xp_ref[...]
        u, vn, bg, cg, z = proj(x, mod[0], mod[1])
        n_chunks = TM // CHUNK
        ya_cols = []
        for h in range(A_HEADS):
            vh = vn[:, h * LANES:(h + 1) * LANES].astype(BF16)
            rhs = jnp.concatenate([vh[c * CHUNK:(c + 1) * CHUNK] for c in range(n_chunks)], axis=1)
            mix = jnp.dot(wt_ref[h], rhs, preferred_element_type=F32)
            bias = bsp_ref[:, h * LANES:(h + 1) * LANES]
            mix = jnp.concatenate([mix[:, c * LANES:(c + 1) * LANES] + bias for c in range(n_chunks)], axis=0)
            ya_cols.append(u[:, h * LANES:(h + 1) * LANES] * mix)
        ya = jnp.concatenate(ya_cols, axis=1)
        cz = cg * z

        @pl.when(i % tiles_per_seq == 0)
        def _():
            carry_ref[...] = jnp.zeros_like(carry_ref)

        c6 = carry_ref[SUBLANES - 2:SUBLANES - 1, :]
        c7 = carry_ref[SUBLANES - 1:SUBLANES, :]
        r = lax.broadcasted_iota(I32, (TM, 1), 0)
        p1 = jnp.where(r == 0, c7, pltpu.roll(cz, 1, 0))
        p2 = jnp.where(r == 0, c6, jnp.where(r == 1, c7, pltpu.roll(cz, 2, 0)))
        wc = wconv_ref[...]
        yb = bg * (p2 * wc[0:1] + p1 * wc[1:2] + cz * wc[2:3])
        carry_ref[...] = cz[TM - SUBLANES:]
        cztail_out[0] = cz[TM - SUBLANES:]
        tail(x, ya, yb, mod[2], mod[3], mod[4], mod[5])

    @pl.when(i == n_ptiles)
    def _sample():
        ms = ms_ref[...]
        nb = TM // DEC_SEQ

        def mod(k):
            return jnp.concatenate([ms[:, k * d:(k + 1) * d]] * DEC_SEQ, axis=0)

        x = xsm_ref[...]
        u, vn, bg, cg, z = proj(x, mod(0), mod(1))
        vt = [vn[t * nb:(t + 1) * nb] for t in range(DEC_SEQ)]
        mixes = []
        for t in range(DEC_SEQ):
            acc = csp_ref[DEC_SEQ * t:DEC_SEQ * t + 1, :] * vt[0]
            for s in range(1, t + 1):
                acc = acc + csp_ref[DEC_SEQ * t + s:DEC_SEQ * t + s + 1, :] * vt[s]
            mixes.append(acc + bsps_ref[t:t + 1, :])
        ya = u * jnp.concatenate(mixes, axis=0)
        cz = cg * z
        czt = [cz[t * nb:(t + 1) * nb] for t in range(DEC_SEQ)]
        full = [st_ref[0], st_ref[1]] + czt
        wc = wconv_ref[...]
        yc = jnp.concatenate(
            [full[t] * wc[0:1] + full[t + 1] * wc[1:2] + full[t + 2] * wc[2:3] for t in range(DEC_SEQ)], axis=0)
        yb = bg * yc
        vs_out[...] = vn
        cz23_out[0] = czt[DEC_SEQ - 2]
        cz23_out[1] = czt[DEC_SEQ - 1]
        cztail_out[0] = cz[TM - SUBLANES:]
        tail(x, ya, yb, mod(2), mod(3), mod(4), mod(5))


def _mixer(xp, xsm, m_all, g1, win, wt, bsp, gv, wconv, goa, gob, wout, g2, wrt, wshgu, wshd, csp, bsps, st,
           *, n_batch, seq):
    t_p = xp.shape[0]
    n_ptiles = t_p // TM
    n_tiles = n_ptiles + 1
    t_all = n_tiles * TM
    dec_batch = xsm.shape[0] // DEC_SEQ
    d = D_MODEL
    kern = functools.partial(_mixer_kernel, n_ptiles=n_ptiles, tiles_per_seq=seq // TM)
    in_specs = [
        pl.BlockSpec((TM, d), lambda i: (jnp.minimum(i, n_ptiles - 1), 0)),
        _const_spec((TM, d)),
        pl.BlockSpec((n_batch, N_MOD * d), lambda i: (dec_batch // n_batch, 0), pipeline_mode=pl.Buffered(1)),
        pl.BlockSpec((dec_batch, N_MOD * d), lambda i: (0, 0), pipeline_mode=pl.Buffered(1)),
        _const_spec(g1.shape), _const_spec(win.shape), _const_spec(wt.shape), _const_spec(bsp.shape),
        _const_spec(gv.shape), _const_spec(wconv.shape), _const_spec(goa.shape), _const_spec(gob.shape),
        _const_spec(wout.shape), _const_spec(g2.shape), _const_spec(wrt.shape), _const_spec(wshgu.shape),
        _const_spec(wshd.shape), _const_spec(csp.shape), _const_spec(bsps.shape), _const_spec(st.shape),
    ]
    out_shape = [
        jax.ShapeDtypeStruct((t_all, d), F32),
        jax.ShapeDtypeStruct((t_all * SUBLANES, LANES), F32),
        jax.ShapeDtypeStruct((N_EXPERTS, t_all), F32),
        jax.ShapeDtypeStruct((n_tiles, SUBLANES, CONV_DIM), F32),
        jax.ShapeDtypeStruct((TM, MIX_A), F32),
        jax.ShapeDtypeStruct((2, dec_batch, CONV_DIM), F32),
    ]
    out_specs = [
        pl.BlockSpec((TM, d), lambda i: (i, 0)),
        pl.BlockSpec((TM * SUBLANES, LANES), lambda i: (i, 0)),
        pl.BlockSpec((N_EXPERTS, TM), lambda i: (0, i)),
        pl.BlockSpec((1, SUBLANES, CONV_DIM), lambda i: (i, 0, 0)),
        pl.BlockSpec((TM, MIX_A), lambda i: (0, 0)),
        pl.BlockSpec((2, dec_batch, CONV_DIM), lambda i: (0, 0, 0)),
    ]
    return pl.pallas_call(
        kern,
        out_shape=out_shape,
        grid=(n_tiles,),
        in_specs=in_specs,
        out_specs=out_specs,
        scratch_shapes=[pltpu.VMEM((SUBLANES, CONV_DIM), F32)],
        compiler_params=pltpu.CompilerParams(dimension_semantics=("arbitrary",),
                                             vmem_limit_bytes=VMEM_LIMIT_BYTES),
        name="mixer",
    )(xp, xsm, m_all, m_all, g1, win, wt, bsp, gv, wconv, goa, gob, wout, g2, wrt, wshgu, wshd, csp, bsps, st)


def _route_kernel(lt_ref, br_ref, e_out, w_out, r_out, cnt_out, carry_ref, tri_ref):
    i = pl.program_id(0)
    neg = -jnp.inf

    @pl.when(i == 0)
    def _():
        carry_ref[...] = jnp.zeros_like(carry_ref)
        a = lax.broadcasted_iota(I32, (TM, TM), 0)
        b = lax.broadcasted_iota(I32, (TM, TM), 1)
        tri_ref[...] = jnp.where(a < b, 1.0, 0.0).astype(BF16)

    scores = jax.nn.sigmoid(lt_ref[...])
    sel = scores + br_ref[...]
    iog = lax.broadcasted_iota(I32, (GROUP_SIZE, TM), 0)
    gs_rows = []
    for g in range(N_GROUPS):
        sg = sel[g * GROUP_SIZE:(g + 1) * GROUP_SIZE]
        m1 = jnp.max(sg, axis=0, keepdims=True)
        i1 = jnp.min(jnp.where(sg == m1, iog, GROUP_SIZE), axis=0, keepdims=True)
        m2 = jnp.max(jnp.where(iog == i1, neg, sg), axis=0, keepdims=True)
        gs_rows.append(m1 + m2)
    gs = jnp.concatenate(gs_rows, axis=0)
    io8 = lax.broadcasted_iota(I32, (N_GROUPS, TM), 0)
    keep = jnp.zeros((N_GROUPS, TM), F32)
    for _ in range(TOPK_GROUPS):
        m = jnp.max(gs, axis=0, keepdims=True)
        idx = jnp.min(jnp.where(gs == m, io8, N_GROUPS), axis=0, keepdims=True)
        hit = io8 == idx
        keep = jnp.where(hit, 1.0, keep)
        gs = jnp.where(hit, neg, gs)
    sel = jnp.concatenate(
        [jnp.where(keep[g:g + 1] > 0.0, sel[g * GROUP_SIZE:(g + 1) * GROUP_SIZE], neg) for g in range(N_GROUPS)],
        axis=0)
    ioe = lax.broadcasted_iota(I32, (N_EXPERTS, TM), 0)
    picked = jnp.zeros((N_EXPERTS, TM), F32)
    e_rows, w_rows = [], []
    for _ in range(TOP_K):
        m = jnp.max(sel, axis=0, keepdims=True)
        idx = jnp.min(jnp.where(sel == m, ioe, N_EXPERTS), axis=0, keepdims=True)
        hit = ioe == idx
        e_rows.append(idx)
        w_rows.append(jnp.sum(jnp.where(hit, scores, 0.0), axis=0, keepdims=True))
        picked = jnp.where(hit, 1.0, picked)
        sel = jnp.where(hit, neg, sel)
    wk = jnp.concatenate(w_rows, axis=0)
    w_out[...] = wk / jnp.sum(wk, axis=0, keepdims=True) * ROUTED_SCALE
    e_out[...] = jnp.concatenate(e_rows, axis=0)
    before = jnp.dot(picked.astype(BF16), tri_ref[...], preferred_element_type=F32) + carry_ref[...]
    r_rows = [jnp.sum(jnp.where(ioe == e_rows[k], before, 0.0), axis=0, keepdims=True) for k in range(TOP_K)]
    r_out[...] = jnp.concatenate(r_rows, axis=0).astype(I32)
    carry_ref[...] = carry_ref[...] + jnp.sum(picked, axis=1, keepdims=True)
    cnt_out[...] = carry_ref[...]


def _route(lt, br):
    t_all = lt.shape[1]
    n_tiles = t_all // TM
    return pl.pallas_call(
        _route_kernel,
        out_shape=[
            jax.ShapeDtypeStruct((TOP_K, t_all), I32),
            jax.ShapeDtypeStruct((TOP_K, t_all), F32),
            jax.ShapeDtypeStruct((TOP_K, t_all), I32),
            jax.ShapeDtypeStruct((N_EXPERTS, 1), F32),
        ],
        grid=(n_tiles,),
        in_specs=[pl.BlockSpec((N_EXPERTS, TM), lambda i: (0, i)), _const_spec((N_EXPERTS, 1))],
        out_specs=[
            pl.BlockSpec((TOP_K, TM), lambda i: (0, i)),
            pl.BlockSpec((TOP_K, TM), lambda i: (0, i)),
            pl.BlockSpec((TOP_K, TM), lambda i: (0, i)),
            pl.BlockSpec((N_EXPERTS, 1), lambda i: (0, 0)),
        ],
        scratch_shapes=[pltpu.VMEM((N_EXPERTS, 1), F32), pltpu.VMEM((TM, TM), BF16)],
        compiler_params=pltpu.CompilerParams(dimension_semantics=("arbitrary",)),
        name="route",
    )(lt, br)


def _plan_kernel(cnt_ref, e_ref, r_ref, dest_out, blk_e_out, nxt_e_out, nblk_out, pstart_ref):
    i = pl.program_id(0)

    @pl.when(i == 0)
    def _():
        nb = jnp.floor((cnt_ref[...] + (BM - 1)) * (1.0 / BM))
        a = lax.broadcasted_iota(I32, (N_EXPERTS, N_EXPERTS), 0)
        b = lax.broadcasted_iota(I32, (N_EXPERTS, N_EXPERTS), 1)
        lower = jnp.where(b < a, 1.0, 0.0).astype(BF16)
        nb_l = jnp.broadcast_to(nb, (N_EXPERTS, LANES)).astype(BF16)
        first_blk = jnp.dot(lower, nb_l, preferred_element_type=F32)[:, 0:1]
        end_blk = first_blk + nb
        pstart_ref[...] = first_blk * BM
        n_lanes = blk_e_out.shape[1]
        blk = lax.broadcasted_iota(I32, (N_EXPERTS, n_lanes), 1).astype(F32)
        owner = jnp.sum(jnp.where(end_blk <= blk, 1.0, 0.0), axis=0, keepdims=True)
        blk_e_out[...] = jnp.minimum(owner, N_EXPERTS - 1.0).astype(I32)
        total = jnp.max(end_blk, axis=0, keepdims=True)
        nblk_out[...] = jnp.broadcast_to(total, nblk_out.shape).astype(I32)
        group_end = jnp.min(jnp.where(end_blk > blk, end_blk, 2.0 * n_lanes), axis=0, keepdims=True)
        nxt = jnp.sum(jnp.where(end_blk <= group_end, 1.0, 0.0), axis=0, keepdims=True)
        nxt_e_out[...] = jnp.where(group_end < total, nxt, -1.0).astype(I32)

    ioe = lax.broadcasted_iota(I32, (N_EXPERTS, TM), 0)
    e = e_ref[...]
    rows = [jnp.sum(jnp.where(ioe == e[k:k + 1], pstart_ref[...], 0.0), axis=0, keepdims=True)
            for k in range(TOP_K)]
    dest_out[...] = jnp.concatenate(rows, axis=0).astype(I32) + r_ref[...]


def _plan(cnt, eidx, rank, n_blocks):
    t_all = eidx.shape[1]
    n_lanes = pl.cdiv(n_blocks, LANES) * LANES
    return pl.pallas_call(
        _plan_kernel,
        out_shape=[
            jax.ShapeDtypeStruct((TOP_K, t_all), I32),
            jax.ShapeDtypeStruct((1, n_lanes), I32),
            jax.ShapeDtypeStruct((1, n_lanes), I32),
            jax.ShapeDtypeStruct((1, LANES), I32),
        ],
        grid=(t_all // TM,),
        in_specs=[
            _const_spec((N_EXPERTS, 1)),
            pl.BlockSpec((TOP_K, TM), lambda i: (0, i)),
            pl.BlockSpec((TOP_K, TM), lambda i: (0, i)),
        ],
        out_specs=[
            pl.BlockSpec((TOP_K, TM), lambda i: (0, i)),
            pl.BlockSpec((1, n_lanes), lambda i: (0, 0)),
            pl.BlockSpec((1, n_lanes), lambda i: (0, 0)),
            pl.BlockSpec((1, LANES), lambda i: (0, 0)),
        ],
        scratch_shapes=[pltpu.VMEM((N_EXPERTS, 1), F32)],
        compiler_params=pltpu.CompilerParams(dimension_semantics=("arbitrary",)),
        name="plan",
    )(cnt, eidx, rank)


def _invert_kernel(dest_hbm, out_hbm, dbuf, obuf, *, t_all, rows_per_worker, chunk):
    wid = lax.axis_index("c") * SC_SUBCORES + lax.axis_index("s")
    lo = wid * rows_per_worker
    lane = lax.iota(I32, SC_LANES)

    @pl.loop(0, rows_per_worker, step=SC_LANES)
    def _(j):
        obuf[pl.ds(j, SC_LANES)] = jnp.zeros((SC_LANES,), I32)

    @pl.loop(0, TOP_K * t_all // chunk)
    def _(c):
        pltpu.sync_copy(dest_hbm.at[pl.ds(c * chunk, chunk)], dbuf)
        t0 = (c % (t_all // chunk)) * chunk

        @pl.loop(0, chunk, step=SC_LANES)
        def _(v):
            idx = dbuf[pl.ds(v, SC_LANES)] - lo
            mine = (idx >= 0) & (idx < rows_per_worker)
            plsc.store_scatter(obuf, [jnp.where(mine, idx, 0)], t0 + v + lane, mask=mine)

    pltpu.sync_copy(obuf, out_hbm.at[pl.ds(lo, rows_per_worker)])


def _invert(dest_flat, n_rows, t_all):
    n_workers = SC_CORES * SC_SUBCORES
    chunk = t_all // SC_LANES
    assert n_rows % (n_workers * SC_LANES) == 0 and t_all % SC_LANES == 0 and chunk % SC_LANES == 0
    kern = functools.partial(_invert_kernel, t_all=t_all, rows_per_worker=n_rows // n_workers, chunk=chunk)
    mesh = plsc.VectorSubcoreMesh(core_axis_name="c", subcore_axis_name="s",
                                  num_cores=SC_CORES, num_subcores=SC_SUBCORES)
    return pl.kernel(
        kern,
        out_type=jax.ShapeDtypeStruct((n_rows,), I32),
        mesh=mesh,
        scratch_types=[pltpu.VMEM((chunk,), I32), pltpu.VMEM((n_rows // n_workers,), I32)],
        compiler_params=pltpu.CompilerParams(needs_layout_passes=False),
        name="invert",
    )(dest_flat)


def _moe_kernel(blk_e_ref, nxt_e_ref, nblk_ref, *refs):
    tok_first = refs[:X_AHEAD]
    (tok_ahead, h2_hbm, wg_hbm, wu_hbm, wd_hbm, y_ref,
     x_ring, wg_l, wu_l, wd_l, wgu_s, wd_s, xsems, sems, cur_ref) = refs[X_AHEAD:]
    b = pl.program_id(0)
    nblk = nblk_ref[0]

    def weight_copies(e, slot):
        return (pltpu.make_async_copy(wg_hbm.at[e], wg_l.at[slot], sems.at[slot, 0]),
                pltpu.make_async_copy(wu_hbm.at[e], wu_l.at[slot], sems.at[slot, 1]),
                pltpu.make_async_copy(wd_hbm.at[e], wd_l.at[slot], sems.at[slot, 2]))

    def gather_rows(tok_ref, blk):
        slot = blk % X_RING

        def issue(jj, carry):
            for u in range(DMA_THREADS):
                j = jj * DMA_THREADS + u
                pltpu.make_async_copy(_rows(h2_hbm, tok_ref[0, 0, j], 1), _rows(x_ring.at[slot], j, 1),
                                      xsems.at[slot]).start(priority=u)
            return carry

        lax.fori_loop(0, BM // DMA_THREADS, issue, 0)

    def wait_rows(blk):
        slot = blk % X_RING
        pltpu.make_async_copy(_rows(h2_hbm, 0, BM), x_ring.at[slot], xsems.at[slot]).wait()

    @pl.when(b < nblk)
    def _():
        e = blk_e_ref[b]

        @pl.when(b == 0)
        def _():
            cur_ref[0] = 0
            for c in weight_copies(e, 0):
                c.start()
            for j in range(X_AHEAD):
                @pl.when(j < nblk)
                def _():
                    gather_rows(tok_first[j], j)

        @pl.when(b + X_AHEAD < nblk)
        def _():
            gather_rows(tok_ahead, b + X_AHEAD)

        @pl.when((b == 0) | (e != blk_e_ref[jnp.maximum(b - 1, 0)]))
        def _():
            @pl.when(b > 0)
            def _():
                cur_ref[0] = 1 - cur_ref[0]

            slot = cur_ref[0]
            for c in weight_copies(e, slot):
                c.wait()
            nxt = nxt_e_ref[b]

            @pl.when(nxt >= 0)
            def _():
                for c in weight_copies(nxt, 1 - slot):
                    c.start()

            wgu_s[:, :D_EXPERT] = wg_l[slot].astype(BF16)
            wgu_s[:, D_EXPERT:] = wu_l[slot].astype(BF16)
            wd_s[...] = wd_l[slot].astype(BF16)

        wait_rows(b)
        x = _row_major(x_ring.at[b % X_RING], BM).astype(BF16)
        gu = jnp.dot(x, wgu_s[...], preferred_element_type=F32)
        h = (jax.nn.silu(gu[:, :D_EXPERT]) * gu[:, D_EXPERT:]).astype(BF16)
        y = jnp.dot(h, wd_s[...], preferred_element_type=F32)
        for j in range(SUBLANES):
            y_ref[pl.ds(j, BM, stride=SUBLANES), :] = y[:, j * LANES:(j + 1) * LANES]

    @pl.when(b >= nblk)
    def _():
        y_ref[...] = jnp.zeros_like(y_ref)


def _moe(blk_e, nxt_e, nblk, row_tok, h2, wg, wu, wd):
    d = D_MODEL
    n_blocks = row_tok.shape[0]

    def tok_spec(index):
        return pl.BlockSpec((1, 1, BM), lambda b, be, nx, nb: (index(b, nb), 0, 0), memory_space=pltpu.SMEM)

    grid_spec = pltpu.PrefetchScalarGridSpec(
        num_scalar_prefetch=3,
        grid=(n_blocks,),
        in_specs=[tok_spec(lambda b, nb, j=j: jnp.minimum(j, nb[0] - 1)) for j in range(X_AHEAD)] + [
            tok_spec(lambda b, nb: jnp.minimum(b + X_AHEAD, nb[0] - 1)),
            pl.BlockSpec(memory_space=pl.ANY),
            pl.BlockSpec(memory_space=pl.ANY),
            pl.BlockSpec(memory_space=pl.ANY),
            pl.BlockSpec(memory_space=pl.ANY),
        ],
        out_specs=pl.BlockSpec((BM * SUBLANES, LANES), lambda b, *_: (b, 0)),
        scratch_shapes=[
            pltpu.VMEM((X_RING, BM * SUBLANES, LANES), F32),
            pltpu.VMEM((2, d, D_EXPERT), F32), pltpu.VMEM((2, d, D_EXPERT), F32), pltpu.VMEM((2, D_EXPERT, d), F32),
            pltpu.VMEM((d, 2 * D_EXPERT), BF16), pltpu.VMEM((D_EXPERT, d), BF16),
            pltpu.SemaphoreType.DMA((X_RING,)), pltpu.SemaphoreType.DMA((2, 3)), pltpu.SMEM((1,), I32),
        ],
    )
    return pl.pallas_call(
        _moe_kernel,
        out_shape=jax.ShapeDtypeStruct((n_blocks * BM * SUBLANES, LANES), F32),
        grid_spec=grid_spec,
        compiler_params=pltpu.CompilerParams(dimension_semantics=("arbitrary",),
                                             vmem_limit_bytes=VMEM_LIMIT_BYTES),
        name="moe",
    )(blk_e, nxt_e, nblk, *([row_tok] * (X_AHEAD + 1)), h2, wg, wu, wd)


def _final_kernel(dest_ref, dnext_ref, xs_ref, w_ref, gp_ref, gs_ref, fp_ref, fs_ref, gf_ref, y_hbm,
                  op_ref, os_ref, buf, sems, *, n_tiles, n_ptiles, tiles_per_seq):
    i = pl.program_id(0)
    d = D_MODEL
    slot = i % 2

    def gather(d_ref, s):
        def issue(t, carry):
            for k in range(TOP_K):
                pltpu.make_async_copy(_rows(y_hbm, d_ref[k, t], 1), _rows(buf.at[s, k], t, 1),
                                      sems.at[s]).start(priority=k % DMA_THREADS)
            return carry

        lax.fori_loop(0, TC, issue, 0)

    @pl.when(i == 0)
    def _():
        gather(dest_ref, 0)

    @pl.when(i + 1 < n_tiles)
    def _():
        gather(dnext_ref, 1 - slot)

    for k in range(TOP_K):
        pltpu.make_async_copy(_rows(y_hbm, 0, TC), buf.at[slot, k], sems.at[slot]).wait()

    w = w_ref[...]
    acc = w[:, 0:1] * _row_major(buf.at[slot, 0], TC)
    for k in range(1, TOP_K):
        acc = acc + w[:, k:k + 1] * _row_major(buf.at[slot, k], TC)

    def finish(gate2, shift, scale, o_ref):
        x2 = xs_ref[...] + gate2 * acc
        o_ref[...] = _rms(x2, gf_ref[...]) * (1.0 + scale) + shift

    @pl.when(i < n_ptiles)
    def _():
        b = i // tiles_per_seq
        f = fp_ref[pl.ds(b, 1), :]
        finish(gp_ref[pl.ds(b, 1), :], f[:, :d], f[:, d:], op_ref)

    @pl.when(i >= n_ptiles)
    def _():
        f = fs_ref[...]
        finish(gs_ref[...], f[:, :d], f[:, d:], os_ref)


def _final(dest, xs, wts, m_all, mf_all, gf, y_sorted, *, n_batch, seq, t_p):
    t_all, d = xs.shape
    n_tiles = t_all // TC
    n_ptiles = t_p // TC
    dec_batch = m_all.shape[0] - n_batch
    kern = functools.partial(_final_kernel, n_tiles=n_tiles, n_ptiles=n_ptiles, tiles_per_seq=seq // TC)
    pb = dec_batch // n_batch
    return pl.pallas_call(
        kern,
        out_shape=[jax.ShapeDtypeStruct((t_p, d), F32), jax.ShapeDtypeStruct((t_all - t_p, d), F32)],
        grid=(n_tiles,),
        in_specs=[
            pl.BlockSpec((TOP_K, TC), lambda i: (0, i), memory_space=pltpu.SMEM),
            pl.BlockSpec((TOP_K, TC), lambda i: (0, jnp.minimum(i + 1, n_tiles - 1)), memory_space=pltpu.SMEM),
            pl.BlockSpec((TC, d), lambda i: (i, 0)),
            pl.BlockSpec((TC, TOP_K), lambda i: (i, 0)),
            pl.BlockSpec((n_batch, d), lambda i: (pb, N_MOD - 1)),
            pl.BlockSpec((dec_batch, d), lambda i: (0, N_MOD - 1)),
            pl.BlockSpec((n_batch, 2 * d), lambda i: (pb, 0)),
            pl.BlockSpec((dec_batch, 2 * d), lambda i: (0, 0)),
            _const_spec(gf.shape),
            pl.BlockSpec(memory_space=pl.ANY),
        ],
        out_specs=[
            pl.BlockSpec((TC, d), lambda i: (jnp.minimum(i, n_ptiles - 1), 0)),
            pl.BlockSpec((TC, d), lambda i: (jnp.maximum(i - n_ptiles, 0), 0)),
        ],
        scratch_shapes=[pltpu.VMEM((2, TOP_K, TC * SUBLANES, LANES), F32), pltpu.SemaphoreType.DMA((2,))],
        compiler_params=pltpu.CompilerParams(dimension_semantics=("arbitrary",),
                                             vmem_limit_bytes=VMEM_LIMIT_BYTES),
        name="final",
    )(dest, dest, xs, wts, m_all, m_all, mf_all, mf_all, gf, y_sorted)


def kernel(x_prompt, x_sample, state_conv, c_prompt, c_sample, w_ada, b_ada, g_norm1, w_in, w_spatial, b_spatial, g_v, w_conv, g_out_a, g_out_b, w_out, g_norm2, w_router, b_router, w_exp_gate, w_exp_up, w_exp_down, w_sh_gate, w_sh_up, w_sh_down, w_ada_final, b_ada_final, g_final):
    n_batch, seq, d = x_prompt.shape
    dec_batch, dec_seq, _ = x_sample.shape
    assert w_ada.shape[0] == 1, "one layer only"
    assert d == D_MODEL and dec_seq == DEC_SEQ and dec_batch * dec_seq == TM and seq % TM == 0
    assert n_batch % SUBLANES == 0 and dec_batch % n_batch == 0
    t_p = n_batch * seq
    t_s = dec_batch * dec_seq
    t_all = t_p + t_s

    c_all = jnp.concatenate([c_sample, c_prompt], axis=0)
    m_all = _ada(c_all, w_ada[0], b_ada[0])
    mf_all = _ada(c_all, w_ada_final, b_ada_final)

    rep = functools.partial(jnp.repeat, repeats=A_HEAD_DIM, axis=1)
    tril = jnp.tril(jnp.ones((CHUNK, CHUNK), dtype=bool))
    wt = jnp.where(tril, w_spatial[0], 0.0).astype(BF16)
    bsp = rep(b_spatial[0].T)
    csp = rep(jnp.transpose(w_spatial[0][:, :DEC_SEQ, :DEC_SEQ], (1, 2, 0)).reshape(DEC_SEQ * DEC_SEQ, A_HEADS))
    bsps = rep(b_spatial[0][:, :DEC_SEQ].T)
    st = jnp.transpose(state_conv[0], (1, 0, 2))
    wshgu = jnp.concatenate([w_sh_gate[0], w_sh_up[0]], axis=1).astype(BF16)
    xsm = jnp.transpose(x_sample, (1, 0, 2)).reshape(t_s, d)

    xs, h2, lt, cztail, vs, cz23 = _mixer(
        x_prompt.reshape(t_p, d), xsm, m_all, g_norm1, w_in[0].astype(BF16), wt, bsp,
        g_v.reshape(1, MIX_A), w_conv[0], g_out_a, g_out_b, w_out[0].astype(BF16), g_norm2,
        w_router[0].T.astype(BF16), wshgu, w_sh_down[0].astype(BF16), csp, bsps, st,
        n_batch=n_batch, seq=seq)

    eidx, wts, rank, cnt = _route(lt, b_router[0].reshape(N_EXPERTS, 1))

    n_blocks = (t_all * TOP_K) // BM + N_EXPERTS
    dest, blk_e, nxt_e, nblk = _plan(cnt, eidx, rank, n_blocks)

    row_tok = _invert(dest.reshape(TOP_K * t_all), n_blocks * BM, t_all).reshape(n_blocks, 1, BM)
    y_sorted = _moe(blk_e[0, :n_blocks], nxt_e[0, :n_blocks], nblk[0, :1], row_tok, h2,
                    w_exp_gate[0], w_exp_up[0], w_exp_down[0])
    y_p, y_s = _final(dest, xs, wts.T, m_all, mf_all, g_final.reshape(1, d), y_sorted,
                      n_batch=n_batch, seq=seq, t_p=t_p)

    tiles_per_seq = seq // TM
    y_prompt = y_p.reshape(n_batch, seq, d)
    y_sample = jnp.transpose(y_s.reshape(dec_seq, dec_batch, d), (1, 0, 2))
    conv_p = cztail[tiles_per_seq - 1:n_batch * tiles_per_seq:tiles_per_seq, SUBLANES - (CONV_W - 1):, :][None]
    conv_s = jnp.transpose(cz23, (1, 0, 2))[None]
    v_s = jnp.transpose(vs.reshape(dec_seq, dec_batch, A_HEADS, A_HEAD_DIM), (1, 0, 2, 3))[None]
    return (y_prompt, y_sample, conv_p, conv_s, v_s)
```

```python
import functools

import jax
import jax.numpy as jnp
from jax import lax
from jax.experimental import pallas as pl
from jax.experimental.pallas import tpu as pltpu

F32 = jnp.float32
BF16 = jnp.bfloat16
I32 = jnp.int32
U32 = jnp.uint32

D_MODEL = 1024
MIX_A = 512
A_HEADS = 4
A_HEAD_DIM = 128
CHUNK = 128
CONV_DIM = 512
CONV_W = 3
PROJ_DIM = 2 * MIX_A + 3 * CONV_DIM
N_EXPERTS = 256
TOP_K = 8
N_GROUPS = 8
TOPK_GROUPS = 4
GROUP_SIZE = N_EXPERTS // N_GROUPS
D_EXPERT = 256
D_SHARED = 256
ROUTED_SCALE = 2.5
N_MOD = 6
RMS_EPS = 1e-6
DEC_SEQ = 4

LANES = 128
SUBLANES = 8
PACK_ROWS = D_MODEL // 2 // LANES
TM = 512
BM = 256
TC = 128
DMA_THREADS = 2
X_AHEAD = 2
X_RING = X_AHEAD + 1
VMEM_LIMIT_BYTES = 56 * 1024 * 1024


def _rms(x, g):
    return x * lax.rsqrt(jnp.mean(x * x, axis=-1, keepdims=True) + RMS_EPS) * g


def _rows(ref, row, n):
    return ref.at[pl.ds(pl.multiple_of(row * SUBLANES, SUBLANES), n * SUBLANES)]


def _row_major(ref, n):
    return jnp.concatenate([ref[pl.ds(j, n, stride=SUBLANES), :] for j in range(SUBLANES)], axis=1)


def _packed_rows(ref, row, n):
    return ref.at[pl.ds(pl.multiple_of(row * PACK_ROWS, PACK_ROWS), n * PACK_ROWS)]


def _pack_bf16_pairs(x):
    half = D_MODEL // 2
    xb = x.astype(BF16).astype(F32)
    lo = lax.shift_right_logical(lax.bitcast_convert_type(xb[:, :half], U32), jnp.uint32(16))
    return lo | lax.bitcast_convert_type(xb[:, half:], U32)


def _unpack_bf16_pairs(ref, n):
    words = [ref[pl.ds(j, n, stride=PACK_ROWS), :] for j in range(PACK_ROWS)]
    lo = [lax.bitcast_convert_type(lax.shift_left(w, jnp.uint32(16)), F32) for w in words]
    hi = [lax.bitcast_convert_type(w & jnp.uint32(0xFFFF0000), F32) for w in words]
    return jnp.concatenate(lo + hi, axis=1).astype(BF16)


def _const_spec(shape):
    nd = len(shape)
    return pl.BlockSpec(shape, lambda *_: (0,) * nd, pipeline_mode=pl.Buffered(1))


def _ada_kernel(c_ref, w_ref, b_ref, o_ref):
    a = jax.nn.silu(c_ref[...]).astype(BF16)
    o_ref[...] = jnp.dot(a, w_ref[...].astype(BF16), preferred_element_type=F32) + b_ref[...]


def _ada(c_all, w, b):
    rows, n = c_all.shape[0], w.shape[1]
    return pl.pallas_call(
        _ada_kernel,
        out_shape=jax.ShapeDtypeStruct((rows, n), F32),
        grid=(n // D_MODEL,),
        in_specs=[
            pl.BlockSpec((rows, D_MODEL), lambda j: (0, 0)),
            pl.BlockSpec((D_MODEL, D_MODEL), lambda j: (0, j)),
            pl.BlockSpec((1, D_MODEL), lambda j: (0, j)),
        ],
        out_specs=pl.BlockSpec((rows, D_MODEL), lambda j: (0, j)),
        compiler_params=pltpu.CompilerParams(dimension_semantics=("arbitrary",)),
        name="ada",
    )(c_all, w, b.reshape(1, n))


def _mixer_kernel(xp_ref, xsm_ref, mp_ref, ms_ref, g1_ref, win_ref, wt_ref, bsp_ref, gv_ref, wconv_ref,
                  goa_ref, gob_ref, wout_ref, g2_ref, wrt_ref, wshgu_ref, wshd_ref, csp_ref, bsps_ref, st_ref,
                  xs_out, h2_out, lt_out, cztail_out, vs_out, cz23_out, carry_ref, *, n_ptiles, tiles_per_seq):
    i = pl.program_id(0)
    d = D_MODEL

    def proj(x, sh1, sc1):
        h = _rms(x, g1_ref[...]) * (1.0 + sc1) + sh1
        p = jnp.dot(h.astype(BF16), win_ref[...], preferred_element_type=F32)
        u = jax.nn.gelu(p[:, :MIX_A])
        v = jax.nn.gelu(p[:, MIX_A:2 * MIX_A])
        vn = jnp.concatenate(
            [_rms(v[:, h * LANES:(h + 1) * LANES], gv_ref[:, h * LANES:(h + 1) * LANES]) for h in range(A_HEADS)],
            axis=1)
        o = 2 * MIX_A
        return u, vn, p[:, o:o + CONV_DIM], p[:, o + CONV_DIM:o + 2 * CONV_DIM], p[:, o + 2 * CONV_DIM:]

    def tail(x, ya, yb, gate1, sh2, sc2, gate2):
        cat = jnp.concatenate([_rms(ya, goa_ref[...]), _rms(yb, gob_ref[...])], axis=1).astype(BF16)
        x1 = x + gate1 * jnp.dot(cat, wout_ref[...], preferred_element_type=F32)
        h2 = _rms(x1, g2_ref[...]) * (1.0 + sc2) + sh2
        h2b = h2.astype(BF16)
        lt_out[...] = lax.dot_general(wrt_ref[...], h2b, (((1,), (1,)), ((), ())), preferred_element_type=F32)
        gu = jnp.dot(h2b, wshgu_ref[...], preferred_element_type=F32)
        hs = (jax.nn.silu(gu[:, :D_SHARED]) * gu[:, D_SHARED:]).astype(BF16)
        xs_out[...] = x1 + gate2 * jnp.dot(hs, wshd_ref[...], preferred_element_type=F32)
        packed = _pack_bf16_pairs(h2)
        for j in range(PACK_ROWS):
            h2_out[pl.ds(j, TM, stride=PACK_ROWS), :] = packed[:, j * LANES:(j + 1) * LANES]

    @pl.when(i < n_ptiles)
    def _prompt():
        b = i // tiles_per_seq
        m = mp_ref[pl.ds(b, 1), :]
        mod = [m[:, k * d:(k + 1) * d] for k in range(N_MOD)]
        x = xp_ref[...]
        u, vn, bg, cg, z = proj(x, mod[0], mod[1])
        n_chunks = TM // CHUNK
        ya_cols = []
        for h in range(A_HEADS):
            vh = vn[:, h * LANES:(h + 1) * LANES].astype(BF16)
            rhs = jnp.concatenate([vh[c * CHUNK:(c + 1) * CHUNK] for c in range(n_chunks)], axis=1)
            mix = jnp.dot(wt_ref[h], rhs, preferred_element_type=F32)
            bias = bsp_ref[:, h * LANES:(h + 1) * LANES]
            mix = jnp.concatenate([mix[:, c * LANES:(c + 1) * LANES] + bias for c in range(n_chunks)], axis=0)
            ya_cols.append(u[:, h * LANES:(h + 1) * LANES] * mix)
        ya = jnp.concatenate(ya_cols, axis=1)
        cz = cg * z

        @pl.when(i % tiles_per_seq == 0)
        def _():
            carry_ref[...] = jnp.zeros_like(carry_ref)

        c6 = carry_ref[SUBLANES - 2:SUBLANES - 1, :]
        c7 = carry_ref[SUBLANES - 1:SUBLANES, :]
        r = lax.broadcasted_iota(I32, (TM, 1), 0)
        p1 = jnp.where(r == 0, c7, pltpu.roll(cz, 1, 0))
        p2 = jnp.where(r == 0, c6, jnp.where(r == 1, c7, pltpu.roll(cz, 2, 0)))
        wc = wconv_ref[...]
        yb = bg * (p2 * wc[0:1] + p1 * wc[1:2] + cz * wc[2:3])
        carry_ref[...] = cz[TM - SUBLANES:]
        cztail_out[0] = cz[TM - SUBLANES:]
        tail(x, ya, yb, mod[2], mod[3], mod[4], mod[5])

    @pl.when(i == n_ptiles)
    def _sample():
        ms = ms_ref[...]
        nb = TM // DEC_SEQ

        def mod(k):
            return jnp.concatenate([ms[:, k * d:(k + 1) * d]] * DEC_SEQ, axis=0)

        x = xsm_ref[...]
        u, vn, bg, cg, z = proj(x, mod(0), mod(1))
        vt = [vn[t * nb:(t + 1) * nb] for t in range(DEC_SEQ)]
        mixes = []
        for t in range(DEC_SEQ):
            acc = csp_ref[DEC_SEQ * t:DEC_SEQ * t + 1, :] * vt[0]
            for s in range(1, t + 1):
                acc = acc + csp_ref[DEC_SEQ * t + s:DEC_SEQ * t + s + 1, :] * vt[s]
            mixes.append(acc + bsps_ref[t:t + 1, :])
        ya = u * jnp.concatenate(mixes, axis=0)
        cz = cg * z
        czt = [cz[t * nb:(t + 1) * nb] for t in range(DEC_SEQ)]
        full = [st_ref[0], st_ref[1]] + czt
        wc = wconv_ref[...]
        yc = jnp.concatenate(
            [full[t] * wc[0:1] + full[t + 1] * wc[1:2] + full[t + 2] * wc[2:3] for t in range(DEC_SEQ)], axis=0)
        yb = bg * yc
        vs_out[...] = vn
        cz23_out[0] = czt[DEC_SEQ - 2]
        cz23_out[1] = czt[DEC_SEQ - 1]
        cztail_out[0] = cz[TM - SUBLANES:]
        tail(x, ya, yb, mod(2), mod(3), mod(4), mod(5))


def _mixer(xp, xsm, m_all, g1, win, wt, bsp, gv, wconv, goa, gob, wout, g2, wrt, wshgu, wshd, csp, bsps, st,
           *, n_batch, seq):
    t_p = xp.shape[0]
    n_ptiles = t_p // TM
    n_tiles = n_ptiles + 1
    t_all = n_tiles * TM
    dec_batch = xsm.shape[0] // DEC_SEQ
    d = D_MODEL
    kern = functools.partial(_mixer_kernel, n_ptiles=n_ptiles, tiles_per_seq=seq // TM)
    in_specs = [
        pl.BlockSpec((TM, d), lambda i: (jnp.minimum(i, n_ptiles - 1), 0)),
        _const_spec((TM, d)),
        pl.BlockSpec((n_batch, N_MOD * d), lambda i: (dec_batch // n_batch, 0), pipeline_mode=pl.Buffered(1)),
        pl.BlockSpec((dec_batch, N_MOD * d), lambda i: (0, 0), pipeline_mode=pl.Buffered(1)),
        _const_spec(g1.shape), _const_spec(win.shape), _const_spec(wt.shape), _const_spec(bsp.shape),
        _const_spec(gv.shape), _const_spec(wconv.shape), _const_spec(goa.shape), _const_spec(gob.shape),
        _const_spec(wout.shape), _const_spec(g2.shape), _const_spec(wrt.shape), _const_spec(wshgu.shape),
        _const_spec(wshd.shape), _const_spec(csp.shape), _const_spec(bsps.shape), _const_spec(st.shape),
    ]
    out_shape = [
        jax.ShapeDtypeStruct((t_all, d), F32),
        jax.ShapeDtypeStruct((t_all * PACK_ROWS, LANES), U32),
        jax.ShapeDtypeStruct((N_EXPERTS, t_all), F32),
        jax.ShapeDtypeStruct((n_tiles, SUBLANES, CONV_DIM), F32),
        jax.ShapeDtypeStruct((TM, MIX_A), F32),
        jax.ShapeDtypeStruct((2, dec_batch, CONV_DIM), F32),
    ]
    out_specs = [
        pl.BlockSpec((TM, d), lambda i: (i, 0)),
        pl.BlockSpec((TM * PACK_ROWS, LANES), lambda i: (i, 0)),
        pl.BlockSpec((N_EXPERTS, TM), lambda i: (0, i)),
        pl.BlockSpec((1, SUBLANES, CONV_DIM), lambda i: (i, 0, 0)),
        pl.BlockSpec((TM, MIX_A), lambda i: (0, 0)),
        pl.BlockSpec((2, dec_batch, CONV_DIM), lambda i: (0, 0, 0)),
    ]
    return pl.pallas_call(
        kern,
        out_shape=out_shape,
        grid=(n_tiles,),
        in_specs=in_specs,
        out_specs=out_specs,
        scratch_shapes=[pltpu.VMEM((SUBLANES, CONV_DIM), F32)],
        compiler_params=pltpu.CompilerParams(dimension_semantics=("arbitrary",),
                                             vmem_limit_bytes=VMEM_LIMIT_BYTES),
        name="mixer",
    )(xp, xsm, m_all, m_all, g1, win, wt, bsp, gv, wconv, goa, gob, wout, g2, wrt, wshgu, wshd, csp, bsps, st)


def _route_kernel(lt_ref, br_ref, e_out, w_out, r_out, cnt_out, carry_ref, tri_ref):
    i = pl.program_id(0)
    neg = -jnp.inf

    @pl.when(i == 0)
    def _():
        carry_ref[...] = jnp.zeros_like(carry_ref)
        a = lax.broadcasted_iota(I32, (TM, TM), 0)
        b = lax.broadcasted_iota(I32, (TM, TM), 1)
        tri_ref[...] = jnp.where(a < b, 1.0, 0.0).astype(BF16)

    scores = jax.nn.sigmoid(lt_ref[...])
    sel = scores + br_ref[...]
    iog = lax.broadcasted_iota(I32, (GROUP_SIZE, TM), 0)
    gs_rows = []
    for g in range(N_GROUPS):
        sg = sel[g * GROUP_SIZE:(g + 1) * GROUP_SIZE]
        m1 = jnp.max(sg, axis=0, keepdims=True)
        i1 = jnp.min(jnp.where(sg == m1, iog, GROUP_SIZE), axis=0, keepdims=True)
        m2 = jnp.max(jnp.where(iog == i1, neg, sg), axis=0, keepdims=True)
        gs_rows.append(m1 + m2)
    gs = jnp.concatenate(gs_rows, axis=0)
    io8 = lax.broadcasted_iota(I32, (N_GROUPS, TM), 0)
    keep = jnp.zeros((N_GROUPS, TM), F32)
    for _ in range(TOPK_GROUPS):
        m = jnp.max(gs, axis=0, keepdims=True)
        idx = jnp.min(jnp.where(gs == m, io8, N_GROUPS), axis=0, keepdims=True)
        hit = io8 == idx
        keep = jnp.where(hit, 1.0, keep)
        gs = jnp.where(hit, neg, gs)
    sel = jnp.concatenate(
        [jnp.where(keep[g:g + 1] > 0.0, sel[g * GROUP_SIZE:(g + 1) * GROUP_SIZE], neg) for g in range(N_GROUPS)],
        axis=0)
    ioe = lax.broadcasted_iota(I32, (N_EXPERTS, TM), 0)
    picked = jnp.zeros((N_EXPERTS, TM), F32)
    e_rows, w_rows = [], []
    for _ in range(TOP_K):
        m = jnp.max(sel, axis=0, keepdims=True)
        idx = jnp.min(jnp.where(sel == m, ioe, N_EXPERTS), axis=0, keepdims=True)
        hit = ioe == idx
        e_rows.append(idx)
        w_rows.append(jnp.sum(jnp.where(hit, scores, 0.0), axis=0, keepdims=True))
        picked = jnp.where(hit, 1.0, picked)
        sel = jnp.where(hit, neg, sel)
    wk = jnp.concatenate(w_rows, axis=0)
    w_out[...] = wk / jnp.sum(wk, axis=0, keepdims=True) * ROUTED_SCALE
    e_out[...] = jnp.concatenate(e_rows, axis=0)
    before = jnp.dot(picked.astype(BF16), tri_ref[...], preferred_element_type=F32) + carry_ref[...]
    r_rows = [jnp.sum(jnp.where(ioe == e_rows[k], before, 0.0), axis=0, keepdims=True) for k in range(TOP_K)]
    r_out[...] = jnp.concatenate(r_rows, axis=0).astype(I32)
    carry_ref[...] = carry_ref[...] + jnp.sum(picked, axis=1, keepdims=True)
    cnt_out[...] = carry_ref[...]


def _route(lt, br):
    t_all = lt.shape[1]
    n_tiles = t_all // TM
    return pl.pallas_call(
        _route_kernel,
        out_shape=[
            jax.ShapeDtypeStruct((TOP_K, t_all), I32),
            jax.ShapeDtypeStruct((TOP_K, t_all), F32),
            jax.ShapeDtypeStruct((TOP_K, t_all), I32),
            jax.ShapeDtypeStruct((N_EXPERTS, 1), F32),
        ],
        grid=(n_tiles,),
        in_specs=[pl.BlockSpec((N_EXPERTS, TM), lambda i: (0, i)), _const_spec((N_EXPERTS, 1))],
        out_specs=[
            pl.BlockSpec((TOP_K, TM), lambda i: (0, i)),
            pl.BlockSpec((TOP_K, TM), lambda i: (0, i)),
            pl.BlockSpec((TOP_K, TM), lambda i: (0, i)),
            pl.BlockSpec((N_EXPERTS, 1), lambda i: (0, 0)),
        ],
        scratch_shapes=[pltpu.VMEM((N_EXPERTS, 1), F32), pltpu.VMEM((TM, TM), BF16)],
        compiler_params=pltpu.CompilerParams(dimension_semantics=("arbitrary",)),
        name="route",
    )(lt, br)


def _plan_kernel(cnt_ref, e_ref, r_ref, dest_out, blk_e_out, nxt_e_out, nblk_out, cnt_row_out, pstart_row_out,
                 pstart_ref):
    i = pl.program_id(0)

    @pl.when(i == 0)
    def _():
        nb = jnp.floor((cnt_ref[...] + (BM - 1)) * (1.0 / BM))
        a = lax.broadcasted_iota(I32, (N_EXPERTS, N_EXPERTS), 0)
        b = lax.broadcasted_iota(I32, (N_EXPERTS, N_EXPERTS), 1)
        lower = jnp.where(b < a, 1.0, 0.0).astype(BF16)
        nb_l = jnp.broadcast_to(nb, (N_EXPERTS, LANES)).astype(BF16)
        first_blk = jnp.dot(lower, nb_l, preferred_element_type=F32)[:, 0:1]
        end_blk = first_blk + nb
        pstart_ref[...] = first_blk * BM
        cnt_row_out[...] = jnp.sum(jnp.where(a == b, cnt_ref[...], 0.0), axis=0, keepdims=True).astype(I32)
        pstart_row_out[...] = jnp.sum(jnp.where(a == b, first_blk * BM, 0.0), axis=0, keepdims=True).astype(I32)
        n_lanes = blk_e_out.shape[1]
        blk = lax.broadcasted_iota(I32, (N_EXPERTS, n_lanes), 1).astype(F32)
        owner = jnp.sum(jnp.where(end_blk <= blk, 1.0, 0.0), axis=0, keepdims=True)
        blk_e_out[...] = jnp.minimum(owner, N_EXPERTS - 1.0).astype(I32)
        total = jnp.max(end_blk, axis=0, keepdims=True)
        nblk_out[...] = jnp.broadcast_to(total, nblk_out.shape).astype(I32)
        group_end = jnp.min(jnp.where(end_blk > blk, end_blk, 2.0 * n_lanes), axis=0, keepdims=True)
        nxt = jnp.sum(jnp.where(end_blk <= group_end, 1.0, 0.0), axis=0, keepdims=True)
        nxt_e_out[...] = jnp.where(group_end < total, nxt, -1.0).astype(I32)

    ioe = lax.broadcasted_iota(I32, (N_EXPERTS, TM), 0)
    e = e_ref[...]
    rows = [jnp.sum(jnp.where(ioe == e[k:k + 1], pstart_ref[...], 0.0), axis=0, keepdims=True)
            for k in range(TOP_K)]
    dest_out[...] = jnp.concatenate(rows, axis=0).astype(I32) + r_ref[...]


def _plan(cnt, eidx, rank, n_blocks):
    t_all = eidx.shape[1]
    n_lanes = pl.cdiv(n_blocks, LANES) * LANES
    return pl.pallas_call(
        _plan_kernel,
        out_shape=[
            jax.ShapeDtypeStruct((TOP_K, t_all), I32),
            jax.ShapeDtypeStruct((1, n_lanes), I32),
            jax.ShapeDtypeStruct((1, n_lanes), I32),
            jax.ShapeDtypeStruct((1, LANES), I32),
            jax.ShapeDtypeStruct((1, N_EXPERTS), I32),
            jax.ShapeDtypeStruct((1, N_EXPERTS), I32),
        ],
        grid=(t_all // TM,),
        in_specs=[
            _const_spec((N_EXPERTS, 1)),
            pl.BlockSpec((TOP_K, TM), lambda i: (0, i)),
            pl.BlockSpec((TOP_K, TM), lambda i: (0, i)),
        ],
        out_specs=[
            pl.BlockSpec((TOP_K, TM), lambda i: (0, i)),
            pl.BlockSpec((1, n_lanes), lambda i: (0, 0)),
            pl.BlockSpec((1, n_lanes), lambda i: (0, 0)),
            pl.BlockSpec((1, LANES), lambda i: (0, 0)),
            pl.BlockSpec((1, N_EXPERTS), lambda i: (0, 0)),
            pl.BlockSpec((1, N_EXPERTS), lambda i: (0, 0)),
        ],
        scratch_shapes=[pltpu.VMEM((N_EXPERTS, 1), F32)],
        compiler_params=pltpu.CompilerParams(dimension_semantics=("arbitrary",)),
        name="plan",
    )(cnt, eidx, rank)


def _dispatch_kernel(cnt_ref, pst_ref, nblk_ref, dest_ref, h2_ref, xs_hbm, zeros, sem, fill_sem,
                     *, n_steps, n_blocks):
    i = pl.program_id(0)
    per_step = pl.cdiv(N_EXPERTS, n_steps)

    @pl.when(i == 0)
    def _():
        zeros[...] = jnp.zeros_like(zeros)

    def issue(t, carry):
        for k in range(TOP_K):
            pltpu.make_async_copy(_packed_rows(h2_ref, t, 1), _packed_rows(xs_hbm, dest_ref[k, t], 1), sem).start(
                priority=k % DMA_THREADS)
        return carry

    lax.fori_loop(0, TM, issue, 0)

    def fills(do):
        def per_expert(j, carry):
            e = i * per_step + j

            @pl.when(e < N_EXPERTS)
            def _():
                cnt = cnt_ref[e]
                base = pst_ref[e]
                padded = (cnt + (BM - 1)) // BM * BM
                mid = jnp.minimum((cnt + (SUBLANES - 1)) // SUBLANES * SUBLANES, padded)

                def one(r, c):
                    do(pltpu.make_async_copy(_packed_rows(zeros, 0, 1), _packed_rows(xs_hbm, base + r, 1), fill_sem))
                    return c

                lax.fori_loop(cnt, mid, one, 0)

                def eight(q, c):
                    r = base + mid + q * SUBLANES
                    do(pltpu.make_async_copy(_packed_rows(zeros, 0, SUBLANES), _packed_rows(xs_hbm, r, SUBLANES),
                                             fill_sem))
                    return c

                lax.fori_loop(0, (padded - mid) // SUBLANES, eight, 0)

            return carry

        lax.fori_loop(0, per_step, per_expert, 0)

        def per_block(j, carry):
            b = nblk_ref[0] + i * per_step + j

            @pl.when(b < n_blocks)
            def _():
                do(pltpu.make_async_copy(zeros, _packed_rows(xs_hbm, b * BM, BM), fill_sem))

            return carry

        lax.fori_loop(0, per_step, per_block, 0)

    fills(lambda c: c.start())
    for k in range(TOP_K):
        pltpu.make_async_copy(h2_ref, _packed_rows(xs_hbm, 0, TM), sem).wait()
    fills(lambda c: c.wait())


def _dispatch(cnt_row, pstart_row, nblk, dest, h2, n_blocks):
    t_all = h2.shape[0] // PACK_ROWS
    n_steps = t_all // TM
    kern = functools.partial(_dispatch_kernel, n_steps=n_steps, n_blocks=n_blocks)
    grid_spec = pltpu.PrefetchScalarGridSpec(
        num_scalar_prefetch=3,
        grid=(n_steps,),
        in_specs=[
            pl.BlockSpec((TOP_K, TM), lambda i, *_: (0, i), memory_space=pltpu.SMEM),
            pl.BlockSpec((TM * PACK_ROWS, LANES), lambda i, *_: (i, 0)),
        ],
        out_specs=pl.BlockSpec(memory_space=pl.ANY),
        scratch_shapes=[pltpu.VMEM((BM * PACK_ROWS, LANES), U32), pltpu.SemaphoreType.DMA(()),
                        pltpu.SemaphoreType.DMA(())],
    )
    return pl.pallas_call(
        kern,
        out_shape=jax.ShapeDtypeStruct((n_blocks * BM * PACK_ROWS, LANES), U32),
        grid_spec=grid_spec,
        compiler_params=pltpu.CompilerParams(dimension_semantics=("arbitrary",)),
        name="dispatch",
    )(cnt_row, pstart_row, nblk, dest, h2)


def _moe_kernel(blk_e_ref, nxt_e_ref, nblk_ref, x_hbm, wg_hbm, wu_hbm, wd_hbm, y_ref,
                x_ring, wg_l, wu_l, wd_l, wgu_s, wd_s, xsems, sems, cur_ref):
    b = pl.program_id(0)
    nblk = nblk_ref[0]

    def weight_copies(e, slot):
        return (pltpu.make_async_copy(wg_hbm.at[e], wg_l.at[slot], sems.at[slot, 0]),
                pltpu.make_async_copy(wu_hbm.at[e], wu_l.at[slot], sems.at[slot, 1]),
                pltpu.make_async_copy(wd_hbm.at[e], wd_l.at[slot], sems.at[slot, 2]))

    def row_copy(blk):
        slot = blk % X_RING
        return pltpu.make_async_copy(_packed_rows(x_hbm, blk * BM, BM), x_ring.at[slot], xsems.at[slot])

    @pl.when(b < nblk)
    def _():
        e = blk_e_ref[b]

        @pl.when(b == 0)
        def _():
            cur_ref[0] = 0
            for c in weight_copies(e, 0):
                c.start()
            for j in range(X_AHEAD):
                @pl.when(j < nblk)
                def _():
                    row_copy(j).start()

        @pl.when(b + X_AHEAD < nblk)
        def _():
            row_copy(b + X_AHEAD).start()

        @pl.when((b == 0) | (e != blk_e_ref[jnp.maximum(b - 1, 0)]))
        def _():
            @pl.when(b > 0)
            def _():
                cur_ref[0] = 1 - cur_ref[0]

            slot = cur_ref[0]
            for c in weight_copies(e, slot):
                c.wait()
            nxt = nxt_e_ref[b]

            @pl.when(nxt >= 0)
            def _():
                for c in weight_copies(nxt, 1 - slot):
                    c.start()

            wgu_s[:, :D_EXPERT] = wg_l[slot].astype(BF16)
            wgu_s[:, D_EXPERT:] = wu_l[slot].astype(BF16)
            wd_s[...] = wd_l[slot].astype(BF16)

        row_copy(b).wait()
        x = _unpack_bf16_pairs(x_ring.at[b % X_RING], BM)
        gu = jnp.dot(x, wgu_s[...], preferred_element_type=F32)
        h = (jax.nn.silu(gu[:, :D_EXPERT]) * gu[:, D_EXPERT:]).astype(BF16)
        y = jnp.dot(h, wd_s[...], preferred_element_type=F32)
        for j in range(SUBLANES):
            y_ref[pl.ds(j, BM, stride=SUBLANES), :] = y[:, j * LANES:(j + 1) * LANES]

    @pl.when(b >= nblk)
    def _():
        y_ref[...] = jnp.zeros_like(y_ref)


def _moe(blk_e, nxt_e, nblk, xs, wg, wu, wd):
    d = D_MODEL
    n_blocks = xs.shape[0] // (BM * PACK_ROWS)
    grid_spec = pltpu.PrefetchScalarGridSpec(
        num_scalar_prefetch=3,
        grid=(n_blocks,),
        in_specs=[
            pl.BlockSpec(memory_space=pl.ANY),
            pl.BlockSpec(memory_space=pl.ANY),
            pl.BlockSpec(memory_space=pl.ANY),
            pl.BlockSpec(memory_space=pl.ANY),
        ],
        out_specs=pl.BlockSpec((BM * SUBLANES, LANES), lambda b, *_: (b, 0)),
        scratch_shapes=[
            pltpu.VMEM((X_RING, BM * PACK_ROWS, LANES), U32),
            pltpu.VMEM((2, d, D_EXPERT), F32), pltpu.VMEM((2, d, D_EXPERT), F32), pltpu.VMEM((2, D_EXPERT, d), F32),
            pltpu.VMEM((d, 2 * D_EXPERT), BF16), pltpu.VMEM((D_EXPERT, d), BF16),
            pltpu.SemaphoreType.DMA((X_RING,)), pltpu.SemaphoreType.DMA((2, 3)), pltpu.SMEM((1,), I32),
        ],
    )
    return pl.pallas_call(
        _moe_kernel,
        out_shape=jax.ShapeDtypeStruct((n_blocks * BM * SUBLANES, LANES), F32),
        grid_spec=grid_spec,
        compiler_params=pltpu.CompilerParams(dimension_semantics=("arbitrary",),
                                             vmem_limit_bytes=VMEM_LIMIT_BYTES),
        name="moe",
    )(blk_e, nxt_e, nblk, xs, wg, wu, wd)


def _final_kernel(dest_ref, dnext_ref, xs_ref, w_ref, gp_ref, gs_ref, fp_ref, fs_ref, gf_ref, y_hbm,
                  op_ref, os_ref, buf, sems, *, n_tiles, n_ptiles, tiles_per_seq):
    i = pl.program_id(0)
    d = D_MODEL
    slot = i % 2

    def gather(d_ref, s):
        def issue(t, carry):
            for k in range(TOP_K):
                pltpu.make_async_copy(_rows(y_hbm, d_ref[k, t], 1), _rows(buf.at[s, k], t, 1),
                                      sems.at[s]).start(priority=k % DMA_THREADS)
            return carry

        lax.fori_loop(0, TC, issue, 0)

    @pl.when(i == 0)
    def _():
        gather(dest_ref, 0)

    @pl.when(i + 1 < n_tiles)
    def _():
        gather(dnext_ref, 1 - slot)

    for k in range(TOP_K):
        pltpu.make_async_copy(_rows(y_hbm, 0, TC), buf.at[slot, k], sems.at[slot]).wait()

    w = w_ref[...]
    acc = w[:, 0:1] * _row_major(buf.at[slot, 0], TC)
    for k in range(1, TOP_K):
        acc = acc + w[:, k:k + 1] * _row_major(buf.at[slot, k], TC)

    def finish(gate2, shift, scale, o_ref):
        x2 = xs_ref[...] + gate2 * acc
        o_ref[...] = _rms(x2, gf_ref[...]) * (1.0 + scale) + shift

    @pl.when(i < n_ptiles)
    def _():
        b = i // tiles_per_seq
        f = fp_ref[pl.ds(b, 1), :]
        finish(gp_ref[pl.ds(b, 1), :], f[:, :d], f[:, d:], op_ref)

    @pl.when(i >= n_ptiles)
    def _():
        f = fs_ref[...]
        finish(gs_ref[...], f[:, :d], f[:, d:], os_ref)


def _final(dest, xs, wts, m_all, mf_all, gf, y_sorted, *, n_batch, seq, t_p):
    t_all, d = xs.shape
    n_tiles = t_all // TC
    n_ptiles = t_p // TC
    dec_batch = m_all.shape[0] - n_batch
    kern = functools.partial(_final_kernel, n_tiles=n_tiles, n_ptiles=n_ptiles, tiles_per_seq=seq // TC)
    pb = dec_batch // n_batch
    return pl.pallas_call(
        kern,
        out_shape=[jax.ShapeDtypeStruct((t_p, d), F32), jax.ShapeDtypeStruct((t_all - t_p, d), F32)],
        grid=(n_tiles,),
        in_specs=[
            pl.BlockSpec((TOP_K, TC), lambda i: (0, i), memory_space=pltpu.SMEM),
            pl.BlockSpec((TOP_K, TC), lambda i: (0, jnp.minimum(i + 1, n_tiles - 1)), memory_space=pltpu.SMEM),
            pl.BlockSpec((TC, d), lambda i: (i, 0)),
            pl.BlockSpec((TC, TOP_K), lambda i: (i, 0)),
            pl.BlockSpec((n_batch, d), lambda i: (pb, N_MOD - 1)),
            pl.BlockSpec((dec_batch, d), lambda i: (0, N_MOD - 1)),
            pl.BlockSpec((n_batch, 2 * d), lambda i: (pb, 0)),
            pl.BlockSpec((dec_batch, 2 * d), lambda i: (0, 0)),
            _const_spec(gf.shape),
            pl.BlockSpec(memory_space=pl.ANY),
        ],
        out_specs=[
            pl.BlockSpec((TC, d), lambda i: (jnp.minimum(i, n_ptiles - 1), 0)),
            pl.BlockSpec((TC, d), lambda i: (jnp.maximum(i - n_ptiles, 0), 0)),
        ],
        scratch_shapes=[pltpu.VMEM((2, TOP_K, TC * SUBLANES, LANES), F32), pltpu.SemaphoreType.DMA((2,))],
        compiler_params=pltpu.CompilerParams(dimension_semantics=("arbitrary",),
                                             vmem_limit_bytes=VMEM_LIMIT_BYTES),
        name="final",
    )(dest, dest, xs, wts, m_all, m_all, mf_all, mf_all, gf, y_sorted)


def kernel(x_prompt, x_sample, state_conv, c_prompt, c_sample, w_ada, b_ada, g_norm1, w_in, w_spatial, b_spatial, g_v, w_conv, g_out_a, g_out_b, w_out, g_norm2, w_router, b_router, w_exp_gate, w_exp_up, w_exp_down, w_sh_gate, w_sh_up, w_sh_down, w_ada_final, b_ada_final, g_final):
    n_batch, seq, d = x_prompt.shape
    dec_batch, dec_seq, _ = x_sample.shape
    assert w_ada.shape[0] == 1, "one layer only"
    assert d == D_MODEL and dec_seq == DEC_SEQ and dec_batch * dec_seq == TM and seq % TM == 0
    assert n_batch % SUBLANES == 0 and dec_batch % n_batch == 0
    t_p = n_batch * seq
    t_s = dec_batch * dec_seq
    t_all = t_p + t_s

    c_all = jnp.concatenate([c_sample, c_prompt], axis=0)
    m_all = _ada(c_all, w_ada[0], b_ada[0])
    mf_all = _ada(c_all, w_ada_final, b_ada_final)

    rep = functools.partial(jnp.repeat, repeats=A_HEAD_DIM, axis=1)
    tril = jnp.tril(jnp.ones((CHUNK, CHUNK), dtype=bool))
    wt = jnp.where(tril, w_spatial[0], 0.0).astype(BF16)
    bsp = rep(b_spatial[0].T)
    csp = rep(jnp.transpose(w_spatial[0][:, :DEC_SEQ, :DEC_SEQ], (1, 2, 0)).reshape(DEC_SEQ * DEC_SEQ, A_HEADS))
    bsps = rep(b_spatial[0][:, :DEC_SEQ].T)
    st = jnp.transpose(state_conv[0], (1, 0, 2))
    wshgu = jnp.concatenate([w_sh_gate[0], w_sh_up[0]], axis=1).astype(BF16)
    xsm = jnp.transpose(x_sample, (1, 0, 2)).reshape(t_s, d)

    xs, h2, lt, cztail, vs, cz23 = _mixer(
        x_prompt.reshape(t_p, d), xsm, m_all, g_norm1, w_in[0].astype(BF16), wt, bsp,
        g_v.reshape(1, MIX_A), w_conv[0], g_out_a, g_out_b, w_out[0].astype(BF16), g_norm2,
        w_router[0].T.astype(BF16), wshgu, w_sh_down[0].astype(BF16), csp, bsps, st,
        n_batch=n_batch, seq=seq)

    eidx, wts, rank, cnt = _route(lt, b_router[0].reshape(N_EXPERTS, 1))

    n_blocks = (t_all * TOP_K) // BM + N_EXPERTS
    dest, blk_e, nxt_e, nblk, cnt_row, pstart_row = _plan(cnt, eidx, rank, n_blocks)

    x_sorted = _dispatch(cnt_row[0], pstart_row[0], nblk[0, :1], dest, h2, n_blocks)
    y_sorted = _moe(blk_e[0, :n_blocks], nxt_e[0, :n_blocks], nblk[0, :1], x_sorted,
                    w_exp_gate[0], w_exp_up[0], w_exp_down[0])
    y_p, y_s = _final(dest, xs, wts.T, m_all, mf_all, g_final.reshape(1, d), y_sorted,
                      n_batch=n_batch, seq=seq, t_p=t_p)

    tiles_per_seq = seq // TM
    y_prompt = y_p.reshape(n_batch, seq, d)
    y_sample = jnp.transpose(y_s.reshape(dec_seq, dec_batch, d), (1, 0, 2))
    conv_p = cztail[tiles_per_seq - 1:n_batch * tiles_per_seq:tiles_per_seq, SUBLANES - (CONV_W - 1):, :][None]
    conv_s = jnp.transpose(cz23, (1, 0, 2))[None]
    v_s = jnp.transpose(vs.reshape(dec_seq, dec_batch, A_HEADS, A_HEAD_DIM), (1, 0, 2, 3))[None]
    return (y_prompt, y_sample, conv_p, conv_s, v_s)
```

```python
import functools

import jax
import jax.numpy as jnp
from jax import lax
from jax.experimental import pallas as pl
from jax.experimental.pallas import tpu as pltpu

F32 = jnp.float32
BF16 = jnp.bfloat16
I32 = jnp.int32
U32 = jnp.uint32

D_MODEL = 1024
MIX_A = 512
A_HEADS = 4
A_HEAD_DIM = 128
CHUNK = 128
CONV_DIM = 512
CONV_W = 3
PROJ_DIM = 2 * MIX_A + 3 * CONV_DIM
N_EXPERTS = 256
TOP_K = 8
N_GROUPS = 8
TOPK_GROUPS = 4
GROUP_SIZE = N_EXPERTS // N_GROUPS
D_EXPERT = 256
D_SHARED = 256
ROUTED_SCALE = 2.5
N_MOD = 6
RMS_EPS = 1e-6
DEC_SEQ = 4

LANES = 128
SUBLANES = 8
PACK_ROWS = D_MODEL // 2 // LANES
TM = 512
BM = 256
TC = 128
DMA_THREADS = 2
MOE_SUB = 4
X_AHEAD = 2
X_RING = X_AHEAD + 1
VMEM_LIMIT_BYTES = 56 * 1024 * 1024


def _rms(x, g):
    return x * lax.rsqrt(jnp.mean(x * x, axis=-1, keepdims=True) + RMS_EPS) * g


def _rows(ref, row, n):
    return ref.at[pl.ds(pl.multiple_of(row * SUBLANES, SUBLANES), n * SUBLANES)]


def _row_major(ref, n):
    return jnp.concatenate([ref[pl.ds(j, n, stride=SUBLANES), :] for j in range(SUBLANES)], axis=1)


def _packed_rows(ref, row, n):
    return ref.at[pl.ds(pl.multiple_of(row * PACK_ROWS, PACK_ROWS), n * PACK_ROWS)]


def _pack_bf16_pairs(x):
    half = D_MODEL // 2
    xb = x.astype(BF16).astype(F32)
    lo = lax.shift_right_logical(lax.bitcast_convert_type(xb[:, :half], U32), jnp.uint32(16))
    return lo | lax.bitcast_convert_type(xb[:, half:], U32)


def _unpack_bf16_pairs(ref, n):
    words = [ref[pl.ds(j, n, stride=PACK_ROWS), :] for j in range(PACK_ROWS)]
    lo = [lax.bitcast_convert_type(lax.shift_left(w, jnp.uint32(16)), F32) for w in words]
    hi = [lax.bitcast_convert_type(w & jnp.uint32(0xFFFF0000), F32) for w in words]
    return jnp.concatenate(lo + hi, axis=1).astype(BF16)


def _const_spec(shape):
    nd = len(shape)
    return pl.BlockSpec(shape, lambda *_: (0,) * nd, pipeline_mode=pl.Buffered(1))


def _ada_kernel(c_ref, w_ref, b_ref, o_ref):
    a = jax.nn.silu(c_ref[...]).astype(BF16)
    o_ref[...] = jnp.dot(a, w_ref[...].astype(BF16), preferred_element_type=F32) + b_ref[...]


def _ada(c_all, w, b):
    rows, n = c_all.shape[0], w.shape[1]
    return pl.pallas_call(
        _ada_kernel,
        out_shape=jax.ShapeDtypeStruct((rows, n), F32),
        grid=(n // D_MODEL,),
        in_specs=[
            pl.BlockSpec((rows, D_MODEL), lambda j: (0, 0)),
            pl.BlockSpec((D_MODEL, D_MODEL), lambda j: (0, j)),
            pl.BlockSpec((1, D_MODEL), lambda j: (0, j)),
        ],
        out_specs=pl.BlockSpec((rows, D_MODEL), lambda j: (0, j)),
        compiler_params=pltpu.CompilerParams(dimension_semantics=("arbitrary",)),
        name="ada",
    )(c_all, w, b.reshape(1, n))


def _mixer_kernel(xp_ref, xsm_ref, mp_ref, ms_ref, g1_ref, win_ref, wt_ref, bsp_ref, gv_ref, wconv_ref,
                  goa_ref, gob_ref, wout_ref, g2_ref, wrt_ref, wshgu_ref, wshd_ref, csp_ref, bsps_ref, st_ref,
                  xs_out, h2_out, lt_out, cztail_out, vs_out, cz23_out, carry_ref, *, n_ptiles, tiles_per_seq):
    i = pl.program_id(0)
    d = D_MODEL

    def proj(x, sh1, sc1):
        h = _rms(x, g1_ref[...]) * (1.0 + sc1) + sh1
        p = jnp.dot(h.astype(BF16), win_ref[...], preferred_element_type=F32)
        u = jax.nn.gelu(p[:, :MIX_A])
        v = jax.nn.gelu(p[:, MIX_A:2 * MIX_A])
        vn = jnp.concatenate(
            [_rms(v[:, h * LANES:(h + 1) * LANES], gv_ref[:, h * LANES:(h + 1) * LANES]) for h in range(A_HEADS)],
            axis=1)
        o = 2 * MIX_A
        return u, vn, p[:, o:o + CONV_DIM], p[:, o + CONV_DIM:o + 2 * CONV_DIM], p[:, o + 2 * CONV_DIM:]

    def tail(x, ya, yb, gate1, sh2, sc2, gate2):
        cat = jnp.concatenate([_rms(ya, goa_ref[...]), _rms(yb, gob_ref[...])], axis=1).astype(BF16)
        x1 = x + gate1 * jnp.dot(cat, wout_ref[...], preferred_element_type=F32)
        h2 = _rms(x1, g2_ref[...]) * (1.0 + sc2) + sh2
        h2b = h2.astype(BF16)
        lt_out[...] = lax.dot_general(wrt_ref[...], h2b, (((1,), (1,)), ((), ())), preferred_element_type=F32)
        gu = jnp.dot(h2b, wshgu_ref[...], preferred_element_type=F32)
        hs = (jax.nn.silu(gu[:, :D_SHARED]) * gu[:, D_SHARED:]).astype(BF16)
        xs_out[...] = x1 + gate2 * jnp.dot(hs, wshd_ref[...], preferred_element_type=F32)
        packed = _pack_bf16_pairs(h2)
        for j in range(PACK_ROWS):
            h2_out[pl.ds(j, TM, stride=PACK_ROWS), :] = packed[:, j * LANES:(j + 1) * LANES]

    @pl.when(i < n_ptiles)
    def _prompt():
        b = i // tiles_per_seq
        m = mp_ref[pl.ds(b, 1), :]
        mod = [m[:, k * d:(k + 1) * d] for k in range(N_MOD)]
        x = xp_ref[...]
        u, vn, bg, cg, z = proj(x, mod[0], mod[1])
        n_chunks = TM // CHUNK
        ya_cols = []
        for h in range(A_HEADS):
            vh = vn[:, h * LANES:(h + 1) * LANES].astype(BF16)
            rhs = jnp.concatenate([vh[c * CHUNK:(c + 1) * CHUNK] for c in range(n_chunks)], axis=1)
            mix = jnp.dot(wt_ref[h], rhs, preferred_element_type=F32)
            bias = bsp_ref[:, h * LANES:(h + 1) * LANES]
            mix = jnp.concatenate([mix[:, c * LANES:(c + 1) * LANES] + bias for c in range(n_chunks)], axis=0)
            ya_cols.append(u[:, h * LANES:(h + 1) * LANES] * mix)
        ya = jnp.concatenate(ya_cols, axis=1)
        cz = cg * z

        @pl.when(i % tiles_per_seq == 0)
        def _():
            carry_ref[...] = jnp.zeros_like(carry_ref)

        c6 = carry_ref[SUBLANES - 2:SUBLANES - 1, :]
        c7 = carry_ref[SUBLANES - 1:SUBLANES, :]
        r = lax.broadcasted_iota(I32, (TM, 1), 0)
        p1 = jnp.where(r == 0, c7, pltpu.roll(cz, 1, 0))
        p2 = jnp.where(r == 0, c6, jnp.where(r == 1, c7, pltpu.roll(cz, 2, 0)))
        wc = wconv_ref[...]
        yb = bg * (p2 * wc[0:1] + p1 * wc[1:2] + cz * wc[2:3])
        carry_ref[...] = cz[TM - SUBLANES:]
        cztail_out[0] = cz[TM - SUBLANES:]
        tail(x, ya, yb, mod[2], mod[3], mod[4], mod[5])

    @pl.when(i == n_ptiles)
    def _sample():
        ms = ms_ref[...]
        nb = TM // DEC_SEQ

        def mod(k):
            return jnp.concatenate([ms[:, k * d:(k + 1) * d]] * DEC_SEQ, axis=0)

        x = xsm_ref[...]
        u, vn, bg, cg, z = proj(x, mod(0), mod(1))
        vt = [vn[t * nb:(t + 1) * nb] for t in range(DEC_SEQ)]
        mixes = []
        for t in range(DEC_SEQ):
            acc = csp_ref[DEC_SEQ * t:DEC_SEQ * t + 1, :] * vt[0]
            for s in range(1, t + 1):
                acc = acc + csp_ref[DEC_SEQ * t + s:DEC_SEQ * t + s + 1, :] * vt[s]
            mixes.append(acc + bsps_ref[t:t + 1, :])
        ya = u * jnp.concatenate(mixes, axis=0)
        cz = cg * z
        czt = [cz[t * nb:(t + 1) * nb] for t in range(DEC_SEQ)]
        full = [st_ref[0], st_ref[1]] + czt
        wc = wconv_ref[...]
        yc = jnp.concatenate(
            [full[t] * wc[0:1] + full[t + 1] * wc[1:2] + full[t + 2] * wc[2:3] for t in range(DEC_SEQ)], axis=0)
        yb = bg * yc
        vs_out[...] = vn
        cz23_out[0] = czt[DEC_SEQ - 2]
        cz23_out[1] = czt[DEC_SEQ - 1]
        cztail_out[0] = cz[TM - SUBLANES:]
        tail(x, ya, yb, mod(2), mod(3), mod(4), mod(5))


def _mixer(xp, xsm, m_all, g1, win, wt, bsp, gv, wconv, goa, gob, wout, g2, wrt, wshgu, wshd, csp, bsps, st,
           *, n_batch, seq):
    t_p = xp.shape[0]
    n_ptiles = t_p // TM
    n_tiles = n_ptiles + 1
    t_all = n_tiles * TM
    dec_batch = xsm.shape[0] // DEC_SEQ
    d = D_MODEL
    kern = functools.partial(_mixer_kernel, n_ptiles=n_ptiles, tiles_per_seq=seq // TM)
    in_specs = [
        pl.BlockSpec((TM, d), lambda i: (jnp.minimum(i, n_ptiles - 1), 0)),
        _const_spec((TM, d)),
        pl.BlockSpec((n_batch, N_MOD * d), lambda i: (dec_batch // n_batch, 0), pipeline_mode=pl.Buffered(1)),
        pl.BlockSpec((dec_batch, N_MOD * d), lambda i: (0, 0), pipeline_mode=pl.Buffered(1)),
        _const_spec(g1.shape), _const_spec(win.shape), _const_spec(wt.shape), _const_spec(bsp.shape),
        _const_spec(gv.shape), _const_spec(wconv.shape), _const_spec(goa.shape), _const_spec(gob.shape),
        _const_spec(wout.shape), _const_spec(g2.shape), _const_spec(wrt.shape), _const_spec(wshgu.shape),
        _const_spec(wshd.shape), _const_spec(csp.shape), _const_spec(bsps.shape), _const_spec(st.shape),
    ]
    out_shape = [
        jax.ShapeDtypeStruct((t_all, d), F32),
        jax.ShapeDtypeStruct((t_all * PACK_ROWS, LANES), U32),
        jax.ShapeDtypeStruct((N_EXPERTS, t_all), F32),
        jax.ShapeDtypeStruct((n_tiles, SUBLANES, CONV_DIM), F32),
        jax.ShapeDtypeStruct((TM, MIX_A), F32),
        jax.ShapeDtypeStruct((2, dec_batch, CONV_DIM), F32),
    ]
    out_specs = [
        pl.BlockSpec((TM, d), lambda i: (i, 0)),
        pl.BlockSpec((TM * PACK_ROWS, LANES), lambda i: (i, 0)),
        pl.BlockSpec((N_EXPERTS, TM), lambda i: (0, i)),
        pl.BlockSpec((1, SUBLANES, CONV_DIM), lambda i: (i, 0, 0)),
        pl.BlockSpec((TM, MIX_A), lambda i: (0, 0)),
        pl.BlockSpec((2, dec_batch, CONV_DIM), lambda i: (0, 0, 0)),
    ]
    return pl.pallas_call(
        kern,
        out_shape=out_shape,
        grid=(n_tiles,),
        in_specs=in_specs,
        out_specs=out_specs,
        scratch_shapes=[pltpu.VMEM((SUBLANES, CONV_DIM), F32)],
        compiler_params=pltpu.CompilerParams(dimension_semantics=("arbitrary",),
                                             vmem_limit_bytes=VMEM_LIMIT_BYTES),
        name="mixer",
    )(xp, xsm, m_all, m_all, g1, win, wt, bsp, gv, wconv, goa, gob, wout, g2, wrt, wshgu, wshd, csp, bsps, st)


def _route_kernel(lt_ref, br_ref, e_out, w_out, r_out, cnt_out, carry_ref, tri_ref):
    i = pl.program_id(0)
    neg = -jnp.inf

    @pl.when(i == 0)
    def _():
        carry_ref[...] = jnp.zeros_like(carry_ref)
        a = lax.broadcasted_iota(I32, (TM, TM), 0)
        b = lax.broadcasted_iota(I32, (TM, TM), 1)
        tri_ref[...] = jnp.where(a < b, 1.0, 0.0).astype(BF16)

    scores = jax.nn.sigmoid(lt_ref[...])
    sel = scores + br_ref[...]
    iog = lax.broadcasted_iota(I32, (GROUP_SIZE, TM), 0)
    gs_rows = []
    for g in range(N_GROUPS):
        sg = sel[g * GROUP_SIZE:(g + 1) * GROUP_SIZE]
        m1 = jnp.max(sg, axis=0, keepdims=True)
        i1 = jnp.min(jnp.where(sg == m1, iog, GROUP_SIZE), axis=0, keepdims=True)
        m2 = jnp.max(jnp.where(iog == i1, neg, sg), axis=0, keepdims=True)
        gs_rows.append(m1 + m2)
    gs = jnp.concatenate(gs_rows, axis=0)
    io8 = lax.broadcasted_iota(I32, (N_GROUPS, TM), 0)
    keep = jnp.zeros((N_GROUPS, TM), F32)
    for _ in range(TOPK_GROUPS):
        m = jnp.max(gs, axis=0, keepdims=True)
        idx = jnp.min(jnp.where(gs == m, io8, N_GROUPS), axis=0, keepdims=True)
        hit = io8 == idx
        keep = jnp.where(hit, 1.0, keep)
        gs = jnp.where(hit, neg, gs)
    sel = jnp.concatenate(
        [jnp.where(keep[g:g + 1] > 0.0, sel[g * GROUP_SIZE:(g + 1) * GROUP_SIZE], neg) for g in range(N_GROUPS)],
        axis=0)
    ioe = lax.broadcasted_iota(I32, (N_EXPERTS, TM), 0)
    picked = jnp.zeros((N_EXPERTS, TM), F32)
    e_rows, w_rows = [], []
    for _ in range(TOP_K):
        m = jnp.max(sel, axis=0, keepdims=True)
        idx = jnp.min(jnp.where(sel == m, ioe, N_EXPERTS), axis=0, keepdims=True)
        hit = ioe == idx
        e_rows.append(idx)
        w_rows.append(jnp.sum(jnp.where(hit, scores, 0.0), axis=0, keepdims=True))
        picked = jnp.where(hit, 1.0, picked)
        sel = jnp.where(hit, neg, sel)
    wk = jnp.concatenate(w_rows, axis=0)
    w_out[...] = wk / jnp.sum(wk, axis=0, keepdims=True) * ROUTED_SCALE
    e_out[...] = jnp.concatenate(e_rows, axis=0)
    before = jnp.dot(picked.astype(BF16), tri_ref[...], preferred_element_type=F32) + carry_ref[...]
    r_rows = [jnp.sum(jnp.where(ioe == e_rows[k], before, 0.0), axis=0, keepdims=True) for k in range(TOP_K)]
    r_out[...] = jnp.concatenate(r_rows, axis=0).astype(I32)
    carry_ref[...] = carry_ref[...] + jnp.sum(picked, axis=1, keepdims=True)
    cnt_out[...] = carry_ref[...]


def _route(lt, br):
    t_all = lt.shape[1]
    n_tiles = t_all // TM
    return pl.pallas_call(
        _route_kernel,
        out_shape=[
            jax.ShapeDtypeStruct((TOP_K, t_all), I32),
            jax.ShapeDtypeStruct((TOP_K, t_all), F32),
            jax.ShapeDtypeStruct((TOP_K, t_all), I32),
            jax.ShapeDtypeStruct((N_EXPERTS, 1), F32),
        ],
        grid=(n_tiles,),
        in_specs=[pl.BlockSpec((N_EXPERTS, TM), lambda i: (0, i)), _const_spec((N_EXPERTS, 1))],
        out_specs=[
            pl.BlockSpec((TOP_K, TM), lambda i: (0, i)),
            pl.BlockSpec((TOP_K, TM), lambda i: (0, i)),
            pl.BlockSpec((TOP_K, TM), lambda i: (0, i)),
            pl.BlockSpec((N_EXPERTS, 1), lambda i: (0, 0)),
        ],
        scratch_shapes=[pltpu.VMEM((N_EXPERTS, 1), F32), pltpu.VMEM((TM, TM), BF16)],
        compiler_params=pltpu.CompilerParams(dimension_semantics=("arbitrary",)),
        name="route",
    )(lt, br)


def _plan_kernel(cnt_ref, e_ref, r_ref, dest_out, blk_e_out, nxt_e_out, nblk_out, cnt_row_out, pstart_row_out,
                 pstart_ref):
    i = pl.program_id(0)

    @pl.when(i == 0)
    def _():
        nb = jnp.floor((cnt_ref[...] + (BM - 1)) * (1.0 / BM))
        a = lax.broadcasted_iota(I32, (N_EXPERTS, N_EXPERTS), 0)
        b = lax.broadcasted_iota(I32, (N_EXPERTS, N_EXPERTS), 1)
        lower = jnp.where(b < a, 1.0, 0.0).astype(BF16)
        nb_l = jnp.broadcast_to(nb, (N_EXPERTS, LANES)).astype(BF16)
        first_blk = jnp.dot(lower, nb_l, preferred_element_type=F32)[:, 0:1]
        end_blk = first_blk + nb
        pstart_ref[...] = first_blk * BM
        cnt_row_out[...] = jnp.sum(jnp.where(a == b, cnt_ref[...], 0.0), axis=0, keepdims=True).astype(I32)
        pstart_row_out[...] = jnp.sum(jnp.where(a == b, first_blk * BM, 0.0), axis=0, keepdims=True).astype(I32)
        n_lanes = blk_e_out.shape[1]
        blk = lax.broadcasted_iota(I32, (N_EXPERTS, n_lanes), 1).astype(F32)
        owner = jnp.sum(jnp.where(end_blk <= blk, 1.0, 0.0), axis=0, keepdims=True)
        blk_e_out[...] = jnp.minimum(owner, N_EXPERTS - 1.0).astype(I32)
        total = jnp.max(end_blk, axis=0, keepdims=True)
        nblk_out[...] = jnp.broadcast_to(total, nblk_out.shape).astype(I32)
        group_end = jnp.min(jnp.where(end_blk > blk, end_blk, 2.0 * n_lanes), axis=0, keepdims=True)
        nxt = jnp.sum(jnp.where(end_blk <= group_end, 1.0, 0.0), axis=0, keepdims=True)
        nxt_e_out[...] = jnp.where(group_end < total, nxt, -1.0).astype(I32)

    ioe = lax.broadcasted_iota(I32, (N_EXPERTS, TM), 0)
    e = e_ref[...]
    rows = [jnp.sum(jnp.where(ioe == e[k:k + 1], pstart_ref[...], 0.0), axis=0, keepdims=True)
            for k in range(TOP_K)]
    dest_out[...] = jnp.concatenate(rows, axis=0).astype(I32) + r_ref[...]


def _plan(cnt, eidx, rank, n_blocks):
    t_all = eidx.shape[1]
    n_lanes = pl.cdiv(n_blocks, LANES) * LANES
    return pl.pallas_call(
        _plan_kernel,
        out_shape=[
            jax.ShapeDtypeStruct((TOP_K, t_all), I32),
            jax.ShapeDtypeStruct((1, n_lanes), I32),
            jax.ShapeDtypeStruct((1, n_lanes), I32),
            jax.ShapeDtypeStruct((1, LANES), I32),
            jax.ShapeDtypeStruct((1, N_EXPERTS), I32),
            jax.ShapeDtypeStruct((1, N_EXPERTS), I32),
        ],
        grid=(t_all // TM,),
        in_specs=[
            _const_spec((N_EXPERTS, 1)),
            pl.BlockSpec((TOP_K, TM), lambda i: (0, i)),
            pl.BlockSpec((TOP_K, TM), lambda i: (0, i)),
        ],
        out_specs=[
            pl.BlockSpec((TOP_K, TM), lambda i: (0, i)),
            pl.BlockSpec((1, n_lanes), lambda i: (0, 0)),
            pl.BlockSpec((1, n_lanes), lambda i: (0, 0)),
            pl.BlockSpec((1, LANES), lambda i: (0, 0)),
            pl.BlockSpec((1, N_EXPERTS), lambda i: (0, 0)),
            pl.BlockSpec((1, N_EXPERTS), lambda i: (0, 0)),
        ],
        scratch_shapes=[pltpu.VMEM((N_EXPERTS, 1), F32)],
        compiler_params=pltpu.CompilerParams(dimension_semantics=("arbitrary",)),
        name="plan",
    )(cnt, eidx, rank)


def _dispatch_kernel(cnt_ref, pst_ref, nblk_ref, dest_ref, h2_ref, xs_hbm, zeros, sem, fill_sem,
                     *, n_steps, n_blocks):
    i = pl.program_id(0)
    per_step = pl.cdiv(N_EXPERTS, n_steps)

    @pl.when(i == 0)
    def _():
        zeros[...] = jnp.zeros_like(zeros)

    def issue(t, carry):
        for k in range(TOP_K):
            pltpu.make_async_copy(_packed_rows(h2_ref, t, 1), _packed_rows(xs_hbm, dest_ref[k, t], 1), sem).start(
                priority=k % DMA_THREADS)
        return carry

    lax.fori_loop(0, TM, issue, 0)

    def fills(do):
        def per_expert(j, carry):
            e = i * per_step + j

            @pl.when(e < N_EXPERTS)
            def _():
                cnt = cnt_ref[e]
                base = pst_ref[e]
                padded = (cnt + (BM - 1)) // BM * BM
                mid = jnp.minimum((cnt + (SUBLANES - 1)) // SUBLANES * SUBLANES, padded)

                def one(r, c):
                    do(pltpu.make_async_copy(_packed_rows(zeros, 0, 1), _packed_rows(xs_hbm, base + r, 1), fill_sem))
                    return c

                lax.fori_loop(cnt, mid, one, 0)

                def eight(q, c):
                    r = base + mid + q * SUBLANES
                    do(pltpu.make_async_copy(_packed_rows(zeros, 0, SUBLANES), _packed_rows(xs_hbm, r, SUBLANES),
                                             fill_sem))
                    return c

                lax.fori_loop(0, (padded - mid) // SUBLANES, eight, 0)

            return carry

        lax.fori_loop(0, per_step, per_expert, 0)

        def per_block(j, carry):
            b = nblk_ref[0] + i * per_step + j

            @pl.when(b < n_blocks)
            def _():
                do(pltpu.make_async_copy(zeros, _packed_rows(xs_hbm, b * BM, BM), fill_sem))

            return carry

        lax.fori_loop(0, per_step, per_block, 0)

    fills(lambda c: c.start())
    for k in range(TOP_K):
        pltpu.make_async_copy(h2_ref, _packed_rows(xs_hbm, 0, TM), sem).wait()
    fills(lambda c: c.wait())


def _dispatch(cnt_row, pstart_row, nblk, dest, h2, n_blocks):
    t_all = h2.shape[0] // PACK_ROWS
    n_steps = t_all // TM
    kern = functools.partial(_dispatch_kernel, n_steps=n_steps, n_blocks=n_blocks)
    grid_spec = pltpu.PrefetchScalarGridSpec(
        num_scalar_prefetch=3,
        grid=(n_steps,),
        in_specs=[
            pl.BlockSpec((TOP_K, TM), lambda i, *_: (0, i), memory_space=pltpu.SMEM),
            pl.BlockSpec((TM * PACK_ROWS, LANES), lambda i, *_: (i, 0)),
        ],
        out_specs=pl.BlockSpec(memory_space=pl.ANY),
        scratch_shapes=[pltpu.VMEM((BM * PACK_ROWS, LANES), U32), pltpu.SemaphoreType.DMA(()),
                        pltpu.SemaphoreType.DMA(())],
    )
    return pl.pallas_call(
        kern,
        out_shape=jax.ShapeDtypeStruct((n_blocks * BM * PACK_ROWS, LANES), U32),
        grid_spec=grid_spec,
        compiler_params=pltpu.CompilerParams(dimension_semantics=("arbitrary",)),
        name="dispatch",
    )(cnt_row, pstart_row, nblk, dest, h2)


def _moe_kernel(blk_e_ref, nxt_e_ref, nblk_ref, x_hbm, wg_hbm, wu_hbm, wd_hbm, y_ref,
                x_ring, wg_l, wu_l, wd_l, wgu_s, wd_s, xsems, sems, cur_ref):
    nblk = nblk_ref[0]

    def weight_copies(e, slot):
        return (pltpu.make_async_copy(wg_hbm.at[e], wg_l.at[slot], sems.at[slot, 0]),
                pltpu.make_async_copy(wu_hbm.at[e], wu_l.at[slot], sems.at[slot, 1]),
                pltpu.make_async_copy(wd_hbm.at[e], wd_l.at[slot], sems.at[slot, 2]))

    def row_copy(blk):
        slot = blk % X_RING
        return pltpu.make_async_copy(_packed_rows(x_hbm, blk * BM, BM), x_ring.at[slot], xsems.at[slot])

    def one_block(sub, carry):
        b = pl.program_id(0) * MOE_SUB + sub
        y_blk = y_ref.at[pl.ds(pl.multiple_of(sub * (BM * SUBLANES), BM * SUBLANES), BM * SUBLANES)]
        _moe_block(b, nblk, blk_e_ref, nxt_e_ref, y_blk, x_ring, wg_l, wu_l, wd_l, wgu_s, wd_s, cur_ref,
                   weight_copies, row_copy)
        return carry

    lax.fori_loop(0, MOE_SUB, one_block, 0)


def _moe_block(b, nblk, blk_e_ref, nxt_e_ref, y_ref, x_ring, wg_l, wu_l, wd_l, wgu_s, wd_s, cur_ref,
               weight_copies, row_copy):
    @pl.when(b < nblk)
    def _():
        e = blk_e_ref[b]

        @pl.when(b == 0)
        def _():
            cur_ref[0] = 0
            for c in weight_copies(e, 0):
                c.start()
            for j in range(X_AHEAD):
                @pl.when(j < nblk)
                def _():
                    row_copy(j).start()

        @pl.when(b + X_AHEAD < nblk)
        def _():
            row_copy(b + X_AHEAD).start()

        @pl.when((b == 0) | (e != blk_e_ref[jnp.maximum(b - 1, 0)]))
        def _():
            @pl.when(b > 0)
            def _():
                cur_ref[0] = 1 - cur_ref[0]

            slot = cur_ref[0]
            for c in weight_copies(e, slot):
                c.wait()
            nxt = nxt_e_ref[b]

            @pl.when(nxt >= 0)
            def _():
                for c in weight_copies(nxt, 1 - slot):
                    c.start()

            wgu_s[:, :D_EXPERT] = wg_l[slot].astype(BF16)
            wgu_s[:, D_EXPERT:] = wu_l[slot].astype(BF16)
            wd_s[...] = wd_l[slot].astype(BF16)

        row_copy(b).wait()
        x = _unpack_bf16_pairs(x_ring.at[b % X_RING], BM)
        gu = jnp.dot(x, wgu_s[...], preferred_element_type=F32)
        h = (jax.nn.silu(gu[:, :D_EXPERT]) * gu[:, D_EXPERT:]).astype(BF16)
        y = jnp.dot(h, wd_s[...], preferred_element_type=F32)
        for j in range(SUBLANES):
            y_ref[pl.ds(j, BM, stride=SUBLANES), :] = y[:, j * LANES:(j + 1) * LANES]

    @pl.when(b >= nblk)
    def _():
        y_ref[...] = jnp.zeros_like(y_ref)


def _moe(blk_e, nxt_e, nblk, xs, wg, wu, wd):
    d = D_MODEL
    n_blocks = xs.shape[0] // (BM * PACK_ROWS)
    assert n_blocks % MOE_SUB == 0
    grid_spec = pltpu.PrefetchScalarGridSpec(
        num_scalar_prefetch=3,
        grid=(n_blocks // MOE_SUB,),
        in_specs=[
            pl.BlockSpec(memory_space=pl.ANY),
            pl.BlockSpec(memory_space=pl.ANY),
            pl.BlockSpec(memory_space=pl.ANY),
            pl.BlockSpec(memory_space=pl.ANY),
        ],
        out_specs=pl.BlockSpec((MOE_SUB * BM * SUBLANES, LANES), lambda s, *_: (s, 0)),
        scratch_shapes=[
            pltpu.VMEM((X_RING, BM * PACK_ROWS, LANES), U32),
            pltpu.VMEM((2, d, D_EXPERT), F32), pltpu.VMEM((2, d, D_EXPERT), F32), pltpu.VMEM((2, D_EXPERT, d), F32),
            pltpu.VMEM((d, 2 * D_EXPERT), BF16), pltpu.VMEM((D_EXPERT, d), BF16),
            pltpu.SemaphoreType.DMA((X_RING,)), pltpu.SemaphoreType.DMA((2, 3)), pltpu.SMEM((1,), I32),
        ],
    )
    return pl.pallas_call(
        _moe_kernel,
        out_shape=jax.ShapeDtypeStruct((n_blocks * BM * SUBLANES, LANES), F32),
        grid_spec=grid_spec,
        compiler_params=pltpu.CompilerParams(dimension_semantics=("arbitrary",),
                                             vmem_limit_bytes=VMEM_LIMIT_BYTES),
        name="moe",
    )(blk_e, nxt_e, nblk, xs, wg, wu, wd)


def _final_kernel(dest_ref, dnext_ref, xs_ref, w_ref, gp_ref, gs_ref, fp_ref, fs_ref, gf_ref, y_hbm,
                  op_ref, os_ref, buf, sems, *, n_tiles, n_ptiles, tiles_per_seq):
    i = pl.program_id(0)
    d = D_MODEL
    slot = i % 2

    def gather(d_ref, s):
        def issue(t, carry):
            for k in range(TOP_K):
                pltpu.make_async_copy(_rows(y_hbm, d_ref[k, t], 1), _rows(buf.at[s, k], t, 1),
                                      sems.at[s]).start(priority=k % DMA_THREADS)
            return carry

        lax.fori_loop(0, TC, issue, 0)

    @pl.when(i == 0)
    def _():
        gather(dest_ref, 0)

    @pl.when(i + 1 < n_tiles)
    def _():
        gather(dnext_ref, 1 - slot)

    for k in range(TOP_K):
        pltpu.make_async_copy(_rows(y_hbm, 0, TC), buf.at[slot, k], sems.at[slot]).wait()

    w = w_ref[...]
    acc = w[:, 0:1] * _row_major(buf.at[slot, 0], TC)
    for k in range(1, TOP_K):
        acc = acc + w[:, k:k + 1] * _row_major(buf.at[slot, k], TC)

    def finish(gate2, shift, scale, o_ref):
        x2 = xs_ref[...] + gate2 * acc
        o_ref[...] = _rms(x2, gf_ref[...]) * (1.0 + scale) + shift

    @pl.when(i < n_ptiles)
    def _():
        b = i // tiles_per_seq
        f = fp_ref[pl.ds(b, 1), :]
        finish(gp_ref[pl.ds(b, 1), :], f[:, :d], f[:, d:], op_ref)

    @pl.when(i >= n_ptiles)
    def _():
        f = fs_ref[...]
        finish(gs_ref[...], f[:, :d], f[:, d:], os_ref)


def _final(dest, xs, wts, m_all, mf_all, gf, y_sorted, *, n_batch, seq, t_p):
    t_all, d = xs.shape
    n_tiles = t_all // TC
    n_ptiles = t_p // TC
    dec_batch = m_all.shape[0] - n_batch
    kern = functools.partial(_final_kernel, n_tiles=n_tiles, n_ptiles=n_ptiles, tiles_per_seq=seq // TC)
    pb = dec_batch // n_batch
    return pl.pallas_call(
        kern,
        out_shape=[jax.ShapeDtypeStruct((t_p, d), F32), jax.ShapeDtypeStruct((t_all - t_p, d), F32)],
        grid=(n_tiles,),
        in_specs=[
            pl.BlockSpec((TOP_K, TC), lambda i: (0, i), memory_space=pltpu.SMEM),
            pl.BlockSpec((TOP_K, TC), lambda i: (0, jnp.minimum(i + 1, n_tiles - 1)), memory_space=pltpu.SMEM),
            pl.BlockSpec((TC, d), lambda i: (i, 0)),
            pl.BlockSpec((TC, TOP_K), lambda i: (i, 0)),
            pl.BlockSpec((n_batch, d), lambda i: (pb, N_MOD - 1)),
            pl.BlockSpec((dec_batch, d), lambda i: (0, N_MOD - 1)),
            pl.BlockSpec((n_batch, 2 * d), lambda i: (pb, 0)),
            pl.BlockSpec((dec_batch, 2 * d), lambda i: (0, 0)),
            _const_spec(gf.shape),
            pl.BlockSpec(memory_space=pl.ANY),
        ],
        out_specs=[
            pl.BlockSpec((TC, d), lambda i: (jnp.minimum(i, n_ptiles - 1), 0)),
            pl.BlockSpec((TC, d), lambda i: (jnp.maximum(i - n_ptiles, 0), 0)),
        ],
        scratch_shapes=[pltpu.VMEM((2, TOP_K, TC * SUBLANES, LANES), F32), pltpu.SemaphoreType.DMA((2,))],
        compiler_params=pltpu.CompilerParams(dimension_semantics=("arbitrary",),
                                             vmem_limit_bytes=VMEM_LIMIT_BYTES),
        name="final",
    )(dest, dest, xs, wts, m_all, m_all, mf_all, mf_all, gf, y_sorted)


def kernel(x_prompt, x_sample, state_conv, c_prompt, c_sample, w_ada, b_ada, g_norm1, w_in, w_spatial, b_spatial, g_v, w_conv, g_out_a, g_out_b, w_out, g_norm2, w_router, b_router, w_exp_gate, w_exp_up, w_exp_down, w_sh_gate, w_sh_up, w_sh_down, w_ada_final, b_ada_final, g_final):
    n_batch, seq, d = x_prompt.shape
    dec_batch, dec_seq, _ = x_sample.shape
    assert w_ada.shape[0] == 1, "one layer only"
    assert d == D_MODEL and dec_seq == DEC_SEQ and dec_batch * dec_seq == TM and seq % TM == 0
    assert n_batch % SUBLANES == 0 and dec_batch % n_batch == 0
    t_p = n_batch * seq
    t_s = dec_batch * dec_seq
    t_all = t_p + t_s

    c_all = jnp.concatenate([c_sample, c_prompt], axis=0)
    m_all = _ada(c_all, w_ada[0], b_ada[0])
    mf_all = _ada(c_all, w_ada_final, b_ada_final)

    rep = functools.partial(jnp.repeat, repeats=A_HEAD_DIM, axis=1)
    tril = jnp.tril(jnp.ones((CHUNK, CHUNK), dtype=bool))
    wt = jnp.where(tril, w_spatial[0], 0.0).astype(BF16)
    bsp = rep(b_spatial[0].T)
    csp = rep(jnp.transpose(w_spatial[0][:, :DEC_SEQ, :DEC_SEQ], (1, 2, 0)).reshape(DEC_SEQ * DEC_SEQ, A_HEADS))
    bsps = rep(b_spatial[0][:, :DEC_SEQ].T)
    st = jnp.transpose(state_conv[0], (1, 0, 2))
    wshgu = jnp.concatenate([w_sh_gate[0], w_sh_up[0]], axis=1).astype(BF16)
    xsm = jnp.transpose(x_sample, (1, 0, 2)).reshape(t_s, d)

    xs, h2, lt, cztail, vs, cz23 = _mixer(
        x_prompt.reshape(t_p, d), xsm, m_all, g_norm1, w_in[0].astype(BF16), wt, bsp,
        g_v.reshape(1, MIX_A), w_conv[0], g_out_a, g_out_b, w_out[0].astype(BF16), g_norm2,
        w_router[0].T.astype(BF16), wshgu, w_sh_down[0].astype(BF16), csp, bsps, st,
        n_batch=n_batch, seq=seq)

    eidx, wts, rank, cnt = _route(lt, b_router[0].reshape(N_EXPERTS, 1))

    n_blocks = (t_all * TOP_K) // BM + N_EXPERTS
    dest, blk_e, nxt_e, nblk, cnt_row, pstart_row = _plan(cnt, eidx, rank, n_blocks)

    x_sorted = _dispatch(cnt_row[0], pstart_row[0], nblk[0, :1], dest, h2, n_blocks)
    y_sorted = _moe(blk_e[0, :n_blocks], nxt_e[0, :n_blocks], nblk[0, :1], x_sorted,
                    w_exp_gate[0], w_exp_up[0], w_exp_down[0])
    y_p, y_s = _final(dest, xs, wts.T, m_all, mf_all, g_final.reshape(1, d), y_sorted,
                      n_batch=n_batch, seq=seq, t_p=t_p)

    tiles_per_seq = seq // TM
    y_prompt = y_p.reshape(n_batch, seq, d)
    y_sample = jnp.transpose(y_s.reshape(dec_seq, dec_batch, d), (1, 0, 2))
    conv_p = cztail[tiles_per_seq - 1:n_batch * tiles_per_seq:tiles_per_seq, SUBLANES - (CONV_W - 1):, :][None]
    conv_s = jnp.transpose(cz23, (1, 0, 2))[None]
    v_s = jnp.transpose(vs.reshape(dec_seq, dec_batch, A_HEADS, A_HEAD_DIM), (1, 0, 2, 3))[None]
    return (y_prompt, y_sample, conv_p, conv_s, v_s)
```

```python
import functools

import jax
import jax.numpy as jnp
from jax import lax
from jax.experimental import pallas as pl
from jax.experimental.pallas import tpu as pltpu

F32 = jnp.float32
BF16 = jnp.bfloat16
I32 = jnp.int32
U32 = jnp.uint32

D_MODEL = 1024
MIX_A = 512
A_HEADS = 4
A_HEAD_DIM = 128
CHUNK = 128
CONV_DIM = 512
CONV_W = 3
PROJ_DIM = 2 * MIX_A + 3 * CONV_DIM
N_EXPERTS = 256
TOP_K = 8
N_GROUPS = 8
TOPK_GROUPS = 4
GROUP_SIZE = N_EXPERTS // N_GROUPS
D_EXPERT = 256
D_SHARED = 256
ROUTED_SCALE = 2.5
N_MOD = 6
RMS_EPS = 1e-6
DEC_SEQ = 4

LANES = 128
SUBLANES = 8
PACK_ROWS = D_MODEL // 2 // LANES
TM = 512
BM = 256
TC = 128
DMA_THREADS = 2
MOE_SUB = 4
X_AHEAD = 2
X_RING = X_AHEAD + 1
VMEM_LIMIT_BYTES = 56 * 1024 * 1024


def _rms(x, g):
    return x * lax.rsqrt(jnp.mean(x * x, axis=-1, keepdims=True) + RMS_EPS) * g


def _packed_rows(ref, row, n):
    return ref.at[pl.ds(pl.multiple_of(row * PACK_ROWS, PACK_ROWS), n * PACK_ROWS)]


def _pack_bf16_pairs(x):
    half = D_MODEL // 2
    xb = x.astype(BF16).astype(F32)
    lo = lax.shift_right_logical(lax.bitcast_convert_type(xb[:, :half], U32), jnp.uint32(16))
    return lo | lax.bitcast_convert_type(xb[:, half:], U32)


def _store_packed(ref, packed, n):
    for j in range(PACK_ROWS):
        ref[pl.ds(j, n, stride=PACK_ROWS), :] = packed[:, j * LANES:(j + 1) * LANES]


def _unpack_bf16_pairs(ref, n):
    words = [ref[pl.ds(j, n, stride=PACK_ROWS), :] for j in range(PACK_ROWS)]
    lo = [lax.bitcast_convert_type(lax.shift_left(w, jnp.uint32(16)), F32) for w in words]
    hi = [lax.bitcast_convert_type(w & jnp.uint32(0xFFFF0000), F32) for w in words]
    return jnp.concatenate(lo + hi, axis=1)


def _const_spec(shape):
    nd = len(shape)
    return pl.BlockSpec(shape, lambda *_: (0,) * nd, pipeline_mode=pl.Buffered(1))


def _ada_kernel(c_ref, w_ref, b_ref, o_ref):
    a = jax.nn.silu(c_ref[...]).astype(BF16)
    o_ref[...] = jnp.dot(a, w_ref[...].astype(BF16), preferred_element_type=F32) + b_ref[...]


def _ada(c_all, w, b):
    rows, n = c_all.shape[0], w.shape[1]
    return pl.pallas_call(
        _ada_kernel,
        out_shape=jax.ShapeDtypeStruct((rows, n), F32),
        grid=(n // D_MODEL,),
        in_specs=[
            pl.BlockSpec((rows, D_MODEL), lambda j: (0, 0)),
            pl.BlockSpec((D_MODEL, D_MODEL), lambda j: (0, j)),
            pl.BlockSpec((1, D_MODEL), lambda j: (0, j)),
        ],
        out_specs=pl.BlockSpec((rows, D_MODEL), lambda j: (0, j)),
        compiler_params=pltpu.CompilerParams(dimension_semantics=("arbitrary",)),
        name="ada",
    )(c_all, w, b.reshape(1, n))


def _mixer_kernel(xp_ref, xsm_ref, mp_ref, ms_ref, g1_ref, win_ref, wt_ref, bsp_ref, gv_ref, wconv_ref,
                  goa_ref, gob_ref, wout_ref, g2_ref, wrt_ref, wshgu_ref, wshd_ref, csp_ref, bsps_ref, st_ref,
                  xs_out, h2_out, lt_out, cztail_out, vs_out, cz23_out, carry_ref, *, n_ptiles, tiles_per_seq):
    i = pl.program_id(0)
    d = D_MODEL

    def proj(x, sh1, sc1):
        h = _rms(x, g1_ref[...]) * (1.0 + sc1) + sh1
        p = jnp.dot(h.astype(BF16), win_ref[...], preferred_element_type=F32)
        u = jax.nn.gelu(p[:, :MIX_A])
        v = jax.nn.gelu(p[:, MIX_A:2 * MIX_A])
        vn = jnp.concatenate(
            [_rms(v[:, h * LANES:(h + 1) * LANES], gv_ref[:, h * LANES:(h + 1) * LANES]) for h in range(A_HEADS)],
            axis=1)
        o = 2 * MIX_A
        return u, vn, p[:, o:o + CONV_DIM], p[:, o + CONV_DIM:o + 2 * CONV_DIM], p[:, o + 2 * CONV_DIM:]

    def tail(x, ya, yb, gate1, sh2, sc2, gate2):
        cat = jnp.concatenate([_rms(ya, goa_ref[...]), _rms(yb, gob_ref[...])], axis=1).astype(BF16)
        x1 = x + gate1 * jnp.dot(cat, wout_ref[...], preferred_element_type=F32)
        h2 = _rms(x1, g2_ref[...]) * (1.0 + sc2) + sh2
        h2b = h2.astype(BF16)
        lt_out[...] = lax.dot_general(wrt_ref[...], h2b, (((1,), (1,)), ((), ())), preferred_element_type=F32)
        gu = jnp.dot(h2b, wshgu_ref[...], preferred_element_type=F32)
        hs = (jax.nn.silu(gu[:, :D_SHARED]) * gu[:, D_SHARED:]).astype(BF16)
        xs_out[...] = x1 + gate2 * jnp.dot(hs, wshd_ref[...], preferred_element_type=F32)
        _store_packed(h2_out, _pack_bf16_pairs(h2), TM)

    @pl.when(i < n_ptiles)
    def _prompt():
        b = i // tiles_per_seq
        m = mp_ref[pl.ds(b, 1), :]
        mod = [m[:, k * d:(k + 1) * d] for k in range(N_MOD)]
        x = xp_ref[...]
        u, vn, bg, cg, z = proj(x, mod[0], mod[1])
        n_chunks = TM // CHUNK
        ya_cols = []
        for h in range(A_HEADS):
            vh = vn[:, h * LANES:(h + 1) * LANES].astype(BF16)
            rhs = jnp.concatenate([vh[c * CHUNK:(c + 1) * CHUNK] for c in range(n_chunks)], axis=1)
            mix = jnp.dot(wt_ref[h], rhs, preferred_element_type=F32)
            bias = bsp_ref[:, h * LANES:(h + 1) * LANES]
            mix = jnp.concatenate([mix[:, c * LANES:(c + 1) * LANES] + bias for c in range(n_chunks)], axis=0)
            ya_cols.append(u[:, h * LANES:(h + 1) * LANES] * mix)
        ya = jnp.concatenate(ya_cols, axis=1)
        cz = cg * z

        @pl.when(i % tiles_per_seq == 0)
        def _():
            carry_ref[...] = jnp.zeros_like(carry_ref)

        c6 = carry_ref[SUBLANES - 2:SUBLANES - 1, :]
        c7 = carry_ref[SUBLANES - 1:SUBLANES, :]
        r = lax.broadcasted_iota(I32, (TM, 1), 0)
        p1 = jnp.where(r == 0, c7, pltpu.roll(cz, 1, 0))
        p2 = jnp.where(r == 0, c6, jnp.where(r == 1, c7, pltpu.roll(cz, 2, 0)))
        wc = wconv_ref[...]
        yb = bg * (p2 * wc[0:1] + p1 * wc[1:2] + cz * wc[2:3])
        carry_ref[...] = cz[TM - SUBLANES:]
        cztail_out[0] = cz[TM - SUBLANES:]
        tail(x, ya, yb, mod[2], mod[3], mod[4], mod[5])

    @pl.when(i == n_ptiles)
    def _sample():
        ms = ms_ref[...]
        nb = TM // DEC_SEQ

        def mod(k):
            return jnp.concatenate([ms[:, k * d:(k + 1) * d]] * DEC_SEQ, axis=0)

        x = xsm_ref[...]
        u, vn, bg, cg, z = proj(x, mod(0), mod(1))
        vt = [vn[t * nb:(t + 1) * nb] for t in range(DEC_SEQ)]
        mixes = []
        for t in range(DEC_SEQ):
            acc = csp_ref[DEC_SEQ * t:DEC_SEQ * t + 1, :] * vt[0]
            for s in range(1, t + 1):
                acc = acc + csp_ref[DEC_SEQ * t + s:DEC_SEQ * t + s + 1, :] * vt[s]
            mixes.append(acc + bsps_ref[t:t + 1, :])
        ya = u * jnp.concatenate(mixes, axis=0)
        cz = cg * z
        czt = [cz[t * nb:(t + 1) * nb] for t in range(DEC_SEQ)]
        full = [st_ref[0], st_ref[1]] + czt
        wc = wconv_ref[...]
        yc = jnp.concatenate(
            [full[t] * wc[0:1] + full[t + 1] * wc[1:2] + full[t + 2] * wc[2:3] for t in range(DEC_SEQ)], axis=0)
        yb = bg * yc
        vs_out[...] = vn
        cz23_out[0] = czt[DEC_SEQ - 2]
        cz23_out[1] = czt[DEC_SEQ - 1]
        cztail_out[0] = cz[TM - SUBLANES:]
        tail(x, ya, yb, mod(2), mod(3), mod(4), mod(5))


def _mixer(xp, xsm, m_all, g1, win, wt, bsp, gv, wconv, goa, gob, wout, g2, wrt, wshgu, wshd, csp, bsps, st,
           *, n_batch, seq):
    t_p = xp.shape[0]
    n_ptiles = t_p // TM
    n_tiles = n_ptiles + 1
    t_all = n_tiles * TM
    dec_batch = xsm.shape[0] // DEC_SEQ
    d = D_MODEL
    kern = functools.partial(_mixer_kernel, n_ptiles=n_ptiles, tiles_per_seq=seq // TM)
    in_specs = [
        pl.BlockSpec((TM, d), lambda i: (jnp.minimum(i, n_ptiles - 1), 0)),
        _const_spec((TM, d)),
        pl.BlockSpec((n_batch, N_MOD * d), lambda i: (dec_batch // n_batch, 0), pipeline_mode=pl.Buffered(1)),
        pl.BlockSpec((dec_batch, N_MOD * d), lambda i: (0, 0), pipeline_mode=pl.Buffered(1)),
        _const_spec(g1.shape), _const_spec(win.shape), _const_spec(wt.shape), _const_spec(bsp.shape),
        _const_spec(gv.shape), _const_spec(wconv.shape), _const_spec(goa.shape), _const_spec(gob.shape),
        _const_spec(wout.shape), _const_spec(g2.shape), _const_spec(wrt.shape), _const_spec(wshgu.shape),
        _const_spec(wshd.shape), _const_spec(csp.shape), _const_spec(bsps.shape), _const_spec(st.shape),
    ]
    out_shape = [
        jax.ShapeDtypeStruct((t_all, d), F32),
        jax.ShapeDtypeStruct((t_all * PACK_ROWS, LANES), U32),
        jax.ShapeDtypeStruct((N_EXPERTS, t_all), F32),
        jax.ShapeDtypeStruct((n_tiles, SUBLANES, CONV_DIM), F32),
        jax.ShapeDtypeStruct((TM, MIX_A), F32),
        jax.ShapeDtypeStruct((2, dec_batch, CONV_DIM), F32),
    ]
    out_specs = [
        pl.BlockSpec((TM, d), lambda i: (i, 0)),
        pl.BlockSpec((TM * PACK_ROWS, LANES), lambda i: (i, 0)),
        pl.BlockSpec((N_EXPERTS, TM), lambda i: (0, i)),
        pl.BlockSpec((1, SUBLANES, CONV_DIM), lambda i: (i, 0, 0)),
        pl.BlockSpec((TM, MIX_A), lambda i: (0, 0)),
        pl.BlockSpec((2, dec_batch, CONV_DIM), lambda i: (0, 0, 0)),
    ]
    return pl.pallas_call(
        kern,
        out_shape=out_shape,
        grid=(n_tiles,),
        in_specs=in_specs,
        out_specs=out_specs,
        scratch_shapes=[pltpu.VMEM((SUBLANES, CONV_DIM), F32)],
        compiler_params=pltpu.CompilerParams(dimension_semantics=("arbitrary",),
                                             vmem_limit_bytes=VMEM_LIMIT_BYTES),
        name="mixer",
    )(xp, xsm, m_all, m_all, g1, win, wt, bsp, gv, wconv, goa, gob, wout, g2, wrt, wshgu, wshd, csp, bsps, st)


def _route_kernel(lt_ref, br_ref, e_out, w_out, r_out, cnt_out, carry_ref, tri_ref):
    i = pl.program_id(0)
    neg = -jnp.inf

    @pl.when(i == 0)
    def _():
        carry_ref[...] = jnp.zeros_like(carry_ref)
        a = lax.broadcasted_iota(I32, (TM, TM), 0)
        b = lax.broadcasted_iota(I32, (TM, TM), 1)
        tri_ref[...] = jnp.where(a < b, 1.0, 0.0).astype(BF16)

    scores = jax.nn.sigmoid(lt_ref[...])
    sel = scores + br_ref[...]
    iog = lax.broadcasted_iota(I32, (GROUP_SIZE, TM), 0)
    gs_rows = []
    for g in range(N_GROUPS):
        sg = sel[g * GROUP_SIZE:(g + 1) * GROUP_SIZE]
        m1 = jnp.max(sg, axis=0, keepdims=True)
        i1 = jnp.min(jnp.where(sg == m1, iog, GROUP_SIZE), axis=0, keepdims=True)
        m2 = jnp.max(jnp.where(iog == i1, neg, sg), axis=0, keepdims=True)
        gs_rows.append(m1 + m2)
    gs = jnp.concatenate(gs_rows, axis=0)
    io8 = lax.broadcasted_iota(I32, (N_GROUPS, TM), 0)
    keep = jnp.zeros((N_GROUPS, TM), F32)
    for _ in range(TOPK_GROUPS):
        m = jnp.max(gs, axis=0, keepdims=True)
        idx = jnp.min(jnp.where(gs == m, io8, N_GROUPS), axis=0, keepdims=True)
        hit = io8 == idx
        keep = jnp.where(hit, 1.0, keep)
        gs = jnp.where(hit, neg, gs)
    sel = jnp.concatenate(
        [jnp.where(keep[g:g + 1] > 0.0, sel[g * GROUP_SIZE:(g + 1) * GROUP_SIZE], neg) for g in range(N_GROUPS)],
        axis=0)
    ioe = lax.broadcasted_iota(I32, (N_EXPERTS, TM), 0)
    picked = jnp.zeros((N_EXPERTS, TM), F32)
    e_rows, w_rows = [], []
    for _ in range(TOP_K):
        m = jnp.max(sel, axis=0, keepdims=True)
        idx = jnp.min(jnp.where(sel == m, ioe, N_EXPERTS), axis=0, keepdims=True)
        hit = ioe == idx
        e_rows.append(idx)
        w_rows.append(jnp.sum(jnp.where(hit, scores, 0.0), axis=0, keepdims=True))
        picked = jnp.where(hit, 1.0, picked)
        sel = jnp.where(hit, neg, sel)
    wk = jnp.concatenate(w_rows, axis=0)
    w_out[...] = wk / jnp.sum(wk, axis=0, keepdims=True) * ROUTED_SCALE
    e_out[...] = jnp.concatenate(e_rows, axis=0)
    before = jnp.dot(picked.astype(BF16), tri_ref[...], preferred_element_type=F32) + carry_ref[...]
    r_rows = [jnp.sum(jnp.where(ioe == e_rows[k], before, 0.0), axis=0, keepdims=True) for k in range(TOP_K)]
    r_out[...] = jnp.concatenate(r_rows, axis=0).astype(I32)
    carry_ref[...] = carry_ref[...] + jnp.sum(picked, axis=1, keepdims=True)
    cnt_out[...] = carry_ref[...]


def _route(lt, br):
    t_all = lt.shape[1]
    n_tiles = t_all // TM
    return pl.pallas_call(
        _route_kernel,
        out_shape=[
            jax.ShapeDtypeStruct((TOP_K, t_all), I32),
            jax.ShapeDtypeStruct((TOP_K, t_all), F32),
            jax.ShapeDtypeStruct((TOP_K, t_all), I32),
            jax.ShapeDtypeStruct((N_EXPERTS, 1), F32),
        ],
        grid=(n_tiles,),
        in_specs=[pl.BlockSpec((N_EXPERTS, TM), lambda i: (0, i)), _const_spec((N_EXPERTS, 1))],
        out_specs=[
            pl.BlockSpec((TOP_K, TM), lambda i: (0, i)),
            pl.BlockSpec((TOP_K, TM), lambda i: (0, i)),
            pl.BlockSpec((TOP_K, TM), lambda i: (0, i)),
            pl.BlockSpec((N_EXPERTS, 1), lambda i: (0, 0)),
        ],
        scratch_shapes=[pltpu.VMEM((N_EXPERTS, 1), F32), pltpu.VMEM((TM, TM), BF16)],
        compiler_params=pltpu.CompilerParams(dimension_semantics=("arbitrary",)),
        name="route",
    )(lt, br)


def _plan_kernel(cnt_ref, e_ref, r_ref, dest_out, blk_e_out, nxt_e_out, nblk_out, cnt_row_out, pstart_row_out,
                 pstart_ref):
    i = pl.program_id(0)

    @pl.when(i == 0)
    def _():
        nb = jnp.floor((cnt_ref[...] + (BM - 1)) * (1.0 / BM))
        a = lax.broadcasted_iota(I32, (N_EXPERTS, N_EXPERTS), 0)
        b = lax.broadcasted_iota(I32, (N_EXPERTS, N_EXPERTS), 1)
        lower = jnp.where(b < a, 1.0, 0.0).astype(BF16)
        nb_l = jnp.broadcast_to(nb, (N_EXPERTS, LANES)).astype(BF16)
        first_blk = jnp.dot(lower, nb_l, preferred_element_type=F32)[:, 0:1]
        end_blk = first_blk + nb
        pstart_ref[...] = first_blk * BM
        cnt_row_out[...] = jnp.sum(jnp.where(a == b, cnt_ref[...], 0.0), axis=0, keepdims=True).astype(I32)
        pstart_row_out[...] = jnp.sum(jnp.where(a == b, first_blk * BM, 0.0), axis=0, keepdims=True).astype(I32)
        n_lanes = blk_e_out.shape[1]
        blk = lax.broadcasted_iota(I32, (N_EXPERTS, n_lanes), 1).astype(F32)
        owner = jnp.sum(jnp.where(end_blk <= blk, 1.0, 0.0), axis=0, keepdims=True)
        blk_e_out[...] = jnp.minimum(owner, N_EXPERTS - 1.0).astype(I32)
        total = jnp.max(end_blk, axis=0, keepdims=True)
        nblk_out[...] = jnp.broadcast_to(total, nblk_out.shape).astype(I32)
        group_end = jnp.min(jnp.where(end_blk > blk, end_blk, 2.0 * n_lanes), axis=0, keepdims=True)
        nxt = jnp.sum(jnp.where(end_blk <= group_end, 1.0, 0.0), axis=0, keepdims=True)
        nxt_e_out[...] = jnp.where(group_end < total, nxt, -1.0).astype(I32)

    ioe = lax.broadcasted_iota(I32, (N_EXPERTS, TM), 0)
    e = e_ref[...]
    rows = [jnp.sum(jnp.where(ioe == e[k:k + 1], pstart_ref[...], 0.0), axis=0, keepdims=True)
            for k in range(TOP_K)]
    dest_out[...] = jnp.concatenate(rows, axis=0).astype(I32) + r_ref[...]


def _plan(cnt, eidx, rank, n_blocks):
    t_all = eidx.shape[1]
    n_lanes = pl.cdiv(n_blocks, LANES) * LANES
    return pl.pallas_call(
        _plan_kernel,
        out_shape=[
            jax.ShapeDtypeStruct((TOP_K, t_all), I32),
            jax.ShapeDtypeStruct((1, n_lanes), I32),
            jax.ShapeDtypeStruct((1, n_lanes), I32),
            jax.ShapeDtypeStruct((1, LANES), I32),
            jax.ShapeDtypeStruct((1, N_EXPERTS), I32),
            jax.ShapeDtypeStruct((1, N_EXPERTS), I32),
        ],
        grid=(t_all // TM,),
        in_specs=[
            _const_spec((N_EXPERTS, 1)),
            pl.BlockSpec((TOP_K, TM), lambda i: (0, i)),
            pl.BlockSpec((TOP_K, TM), lambda i: (0, i)),
        ],
        out_specs=[
            pl.BlockSpec((TOP_K, TM), lambda i: (0, i)),
            pl.BlockSpec((1, n_lanes), lambda i: (0, 0)),
            pl.BlockSpec((1, n_lanes), lambda i: (0, 0)),
            pl.BlockSpec((1, LANES), lambda i: (0, 0)),
            pl.BlockSpec((1, N_EXPERTS), lambda i: (0, 0)),
            pl.BlockSpec((1, N_EXPERTS), lambda i: (0, 0)),
        ],
        scratch_shapes=[pltpu.VMEM((N_EXPERTS, 1), F32)],
        compiler_params=pltpu.CompilerParams(dimension_semantics=("arbitrary",)),
        name="plan",
    )(cnt, eidx, rank)


def _dispatch_kernel(cnt_ref, pst_ref, nblk_ref, dest_ref, h2_ref, xs_hbm, zeros, sem, fill_sem,
                     *, n_steps, n_blocks):
    i = pl.program_id(0)
    per_step = pl.cdiv(N_EXPERTS, n_steps)

    @pl.when(i == 0)
    def _():
        zeros[...] = jnp.zeros_like(zeros)

    def issue(t, carry):
        for k in range(TOP_K):
            pltpu.make_async_copy(_packed_rows(h2_ref, t, 1), _packed_rows(xs_hbm, dest_ref[k, t], 1), sem).start(
                priority=k % DMA_THREADS)
        return carry

    lax.fori_loop(0, TM, issue, 0)

    def fills(do):
        def per_expert(j, carry):
            e = i * per_step + j

            @pl.when(e < N_EXPERTS)
            def _():
                cnt = cnt_ref[e]
                base = pst_ref[e]
                padded = (cnt + (BM - 1)) // BM * BM
                mid = jnp.minimum((cnt + (SUBLANES - 1)) // SUBLANES * SUBLANES, padded)

                def one(r, c):
                    do(pltpu.make_async_copy(_packed_rows(zeros, 0, 1), _packed_rows(xs_hbm, base + r, 1), fill_sem))
                    return c

                lax.fori_loop(cnt, mid, one, 0)

                def eight(q, c):
                    r = base + mid + q * SUBLANES
                    do(pltpu.make_async_copy(_packed_rows(zeros, 0, SUBLANES), _packed_rows(xs_hbm, r, SUBLANES),
                                             fill_sem))
                    return c

                lax.fori_loop(0, (padded - mid) // SUBLANES, eight, 0)

            return carry

        lax.fori_loop(0, per_step, per_expert, 0)

        def per_block(j, carry):
            b = nblk_ref[0] + i * per_step + j

            @pl.when(b < n_blocks)
            def _():
                do(pltpu.make_async_copy(zeros, _packed_rows(xs_hbm, b * BM, BM), fill_sem))

            return carry

        lax.fori_loop(0, per_step, per_block, 0)

    fills(lambda c: c.start())
    for k in range(TOP_K):
        pltpu.make_async_copy(h2_ref, _packed_rows(xs_hbm, 0, TM), sem).wait()
    fills(lambda c: c.wait())


def _dispatch(cnt_row, pstart_row, nblk, dest, h2, n_blocks):
    t_all = h2.shape[0] // PACK_ROWS
    n_steps = t_all // TM
    kern = functools.partial(_dispatch_kernel, n_steps=n_steps, n_blocks=n_blocks)
    grid_spec = pltpu.PrefetchScalarGridSpec(
        num_scalar_prefetch=3,
        grid=(n_steps,),
        in_specs=[
            pl.BlockSpec((TOP_K, TM), lambda i, *_: (0, i), memory_space=pltpu.SMEM),
            pl.BlockSpec((TM * PACK_ROWS, LANES), lambda i, *_: (i, 0)),
        ],
        out_specs=pl.BlockSpec(memory_space=pl.ANY),
        scratch_shapes=[pltpu.VMEM((BM * PACK_ROWS, LANES), U32), pltpu.SemaphoreType.DMA(()),
                        pltpu.SemaphoreType.DMA(())],
    )
    return pl.pallas_call(
        kern,
        out_shape=jax.ShapeDtypeStruct((n_blocks * BM * PACK_ROWS, LANES), U32),
        grid_spec=grid_spec,
        compiler_params=pltpu.CompilerParams(dimension_semantics=("arbitrary",)),
        name="dispatch",
    )(cnt_row, pstart_row, nblk, dest, h2)


def _moe_kernel(blk_e_ref, nxt_e_ref, nblk_ref, x_hbm, wg_hbm, wu_hbm, wd_hbm, y_ref,
                x_ring, wg_l, wu_l, wd_l, wgu_s, wd_s, xsems, sems, cur_ref):
    nblk = nblk_ref[0]

    def weight_copies(e, slot):
        return (pltpu.make_async_copy(wg_hbm.at[e], wg_l.at[slot], sems.at[slot, 0]),
                pltpu.make_async_copy(wu_hbm.at[e], wu_l.at[slot], sems.at[slot, 1]),
                pltpu.make_async_copy(wd_hbm.at[e], wd_l.at[slot], sems.at[slot, 2]))

    def row_copy(blk):
        slot = blk % X_RING
        return pltpu.make_async_copy(_packed_rows(x_hbm, blk * BM, BM), x_ring.at[slot], xsems.at[slot])

    def one_block(sub, carry):
        b = pl.program_id(0) * MOE_SUB + sub
        y_blk = y_ref.at[pl.ds(pl.multiple_of(sub * (BM * PACK_ROWS), BM * PACK_ROWS), BM * PACK_ROWS)]
        _moe_block(b, nblk, blk_e_ref, nxt_e_ref, y_blk, x_ring, wg_l, wu_l, wd_l, wgu_s, wd_s, cur_ref,
                   weight_copies, row_copy)
        return carry

    lax.fori_loop(0, MOE_SUB, one_block, 0)


def _moe_block(b, nblk, blk_e_ref, nxt_e_ref, y_ref, x_ring, wg_l, wu_l, wd_l, wgu_s, wd_s, cur_ref,
               weight_copies, row_copy):
    @pl.when(b < nblk)
    def _():
        e = blk_e_ref[b]

        @pl.when(b == 0)
        def _():
            cur_ref[0] = 0
            for c in weight_copies(e, 0):
                c.start()
            for j in range(X_AHEAD):
                @pl.when(j < nblk)
                def _():
                    row_copy(j).start()

        @pl.when(b + X_AHEAD < nblk)
        def _():
            row_copy(b + X_AHEAD).start()

        @pl.when((b == 0) | (e != blk_e_ref[jnp.maximum(b - 1, 0)]))
        def _():
            @pl.when(b > 0)
            def _():
                cur_ref[0] = 1 - cur_ref[0]

            slot = cur_ref[0]
            for c in weight_copies(e, slot):
                c.wait()
            nxt = nxt_e_ref[b]

            @pl.when(nxt >= 0)
            def _():
                for c in weight_copies(nxt, 1 - slot):
                    c.start()

            wgu_s[:, :D_EXPERT] = wg_l[slot].astype(BF16)
            wgu_s[:, D_EXPERT:] = wu_l[slot].astype(BF16)
            wd_s[...] = wd_l[slot].astype(BF16)

        row_copy(b).wait()
        x = _unpack_bf16_pairs(x_ring.at[b % X_RING], BM).astype(BF16)
        gu = jnp.dot(x, wgu_s[...], preferred_element_type=F32)
        h = (jax.nn.silu(gu[:, :D_EXPERT]) * gu[:, D_EXPERT:]).astype(BF16)
        y = jnp.dot(h, wd_s[...], preferred_element_type=F32)
        _store_packed(y_ref, _pack_bf16_pairs(y), BM)

    @pl.when(b >= nblk)
    def _():
        y_ref[...] = jnp.zeros_like(y_ref)


def _moe(blk_e, nxt_e, nblk, xs, wg, wu, wd):
    d = D_MODEL
    n_blocks = xs.shape[0] // (BM * PACK_ROWS)
    assert n_blocks % MOE_SUB == 0
    grid_spec = pltpu.PrefetchScalarGridSpec(
        num_scalar_prefetch=3,
        grid=(n_blocks // MOE_SUB,),
        in_specs=[
            pl.BlockSpec(memory_space=pl.ANY),
            pl.BlockSpec(memory_space=pl.ANY),
            pl.BlockSpec(memory_space=pl.ANY),
            pl.BlockSpec(memory_space=pl.ANY),
        ],
        out_specs=pl.BlockSpec((MOE_SUB * BM * PACK_ROWS, LANES), lambda s, *_: (s, 0)),
        scratch_shapes=[
            pltpu.VMEM((X_RING, BM * PACK_ROWS, LANES), U32),
            pltpu.VMEM((2, d, D_EXPERT), F32), pltpu.VMEM((2, d, D_EXPERT), F32), pltpu.VMEM((2, D_EXPERT, d), F32),
            pltpu.VMEM((d, 2 * D_EXPERT), BF16), pltpu.VMEM((D_EXPERT, d), BF16),
            pltpu.SemaphoreType.DMA((X_RING,)), pltpu.SemaphoreType.DMA((2, 3)), pltpu.SMEM((1,), I32),
        ],
    )
    return pl.pallas_call(
        _moe_kernel,
        out_shape=jax.ShapeDtypeStruct((n_blocks * BM * PACK_ROWS, LANES), U32),
        grid_spec=grid_spec,
        compiler_params=pltpu.CompilerParams(dimension_semantics=("arbitrary",),
                                             vmem_limit_bytes=VMEM_LIMIT_BYTES),
        name="moe",
    )(blk_e, nxt_e, nblk, xs, wg, wu, wd)


def _final_kernel(dest_ref, dnext_ref, xs_ref, w_ref, gp_ref, gs_ref, fp_ref, fs_ref, gf_ref, y_hbm,
                  op_ref, os_ref, buf, sems, *, n_tiles, n_ptiles, tiles_per_seq):
    i = pl.program_id(0)
    d = D_MODEL
    slot = i % 2

    def gather(d_ref, s):
        def issue(t, carry):
            for k in range(TOP_K):
                pltpu.make_async_copy(_packed_rows(y_hbm, d_ref[k, t], 1), _packed_rows(buf.at[s, k], t, 1),
                                      sems.at[s]).start(priority=k % DMA_THREADS)
            return carry

        lax.fori_loop(0, TC, issue, 0)

    @pl.when(i == 0)
    def _():
        gather(dest_ref, 0)

    @pl.when(i + 1 < n_tiles)
    def _():
        gather(dnext_ref, 1 - slot)

    for k in range(TOP_K):
        pltpu.make_async_copy(_packed_rows(y_hbm, 0, TC), buf.at[slot, k], sems.at[slot]).wait()

    w = w_ref[...]
    acc = w[:, 0:1] * _unpack_bf16_pairs(buf.at[slot, 0], TC)
    for k in range(1, TOP_K):
        acc = acc + w[:, k:k + 1] * _unpack_bf16_pairs(buf.at[slot, k], TC)

    def finish(gate2, shift, scale, o_ref):
        x2 = xs_ref[...] + gate2 * acc
        o_ref[...] = _rms(x2, gf_ref[...]) * (1.0 + scale) + shift

    @pl.when(i < n_ptiles)
    def _():
        b = i // tiles_per_seq
        f = fp_ref[pl.ds(b, 1), :]
        finish(gp_ref[pl.ds(b, 1), :], f[:, :d], f[:, d:], op_ref)

    @pl.when(i >= n_ptiles)
    def _():
        f = fs_ref[...]
        finish(gs_ref[...], f[:, :d], f[:, d:], os_ref)


def _final(dest, xs, wts, m_all, mf_all, gf, y_sorted, *, n_batch, seq, t_p):
    t_all, d = xs.shape
    n_tiles = t_all // TC
    n_ptiles = t_p // TC
    dec_batch = m_all.shape[0] - n_batch
    kern = functools.partial(_final_kernel, n_tiles=n_tiles, n_ptiles=n_ptiles, tiles_per_seq=seq // TC)
    pb = dec_batch // n_batch
    return pl.pallas_call(
        kern,
        out_shape=[jax.ShapeDtypeStruct((t_p, d), F32), jax.ShapeDtypeStruct((t_all - t_p, d), F32)],
        grid=(n_tiles,),
        in_specs=[
            pl.BlockSpec((TOP_K, TC), lambda i: (0, i), memory_space=pltpu.SMEM),
            pl.BlockSpec((TOP_K, TC), lambda i: (0, jnp.minimum(i + 1, n_tiles - 1)), memory_space=pltpu.SMEM),
            pl.BlockSpec((TC, d), lambda i: (i, 0)),
            pl.BlockSpec((TC, TOP_K), lambda i: (i, 0)),
            pl.BlockSpec((n_batch, d), lambda i: (pb, N_MOD - 1)),
            pl.BlockSpec((dec_batch, d), lambda i: (0, N_MOD - 1)),
            pl.BlockSpec((n_batch, 2 * d), lambda i: (pb, 0)),
            pl.BlockSpec((dec_batch, 2 * d), lambda i: (0, 0)),
            _const_spec(gf.shape),
            pl.BlockSpec(memory_space=pl.ANY),
        ],
        out_specs=[
            pl.BlockSpec((TC, d), lambda i: (jnp.minimum(i, n_ptiles - 1), 0)),
            pl.BlockSpec((TC, d), lambda i: (jnp.maximum(i - n_ptiles, 0), 0)),
        ],
        scratch_shapes=[pltpu.VMEM((2, TOP_K, TC * PACK_ROWS, LANES), U32), pltpu.SemaphoreType.DMA((2,))],
        compiler_params=pltpu.CompilerParams(dimension_semantics=("arbitrary",),
                                             vmem_limit_bytes=VMEM_LIMIT_BYTES),
        name="final",
    )(dest, dest, xs, wts, m_all, m_all, mf_all, mf_all, gf, y_sorted)


def kernel(x_prompt, x_sample, state_conv, c_prompt, c_sample, w_ada, b_ada, g_norm1, w_in, w_spatial, b_spatial, g_v, w_conv, g_out_a, g_out_b, w_out, g_norm2, w_router, b_router, w_exp_gate, w_exp_up, w_exp_down, w_sh_gate, w_sh_up, w_sh_down, w_ada_final, b_ada_final, g_final):
    n_batch, seq, d = x_prompt.shape
    dec_batch, dec_seq, _ = x_sample.shape
    assert w_ada.shape[0] == 1, "one layer only"
    assert d == D_MODEL and dec_seq == DEC_SEQ and dec_batch * dec_seq == TM and seq % TM == 0
    assert n_batch % SUBLANES == 0 and dec_batch % n_batch == 0
    t_p = n_batch * seq
    t_s = dec_batch * dec_seq
    t_all = t_p + t_s

    c_all = jnp.concatenate([c_sample, c_prompt], axis=0)
    m_all = _ada(c_all, w_ada[0], b_ada[0])
    mf_all = _ada(c_all, w_ada_final, b_ada_final)

    rep = functools.partial(jnp.repeat, repeats=A_HEAD_DIM, axis=1)
    tril = jnp.tril(jnp.ones((CHUNK, CHUNK), dtype=bool))
    wt = jnp.where(tril, w_spatial[0], 0.0).astype(BF16)
    bsp = rep(b_spatial[0].T)
    csp = rep(jnp.transpose(w_spatial[0][:, :DEC_SEQ, :DEC_SEQ], (1, 2, 0)).reshape(DEC_SEQ * DEC_SEQ, A_HEADS))
    bsps = rep(b_spatial[0][:, :DEC_SEQ].T)
    st = jnp.transpose(state_conv[0], (1, 0, 2))
    wshgu = jnp.concatenate([w_sh_gate[0], w_sh_up[0]], axis=1).astype(BF16)
    xsm = jnp.transpose(x_sample, (1, 0, 2)).reshape(t_s, d)

    xs, h2, lt, cztail, vs, cz23 = _mixer(
        x_prompt.reshape(t_p, d), xsm, m_all, g_norm1, w_in[0].astype(BF16), wt, bsp,
        g_v.reshape(1, MIX_A), w_conv[0], g_out_a, g_out_b, w_out[0].astype(BF16), g_norm2,
        w_router[0].T.astype(BF16), wshgu, w_sh_down[0].astype(BF16), csp, bsps, st,
        n_batch=n_batch, seq=seq)

    eidx, wts, rank, cnt = _route(lt, b_router[0].reshape(N_EXPERTS, 1))

    n_blocks = (t_all * TOP_K) // BM + N_EXPERTS
    dest, blk_e, nxt_e, nblk, cnt_row, pstart_row = _plan(cnt, eidx, rank, n_blocks)

    x_sorted = _dispatch(cnt_row[0], pstart_row[0], nblk[0, :1], dest, h2, n_blocks)
    y_sorted = _moe(blk_e[0, :n_blocks], nxt_e[0, :n_blocks], nblk[0, :1], x_sorted,
                    w_exp_gate[0], w_exp_up[0], w_exp_down[0])
    y_p, y_s = _final(dest, xs, wts.T, m_all, mf_all, g_final.reshape(1, d), y_sorted,
                      n_batch=n_batch, seq=seq, t_p=t_p)

    tiles_per_seq = seq // TM
    y_prompt = y_p.reshape(n_batch, seq, d)
    y_sample = jnp.transpose(y_s.reshape(dec_seq, dec_batch, d), (1, 0, 2))
    conv_p = cztail[tiles_per_seq - 1:n_batch * tiles_per_seq:tiles_per_seq, SUBLANES - (CONV_W - 1):, :][None]
    conv_s = jnp.transpose(cz23, (1, 0, 2))[None]
    v_s = jnp.transpose(vs.reshape(dec_seq, dec_batch, A_HEADS, A_HEAD_DIM), (1, 0, 2, 3))[None]
    return (y_prompt, y_sample, conv_p, conv_s, v_s)
```

```python
import functools

import jax
import jax.numpy as jnp
from jax import lax
from jax.experimental import pallas as pl
from jax.experimental.pallas import tpu as pltpu

F32 = jnp.float32
BF16 = jnp.bfloat16
I32 = jnp.int32
U32 = jnp.uint32

D_MODEL = 1024
MIX_A = 512
A_HEADS = 4
A_HEAD_DIM = 128
CHUNK = 128
CONV_DIM = 512
CONV_W = 3
PROJ_DIM = 2 * MIX_A + 3 * CONV_DIM
N_EXPERTS = 256
TOP_K = 8
N_GROUPS = 8
TOPK_GROUPS = 4
GROUP_SIZE = N_EXPERTS // N_GROUPS
D_EXPERT = 256
D_SHARED = 256
ROUTED_SCALE = 2.5
N_MOD = 6
RMS_EPS = 1e-6
DEC_SEQ = 4

LANES = 128
SUBLANES = 8
PACK_ROWS = D_MODEL // 2 // LANES
TM = 512
BM = 256
TC = 128
DMA_THREADS = 2
ISSUE_UNROLL = 4
MOE_SUB = 4
X_AHEAD = 2
X_RING = X_AHEAD + 1
VMEM_LIMIT_BYTES = 56 * 1024 * 1024


def _rms(x, g):
    return x * lax.rsqrt(jnp.mean(x * x, axis=-1, keepdims=True) + RMS_EPS) * g


def _packed_rows(ref, row, n):
    return ref.at[pl.ds(pl.multiple_of(row * PACK_ROWS, PACK_ROWS), n * PACK_ROWS)]


def _pack_bf16_pairs(x):
    half = D_MODEL // 2
    xb = x.astype(BF16).astype(F32)
    lo = lax.shift_right_logical(lax.bitcast_convert_type(xb[:, :half], U32), jnp.uint32(16))
    return lo | lax.bitcast_convert_type(xb[:, half:], U32)


def _store_packed(ref, packed, n):
    for j in range(PACK_ROWS):
        ref[pl.ds(j, n, stride=PACK_ROWS), :] = packed[:, j * LANES:(j + 1) * LANES]


def _unpack_bf16_pairs(ref, n):
    words = [ref[pl.ds(j, n, stride=PACK_ROWS), :] for j in range(PACK_ROWS)]
    lo = [lax.bitcast_convert_type(lax.shift_left(w, jnp.uint32(16)), F32) for w in words]
    hi = [lax.bitcast_convert_type(w & jnp.uint32(0xFFFF0000), F32) for w in words]
    return jnp.concatenate(lo + hi, axis=1)


def _const_spec(shape):
    nd = len(shape)
    return pl.BlockSpec(shape, lambda *_: (0,) * nd, pipeline_mode=pl.Buffered(1))


def _ada_kernel(c_ref, w_ref, b_ref, o_ref):
    a = jax.nn.silu(c_ref[...]).astype(BF16)
    o_ref[...] = jnp.dot(a, w_ref[...].astype(BF16), preferred_element_type=F32) + b_ref[...]


def _ada(c_all, w, b):
    rows, n = c_all.shape[0], w.shape[1]
    return pl.pallas_call(
        _ada_kernel,
        out_shape=jax.ShapeDtypeStruct((rows, n), F32),
        grid=(n // D_MODEL,),
        in_specs=[
            pl.BlockSpec((rows, D_MODEL), lambda j: (0, 0)),
            pl.BlockSpec((D_MODEL, D_MODEL), lambda j: (0, j)),
            pl.BlockSpec((1, D_MODEL), lambda j: (0, j)),
        ],
        out_specs=pl.BlockSpec((rows, D_MODEL), lambda j: (0, j)),
        compiler_params=pltpu.CompilerParams(dimension_semantics=("arbitrary",)),
        name="ada",
    )(c_all, w, b.reshape(1, n))


def _mixer_kernel(xp_ref, xsm_ref, mp_ref, ms_ref, g1_ref, win_ref, wt_ref, bsp_ref, gv_ref, wconv_ref,
                  goa_ref, gob_ref, wout_ref, g2_ref, wrt_ref, wshgu_ref, wshd_ref, csp_ref, bsps_ref, st_ref,
                  xs_out, h2_out, lt_out, cztail_out, vs_out, cz23_out, carry_ref, *, n_ptiles, tiles_per_seq):
    i = pl.program_id(0)
    d = D_MODEL

    def proj(x, sh1, sc1):
        h = _rms(x, g1_ref[...]) * (1.0 + sc1) + sh1
        p = jnp.dot(h.astype(BF16), win_ref[...], preferred_element_type=F32)
        u = jax.nn.gelu(p[:, :MIX_A])
        v = jax.nn.gelu(p[:, MIX_A:2 * MIX_A])
        vn = jnp.concatenate(
            [_rms(v[:, h * LANES:(h + 1) * LANES], gv_ref[:, h * LANES:(h + 1) * LANES]) for h in range(A_HEADS)],
            axis=1)
        o = 2 * MIX_A
        return u, vn, p[:, o:o + CONV_DIM], p[:, o + CONV_DIM:o + 2 * CONV_DIM], p[:, o + 2 * CONV_DIM:]

    def tail(x, ya, yb, gate1, sh2, sc2, gate2):
        cat = jnp.concatenate([_rms(ya, goa_ref[...]), _rms(yb, gob_ref[...])], axis=1).astype(BF16)
        x1 = x + gate1 * jnp.dot(cat, wout_ref[...], preferred_element_type=F32)
        h2 = _rms(x1, g2_ref[...]) * (1.0 + sc2) + sh2
        h2b = h2.astype(BF16)
        lt_out[...] = lax.dot_general(wrt_ref[...], h2b, (((1,), (1,)), ((), ())), preferred_element_type=F32)
        gu = jnp.dot(h2b, wshgu_ref[...], preferred_element_type=F32)
        hs = (jax.nn.silu(gu[:, :D_SHARED]) * gu[:, D_SHARED:]).astype(BF16)
        xs_out[...] = x1 + gate2 * jnp.dot(hs, wshd_ref[...], preferred_element_type=F32)
        _store_packed(h2_out, _pack_bf16_pairs(h2), TM)

    @pl.when(i < n_ptiles)
    def _prompt():
        b = i // tiles_per_seq
        m = mp_ref[pl.ds(b, 1), :]
        mod = [m[:, k * d:(k + 1) * d] for k in range(N_MOD)]
        x = xp_ref[...]
        u, vn, bg, cg, z = proj(x, mod[0], mod[1])
        n_chunks = TM // CHUNK
        ya_cols = []
        for h in range(A_HEADS):
            vh = vn[:, h * LANES:(h + 1) * LANES].astype(BF16)
            rhs = jnp.concatenate([vh[c * CHUNK:(c + 1) * CHUNK] for c in range(n_chunks)], axis=1)
            mix = jnp.dot(wt_ref[h], rhs, preferred_element_type=F32)
            bias = bsp_ref[:, h * LANES:(h + 1) * LANES]
            mix = jnp.concatenate([mix[:, c * LANES:(c + 1) * LANES] + bias for c in range(n_chunks)], axis=0)
            ya_cols.append(u[:, h * LANES:(h + 1) * LANES] * mix)
        ya = jnp.concatenate(ya_cols, axis=1)
        cz = cg * z

        @pl.when(i % tiles_per_seq == 0)
        def _():
            carry_ref[...] = jnp.zeros_like(carry_ref)

        c6 = carry_ref[SUBLANES - 2:SUBLANES - 1, :]
        c7 = carry_ref[SUBLANES - 1:SUBLANES, :]
        r = lax.broadcasted_iota(I32, (TM, 1), 0)
        p1 = jnp.where(r == 0, c7, pltpu.roll(cz, 1, 0))
        p2 = jnp.where(r == 0, c6, jnp.where(r == 1, c7, pltpu.roll(cz, 2, 0)))
        wc = wconv_ref[...]
        yb = bg * (p2 * wc[0:1] + p1 * wc[1:2] + cz * wc[2:3])
        carry_ref[...] = cz[TM - SUBLANES:]
        cztail_out[0] = cz[TM - SUBLANES:]
        tail(x, ya, yb, mod[2], mod[3], mod[4], mod[5])

    @pl.when(i == n_ptiles)
    def _sample():
        ms = ms_ref[...]
        nb = TM // DEC_SEQ

        def mod(k):
            return jnp.concatenate([ms[:, k * d:(k + 1) * d]] * DEC_SEQ, axis=0)

        x = xsm_ref[...]
        u, vn, bg, cg, z = proj(x, mod(0), mod(1))
        vt = [vn[t * nb:(t + 1) * nb] for t in range(DEC_SEQ)]
        mixes = []
        for t in range(DEC_SEQ):
            acc = csp_ref[DEC_SEQ * t:DEC_SEQ * t + 1, :] * vt[0]
            for s in range(1, t + 1):
                acc = acc + csp_ref[DEC_SEQ * t + s:DEC_SEQ * t + s + 1, :] * vt[s]
            mixes.append(acc + bsps_ref[t:t + 1, :])
        ya = u * jnp.concatenate(mixes, axis=0)
        cz = cg * z
        czt = [cz[t * nb:(t + 1) * nb] for t in range(DEC_SEQ)]
        full = [st_ref[0], st_ref[1]] + czt
        wc = wconv_ref[...]
        yc = jnp.concatenate(
            [full[t] * wc[0:1] + full[t + 1] * wc[1:2] + full[t + 2] * wc[2:3] for t in range(DEC_SEQ)], axis=0)
        yb = bg * yc
        vs_out[...] = vn
        cz23_out[0] = czt[DEC_SEQ - 2]
        cz23_out[1] = czt[DEC_SEQ - 1]
        cztail_out[0] = cz[TM - SUBLANES:]
        tail(x, ya, yb, mod(2), mod(3), mod(4), mod(5))


def _mixer(xp, xsm, m_all, g1, win, wt, bsp, gv, wconv, goa, gob, wout, g2, wrt, wshgu, wshd, csp, bsps, st,
           *, n_batch, seq):
    t_p = xp.shape[0]
    n_ptiles = t_p // TM
    n_tiles = n_ptiles + 1
    t_all = n_tiles * TM
    dec_batch = xsm.shape[0] // DEC_SEQ
    d = D_MODEL
    kern = functools.partial(_mixer_kernel, n_ptiles=n_ptiles, tiles_per_seq=seq // TM)
    in_specs = [
        pl.BlockSpec((TM, d), lambda i: (jnp.minimum(i, n_ptiles - 1), 0)),
        _const_spec((TM, d)),
        pl.BlockSpec((n_batch, N_MOD * d), lambda i: (dec_batch // n_batch, 0), pipeline_mode=pl.Buffered(1)),
        pl.BlockSpec((dec_batch, N_MOD * d), lambda i: (0, 0), pipeline_mode=pl.Buffered(1)),
        _const_spec(g1.shape), _const_spec(win.shape), _const_spec(wt.shape), _const_spec(bsp.shape),
        _const_spec(gv.shape), _const_spec(wconv.shape), _const_spec(goa.shape), _const_spec(gob.shape),
        _const_spec(wout.shape), _const_spec(g2.shape), _const_spec(wrt.shape), _const_spec(wshgu.shape),
        _const_spec(wshd.shape), _const_spec(csp.shape), _const_spec(bsps.shape), _const_spec(st.shape),
    ]
    out_shape = [
        jax.ShapeDtypeStruct((t_all, d), F32),
        jax.ShapeDtypeStruct((t_all * PACK_ROWS, LANES), U32),
        jax.ShapeDtypeStruct((N_EXPERTS, t_all), F32),
        jax.ShapeDtypeStruct((n_tiles, SUBLANES, CONV_DIM), F32),
        jax.ShapeDtypeStruct((TM, MIX_A), F32),
        jax.ShapeDtypeStruct((2, dec_batch, CONV_DIM), F32),
    ]
    out_specs = [
        pl.BlockSpec((TM, d), lambda i: (i, 0)),
        pl.BlockSpec((TM * PACK_ROWS, LANES), lambda i: (i, 0)),
        pl.BlockSpec((N_EXPERTS, TM), lambda i: (0, i)),
        pl.BlockSpec((1, SUBLANES, CONV_DIM), lambda i: (i, 0, 0)),
        pl.BlockSpec((TM, MIX_A), lambda i: (0, 0)),
        pl.BlockSpec((2, dec_batch, CONV_DIM), lambda i: (0, 0, 0)),
    ]
    return pl.pallas_call(
        kern,
        out_shape=out_shape,
        grid=(n_tiles,),
        in_specs=in_specs,
        out_specs=out_specs,
        scratch_shapes=[pltpu.VMEM((SUBLANES, CONV_DIM), F32)],
        compiler_params=pltpu.CompilerParams(dimension_semantics=("arbitrary",),
                                             vmem_limit_bytes=VMEM_LIMIT_BYTES),
        name="mixer",
    )(xp, xsm, m_all, m_all, g1, win, wt, bsp, gv, wconv, goa, gob, wout, g2, wrt, wshgu, wshd, csp, bsps, st)


def _route_kernel(lt_ref, br_ref, e_out, w_out, r_out, cnt_out, carry_ref, tri_ref):
    i = pl.program_id(0)
    neg = -jnp.inf

    @pl.when(i == 0)
    def _():
        carry_ref[...] = jnp.zeros_like(carry_ref)
        a = lax.broadcasted_iota(I32, (TM, TM), 0)
        b = lax.broadcasted_iota(I32, (TM, TM), 1)
        tri_ref[...] = jnp.where(a < b, 1.0, 0.0).astype(BF16)

    scores = jax.nn.sigmoid(lt_ref[...])
    sel = scores + br_ref[...]
    iog = lax.broadcasted_iota(I32, (GROUP_SIZE, TM), 0)
    gs_rows = []
    for g in range(N_GROUPS):
        sg = sel[g * GROUP_SIZE:(g + 1) * GROUP_SIZE]
        m1 = jnp.max(sg, axis=0, keepdims=True)
        i1 = jnp.min(jnp.where(sg == m1, iog, GROUP_SIZE), axis=0, keepdims=True)
        m2 = jnp.max(jnp.where(iog == i1, neg, sg), axis=0, keepdims=True)
        gs_rows.append(m1 + m2)
    gs = jnp.concatenate(gs_rows, axis=0)
    io8 = lax.broadcasted_iota(I32, (N_GROUPS, TM), 0)
    keep = jnp.zeros((N_GROUPS, TM), F32)
    for _ in range(TOPK_GROUPS):
        m = jnp.max(gs, axis=0, keepdims=True)
        idx = jnp.min(jnp.where(gs == m, io8, N_GROUPS), axis=0, keepdims=True)
        hit = io8 == idx
        keep = jnp.where(hit, 1.0, keep)
        gs = jnp.where(hit, neg, gs)
    sel = jnp.concatenate(
        [jnp.where(keep[g:g + 1] > 0.0, sel[g * GROUP_SIZE:(g + 1) * GROUP_SIZE], neg) for g in range(N_GROUPS)],
        axis=0)
    ioe = lax.broadcasted_iota(I32, (N_EXPERTS, TM), 0)
    picked = jnp.zeros((N_EXPERTS, TM), F32)
    e_rows, w_rows = [], []
    for _ in range(TOP_K):
        m = jnp.max(sel, axis=0, keepdims=True)
        idx = jnp.min(jnp.where(sel == m, ioe, N_EXPERTS), axis=0, keepdims=True)
        hit = ioe == idx
        e_rows.append(idx)
        w_rows.append(jnp.sum(jnp.where(hit, scores, 0.0), axis=0, keepdims=True))
        picked = jnp.where(hit, 1.0, picked)
        sel = jnp.where(hit, neg, sel)
    wk = jnp.concatenate(w_rows, axis=0)
    w_out[...] = wk / jnp.sum(wk, axis=0, keepdims=True) * ROUTED_SCALE
    e_out[...] = jnp.concatenate(e_rows, axis=0)
    before = jnp.dot(picked.astype(BF16), tri_ref[...], preferred_element_type=F32) + carry_ref[...]
    r_rows = [jnp.sum(jnp.where(ioe == e_rows[k], before, 0.0), axis=0, keepdims=True) for k in range(TOP_K)]
    r_out[...] = jnp.concatenate(r_rows, axis=0).astype(I32)
    carry_ref[...] = carry_ref[...] + jnp.sum(picked, axis=1, keepdims=True)
    cnt_out[...] = carry_ref[...]


def _route(lt, br):
    t_all = lt.shape[1]
    n_tiles = t_all // TM
    return pl.pallas_call(
        _route_kernel,
        out_shape=[
            jax.ShapeDtypeStruct((TOP_K, t_all), I32),
            jax.ShapeDtypeStruct((TOP_K, t_all), F32),
            jax.ShapeDtypeStruct((TOP_K, t_all), I32),
            jax.ShapeDtypeStruct((N_EXPERTS, 1), F32),
        ],
        grid=(n_tiles,),
        in_specs=[pl.BlockSpec((N_EXPERTS, TM), lambda i: (0, i)), _const_spec((N_EXPERTS, 1))],
        out_specs=[
            pl.BlockSpec((TOP_K, TM), lambda i: (0, i)),
            pl.BlockSpec((TOP_K, TM), lambda i: (0, i)),
            pl.BlockSpec((TOP_K, TM), lambda i: (0, i)),
            pl.BlockSpec((N_EXPERTS, 1), lambda i: (0, 0)),
        ],
        scratch_shapes=[pltpu.VMEM((N_EXPERTS, 1), F32), pltpu.VMEM((TM, TM), BF16)],
        compiler_params=pltpu.CompilerParams(dimension_semantics=("arbitrary",)),
        name="route",
    )(lt, br)


def _plan_kernel(cnt_ref, e_ref, r_ref, dest_out, blk_e_out, nxt_e_out, nblk_out, cnt_row_out, pstart_row_out,
                 pstart_ref):
    i = pl.program_id(0)

    @pl.when(i == 0)
    def _():
        nb = jnp.floor((cnt_ref[...] + (BM - 1)) * (1.0 / BM))
        a = lax.broadcasted_iota(I32, (N_EXPERTS, N_EXPERTS), 0)
        b = lax.broadcasted_iota(I32, (N_EXPERTS, N_EXPERTS), 1)
        lower = jnp.where(b < a, 1.0, 0.0).astype(BF16)
        nb_l = jnp.broadcast_to(nb, (N_EXPERTS, LANES)).astype(BF16)
        first_blk = jnp.dot(lower, nb_l, preferred_element_type=F32)[:, 0:1]
        end_blk = first_blk + nb
        pstart_ref[...] = first_blk * BM
        cnt_row_out[...] = jnp.sum(jnp.where(a == b, cnt_ref[...], 0.0), axis=0, keepdims=True).astype(I32)
        pstart_row_out[...] = jnp.sum(jnp.where(a == b, first_blk * BM, 0.0), axis=0, keepdims=True).astype(I32)
        n_lanes = blk_e_out.shape[1]
        blk = lax.broadcasted_iota(I32, (N_EXPERTS, n_lanes), 1).astype(F32)
        owner = jnp.sum(jnp.where(end_blk <= blk, 1.0, 0.0), axis=0, keepdims=True)
        blk_e_out[...] = jnp.minimum(owner, N_EXPERTS - 1.0).astype(I32)
        total = jnp.max(end_blk, axis=0, keepdims=True)
        nblk_out[...] = jnp.broadcast_to(total, nblk_out.shape).astype(I32)
        group_end = jnp.min(jnp.where(end_blk > blk, end_blk, 2.0 * n_lanes), axis=0, keepdims=True)
        nxt = jnp.sum(jnp.where(end_blk <= group_end, 1.0, 0.0), axis=0, keepdims=True)
        nxt_e_out[...] = jnp.where(group_end < total, nxt, -1.0).astype(I32)

    ioe = lax.broadcasted_iota(I32, (N_EXPERTS, TM), 0)
    e = e_ref[...]
    rows = [jnp.sum(jnp.where(ioe == e[k:k + 1], pstart_ref[...], 0.0), axis=0, keepdims=True)
            for k in range(TOP_K)]
    dest_out[...] = jnp.concatenate(rows, axis=0).astype(I32) + r_ref[...]


def _plan(cnt, eidx, rank, n_blocks):
    t_all = eidx.shape[1]
    n_lanes = pl.cdiv(n_blocks, LANES) * LANES
    return pl.pallas_call(
        _plan_kernel,
        out_shape=[
            jax.ShapeDtypeStruct((TOP_K, t_all), I32),
            jax.ShapeDtypeStruct((1, n_lanes), I32),
            jax.ShapeDtypeStruct((1, n_lanes), I32),
            jax.ShapeDtypeStruct((1, LANES), I32),
            jax.ShapeDtypeStruct((1, N_EXPERTS), I32),
            jax.ShapeDtypeStruct((1, N_EXPERTS), I32),
        ],
        grid=(t_all // TM,),
        in_specs=[
            _const_spec((N_EXPERTS, 1)),
            pl.BlockSpec((TOP_K, TM), lambda i: (0, i)),
            pl.BlockSpec((TOP_K, TM), lambda i: (0, i)),
        ],
        out_specs=[
            pl.BlockSpec((TOP_K, TM), lambda i: (0, i)),
            pl.BlockSpec((1, n_lanes), lambda i: (0, 0)),
            pl.BlockSpec((1, n_lanes), lambda i: (0, 0)),
            pl.BlockSpec((1, LANES), lambda i: (0, 0)),
            pl.BlockSpec((1, N_EXPERTS), lambda i: (0, 0)),
            pl.BlockSpec((1, N_EXPERTS), lambda i: (0, 0)),
        ],
        scratch_shapes=[pltpu.VMEM((N_EXPERTS, 1), F32)],
        compiler_params=pltpu.CompilerParams(dimension_semantics=("arbitrary",)),
        name="plan",
    )(cnt, eidx, rank)


def _dispatch_kernel(cnt_ref, pst_ref, nblk_ref, dest_ref, h2_ref, xs_hbm, zeros, sem, fill_sem,
                     *, n_steps, n_blocks):
    i = pl.program_id(0)
    per_step = pl.cdiv(N_EXPERTS, n_steps)

    @pl.when(i == 0)
    def _():
        zeros[...] = jnp.zeros_like(zeros)

    def issue(t, carry):
        for k in range(TOP_K):
            pltpu.make_async_copy(_packed_rows(h2_ref, t, 1), _packed_rows(xs_hbm, dest_ref[t * TOP_K + k], 1),
                                  sem).start(priority=k % DMA_THREADS)
        return carry

    lax.fori_loop(0, TM, issue, 0, unroll=ISSUE_UNROLL)

    def fills(do):
        def per_expert(j, carry):
            e = i * per_step + j

            @pl.when(e < N_EXPERTS)
            def _():
                cnt = cnt_ref[e]
                base = pst_ref[e]
                padded = (cnt + (BM - 1)) // BM * BM
                mid = jnp.minimum((cnt + (SUBLANES - 1)) // SUBLANES * SUBLANES, padded)

                def one(r, c):
                    do(pltpu.make_async_copy(_packed_rows(zeros, 0, 1), _packed_rows(xs_hbm, base + r, 1), fill_sem))
                    return c

                lax.fori_loop(cnt, mid, one, 0)

                def eight(q, c):
                    r = base + mid + q * SUBLANES
                    do(pltpu.make_async_copy(_packed_rows(zeros, 0, SUBLANES), _packed_rows(xs_hbm, r, SUBLANES),
                                             fill_sem))
                    return c

                lax.fori_loop(0, (padded - mid) // SUBLANES, eight, 0)

            return carry

        lax.fori_loop(0, per_step, per_expert, 0)

        def per_block(j, carry):
            b = nblk_ref[0] + i * per_step + j

            @pl.when(b < n_blocks)
            def _():
                do(pltpu.make_async_copy(zeros, _packed_rows(xs_hbm, b * BM, BM), fill_sem))

            return carry

        lax.fori_loop(0, per_step, per_block, 0)

    fills(lambda c: c.start())
    for k in range(TOP_K):
        pltpu.make_async_copy(h2_ref, _packed_rows(xs_hbm, 0, TM), sem).wait()
    fills(lambda c: c.wait())


def _dispatch(cnt_row, pstart_row, nblk, dest, h2, n_blocks):
    t_all = h2.shape[0] // PACK_ROWS
    n_steps = t_all // TM
    kern = functools.partial(_dispatch_kernel, n_steps=n_steps, n_blocks=n_blocks)
    grid_spec = pltpu.PrefetchScalarGridSpec(
        num_scalar_prefetch=3,
        grid=(n_steps,),
        in_specs=[
            pl.BlockSpec((TM * TOP_K,), lambda i, *_: (i,), memory_space=pltpu.SMEM),
            pl.BlockSpec((TM * PACK_ROWS, LANES), lambda i, *_: (i, 0)),
        ],
        out_specs=pl.BlockSpec(memory_space=pl.ANY),
        scratch_shapes=[pltpu.VMEM((BM * PACK_ROWS, LANES), U32), pltpu.SemaphoreType.DMA(()),
                        pltpu.SemaphoreType.DMA(())],
    )
    return pl.pallas_call(
        kern,
        out_shape=jax.ShapeDtypeStruct((n_blocks * BM * PACK_ROWS, LANES), U32),
        grid_spec=grid_spec,
        compiler_params=pltpu.CompilerParams(dimension_semantics=("arbitrary",)),
        name="dispatch",
    )(cnt_row, pstart_row, nblk, dest, h2)


def _moe_kernel(blk_e_ref, nxt_e_ref, nblk_ref, x_hbm, wg_hbm, wu_hbm, wd_hbm, y_ref,
                x_ring, wg_l, wu_l, wd_l, wgu_s, wd_s, xsems, sems, cur_ref):
    nblk = nblk_ref[0]

    def weight_copies(e, slot):
        return (pltpu.make_async_copy(wg_hbm.at[e], wg_l.at[slot], sems.at[slot, 0]),
                pltpu.make_async_copy(wu_hbm.at[e], wu_l.at[slot], sems.at[slot, 1]),
                pltpu.make_async_copy(wd_hbm.at[e], wd_l.at[slot], sems.at[slot, 2]))

    def row_copy(blk):
        slot = blk % X_RING
        return pltpu.make_async_copy(_packed_rows(x_hbm, blk * BM, BM), x_ring.at[slot], xsems.at[slot])

    def one_block(sub, carry):
        b = pl.program_id(0) * MOE_SUB + sub
        y_blk = y_ref.at[pl.ds(pl.multiple_of(sub * (BM * PACK_ROWS), BM * PACK_ROWS), BM * PACK_ROWS)]
        _moe_block(b, nblk, blk_e_ref, nxt_e_ref, y_blk, x_ring, wg_l, wu_l, wd_l, wgu_s, wd_s, cur_ref,
                   weight_copies, row_copy)
        return carry

    lax.fori_loop(0, MOE_SUB, one_block, 0)


def _moe_block(b, nblk, blk_e_ref, nxt_e_ref, y_ref, x_ring, wg_l, wu_l, wd_l, wgu_s, wd_s, cur_ref,
               weight_copies, row_copy):
    @pl.when(b < nblk)
    def _():
        e = blk_e_ref[b]

        @pl.when(b == 0)
        def _():
            cur_ref[0] = 0
            for c in weight_copies(e, 0):
                c.start()
            for j in range(X_AHEAD):
                @pl.when(j < nblk)
                def _():
                    row_copy(j).start()

        @pl.when(b + X_AHEAD < nblk)
        def _():
            row_copy(b + X_AHEAD).start()

        @pl.when((b == 0) | (e != blk_e_ref[jnp.maximum(b - 1, 0)]))
        def _():
            @pl.when(b > 0)
            def _():
                cur_ref[0] = 1 - cur_ref[0]

            slot = cur_ref[0]
            for c in weight_copies(e, slot):
                c.wait()
            nxt = nxt_e_ref[b]

            @pl.when(nxt >= 0)
            def _():
                for c in weight_copies(nxt, 1 - slot):
                    c.start()

            wgu_s[:, :D_EXPERT] = wg_l[slot].astype(BF16)
            wgu_s[:, D_EXPERT:] = wu_l[slot].astype(BF16)
            wd_s[...] = wd_l[slot].astype(BF16)

        row_copy(b).wait()
        x = _unpack_bf16_pairs(x_ring.at[b % X_RING], BM).astype(BF16)
        gu = jnp.dot(x, wgu_s[...], preferred_element_type=F32)
        h = (jax.nn.silu(gu[:, :D_EXPERT]) * gu[:, D_EXPERT:]).astype(BF16)
        y = jnp.dot(h, wd_s[...], preferred_element_type=F32)
        _store_packed(y_ref, _pack_bf16_pairs(y), BM)

    @pl.when(b >= nblk)
    def _():
        y_ref[...] = jnp.zeros_like(y_ref)


def _moe(blk_e, nxt_e, nblk, xs, wg, wu, wd):
    d = D_MODEL
    n_blocks = xs.shape[0] // (BM * PACK_ROWS)
    assert n_blocks % MOE_SUB == 0
    grid_spec = pltpu.PrefetchScalarGridSpec(
        num_scalar_prefetch=3,
        grid=(n_blocks // MOE_SUB,),
        in_specs=[
            pl.BlockSpec(memory_space=pl.ANY),
            pl.BlockSpec(memory_space=pl.ANY),
            pl.BlockSpec(memory_space=pl.ANY),
            pl.BlockSpec(memory_space=pl.ANY),
        ],
        out_specs=pl.BlockSpec((MOE_SUB * BM * PACK_ROWS, LANES), lambda s, *_: (s, 0)),
        scratch_shapes=[
            pltpu.VMEM((X_RING, BM * PACK_ROWS, LANES), U32),
            pltpu.VMEM((2, d, D_EXPERT), F32), pltpu.VMEM((2, d, D_EXPERT), F32), pltpu.VMEM((2, D_EXPERT, d), F32),
            pltpu.VMEM((d, 2 * D_EXPERT), BF16), pltpu.VMEM((D_EXPERT, d), BF16),
            pltpu.SemaphoreType.DMA((X_RING,)), pltpu.SemaphoreType.DMA((2, 3)), pltpu.SMEM((1,), I32),
        ],
    )
    return pl.pallas_call(
        _moe_kernel,
        out_shape=jax.ShapeDtypeStruct((n_blocks * BM * PACK_ROWS, LANES), U32),
        grid_spec=grid_spec,
        compiler_params=pltpu.CompilerParams(dimension_semantics=("arbitrary",),
                                             vmem_limit_bytes=VMEM_LIMIT_BYTES),
        name="moe",
    )(blk_e, nxt_e, nblk, xs, wg, wu, wd)


def _final_kernel(dest_ref, dnext_ref, xs_ref, w_ref, gp_ref, gs_ref, fp_ref, fs_ref, gf_ref, y_hbm,
                  op_ref, os_ref, buf, sems, *, n_tiles, n_ptiles, tiles_per_seq):
    i = pl.program_id(0)
    d = D_MODEL
    slot = i % 2

    def gather(d_ref, s):
        def issue(t, carry):
            for k in range(TOP_K):
                pltpu.make_async_copy(_packed_rows(y_hbm, d_ref[t * TOP_K + k], 1), _packed_rows(buf.at[s, k], t, 1),
                                      sems.at[s]).start(priority=k % DMA_THREADS)
            return carry

        lax.fori_loop(0, TC, issue, 0, unroll=ISSUE_UNROLL)

    @pl.when(i == 0)
    def _():
        gather(dest_ref, 0)

    @pl.when(i + 1 < n_tiles)
    def _():
        gather(dnext_ref, 1 - slot)

    for k in range(TOP_K):
        pltpu.make_async_copy(_packed_rows(y_hbm, 0, TC), buf.at[slot, k], sems.at[slot]).wait()

    w = w_ref[...]
    acc = w[:, 0:1] * _unpack_bf16_pairs(buf.at[slot, 0], TC)
    for k in range(1, TOP_K):
        acc = acc + w[:, k:k + 1] * _unpack_bf16_pairs(buf.at[slot, k], TC)

    def finish(gate2, shift, scale, o_ref):
        x2 = xs_ref[...] + gate2 * acc
        o_ref[...] = _rms(x2, gf_ref[...]) * (1.0 + scale) + shift

    @pl.when(i < n_ptiles)
    def _():
        b = i // tiles_per_seq
        f = fp_ref[pl.ds(b, 1), :]
        finish(gp_ref[pl.ds(b, 1), :], f[:, :d], f[:, d:], op_ref)

    @pl.when(i >= n_ptiles)
    def _():
        f = fs_ref[...]
        finish(gs_ref[...], f[:, :d], f[:, d:], os_ref)


def _final(dest, xs, wts, m_all, mf_all, gf, y_sorted, *, n_batch, seq, t_p):
    t_all, d = xs.shape
    n_tiles = t_all // TC
    n_ptiles = t_p // TC
    dec_batch = m_all.shape[0] - n_batch
    kern = functools.partial(_final_kernel, n_tiles=n_tiles, n_ptiles=n_ptiles, tiles_per_seq=seq // TC)
    pb = dec_batch // n_batch
    return pl.pallas_call(
        kern,
        out_shape=[jax.ShapeDtypeStruct((t_p, d), F32), jax.ShapeDtypeStruct((t_all - t_p, d), F32)],
        grid=(n_tiles,),
        in_specs=[
            pl.BlockSpec((TC * TOP_K,), lambda i: (i,), memory_space=pltpu.SMEM),
            pl.BlockSpec((TC * TOP_K,), lambda i: (jnp.minimum(i + 1, n_tiles - 1),), memory_space=pltpu.SMEM),
            pl.BlockSpec((TC, d), lambda i: (i, 0)),
            pl.BlockSpec((TC, TOP_K), lambda i: (i, 0)),
            pl.BlockSpec((n_batch, d), lambda i: (pb, N_MOD - 1)),
            pl.BlockSpec((dec_batch, d), lambda i: (0, N_MOD - 1)),
            pl.BlockSpec((n_batch, 2 * d), lambda i: (pb, 0)),
            pl.BlockSpec((dec_batch, 2 * d), lambda i: (0, 0)),
            _const_spec(gf.shape),
            pl.BlockSpec(memory_space=pl.ANY),
        ],
        out_specs=[
            pl.BlockSpec((TC, d), lambda i: (jnp.minimum(i, n_ptiles - 1), 0)),
            pl.BlockSpec((TC, d), lambda i: (jnp.maximum(i - n_ptiles, 0), 0)),
        ],
        scratch_shapes=[pltpu.VMEM((2, TOP_K, TC * PACK_ROWS, LANES), U32), pltpu.SemaphoreType.DMA((2,))],
        compiler_params=pltpu.CompilerParams(dimension_semantics=("arbitrary",),
                                             vmem_limit_bytes=VMEM_LIMIT_BYTES),
        name="final",
    )(dest, dest, xs, wts, m_all, m_all, mf_all, mf_all, gf, y_sorted)


def kernel(x_prompt, x_sample, state_conv, c_prompt, c_sample, w_ada, b_ada, g_norm1, w_in, w_spatial, b_spatial, g_v, w_conv, g_out_a, g_out_b, w_out, g_norm2, w_router, b_router, w_exp_gate, w_exp_up, w_exp_down, w_sh_gate, w_sh_up, w_sh_down, w_ada_final, b_ada_final, g_final):
    n_batch, seq, d = x_prompt.shape
    dec_batch, dec_seq, _ = x_sample.shape
    assert w_ada.shape[0] == 1, "one layer only"
    assert d == D_MODEL and dec_seq == DEC_SEQ and dec_batch * dec_seq == TM and seq % TM == 0
    assert n_batch % SUBLANES == 0 and dec_batch % n_batch == 0
    t_p = n_batch * seq
    t_s = dec_batch * dec_seq
    t_all = t_p + t_s

    c_all = jnp.concatenate([c_sample, c_prompt], axis=0)
    m_all = _ada(c_all, w_ada[0], b_ada[0])
    mf_all = _ada(c_all, w_ada_final, b_ada_final)

    rep = functools.partial(jnp.repeat, repeats=A_HEAD_DIM, axis=1)
    tril = jnp.tril(jnp.ones((CHUNK, CHUNK), dtype=bool))
    wt = jnp.where(tril, w_spatial[0], 0.0).astype(BF16)
    bsp = rep(b_spatial[0].T)
    csp = rep(jnp.transpose(w_spatial[0][:, :DEC_SEQ, :DEC_SEQ], (1, 2, 0)).reshape(DEC_SEQ * DEC_SEQ, A_HEADS))
    bsps = rep(b_spatial[0][:, :DEC_SEQ].T)
    st = jnp.transpose(state_conv[0], (1, 0, 2))
    wshgu = jnp.concatenate([w_sh_gate[0], w_sh_up[0]], axis=1).astype(BF16)
    xsm = jnp.transpose(x_sample, (1, 0, 2)).reshape(t_s, d)

    xs, h2, lt, cztail, vs, cz23 = _mixer(
        x_prompt.reshape(t_p, d), xsm, m_all, g_norm1, w_in[0].astype(BF16), wt, bsp,
        g_v.reshape(1, MIX_A), w_conv[0], g_out_a, g_out_b, w_out[0].astype(BF16), g_norm2,
        w_router[0].T.astype(BF16), wshgu, w_sh_down[0].astype(BF16), csp, bsps, st,
        n_batch=n_batch, seq=seq)

    eidx, wts, rank, cnt = _route(lt, b_router[0].reshape(N_EXPERTS, 1))

    n_blocks = (t_all * TOP_K) // BM + N_EXPERTS
    dest, blk_e, nxt_e, nblk, cnt_row, pstart_row = _plan(cnt, eidx, rank, n_blocks)

    dest_tok = dest.T.reshape(t_all * TOP_K)
    x_sorted = _dispatch(cnt_row[0], pstart_row[0], nblk[0, :1], dest_tok, h2, n_blocks)
    y_sorted = _moe(blk_e[0, :n_blocks], nxt_e[0, :n_blocks], nblk[0, :1], x_sorted,
                    w_exp_gate[0], w_exp_up[0], w_exp_down[0])
    y_p, y_s = _final(dest_tok, xs, wts.T, m_all, mf_all, g_final.reshape(1, d), y_sorted,
                      n_batch=n_batch, seq=seq, t_p=t_p)

    tiles_per_seq = seq // TM
    y_prompt = y_p.reshape(n_batch, seq, d)
    y_sample = jnp.transpose(y_s.reshape(dec_seq, dec_batch, d), (1, 0, 2))
    conv_p = cztail[tiles_per_seq - 1:n_batch * tiles_per_seq:tiles_per_seq, SUBLANES - (CONV_W - 1):, :][None]
    conv_s = jnp.transpose(cz23, (1, 0, 2))[None]
    v_s = jnp.transpose(vs.reshape(dec_seq, dec_batch, A_HEADS, A_HEAD_DIM), (1, 0, 2, 3))[None]
    return (y_prompt, y_sample, conv_p, conv_s, v_s)
```

```python
import functools

import jax
import jax.numpy as jnp
from jax import lax
from jax.experimental import pallas as pl
from jax.experimental.pallas import tpu as pltpu

F32 = jnp.float32
BF16 = jnp.bfloat16
I32 = jnp.int32
U32 = jnp.uint32

D_MODEL = 1024
MIX_A = 512
A_HEADS = 4
A_HEAD_DIM = 128
CHUNK = 128
CONV_DIM = 512
CONV_W = 3
PROJ_DIM = 2 * MIX_A + 3 * CONV_DIM
N_EXPERTS = 256
TOP_K = 8
N_GROUPS = 8
TOPK_GROUPS = 4
GROUP_SIZE = N_EXPERTS // N_GROUPS
D_EXPERT = 256
D_SHARED = 256
ROUTED_SCALE = 2.5
N_MOD = 6
RMS_EPS = 1e-6
DEC_SEQ = 4

LANES = 128
SUBLANES = 8
PACK_ROWS = D_MODEL // 2 // LANES
TM = 512
BM = 256
TC = 128
DMA_THREADS = 2
ISSUE_UNROLL = 4
MOE_SUB = 2
X_AHEAD = 2
X_RING = MOE_SUB + X_AHEAD
VMEM_LIMIT_BYTES = 56 * 1024 * 1024


def _rms(x, g):
    return x * lax.rsqrt(jnp.mean(x * x, axis=-1, keepdims=True) + RMS_EPS) * g


def _packed_rows(ref, row, n):
    return ref.at[pl.ds(pl.multiple_of(row * PACK_ROWS, PACK_ROWS), n * PACK_ROWS)]


def _pack_bf16_pairs(x):
    half = D_MODEL // 2
    xb = x.astype(BF16).astype(F32)
    lo = lax.shift_right_logical(lax.bitcast_convert_type(xb[:, :half], U32), jnp.uint32(16))
    return lo | lax.bitcast_convert_type(xb[:, half:], U32)


def _store_packed(ref, packed, n):
    for j in range(PACK_ROWS):
        ref[pl.ds(j, n, stride=PACK_ROWS), :] = packed[:, j * LANES:(j + 1) * LANES]


def _unpack_bf16_pairs(ref, n):
    words = [ref[pl.ds(j, n, stride=PACK_ROWS), :] for j in range(PACK_ROWS)]
    lo = [lax.bitcast_convert_type(lax.shift_left(w, jnp.uint32(16)), F32) for w in words]
    hi = [lax.bitcast_convert_type(w & jnp.uint32(0xFFFF0000), F32) for w in words]
    return jnp.concatenate(lo + hi, axis=1)


def _const_spec(shape):
    nd = len(shape)
    return pl.BlockSpec(shape, lambda *_: (0,) * nd, pipeline_mode=pl.Buffered(1))


def _ada_kernel(c_ref, w_ref, b_ref, o_ref):
    a = jax.nn.silu(c_ref[...]).astype(BF16)
    o_ref[...] = jnp.dot(a, w_ref[...].astype(BF16), preferred_element_type=F32) + b_ref[...]


def _ada(c_all, w, b):
    rows, n = c_all.shape[0], w.shape[1]
    return pl.pallas_call(
        _ada_kernel,
        out_shape=jax.ShapeDtypeStruct((rows, n), F32),
        grid=(n // D_MODEL,),
        in_specs=[
            pl.BlockSpec((rows, D_MODEL), lambda j: (0, 0)),
            pl.BlockSpec((D_MODEL, D_MODEL), lambda j: (0, j)),
            pl.BlockSpec((1, D_MODEL), lambda j: (0, j)),
        ],
        out_specs=pl.BlockSpec((rows, D_MODEL), lambda j: (0, j)),
        compiler_params=pltpu.CompilerParams(dimension_semantics=("arbitrary",)),
        name="ada",
    )(c_all, w, b.reshape(1, n))


def _mixer_kernel(xp_ref, xsm_ref, mp_ref, ms_ref, g1_ref, win_ref, wt_ref, bsp_ref, gv_ref, wconv_ref,
                  goa_ref, gob_ref, wout_ref, g2_ref, wrt_ref, wshgu_ref, wshd_ref, csp_ref, bsps_ref, st_ref,
                  xs_out, h2_out, lt_out, cztail_out, vs_out, cz23_out, carry_ref, *, n_ptiles, tiles_per_seq):
    i = pl.program_id(0)
    d = D_MODEL

    def proj(x, sh1, sc1):
        h = _rms(x, g1_ref[...]) * (1.0 + sc1) + sh1
        p = jnp.dot(h.astype(BF16), win_ref[...], preferred_element_type=F32)
        u = jax.nn.gelu(p[:, :MIX_A])
        v = jax.nn.gelu(p[:, MIX_A:2 * MIX_A])
        vn = jnp.concatenate(
            [_rms(v[:, h * LANES:(h + 1) * LANES], gv_ref[:, h * LANES:(h + 1) * LANES]) for h in range(A_HEADS)],
            axis=1)
        o = 2 * MIX_A
        return u, vn, p[:, o:o + CONV_DIM], p[:, o + CONV_DIM:o + 2 * CONV_DIM], p[:, o + 2 * CONV_DIM:]

    def tail(x, ya, yb, gate1, sh2, sc2, gate2):
        cat = jnp.concatenate([_rms(ya, goa_ref[...]), _rms(yb, gob_ref[...])], axis=1).astype(BF16)
        x1 = x + gate1 * jnp.dot(cat, wout_ref[...], preferred_element_type=F32)
        h2 = _rms(x1, g2_ref[...]) * (1.0 + sc2) + sh2
        h2b = h2.astype(BF16)
        lt_out[...] = lax.dot_general(wrt_ref[...], h2b, (((1,), (1,)), ((), ())), preferred_element_type=F32)
        gu = jnp.dot(h2b, wshgu_ref[...], preferred_element_type=F32)
        hs = (jax.nn.silu(gu[:, :D_SHARED]) * gu[:, D_SHARED:]).astype(BF16)
        xs_out[...] = x1 + gate2 * jnp.dot(hs, wshd_ref[...], preferred_element_type=F32)
        _store_packed(h2_out, _pack_bf16_pairs(h2), TM)

    @pl.when(i < n_ptiles)
    def _prompt():
        b = i // tiles_per_seq
        m = mp_ref[pl.ds(b, 1), :]
        mod = [m[:, k * d:(k + 1) * d] for k in range(N_MOD)]
        x = xp_ref[...]
        u, vn, bg, cg, z = proj(x, mod[0], mod[1])
        n_chunks = TM // CHUNK
        ya_cols = []
        for h in range(A_HEADS):
            vh = vn[:, h * LANES:(h + 1) * LANES].astype(BF16)
            rhs = jnp.concatenate([vh[c * CHUNK:(c + 1) * CHUNK] for c in range(n_chunks)], axis=1)
            mix = jnp.dot(wt_ref[h], rhs, preferred_element_type=F32)
            bias = bsp_ref[:, h * LANES:(h + 1) * LANES]
            mix = jnp.concatenate([mix[:, c * LANES:(c + 1) * LANES] + bias for c in range(n_chunks)], axis=0)
            ya_cols.append(u[:, h * LANES:(h + 1) * LANES] * mix)
        ya = jnp.concatenate(ya_cols, axis=1)
        cz = cg * z

        @pl.when(i % tiles_per_seq == 0)
        def _():
            carry_ref[...] = jnp.zeros_like(carry_ref)

        c6 = carry_ref[SUBLANES - 2:SUBLANES - 1, :]
        c7 = carry_ref[SUBLANES - 1:SUBLANES, :]
        r = lax.broadcasted_iota(I32, (TM, 1), 0)
        p1 = jnp.where(r == 0, c7, pltpu.roll(cz, 1, 0))
        p2 = jnp.where(r == 0, c6, jnp.where(r == 1, c7, pltpu.roll(cz, 2, 0)))
        wc = wconv_ref[...]
        yb = bg * (p2 * wc[0:1] + p1 * wc[1:2] + cz * wc[2:3])
        carry_ref[...] = cz[TM - SUBLANES:]
        cztail_out[0] = cz[TM - SUBLANES:]
        tail(x, ya, yb, mod[2], mod[3], mod[4], mod[5])

    @pl.when(i == n_ptiles)
    def _sample():
        ms = ms_ref[...]
        nb = TM // DEC_SEQ

        def mod(k):
            return jnp.concatenate([ms[:, k * d:(k + 1) * d]] * DEC_SEQ, axis=0)

        x = xsm_ref[...]
        u, vn, bg, cg, z = proj(x, mod(0), mod(1))
        vt = [vn[t * nb:(t + 1) * nb] for t in range(DEC_SEQ)]
        mixes = []
        for t in range(DEC_SEQ):
            acc = csp_ref[DEC_SEQ * t:DEC_SEQ * t + 1, :] * vt[0]
            for s in range(1, t + 1):
                acc = acc + csp_ref[DEC_SEQ * t + s:DEC_SEQ * t + s + 1, :] * vt[s]
            mixes.append(acc + bsps_ref[t:t + 1, :])
        ya = u * jnp.concatenate(mixes, axis=0)
        cz = cg * z
        czt = [cz[t * nb:(t + 1) * nb] for t in range(DEC_SEQ)]
        full = [st_ref[0], st_ref[1]] + czt
        wc = wconv_ref[...]
        yc = jnp.concatenate(
            [full[t] * wc[0:1] + full[t + 1] * wc[1:2] + full[t + 2] * wc[2:3] for t in range(DEC_SEQ)], axis=0)
        yb = bg * yc
        vs_out[...] = vn
        cz23_out[0] = czt[DEC_SEQ - 2]
        cz23_out[1] = czt[DEC_SEQ - 1]
        cztail_out[0] = cz[TM - SUBLANES:]
        tail(x, ya, yb, mod(2), mod(3), mod(4), mod(5))


def _mixer(xp, xsm, m_all, g1, win, wt, bsp, gv, wconv, goa, gob, wout, g2, wrt, wshgu, wshd, csp, bsps, st,
           *, n_batch, seq):
    t_p = xp.shape[0]
    n_ptiles = t_p // TM
    n_tiles = n_ptiles + 1
    t_all = n_tiles * TM
    dec_batch = xsm.shape[0] // DEC_SEQ
    d = D_MODEL
    kern = functools.partial(_mixer_kernel, n_ptiles=n_ptiles, tiles_per_seq=seq // TM)
    in_specs = [
        pl.BlockSpec((TM, d), lambda i: (jnp.minimum(i, n_ptiles - 1), 0)),
        _const_spec((TM, d)),
        pl.BlockSpec((n_batch, N_MOD * d), lambda i: (dec_batch // n_batch, 0), pipeline_mode=pl.Buffered(1)),
        pl.BlockSpec((dec_batch, N_MOD * d), lambda i: (0, 0), pipeline_mode=pl.Buffered(1)),
        _const_spec(g1.shape), _const_spec(win.shape), _const_spec(wt.shape), _const_spec(bsp.shape),
        _const_spec(gv.shape), _const_spec(wconv.shape), _const_spec(goa.shape), _const_spec(gob.shape),
        _const_spec(wout.shape), _const_spec(g2.shape), _const_spec(wrt.shape), _const_spec(wshgu.shape),
        _const_spec(wshd.shape), _const_spec(csp.shape), _const_spec(bsps.shape), _const_spec(st.shape),
    ]
    out_shape = [
        jax.ShapeDtypeStruct((t_all, d), F32),
        jax.ShapeDtypeStruct((t_all * PACK_ROWS, LANES), U32),
        jax.ShapeDtypeStruct((N_EXPERTS, t_all), F32),
        jax.ShapeDtypeStruct((n_tiles, SUBLANES, CONV_DIM), F32),
        jax.ShapeDtypeStruct((TM, MIX_A), F32),
        jax.ShapeDtypeStruct((2, dec_batch, CONV_DIM), F32),
    ]
    out_specs = [
        pl.BlockSpec((TM, d), lambda i: (i, 0)),
        pl.BlockSpec((TM * PACK_ROWS, LANES), lambda i: (i, 0)),
        pl.BlockSpec((N_EXPERTS, TM), lambda i: (0, i)),
        pl.BlockSpec((1, SUBLANES, CONV_DIM), lambda i: (i, 0, 0)),
        pl.BlockSpec((TM, MIX_A), lambda i: (0, 0)),
        pl.BlockSpec((2, dec_batch, CONV_DIM), lambda i: (0, 0, 0)),
    ]
    return pl.pallas_call(
        kern,
        out_shape=out_shape,
        grid=(n_tiles,),
        in_specs=in_specs,
        out_specs=out_specs,
        scratch_shapes=[pltpu.VMEM((SUBLANES, CONV_DIM), F32)],
        compiler_params=pltpu.CompilerParams(dimension_semantics=("arbitrary",),
                                             vmem_limit_bytes=VMEM_LIMIT_BYTES),
        name="mixer",
    )(xp, xsm, m_all, m_all, g1, win, wt, bsp, gv, wconv, goa, gob, wout, g2, wrt, wshgu, wshd, csp, bsps, st)


def _route_kernel(lt_ref, br_ref, e_out, w_out, r_out, cnt_out, carry_ref, tri_ref):
    i = pl.program_id(0)
    neg = -jnp.inf

    @pl.when(i == 0)
    def _():
        carry_ref[...] = jnp.zeros_like(carry_ref)
        a = lax.broadcasted_iota(I32, (TM, TM), 0)
        b = lax.broadcasted_iota(I32, (TM, TM), 1)
        tri_ref[...] = jnp.where(a < b, 1.0, 0.0).astype(BF16)

    scores = jax.nn.sigmoid(lt_ref[...])
    sel = scores + br_ref[...]
    iog = lax.broadcasted_iota(I32, (GROUP_SIZE, TM), 0)
    gs_rows = []
    for g in range(N_GROUPS):
        sg = sel[g * GROUP_SIZE:(g + 1) * GROUP_SIZE]
        m1 = jnp.max(sg, axis=0, keepdims=True)
        i1 = jnp.min(jnp.where(sg == m1, iog, GROUP_SIZE), axis=0, keepdims=True)
        m2 = jnp.max(jnp.where(iog == i1, neg, sg), axis=0, keepdims=True)
        gs_rows.append(m1 + m2)
    gs = jnp.concatenate(gs_rows, axis=0)
    io8 = lax.broadcasted_iota(I32, (N_GROUPS, TM), 0)
    keep = jnp.zeros((N_GROUPS, TM), F32)
    for _ in range(TOPK_GROUPS):
        m = jnp.max(gs, axis=0, keepdims=True)
        idx = jnp.min(jnp.where(gs == m, io8, N_GROUPS), axis=0, keepdims=True)
        hit = io8 == idx
        keep = jnp.where(hit, 1.0, keep)
        gs = jnp.where(hit, neg, gs)
    sel = jnp.concatenate(
        [jnp.where(keep[g:g + 1] > 0.0, sel[g * GROUP_SIZE:(g + 1) * GROUP_SIZE], neg) for g in range(N_GROUPS)],
        axis=0)
    ioe = lax.broadcasted_iota(I32, (N_EXPERTS, TM), 0)
    picked = jnp.zeros((N_EXPERTS, TM), F32)
    e_rows, w_rows = [], []
    for _ in range(TOP_K):
        m = jnp.max(sel, axis=0, keepdims=True)
        idx = jnp.min(jnp.where(sel == m, ioe, N_EXPERTS), axis=0, keepdims=True)
        hit = ioe == idx
        e_rows.append(idx)
        w_rows.append(jnp.sum(jnp.where(hit, scores, 0.0), axis=0, keepdims=True))
        picked = jnp.where(hit, 1.0, picked)
        sel = jnp.where(hit, neg, sel)
    wk = jnp.concatenate(w_rows, axis=0)
    w_out[...] = wk / jnp.sum(wk, axis=0, keepdims=True) * ROUTED_SCALE
    e_out[...] = jnp.concatenate(e_rows, axis=0)
    before = jnp.dot(picked.astype(BF16), tri_ref[...], preferred_element_type=F32) + carry_ref[...]
    r_rows = [jnp.sum(jnp.where(ioe == e_rows[k], before, 0.0), axis=0, keepdims=True) for k in range(TOP_K)]
    r_out[...] = jnp.concatenate(r_rows, axis=0).astype(I32)
    carry_ref[...] = carry_ref[...] + jnp.sum(picked, axis=1, keepdims=True)
    cnt_out[...] = carry_ref[...]


def _route(lt, br):
    t_all = lt.shape[1]
    n_tiles = t_all // TM
    return pl.pallas_call(
        _route_kernel,
        out_shape=[
            jax.ShapeDtypeStruct((TOP_K, t_all), I32),
            jax.ShapeDtypeStruct((TOP_K, t_all), F32),
            jax.ShapeDtypeStruct((TOP_K, t_all), I32),
            jax.ShapeDtypeStruct((N_EXPERTS, 1), F32),
        ],
        grid=(n_tiles,),
        in_specs=[pl.BlockSpec((N_EXPERTS, TM), lambda i: (0, i)), _const_spec((N_EXPERTS, 1))],
        out_specs=[
            pl.BlockSpec((TOP_K, TM), lambda i: (0, i)),
            pl.BlockSpec((TOP_K, TM), lambda i: (0, i)),
            pl.BlockSpec((TOP_K, TM), lambda i: (0, i)),
            pl.BlockSpec((N_EXPERTS, 1), lambda i: (0, 0)),
        ],
        scratch_shapes=[pltpu.VMEM((N_EXPERTS, 1), F32), pltpu.VMEM((TM, TM), BF16)],
        compiler_params=pltpu.CompilerParams(dimension_semantics=("arbitrary",)),
        name="route",
    )(lt, br)


def _plan_kernel(cnt_ref, e_ref, r_ref, dest_out, blk_e_out, nxt_e_out, nblk_out, cnt_row_out, pstart_row_out,
                 pstart_ref):
    i = pl.program_id(0)

    @pl.when(i == 0)
    def _():
        nb = jnp.floor((cnt_ref[...] + (BM - 1)) * (1.0 / BM))
        a = lax.broadcasted_iota(I32, (N_EXPERTS, N_EXPERTS), 0)
        b = lax.broadcasted_iota(I32, (N_EXPERTS, N_EXPERTS), 1)
        lower = jnp.where(b < a, 1.0, 0.0).astype(BF16)
        nb_l = jnp.broadcast_to(nb, (N_EXPERTS, LANES)).astype(BF16)
        first_blk = jnp.dot(lower, nb_l, preferred_element_type=F32)[:, 0:1]
        end_blk = first_blk + nb
        pstart_ref[...] = first_blk * BM
        cnt_row_out[...] = jnp.sum(jnp.where(a == b, cnt_ref[...], 0.0), axis=0, keepdims=True).astype(I32)
        pstart_row_out[...] = jnp.sum(jnp.where(a == b, first_blk * BM, 0.0), axis=0, keepdims=True).astype(I32)
        n_lanes = blk_e_out.shape[1]
        blk = lax.broadcasted_iota(I32, (N_EXPERTS, n_lanes), 1).astype(F32)
        owner = jnp.sum(jnp.where(end_blk <= blk, 1.0, 0.0), axis=0, keepdims=True)
        blk_e_out[...] = jnp.minimum(owner, N_EXPERTS - 1.0).astype(I32)
        total = jnp.max(end_blk, axis=0, keepdims=True)
        nblk_out[...] = jnp.broadcast_to(total, nblk_out.shape).astype(I32)
        group_end = jnp.min(jnp.where(end_blk > blk, end_blk, 2.0 * n_lanes), axis=0, keepdims=True)
        nxt = jnp.sum(jnp.where(end_blk <= group_end, 1.0, 0.0), axis=0, keepdims=True)
        nxt_e_out[...] = jnp.where(group_end < total, nxt, -1.0).astype(I32)

    ioe = lax.broadcasted_iota(I32, (N_EXPERTS, TM), 0)
    e = e_ref[...]
    rows = [jnp.sum(jnp.where(ioe == e[k:k + 1], pstart_ref[...], 0.0), axis=0, keepdims=True)
            for k in range(TOP_K)]
    dest_out[...] = jnp.concatenate(rows, axis=0).astype(I32) + r_ref[...]


def _plan(cnt, eidx, rank, n_blocks):
    t_all = eidx.shape[1]
    n_lanes = pl.cdiv(n_blocks, LANES) * LANES
    return pl.pallas_call(
        _plan_kernel,
        out_shape=[
            jax.ShapeDtypeStruct((TOP_K, t_all), I32),
            jax.ShapeDtypeStruct((1, n_lanes), I32),
            jax.ShapeDtypeStruct((1, n_lanes), I32),
            jax.ShapeDtypeStruct((1, LANES), I32),
            jax.ShapeDtypeStruct((1, N_EXPERTS), I32),
            jax.ShapeDtypeStruct((1, N_EXPERTS), I32),
        ],
        grid=(t_all // TM,),
        in_specs=[
            _const_spec((N_EXPERTS, 1)),
            pl.BlockSpec((TOP_K, TM), lambda i: (0, i)),
            pl.BlockSpec((TOP_K, TM), lambda i: (0, i)),
        ],
        out_specs=[
            pl.BlockSpec((TOP_K, TM), lambda i: (0, i)),
            pl.BlockSpec((1, n_lanes), lambda i: (0, 0)),
            pl.BlockSpec((1, n_lanes), lambda i: (0, 0)),
            pl.BlockSpec((1, LANES), lambda i: (0, 0)),
            pl.BlockSpec((1, N_EXPERTS), lambda i: (0, 0)),
            pl.BlockSpec((1, N_EXPERTS), lambda i: (0, 0)),
        ],
        scratch_shapes=[pltpu.VMEM((N_EXPERTS, 1), F32)],
        compiler_params=pltpu.CompilerParams(dimension_semantics=("arbitrary",)),
        name="plan",
    )(cnt, eidx, rank)


def _dispatch_kernel(cnt_ref, pst_ref, nblk_ref, dest_ref, h2_ref, xs_hbm, zeros, sem, fill_sem,
                     *, n_steps, n_blocks):
    i = pl.program_id(0)
    per_step = pl.cdiv(N_EXPERTS, n_steps)

    @pl.when(i == 0)
    def _():
        zeros[...] = jnp.zeros_like(zeros)

    def issue(t, carry):
        for k in range(TOP_K):
            pltpu.make_async_copy(_packed_rows(h2_ref, t, 1), _packed_rows(xs_hbm, dest_ref[t * TOP_K + k], 1),
                                  sem).start(priority=k % DMA_THREADS)
        return carry

    lax.fori_loop(0, TM, issue, 0, unroll=ISSUE_UNROLL)

    def fills(do):
        def per_expert(j, carry):
            e = i * per_step + j

            @pl.when(e < N_EXPERTS)
            def _():
                cnt = cnt_ref[e]
                base = pst_ref[e]
                padded = (cnt + (BM - 1)) // BM * BM
                mid = jnp.minimum((cnt + (SUBLANES - 1)) // SUBLANES * SUBLANES, padded)

                def one(r, c):
                    do(pltpu.make_async_copy(_packed_rows(zeros, 0, 1), _packed_rows(xs_hbm, base + r, 1), fill_sem))
                    return c

                lax.fori_loop(cnt, mid, one, 0)

                def eight(q, c):
                    r = base + mid + q * SUBLANES
                    do(pltpu.make_async_copy(_packed_rows(zeros, 0, SUBLANES), _packed_rows(xs_hbm, r, SUBLANES),
                                             fill_sem))
                    return c

                lax.fori_loop(0, (padded - mid) // SUBLANES, eight, 0)

            return carry

        lax.fori_loop(0, per_step, per_expert, 0)

        def per_block(j, carry):
            b = nblk_ref[0] + i * per_step + j

            @pl.when(b < n_blocks)
            def _():
                do(pltpu.make_async_copy(zeros, _packed_rows(xs_hbm, b * BM, BM), fill_sem))

            return carry

        lax.fori_loop(0, per_step, per_block, 0)

    fills(lambda c: c.start())
    for k in range(TOP_K):
        pltpu.make_async_copy(h2_ref, _packed_rows(xs_hbm, 0, TM), sem).wait()
    fills(lambda c: c.wait())


def _dispatch(cnt_row, pstart_row, nblk, dest, h2, n_blocks):
    t_all = h2.shape[0] // PACK_ROWS
    n_steps = t_all // TM
    kern = functools.partial(_dispatch_kernel, n_steps=n_steps, n_blocks=n_blocks)
    grid_spec = pltpu.PrefetchScalarGridSpec(
        num_scalar_prefetch=3,
        grid=(n_steps,),
        in_specs=[
            pl.BlockSpec((TM * TOP_K,), lambda i, *_: (i,), memory_space=pltpu.SMEM),
            pl.BlockSpec((TM * PACK_ROWS, LANES), lambda i, *_: (i, 0)),
        ],
        out_specs=pl.BlockSpec(memory_space=pl.ANY),
        scratch_shapes=[pltpu.VMEM((BM * PACK_ROWS, LANES), U32), pltpu.SemaphoreType.DMA(()),
                        pltpu.SemaphoreType.DMA(())],
    )
    return pl.pallas_call(
        kern,
        out_shape=jax.ShapeDtypeStruct((n_blocks * BM * PACK_ROWS, LANES), U32),
        grid_spec=grid_spec,
        compiler_params=pltpu.CompilerParams(dimension_semantics=("arbitrary",)),
        name="dispatch",
    )(cnt_row, pstart_row, nblk, dest, h2)


def _moe_kernel(blk_e_ref, nxt_e_ref, nblk_ref, x_hbm, wg_hbm, wu_hbm, wd_hbm, y_ref,
                x_ring, wg_l, wu_l, wd_l, wgu_s, wd_s, xsems, sems, cur_ref):
    nblk = nblk_ref[0]
    first = pl.program_id(0) * MOE_SUB

    def weight_copies(e, slot):
        return (pltpu.make_async_copy(wg_hbm.at[e], wg_l.at[slot], sems.at[slot, 0]),
                pltpu.make_async_copy(wu_hbm.at[e], wu_l.at[slot], sems.at[slot, 1]),
                pltpu.make_async_copy(wd_hbm.at[e], wd_l.at[slot], sems.at[slot, 2]))

    def row_copy(blk):
        slot = blk % X_RING
        return pltpu.make_async_copy(_packed_rows(x_hbm, blk * BM, BM), x_ring.at[slot], xsems.at[slot])

    def prepare(b):
        @pl.when(b < nblk)
        def _():
            e = blk_e_ref[b]

            @pl.when(b == 0)
            def _():
                cur_ref[0] = 0
                for c in weight_copies(e, 0):
                    c.start()
                for j in range(X_AHEAD):
                    @pl.when(j < nblk)
                    def _():
                        row_copy(j).start()

            @pl.when(b + X_AHEAD < nblk)
            def _():
                row_copy(b + X_AHEAD).start()

            @pl.when((b == 0) | (e != blk_e_ref[jnp.maximum(b - 1, 0)]))
            def _():
                @pl.when(b > 0)
                def _():
                    cur_ref[0] = 1 - cur_ref[0]

                slot = cur_ref[0]
                for c in weight_copies(e, slot):
                    c.wait()
                nxt = nxt_e_ref[b]

                @pl.when(nxt >= 0)
                def _():
                    for c in weight_copies(nxt, 1 - slot):
                        c.start()

                wgu_s[slot, :, :D_EXPERT] = wg_l[slot].astype(BF16)
                wgu_s[slot, :, D_EXPERT:] = wu_l[slot].astype(BF16)
                wd_s[slot] = wd_l[slot].astype(BF16)

            row_copy(b).wait()

        return cur_ref[0]

    slots = [prepare(first + j) for j in range(MOE_SUB)]

    @pl.when(first < nblk)
    def _():
        for j in range(MOE_SUB):
            b = first + j
            x = _unpack_bf16_pairs(x_ring.at[b % X_RING], BM).astype(BF16)
            gu = jnp.dot(x, wgu_s[slots[j]], preferred_element_type=F32)
            h = (jax.nn.silu(gu[:, :D_EXPERT]) * gu[:, D_EXPERT:]).astype(BF16)
            y = jnp.dot(h, wd_s[slots[j]], preferred_element_type=F32)
            y = jnp.where(b < nblk, y, 0.0)
            _store_packed(y_ref.at[pl.ds(j * BM * PACK_ROWS, BM * PACK_ROWS)], _pack_bf16_pairs(y), BM)

    @pl.when(first >= nblk)
    def _():
        y_ref[...] = jnp.zeros_like(y_ref)


def _moe(blk_e, nxt_e, nblk, xs, wg, wu, wd):
    d = D_MODEL
    n_blocks = xs.shape[0] // (BM * PACK_ROWS)
    assert n_blocks % MOE_SUB == 0
    grid_spec = pltpu.PrefetchScalarGridSpec(
        num_scalar_prefetch=3,
        grid=(n_blocks // MOE_SUB,),
        in_specs=[
            pl.BlockSpec(memory_space=pl.ANY),
            pl.BlockSpec(memory_space=pl.ANY),
            pl.BlockSpec(memory_space=pl.ANY),
            pl.BlockSpec(memory_space=pl.ANY),
        ],
        out_specs=pl.BlockSpec((MOE_SUB * BM * PACK_ROWS, LANES), lambda s, *_: (s, 0)),
        scratch_shapes=[
            pltpu.VMEM((X_RING, BM * PACK_ROWS, LANES), U32),
            pltpu.VMEM((2, d, D_EXPERT), F32), pltpu.VMEM((2, d, D_EXPERT), F32), pltpu.VMEM((2, D_EXPERT, d), F32),
            pltpu.VMEM((2, d, 2 * D_EXPERT), BF16), pltpu.VMEM((2, D_EXPERT, d), BF16),
            pltpu.SemaphoreType.DMA((X_RING,)), pltpu.SemaphoreType.DMA((2, 3)), pltpu.SMEM((1,), I32),
        ],
    )
    return pl.pallas_call(
        _moe_kernel,
        out_shape=jax.ShapeDtypeStruct((n_blocks * BM * PACK_ROWS, LANES), U32),
        grid_spec=grid_spec,
        compiler_params=pltpu.CompilerParams(dimension_semantics=("arbitrary",),
                                             vmem_limit_bytes=VMEM_LIMIT_BYTES),
        name="moe",
    )(blk_e, nxt_e, nblk, xs, wg, wu, wd)


def _final_kernel(dest_ref, dnext_ref, xs_ref, w_ref, gp_ref, gs_ref, fp_ref, fs_ref, gf_ref, y_hbm,
                  op_ref, os_ref, buf, sems, *, n_tiles, n_ptiles, tiles_per_seq):
    i = pl.program_id(0)
    d = D_MODEL
    slot = i % 2

    def gather(d_ref, s):
        def issue(t, carry):
            for k in range(TOP_K):
                pltpu.make_async_copy(_packed_rows(y_hbm, d_ref[t * TOP_K + k], 1), _packed_rows(buf.at[s, k], t, 1),
                                      sems.at[s]).start(priority=k % DMA_THREADS)
            return carry

        lax.fori_loop(0, TC, issue, 0, unroll=ISSUE_UNROLL)

    @pl.when(i == 0)
    def _():
        gather(dest_ref, 0)

    @pl.when(i + 1 < n_tiles)
    def _():
        gather(dnext_ref, 1 - slot)

    for k in range(TOP_K):
        pltpu.make_async_copy(_packed_rows(y_hbm, 0, TC), buf.at[slot, k], sems.at[slot]).wait()

    w = w_ref[...]
    acc = w[:, 0:1] * _unpack_bf16_pairs(buf.at[slot, 0], TC)
    for k in range(1, TOP_K):
        acc = acc + w[:, k:k + 1] * _unpack_bf16_pairs(buf.at[slot, k], TC)

    def finish(gate2, shift, scale, o_ref):
        x2 = xs_ref[...] + gate2 * acc
        o_ref[...] = _rms(x2, gf_ref[...]) * (1.0 + scale) + shift

    @pl.when(i < n_ptiles)
    def _():
        b = i // tiles_per_seq
        f = fp_ref[pl.ds(b, 1), :]
        finish(gp_ref[pl.ds(b, 1), :], f[:, :d], f[:, d:], op_ref)

    @pl.when(i >= n_ptiles)
    def _():
        f = fs_ref[...]
        finish(gs_ref[...], f[:, :d], f[:, d:], os_ref)


def _final(dest, xs, wts, m_all, mf_all, gf, y_sorted, *, n_batch, seq, t_p):
    t_all, d = xs.shape
    n_tiles = t_all // TC
    n_ptiles = t_p // TC
    dec_batch = m_all.shape[0] - n_batch
    kern = functools.partial(_final_kernel, n_tiles=n_tiles, n_ptiles=n_ptiles, tiles_per_seq=seq // TC)
    pb = dec_batch // n_batch
    return pl.pallas_call(
        kern,
        out_shape=[jax.ShapeDtypeStruct((t_p, d), F32), jax.ShapeDtypeStruct((t_all - t_p, d), F32)],
        grid=(n_tiles,),
        in_specs=[
            pl.BlockSpec((TC * TOP_K,), lambda i: (i,), memory_space=pltpu.SMEM),
            pl.BlockSpec((TC * TOP_K,), lambda i: (jnp.minimum(i + 1, n_tiles - 1),), memory_space=pltpu.SMEM),
            pl.BlockSpec((TC, d), lambda i: (i, 0)),
            pl.BlockSpec((TC, TOP_K), lambda i: (i, 0)),
            pl.BlockSpec((n_batch, d), lambda i: (pb, N_MOD - 1)),
            pl.BlockSpec((dec_batch, d), lambda i: (0, N_MOD - 1)),
            pl.BlockSpec((n_batch, 2 * d), lambda i: (pb, 0)),
            pl.BlockSpec((dec_batch, 2 * d), lambda i: (0, 0)),
            _const_spec(gf.shape),
            pl.BlockSpec(memory_space=pl.ANY),
        ],
        out_specs=[
            pl.BlockSpec((TC, d), lambda i: (jnp.minimum(i, n_ptiles - 1), 0)),
            pl.BlockSpec((TC, d), lambda i: (jnp.maximum(i - n_ptiles, 0), 0)),
        ],
        scratch_shapes=[pltpu.VMEM((2, TOP_K, TC * PACK_ROWS, LANES), U32), pltpu.SemaphoreType.DMA((2,))],
        compiler_params=pltpu.CompilerParams(dimension_semantics=("arbitrary",),
                                             vmem_limit_bytes=VMEM_LIMIT_BYTES),
        name="final",
    )(dest, dest, xs, wts, m_all, m_all, mf_all, mf_all, gf, y_sorted)


def kernel(x_prompt, x_sample, state_conv, c_prompt, c_sample, w_ada, b_ada, g_norm1, w_in, w_spatial, b_spatial, g_v, w_conv, g_out_a, g_out_b, w_out, g_norm2, w_router, b_router, w_exp_gate, w_exp_up, w_exp_down, w_sh_gate, w_sh_up, w_sh_down, w_ada_final, b_ada_final, g_final):
    n_batch, seq, d = x_prompt.shape
    dec_batch, dec_seq, _ = x_sample.shape
    assert w_ada.shape[0] == 1, "one layer only"
    assert d == D_MODEL and dec_seq == DEC_SEQ and dec_batch * dec_seq == TM and seq % TM == 0
    assert n_batch % SUBLANES == 0 and dec_batch % n_batch == 0
    t_p = n_batch * seq
    t_s = dec_batch * dec_seq
    t_all = t_p + t_s

    c_all = jnp.concatenate([c_sample, c_prompt], axis=0)
    m_all = _ada(c_all, w_ada[0], b_ada[0])
    mf_all = _ada(c_all, w_ada_final, b_ada_final)

    rep = functools.partial(jnp.repeat, repeats=A_HEAD_DIM, axis=1)
    tril = jnp.tril(jnp.ones((CHUNK, CHUNK), dtype=bool))
    wt = jnp.where(tril, w_spatial[0], 0.0).astype(BF16)
    bsp = rep(b_spatial[0].T)
    csp = rep(jnp.transpose(w_spatial[0][:, :DEC_SEQ, :DEC_SEQ], (1, 2, 0)).reshape(DEC_SEQ * DEC_SEQ, A_HEADS))
    bsps = rep(b_spatial[0][:, :DEC_SEQ].T)
    st = jnp.transpose(state_conv[0], (1, 0, 2))
    wshgu = jnp.concatenate([w_sh_gate[0], w_sh_up[0]], axis=1).astype(BF16)
    xsm = jnp.transpose(x_sample, (1, 0, 2)).reshape(t_s, d)

    xs, h2, lt, cztail, vs, cz23 = _mixer(
        x_prompt.reshape(t_p, d), xsm, m_all, g_norm1, w_in[0].astype(BF16), wt, bsp,
        g_v.reshape(1, MIX_A), w_conv[0], g_out_a, g_out_b, w_out[0].astype(BF16), g_norm2,
        w_router[0].T.astype(BF16), wshgu, w_sh_down[0].astype(BF16), csp, bsps, st,
        n_batch=n_batch, seq=seq)

    eidx, wts, rank, cnt = _route(lt, b_router[0].reshape(N_EXPERTS, 1))

    n_blocks = (t_all * TOP_K) // BM + N_EXPERTS
    dest, blk_e, nxt_e, nblk, cnt_row, pstart_row = _plan(cnt, eidx, rank, n_blocks)

    dest_tok = dest.T.reshape(t_all * TOP_K)
    x_sorted = _dispatch(cnt_row[0], pstart_row[0], nblk[0, :1], dest_tok, h2, n_blocks)
    y_sorted = _moe(blk_e[0, :n_blocks], nxt_e[0, :n_blocks], nblk[0, :1], x_sorted,
                    w_exp_gate[0], w_exp_up[0], w_exp_down[0])
    y_p, y_s = _final(dest_tok, xs, wts.T, m_all, mf_all, g_final.reshape(1, d), y_sorted,
                      n_batch=n_batch, seq=seq, t_p=t_p)

    tiles_per_seq = seq // TM
    y_prompt = y_p.reshape(n_batch, seq, d)
    y_sample = jnp.transpose(y_s.reshape(dec_seq, dec_batch, d), (1, 0, 2))
    conv_p = cztail[tiles_per_seq - 1:n_batch * tiles_per_seq:tiles_per_seq, SUBLANES - (CONV_W - 1):, :][None]
    conv_s = jnp.transpose(cz23, (1, 0, 2))[None]
    v_s = jnp.transpose(vs.reshape(dec_seq, dec_batch, A_HEADS, A_HEAD_DIM), (1, 0, 2, 3))[None]
    return (y_prompt, y_sample, conv_p, conv_s, v_s)
```

```python
import functools

import jax
import jax.numpy as jnp
from jax import lax
from jax.experimental import pallas as pl
from jax.experimental.pallas import tpu as pltpu

F32 = jnp.float32
BF16 = jnp.bfloat16
I32 = jnp.int32
U32 = jnp.uint32

D_MODEL = 1024
MIX_A = 512
A_HEADS = 4
A_HEAD_DIM = 128
CHUNK = 128
CONV_DIM = 512
CONV_W = 3
PROJ_DIM = 2 * MIX_A + 3 * CONV_DIM
N_EXPERTS = 256
TOP_K = 8
N_GROUPS = 8
TOPK_GROUPS = 4
GROUP_SIZE = N_EXPERTS // N_GROUPS
D_EXPERT = 256
D_SHARED = 256
ROUTED_SCALE = 2.5
N_MOD = 6
RMS_EPS = 1e-6
DEC_SEQ = 4

LANES = 128
SUBLANES = 8
PACK_ROWS = D_MODEL // 2 // LANES
TM = 512
BM = 256
TC = 128
DMA_THREADS = 2
ISSUE_UNROLL = 4
MOE_SUB = 2
W_AHEAD = 2
W_RING = W_AHEAD + 1
X_AHEAD = 2
X_RING = MOE_SUB + X_AHEAD
VMEM_LIMIT_BYTES = 56 * 1024 * 1024


def _rms(x, g):
    return x * lax.rsqrt(jnp.mean(x * x, axis=-1, keepdims=True) + RMS_EPS) * g


def _packed_rows(ref, row, n):
    return ref.at[pl.ds(pl.multiple_of(row * PACK_ROWS, PACK_ROWS), n * PACK_ROWS)]


def _pack_bf16_pairs(x):
    half = D_MODEL // 2
    xb = x.astype(BF16).astype(F32)
    lo = lax.shift_right_logical(lax.bitcast_convert_type(xb[:, :half], U32), jnp.uint32(16))
    return lo | lax.bitcast_convert_type(xb[:, half:], U32)


def _store_packed(ref, packed, n):
    for j in range(PACK_ROWS):
        ref[pl.ds(j, n, stride=PACK_ROWS), :] = packed[:, j * LANES:(j + 1) * LANES]


def _unpack_bf16_pairs(ref, n):
    words = [ref[pl.ds(j, n, stride=PACK_ROWS), :] for j in range(PACK_ROWS)]
    lo = [lax.bitcast_convert_type(lax.shift_left(w, jnp.uint32(16)), F32) for w in words]
    hi = [lax.bitcast_convert_type(w & jnp.uint32(0xFFFF0000), F32) for w in words]
    return jnp.concatenate(lo + hi, axis=1)


def _const_spec(shape):
    nd = len(shape)
    return pl.BlockSpec(shape, lambda *_: (0,) * nd, pipeline_mode=pl.Buffered(1))


def _ada_kernel(c_ref, w_ref, b_ref, o_ref):
    a = jax.nn.silu(c_ref[...]).astype(BF16)
    o_ref[...] = jnp.dot(a, w_ref[...].astype(BF16), preferred_element_type=F32) + b_ref[...]


def _ada(c_all, w, b):
    rows, n = c_all.shape[0], w.shape[1]
    return pl.pallas_call(
        _ada_kernel,
        out_shape=jax.ShapeDtypeStruct((rows, n), F32),
        grid=(n // D_MODEL,),
        in_specs=[
            pl.BlockSpec((rows, D_MODEL), lambda j: (0, 0)),
            pl.BlockSpec((D_MODEL, D_MODEL), lambda j: (0, j)),
            pl.BlockSpec((1, D_MODEL), lambda j: (0, j)),
        ],
        out_specs=pl.BlockSpec((rows, D_MODEL), lambda j: (0, j)),
        compiler_params=pltpu.CompilerParams(dimension_semantics=("arbitrary",)),
        name="ada",
    )(c_all, w, b.reshape(1, n))


def _mixer_kernel(xp_ref, xsm_ref, mp_ref, ms_ref, g1_ref, win_ref, wt_ref, bsp_ref, gv_ref, wconv_ref,
                  goa_ref, gob_ref, wout_ref, g2_ref, wrt_ref, wshgu_ref, wshd_ref, csp_ref, bsps_ref, st_ref,
                  xs_out, h2_out, lt_out, cztail_out, vs_out, cz23_out, carry_ref, *, n_ptiles, tiles_per_seq):
    i = pl.program_id(0)
    d = D_MODEL

    def proj(x, sh1, sc1):
        h = _rms(x, g1_ref[...]) * (1.0 + sc1) + sh1
        p = jnp.dot(h.astype(BF16), win_ref[...], preferred_element_type=F32)
        u = jax.nn.gelu(p[:, :MIX_A])
        v = jax.nn.gelu(p[:, MIX_A:2 * MIX_A])
        vn = jnp.concatenate(
            [_rms(v[:, h * LANES:(h + 1) * LANES], gv_ref[:, h * LANES:(h + 1) * LANES]) for h in range(A_HEADS)],
            axis=1)
        o = 2 * MIX_A
        return u, vn, p[:, o:o + CONV_DIM], p[:, o + CONV_DIM:o + 2 * CONV_DIM], p[:, o + 2 * CONV_DIM:]

    def tail(x, ya, yb, gate1, sh2, sc2, gate2):
        cat = jnp.concatenate([_rms(ya, goa_ref[...]), _rms(yb, gob_ref[...])], axis=1).astype(BF16)
        x1 = x + gate1 * jnp.dot(cat, wout_ref[...], preferred_element_type=F32)
        h2 = _rms(x1, g2_ref[...]) * (1.0 + sc2) + sh2
        h2b = h2.astype(BF16)
        lt_out[...] = lax.dot_general(wrt_ref[...], h2b, (((1,), (1,)), ((), ())), preferred_element_type=F32)
        gu = jnp.dot(h2b, wshgu_ref[...], preferred_element_type=F32)
        hs = (jax.nn.silu(gu[:, :D_SHARED]) * gu[:, D_SHARED:]).astype(BF16)
        xs_out[...] = x1 + gate2 * jnp.dot(hs, wshd_ref[...], preferred_element_type=F32)
        _store_packed(h2_out, _pack_bf16_pairs(h2), TM)

    @pl.when(i < n_ptiles)
    def _prompt():
        b = i // tiles_per_seq
        m = mp_ref[pl.ds(b, 1), :]
        mod = [m[:, k * d:(k + 1) * d] for k in range(N_MOD)]
        x = xp_ref[...]
        u, vn, bg, cg, z = proj(x, mod[0], mod[1])
        n_chunks = TM // CHUNK
        ya_cols = []
        for h in range(A_HEADS):
            vh = vn[:, h * LANES:(h + 1) * LANES].astype(BF16)
            rhs = jnp.concatenate([vh[c * CHUNK:(c + 1) * CHUNK] for c in range(n_chunks)], axis=1)
            mix = jnp.dot(wt_ref[h], rhs, preferred_element_type=F32)
            bias = bsp_ref[:, h * LANES:(h + 1) * LANES]
            mix = jnp.concatenate([mix[:, c * LANES:(c + 1) * LANES] + bias for c in range(n_chunks)], axis=0)
            ya_cols.append(u[:, h * LANES:(h + 1) * LANES] * mix)
        ya = jnp.concatenate(ya_cols, axis=1)
        cz = cg * z

        @pl.when(i % tiles_per_seq == 0)
        def _():
            carry_ref[...] = jnp.zeros_like(carry_ref)

        c6 = carry_ref[SUBLANES - 2:SUBLANES - 1, :]
        c7 = carry_ref[SUBLANES - 1:SUBLANES, :]
        r = lax.broadcasted_iota(I32, (TM, 1), 0)
        p1 = jnp.where(r == 0, c7, pltpu.roll(cz, 1, 0))
        p2 = jnp.where(r == 0, c6, jnp.where(r == 1, c7, pltpu.roll(cz, 2, 0)))
        wc = wconv_ref[...]
        yb = bg * (p2 * wc[0:1] + p1 * wc[1:2] + cz * wc[2:3])
        carry_ref[...] = cz[TM - SUBLANES:]
        cztail_out[0] = cz[TM - SUBLANES:]
        tail(x, ya, yb, mod[2], mod[3], mod[4], mod[5])

    @pl.when(i == n_ptiles)
    def _sample():
        ms = ms_ref[...]
        nb = TM // DEC_SEQ

        def mod(k):
            return jnp.concatenate([ms[:, k * d:(k + 1) * d]] * DEC_SEQ, axis=0)

        x = xsm_ref[...]
        u, vn, bg, cg, z = proj(x, mod(0), mod(1))
        vt = [vn[t * nb:(t + 1) * nb] for t in range(DEC_SEQ)]
        mixes = []
        for t in range(DEC_SEQ):
            acc = csp_ref[DEC_SEQ * t:DEC_SEQ * t + 1, :] * vt[0]
            for s in range(1, t + 1):
                acc = acc + csp_ref[DEC_SEQ * t + s:DEC_SEQ * t + s + 1, :] * vt[s]
            mixes.append(acc + bsps_ref[t:t + 1, :])
        ya = u * jnp.concatenate(mixes, axis=0)
        cz = cg * z
        czt = [cz[t * nb:(t + 1) * nb] for t in range(DEC_SEQ)]
        full = [st_ref[0], st_ref[1]] + czt
        wc = wconv_ref[...]
        yc = jnp.concatenate(
            [full[t] * wc[0:1] + full[t + 1] * wc[1:2] + full[t + 2] * wc[2:3] for t in range(DEC_SEQ)], axis=0)
        yb = bg * yc
        vs_out[...] = vn
        cz23_out[0] = czt[DEC_SEQ - 2]
        cz23_out[1] = czt[DEC_SEQ - 1]
        cztail_out[0] = cz[TM - SUBLANES:]
        tail(x, ya, yb, mod(2), mod(3), mod(4), mod(5))


def _mixer(xp, xsm, m_all, g1, win, wt, bsp, gv, wconv, goa, gob, wout, g2, wrt, wshgu, wshd, csp, bsps, st,
           *, n_batch, seq):
    t_p = xp.shape[0]
    n_ptiles = t_p // TM
    n_tiles = n_ptiles + 1
    t_all = n_tiles * TM
    dec_batch = xsm.shape[0] // DEC_SEQ
    d = D_MODEL
    kern = functools.partial(_mixer_kernel, n_ptiles=n_ptiles, tiles_per_seq=seq // TM)
    in_specs = [
        pl.BlockSpec((TM, d), lambda i: (jnp.minimum(i, n_ptiles - 1), 0)),
        _const_spec((TM, d)),
        pl.BlockSpec((n_batch, N_MOD * d), lambda i: (dec_batch // n_batch, 0), pipeline_mode=pl.Buffered(1)),
        pl.BlockSpec((dec_batch, N_MOD * d), lambda i: (0, 0), pipeline_mode=pl.Buffered(1)),
        _const_spec(g1.shape), _const_spec(win.shape), _const_spec(wt.shape), _const_spec(bsp.shape),
        _const_spec(gv.shape), _const_spec(wconv.shape), _const_spec(goa.shape), _const_spec(gob.shape),
        _const_spec(wout.shape), _const_spec(g2.shape), _const_spec(wrt.shape), _const_spec(wshgu.shape),
        _const_spec(wshd.shape), _const_spec(csp.shape), _const_spec(bsps.shape), _const_spec(st.shape),
    ]
    out_shape = [
        jax.ShapeDtypeStruct((t_all, d), F32),
        jax.ShapeDtypeStruct((t_all * PACK_ROWS, LANES), U32),
        jax.ShapeDtypeStruct((N_EXPERTS, t_all), F32),
        jax.ShapeDtypeStruct((n_tiles, SUBLANES, CONV_DIM), F32),
        jax.ShapeDtypeStruct((TM, MIX_A), F32),
        jax.ShapeDtypeStruct((2, dec_batch, CONV_DIM), F32),
    ]
    out_specs = [
        pl.BlockSpec((TM, d), lambda i: (i, 0)),
        pl.BlockSpec((TM * PACK_ROWS, LANES), lambda i: (i, 0)),
        pl.BlockSpec((N_EXPERTS, TM), lambda i: (0, i)),
        pl.BlockSpec((1, SUBLANES, CONV_DIM), lambda i: (i, 0, 0)),
        pl.BlockSpec((TM, MIX_A), lambda i: (0, 0)),
        pl.BlockSpec((2, dec_batch, CONV_DIM), lambda i: (0, 0, 0)),
    ]
    return pl.pallas_call(
        kern,
        out_shape=out_shape,
        grid=(n_tiles,),
        in_specs=in_specs,
        out_specs=out_specs,
        scratch_shapes=[pltpu.VMEM((SUBLANES, CONV_DIM), F32)],
        compiler_params=pltpu.CompilerParams(dimension_semantics=("arbitrary",),
                                             vmem_limit_bytes=VMEM_LIMIT_BYTES),
        name="mixer",
    )(xp, xsm, m_all, m_all, g1, win, wt, bsp, gv, wconv, goa, gob, wout, g2, wrt, wshgu, wshd, csp, bsps, st)


def _route_kernel(lt_ref, br_ref, e_out, w_out, r_out, cnt_out, carry_ref, tri_ref):
    i = pl.program_id(0)
    neg = -jnp.inf

    @pl.when(i == 0)
    def _():
        carry_ref[...] = jnp.zeros_like(carry_ref)
        a = lax.broadcasted_iota(I32, (TM, TM), 0)
        b = lax.broadcasted_iota(I32, (TM, TM), 1)
        tri_ref[...] = jnp.where(a < b, 1.0, 0.0).astype(BF16)

    scores = jax.nn.sigmoid(lt_ref[...])
    sel = scores + br_ref[...]
    iog = lax.broadcasted_iota(I32, (GROUP_SIZE, TM), 0)
    gs_rows = []
    for g in range(N_GROUPS):
        sg = sel[g * GROUP_SIZE:(g + 1) * GROUP_SIZE]
        m1 = jnp.max(sg, axis=0, keepdims=True)
        i1 = jnp.min(jnp.where(sg == m1, iog, GROUP_SIZE), axis=0, keepdims=True)
        m2 = jnp.max(jnp.where(iog == i1, neg, sg), axis=0, keepdims=True)
        gs_rows.append(m1 + m2)
    gs = jnp.concatenate(gs_rows, axis=0)
    io8 = lax.broadcasted_iota(I32, (N_GROUPS, TM), 0)
    keep = jnp.zeros((N_GROUPS, TM), F32)
    for _ in range(TOPK_GROUPS):
        m = jnp.max(gs, axis=0, keepdims=True)
        idx = jnp.min(jnp.where(gs == m, io8, N_GROUPS), axis=0, keepdims=True)
        hit = io8 == idx
        keep = jnp.where(hit, 1.0, keep)
        gs = jnp.where(hit, neg, gs)
    sel = jnp.concatenate(
        [jnp.where(keep[g:g + 1] > 0.0, sel[g * GROUP_SIZE:(g + 1) * GROUP_SIZE], neg) for g in range(N_GROUPS)],
        axis=0)
    ioe = lax.broadcasted_iota(I32, (N_EXPERTS, TM), 0)
    picked = jnp.zeros((N_EXPERTS, TM), F32)
    e_rows, w_rows = [], []
    for _ in range(TOP_K):
        m = jnp.max(sel, axis=0, keepdims=True)
        idx = jnp.min(jnp.where(sel == m, ioe, N_EXPERTS), axis=0, keepdims=True)
        hit = ioe == idx
        e_rows.append(idx)
        w_rows.append(jnp.sum(jnp.where(hit, scores, 0.0), axis=0, keepdims=True))
        picked = jnp.where(hit, 1.0, picked)
        sel = jnp.where(hit, neg, sel)
    wk = jnp.concatenate(w_rows, axis=0)
    w_out[...] = wk / jnp.sum(wk, axis=0, keepdims=True) * ROUTED_SCALE
    e_out[...] = jnp.concatenate(e_rows, axis=0)
    before = jnp.dot(picked.astype(BF16), tri_ref[...], preferred_element_type=F32) + carry_ref[...]
    r_rows = [jnp.sum(jnp.where(ioe == e_rows[k], before, 0.0), axis=0, keepdims=True) for k in range(TOP_K)]
    r_out[...] = jnp.concatenate(r_rows, axis=0).astype(I32)
    carry_ref[...] = carry_ref[...] + jnp.sum(picked, axis=1, keepdims=True)
    cnt_out[...] = carry_ref[...]


def _route(lt, br):
    t_all = lt.shape[1]
    n_tiles = t_all // TM
    return pl.pallas_call(
        _route_kernel,
        out_shape=[
            jax.ShapeDtypeStruct((TOP_K, t_all), I32),
            jax.ShapeDtypeStruct((TOP_K, t_all), F32),
            jax.ShapeDtypeStruct((TOP_K, t_all), I32),
            jax.ShapeDtypeStruct((N_EXPERTS, 1), F32),
        ],
        grid=(n_tiles,),
        in_specs=[pl.BlockSpec((N_EXPERTS, TM), lambda i: (0, i)), _const_spec((N_EXPERTS, 1))],
        out_specs=[
            pl.BlockSpec((TOP_K, TM), lambda i: (0, i)),
            pl.BlockSpec((TOP_K, TM), lambda i: (0, i)),
            pl.BlockSpec((TOP_K, TM), lambda i: (0, i)),
            pl.BlockSpec((N_EXPERTS, 1), lambda i: (0, 0)),
        ],
        scratch_shapes=[pltpu.VMEM((N_EXPERTS, 1), F32), pltpu.VMEM((TM, TM), BF16)],
        compiler_params=pltpu.CompilerParams(dimension_semantics=("arbitrary",)),
        name="route",
    )(lt, br)


def _plan_kernel(cnt_ref, e_ref, r_ref, dest_out, blk_e_out, nxt_e_out, nblk_out, cnt_row_out, pstart_row_out,
                 pstart_ref):
    i = pl.program_id(0)

    @pl.when(i == 0)
    def _():
        nb = jnp.floor((cnt_ref[...] + (BM - 1)) * (1.0 / BM))
        a = lax.broadcasted_iota(I32, (N_EXPERTS, N_EXPERTS), 0)
        b = lax.broadcasted_iota(I32, (N_EXPERTS, N_EXPERTS), 1)
        lower = jnp.where(b < a, 1.0, 0.0).astype(BF16)
        nb_l = jnp.broadcast_to(nb, (N_EXPERTS, LANES)).astype(BF16)
        first_blk = jnp.dot(lower, nb_l, preferred_element_type=F32)[:, 0:1]
        end_blk = first_blk + nb
        pstart_ref[...] = first_blk * BM
        cnt_row_out[...] = jnp.sum(jnp.where(a == b, cnt_ref[...], 0.0), axis=0, keepdims=True).astype(I32)
        pstart_row_out[...] = jnp.sum(jnp.where(a == b, first_blk * BM, 0.0), axis=0, keepdims=True).astype(I32)
        n_lanes = blk_e_out.shape[1]
        blk = lax.broadcasted_iota(I32, (N_EXPERTS, n_lanes), 1).astype(F32)
        owner = jnp.sum(jnp.where(end_blk <= blk, 1.0, 0.0), axis=0, keepdims=True)
        blk_e_out[...] = jnp.minimum(owner, N_EXPERTS - 1.0).astype(I32)
        total = jnp.max(end_blk, axis=0, keepdims=True)
        nblk_out[...] = jnp.broadcast_to(total, nblk_out.shape).astype(I32)
        group_end = blk
        for h in range(W_AHEAD):
            group_end = jnp.min(jnp.where(end_blk > group_end, end_blk, 2.0 * n_lanes), axis=0, keepdims=True)
            nxt = jnp.sum(jnp.where(end_blk <= group_end, 1.0, 0.0), axis=0, keepdims=True)
            nxt_e_out[h:h + 1, :] = jnp.where(group_end < total, nxt, -1.0).astype(I32)

    ioe = lax.broadcasted_iota(I32, (N_EXPERTS, TM), 0)
    e = e_ref[...]
    rows = [jnp.sum(jnp.where(ioe == e[k:k + 1], pstart_ref[...], 0.0), axis=0, keepdims=True)
            for k in range(TOP_K)]
    dest_out[...] = jnp.concatenate(rows, axis=0).astype(I32) + r_ref[...]


def _plan(cnt, eidx, rank, n_blocks):
    t_all = eidx.shape[1]
    n_lanes = pl.cdiv(n_blocks, LANES) * LANES
    return pl.pallas_call(
        _plan_kernel,
        out_shape=[
            jax.ShapeDtypeStruct((TOP_K, t_all), I32),
            jax.ShapeDtypeStruct((1, n_lanes), I32),
            jax.ShapeDtypeStruct((W_AHEAD, n_lanes), I32),
            jax.ShapeDtypeStruct((1, LANES), I32),
            jax.ShapeDtypeStruct((1, N_EXPERTS), I32),
            jax.ShapeDtypeStruct((1, N_EXPERTS), I32),
        ],
        grid=(t_all // TM,),
        in_specs=[
            _const_spec((N_EXPERTS, 1)),
            pl.BlockSpec((TOP_K, TM), lambda i: (0, i)),
            pl.BlockSpec((TOP_K, TM), lambda i: (0, i)),
        ],
        out_specs=[
            pl.BlockSpec((TOP_K, TM), lambda i: (0, i)),
            pl.BlockSpec((1, n_lanes), lambda i: (0, 0)),
            pl.BlockSpec((W_AHEAD, n_lanes), lambda i: (0, 0)),
            pl.BlockSpec((1, LANES), lambda i: (0, 0)),
            pl.BlockSpec((1, N_EXPERTS), lambda i: (0, 0)),
            pl.BlockSpec((1, N_EXPERTS), lambda i: (0, 0)),
        ],
        scratch_shapes=[pltpu.VMEM((N_EXPERTS, 1), F32)],
        compiler_params=pltpu.CompilerParams(dimension_semantics=("arbitrary",)),
        name="plan",
    )(cnt, eidx, rank)


def _dispatch_kernel(cnt_ref, pst_ref, nblk_ref, dest_ref, h2_ref, xs_hbm, zeros, sem, fill_sem,
                     *, n_steps, n_blocks):
    i = pl.program_id(0)
    per_step = pl.cdiv(N_EXPERTS, n_steps)

    @pl.when(i == 0)
    def _():
        zeros[...] = jnp.zeros_like(zeros)

    def issue(t, carry):
        for k in range(TOP_K):
            pltpu.make_async_copy(_packed_rows(h2_ref, t, 1), _packed_rows(xs_hbm, dest_ref[t * TOP_K + k], 1),
                                  sem).start(priority=k % DMA_THREADS)
        return carry

    lax.fori_loop(0, TM, issue, 0, unroll=ISSUE_UNROLL)

    def fills(do):
        def per_expert(j, carry):
            e = i * per_step + j

            @pl.when(e < N_EXPERTS)
            def _():
                cnt = cnt_ref[e]
                base = pst_ref[e]
                padded = (cnt + (BM - 1)) // BM * BM
                mid = jnp.minimum((cnt + (SUBLANES - 1)) // SUBLANES * SUBLANES, padded)

                def one(r, c):
                    do(pltpu.make_async_copy(_packed_rows(zeros, 0, 1), _packed_rows(xs_hbm, base + r, 1), fill_sem))
                    return c

                lax.fori_loop(cnt, mid, one, 0)

                def eight(q, c):
                    r = base + mid + q * SUBLANES
                    do(pltpu.make_async_copy(_packed_rows(zeros, 0, SUBLANES), _packed_rows(xs_hbm, r, SUBLANES),
                                             fill_sem))
                    return c

                lax.fori_loop(0, (padded - mid) // SUBLANES, eight, 0)

            return carry

        lax.fori_loop(0, per_step, per_expert, 0)

        def per_block(j, carry):
            b = nblk_ref[0] + i * per_step + j

            @pl.when(b < n_blocks)
            def _():
                do(pltpu.make_async_copy(zeros, _packed_rows(xs_hbm, b * BM, BM), fill_sem))

            return carry

        lax.fori_loop(0, per_step, per_block, 0)

    fills(lambda c: c.start())
    for k in range(TOP_K):
        pltpu.make_async_copy(h2_ref, _packed_rows(xs_hbm, 0, TM), sem).wait()
    fills(lambda c: c.wait())


def _dispatch(cnt_row, pstart_row, nblk, dest, h2, n_blocks):
    t_all = h2.shape[0] // PACK_ROWS
    n_steps = t_all // TM
    kern = functools.partial(_dispatch_kernel, n_steps=n_steps, n_blocks=n_blocks)
    grid_spec = pltpu.PrefetchScalarGridSpec(
        num_scalar_prefetch=3,
        grid=(n_steps,),
        in_specs=[
            pl.BlockSpec((TM * TOP_K,), lambda i, *_: (i,), memory_space=pltpu.SMEM),
            pl.BlockSpec((TM * PACK_ROWS, LANES), lambda i, *_: (i, 0)),
        ],
        out_specs=pl.BlockSpec(memory_space=pl.ANY),
        scratch_shapes=[pltpu.VMEM((BM * PACK_ROWS, LANES), U32), pltpu.SemaphoreType.DMA(()),
                        pltpu.SemaphoreType.DMA(())],
    )
    return pl.pallas_call(
        kern,
        out_shape=jax.ShapeDtypeStruct((n_blocks * BM * PACK_ROWS, LANES), U32),
        grid_spec=grid_spec,
        compiler_params=pltpu.CompilerParams(dimension_semantics=("arbitrary",)),
        name="dispatch",
    )(cnt_row, pstart_row, nblk, dest, h2)


def _moe_kernel(blk_e_ref, nxt_e_ref, nblk_ref, x_hbm, wg_hbm, wu_hbm, wd_hbm, y_ref,
                x_ring, wg_l, wu_l, wd_l, wgu_s, wd_s, xsems, sems, cur_ref):
    nblk = nblk_ref[0]
    first = pl.program_id(0) * MOE_SUB

    def weight_copies(e, slot):
        return (pltpu.make_async_copy(wg_hbm.at[e], wg_l.at[slot], sems.at[slot, 0]),
                pltpu.make_async_copy(wu_hbm.at[e], wu_l.at[slot], sems.at[slot, 1]),
                pltpu.make_async_copy(wd_hbm.at[e], wd_l.at[slot], sems.at[slot, 2]))

    def row_copy(blk):
        slot = blk % X_RING
        return pltpu.make_async_copy(_packed_rows(x_hbm, blk * BM, BM), x_ring.at[slot], xsems.at[slot])

    def prepare(b):
        @pl.when(b < nblk)
        def _():
            e = blk_e_ref[b]

            @pl.when(b == 0)
            def _():
                cur_ref[0] = 0
                for c in weight_copies(e, 0):
                    c.start()
                for h in range(W_AHEAD - 1):
                    ahead = nxt_e_ref[h, 0]

                    @pl.when(ahead >= 0)
                    def _():
                        for c in weight_copies(ahead, h + 1):
                            c.start()

                for j in range(X_AHEAD):
                    @pl.when(j < nblk)
                    def _():
                        row_copy(j).start()

            @pl.when(b + X_AHEAD < nblk)
            def _():
                row_copy(b + X_AHEAD).start()

            @pl.when((b == 0) | (e != blk_e_ref[jnp.maximum(b - 1, 0)]))
            def _():
                @pl.when(b > 0)
                def _():
                    cur_ref[0] = cur_ref[0] + 1

                group = cur_ref[0]
                landed = group % W_RING
                for c in weight_copies(e, landed):
                    c.wait()
                ahead = nxt_e_ref[W_AHEAD - 1, b]

                @pl.when(ahead >= 0)
                def _():
                    for c in weight_copies(ahead, (group + W_AHEAD) % W_RING):
                        c.start()

                slot = group % 2
                wgu_s[slot, :, :D_EXPERT] = wg_l[landed].astype(BF16)
                wgu_s[slot, :, D_EXPERT:] = wu_l[landed].astype(BF16)
                wd_s[slot] = wd_l[landed].astype(BF16)

            row_copy(b).wait()

        return cur_ref[0] % 2

    slots = [prepare(first + j) for j in range(MOE_SUB)]

    @pl.when(first < nblk)
    def _():
        for j in range(MOE_SUB):
            b = first + j
            x = _unpack_bf16_pairs(x_ring.at[b % X_RING], BM).astype(BF16)
            gu = jnp.dot(x, wgu_s[slots[j]], preferred_element_type=F32)
            h = (jax.nn.silu(gu[:, :D_EXPERT]) * gu[:, D_EXPERT:]).astype(BF16)
            y = jnp.dot(h, wd_s[slots[j]], preferred_element_type=F32)
            y = jnp.where(b < nblk, y, 0.0)
            _store_packed(y_ref.at[pl.ds(j * BM * PACK_ROWS, BM * PACK_ROWS)], _pack_bf16_pairs(y), BM)

    @pl.when(first >= nblk)
    def _():
        y_ref[...] = jnp.zeros_like(y_ref)


def _moe(blk_e, nxt_e, nblk, xs, wg, wu, wd):
    d = D_MODEL
    n_blocks = xs.shape[0] // (BM * PACK_ROWS)
    assert n_blocks % MOE_SUB == 0
    grid_spec = pltpu.PrefetchScalarGridSpec(
        num_scalar_prefetch=3,
        grid=(n_blocks // MOE_SUB,),
        in_specs=[
            pl.BlockSpec(memory_space=pl.ANY),
            pl.BlockSpec(memory_space=pl.ANY),
            pl.BlockSpec(memory_space=pl.ANY),
            pl.BlockSpec(memory_space=pl.ANY),
        ],
        out_specs=pl.BlockSpec((MOE_SUB * BM * PACK_ROWS, LANES), lambda s, *_: (s, 0)),
        scratch_shapes=[
            pltpu.VMEM((X_RING, BM * PACK_ROWS, LANES), U32),
            pltpu.VMEM((W_RING, d, D_EXPERT), F32), pltpu.VMEM((W_RING, d, D_EXPERT), F32),
            pltpu.VMEM((W_RING, D_EXPERT, d), F32),
            pltpu.VMEM((2, d, 2 * D_EXPERT), BF16), pltpu.VMEM((2, D_EXPERT, d), BF16),
            pltpu.SemaphoreType.DMA((X_RING,)), pltpu.SemaphoreType.DMA((W_RING, 3)), pltpu.SMEM((1,), I32),
        ],
    )
    return pl.pallas_call(
        _moe_kernel,
        out_shape=jax.ShapeDtypeStruct((n_blocks * BM * PACK_ROWS, LANES), U32),
        grid_spec=grid_spec,
        compiler_params=pltpu.CompilerParams(dimension_semantics=("arbitrary",),
                                             vmem_limit_bytes=VMEM_LIMIT_BYTES),
        name="moe",
    )(blk_e, nxt_e, nblk, xs, wg, wu, wd)


def _final_kernel(dest_ref, dnext_ref, xs_ref, w_ref, gp_ref, gs_ref, fp_ref, fs_ref, gf_ref, y_hbm,
                  op_ref, os_ref, buf, sems, *, n_tiles, n_ptiles, tiles_per_seq):
    i = pl.program_id(0)
    d = D_MODEL
    slot = i % 2

    def gather(d_ref, s):
        def issue(t, carry):
            for k in range(TOP_K):
                pltpu.make_async_copy(_packed_rows(y_hbm, d_ref[t * TOP_K + k], 1), _packed_rows(buf.at[s, k], t, 1),
                                      sems.at[s]).start(priority=k % DMA_THREADS)
            return carry

        lax.fori_loop(0, TC, issue, 0, unroll=ISSUE_UNROLL)

    @pl.when(i == 0)
    def _():
        gather(dest_ref, 0)

    @pl.when(i + 1 < n_tiles)
    def _():
        gather(dnext_ref, 1 - slot)

    for k in range(TOP_K):
        pltpu.make_async_copy(_packed_rows(y_hbm, 0, TC), buf.at[slot, k], sems.at[slot]).wait()

    w = w_ref[...]
    acc = w[:, 0:1] * _unpack_bf16_pairs(buf.at[slot, 0], TC)
    for k in range(1, TOP_K):
        acc = acc + w[:, k:k + 1] * _unpack_bf16_pairs(buf.at[slot, k], TC)

    def finish(gate2, shift, scale, o_ref):
        x2 = xs_ref[...] + gate2 * acc
        o_ref[...] = _rms(x2, gf_ref[...]) * (1.0 + scale) + shift

    @pl.when(i < n_ptiles)
    def _():
        b = i // tiles_per_seq
        f = fp_ref[pl.ds(b, 1), :]
        finish(gp_ref[pl.ds(b, 1), :], f[:, :d], f[:, d:], op_ref)

    @pl.when(i >= n_ptiles)
    def _():
        f = fs_ref[...]
        finish(gs_ref[...], f[:, :d], f[:, d:], os_ref)


def _final(dest, xs, wts, m_all, mf_all, gf, y_sorted, *, n_batch, seq, t_p):
    t_all, d = xs.shape
    n_tiles = t_all // TC
    n_ptiles = t_p // TC
    dec_batch = m_all.shape[0] - n_batch
    kern = functools.partial(_final_kernel, n_tiles=n_tiles, n_ptiles=n_ptiles, tiles_per_seq=seq // TC)
    pb = dec_batch // n_batch
    return pl.pallas_call(
        kern,
        out_shape=[jax.ShapeDtypeStruct((t_p, d), F32), jax.ShapeDtypeStruct((t_all - t_p, d), F32)],
        grid=(n_tiles,),
        in_specs=[
            pl.BlockSpec((TC * TOP_K,), lambda i: (i,), memory_space=pltpu.SMEM),
            pl.BlockSpec((TC * TOP_K,), lambda i: (jnp.minimum(i + 1, n_tiles - 1),), memory_space=pltpu.SMEM),
            pl.BlockSpec((TC, d), lambda i: (i, 0)),
            pl.BlockSpec((TC, TOP_K), lambda i: (i, 0)),
            pl.BlockSpec((n_batch, d), lambda i: (pb, N_MOD - 1)),
            pl.BlockSpec((dec_batch, d), lambda i: (0, N_MOD - 1)),
            pl.BlockSpec((n_batch, 2 * d), lambda i: (pb, 0)),
            pl.BlockSpec((dec_batch, 2 * d), lambda i: (0, 0)),
            _const_spec(gf.shape),
            pl.BlockSpec(memory_space=pl.ANY),
        ],
        out_specs=[
            pl.BlockSpec((TC, d), lambda i: (jnp.minimum(i, n_ptiles - 1), 0)),
            pl.BlockSpec((TC, d), lambda i: (jnp.maximum(i - n_ptiles, 0), 0)),
        ],
        scratch_shapes=[pltpu.VMEM((2, TOP_K, TC * PACK_ROWS, LANES), U32), pltpu.SemaphoreType.DMA((2,))],
        compiler_params=pltpu.CompilerParams(dimension_semantics=("arbitrary",),
                                             vmem_limit_bytes=VMEM_LIMIT_BYTES),
        name="final",
    )(dest, dest, xs, wts, m_all, m_all, mf_all, mf_all, gf, y_sorted)


def kernel(x_prompt, x_sample, state_conv, c_prompt, c_sample, w_ada, b_ada, g_norm1, w_in, w_spatial, b_spatial, g_v, w_conv, g_out_a, g_out_b, w_out, g_norm2, w_router, b_router, w_exp_gate, w_exp_up, w_exp_down, w_sh_gate, w_sh_up, w_sh_down, w_ada_final, b_ada_final, g_final):
    n_batch, seq, d = x_prompt.shape
    dec_batch, dec_seq, _ = x_sample.shape
    assert w_ada.shape[0] == 1, "one layer only"
    assert d == D_MODEL and dec_seq == DEC_SEQ and dec_batch * dec_seq == TM and seq % TM == 0
    assert n_batch % SUBLANES == 0 and dec_batch % n_batch == 0
    t_p = n_batch * seq
    t_s = dec_batch * dec_seq
    t_all = t_p + t_s

    c_all = jnp.concatenate([c_sample, c_prompt], axis=0)
    m_all = _ada(c_all, w_ada[0], b_ada[0])
    mf_all = _ada(c_all, w_ada_final, b_ada_final)

    rep = functools.partial(jnp.repeat, repeats=A_HEAD_DIM, axis=1)
    tril = jnp.tril(jnp.ones((CHUNK, CHUNK), dtype=bool))
    wt = jnp.where(tril, w_spatial[0], 0.0).astype(BF16)
    bsp = rep(b_spatial[0].T)
    csp = rep(jnp.transpose(w_spatial[0][:, :DEC_SEQ, :DEC_SEQ], (1, 2, 0)).reshape(DEC_SEQ * DEC_SEQ, A_HEADS))
    bsps = rep(b_spatial[0][:, :DEC_SEQ].T)
    st = jnp.transpose(state_conv[0], (1, 0, 2))
    wshgu = jnp.concatenate([w_sh_gate[0], w_sh_up[0]], axis=1).astype(BF16)
    xsm = jnp.transpose(x_sample, (1, 0, 2)).reshape(t_s, d)

    xs, h2, lt, cztail, vs, cz23 = _mixer(
        x_prompt.reshape(t_p, d), xsm, m_all, g_norm1, w_in[0].astype(BF16), wt, bsp,
        g_v.reshape(1, MIX_A), w_conv[0], g_out_a, g_out_b, w_out[0].astype(BF16), g_norm2,
        w_router[0].T.astype(BF16), wshgu, w_sh_down[0].astype(BF16), csp, bsps, st,
        n_batch=n_batch, seq=seq)

    eidx, wts, rank, cnt = _route(lt, b_router[0].reshape(N_EXPERTS, 1))

    n_blocks = (t_all * TOP_K) // BM + N_EXPERTS
    dest, blk_e, nxt_e, nblk, cnt_row, pstart_row = _plan(cnt, eidx, rank, n_blocks)

    dest_tok = dest.T.reshape(t_all * TOP_K)
    x_sorted = _dispatch(cnt_row[0], pstart_row[0], nblk[0, :1], dest_tok, h2, n_blocks)
    y_sorted = _moe(blk_e[0, :n_blocks], nxt_e[:, :n_blocks], nblk[0, :1], x_sorted,
                    w_exp_gate[0], w_exp_up[0], w_exp_down[0])
    y_p, y_s = _final(dest_tok, xs, wts.T, m_all, mf_all, g_final.reshape(1, d), y_sorted,
                      n_batch=n_batch, seq=seq, t_p=t_p)

    tiles_per_seq = seq // TM
    y_prompt = y_p.reshape(n_batch, seq, d)
    y_sample = jnp.transpose(y_s.reshape(dec_seq, dec_batch, d), (1, 0, 2))
    conv_p = cztail[tiles_per_seq - 1:n_batch * tiles_per_seq:tiles_per_seq, SUBLANES - (CONV_W - 1):, :][None]
    conv_s = jnp.transpose(cz23, (1, 0, 2))[None]
    v_s = jnp.transpose(vs.reshape(dec_seq, dec_batch, A_HEADS, A_HEAD_DIM), (1, 0, 2, 3))[None]
    return (y_prompt, y_sample, conv_p, conv_s, v_s)
```

```python
import functools

import jax
import jax.numpy as jnp
from jax import lax
from jax.experimental import pallas as pl
from jax.experimental.pallas import tpu as pltpu

F32 = jnp.float32
BF16 = jnp.bfloat16
I32 = jnp.int32
U32 = jnp.uint32

D_MODEL = 1024
MIX_A = 512
A_HEADS = 4
A_HEAD_DIM = 128
CHUNK = 128
CONV_DIM = 512
CONV_W = 3
PROJ_DIM = 2 * MIX_A + 3 * CONV_DIM
N_EXPERTS = 256
TOP_K = 8
N_GROUPS = 8
TOPK_GROUPS = 4
GROUP_SIZE = N_EXPERTS // N_GROUPS
D_EXPERT = 256
D_SHARED = 256
ROUTED_SCALE = 2.5
N_MOD = 6
RMS_EPS = 1e-6
DEC_SEQ = 4

LANES = 128
SUBLANES = 8
PACK_ROWS = D_MODEL // 2 // LANES
TM = 512
BM = 256
TC = 128
DMA_THREADS = 2
ISSUE_UNROLL = 8
MOE_SUB = 2
W_AHEAD = 2
W_RING = W_AHEAD + 1
X_AHEAD = 2
X_RING = MOE_SUB + X_AHEAD
VMEM_LIMIT_BYTES = 56 * 1024 * 1024


def _rms(x, g):
    return x * lax.rsqrt(jnp.mean(x * x, axis=-1, keepdims=True) + RMS_EPS) * g


def _packed_rows(ref, row, n):
    return ref.at[pl.ds(pl.multiple_of(row * PACK_ROWS, PACK_ROWS), n * PACK_ROWS)]


def _pack_bf16_pairs(x):
    half = D_MODEL // 2
    xb = x.astype(BF16).astype(F32)
    lo = lax.shift_right_logical(lax.bitcast_convert_type(xb[:, :half], U32), jnp.uint32(16))
    return lo | lax.bitcast_convert_type(xb[:, half:], U32)


def _store_packed(ref, packed, n):
    for j in range(PACK_ROWS):
        ref[pl.ds(j, n, stride=PACK_ROWS), :] = packed[:, j * LANES:(j + 1) * LANES]


def _unpack_bf16_pairs(ref, n):
    words = [ref[pl.ds(j, n, stride=PACK_ROWS), :] for j in range(PACK_ROWS)]
    lo = [lax.bitcast_convert_type(lax.shift_left(w, jnp.uint32(16)), F32) for w in words]
    hi = [lax.bitcast_convert_type(w & jnp.uint32(0xFFFF0000), F32) for w in words]
    return jnp.concatenate(lo + hi, axis=1)


def _const_spec(shape):
    nd = len(shape)
    return pl.BlockSpec(shape, lambda *_: (0,) * nd, pipeline_mode=pl.Buffered(1))


def _ada_kernel(c_ref, w_ref, b_ref, o_ref):
    a = jax.nn.silu(c_ref[...]).astype(BF16)
    o_ref[...] = jnp.dot(a, w_ref[...].astype(BF16), preferred_element_type=F32) + b_ref[...]


def _ada(c_all, w, b):
    rows, n = c_all.shape[0], w.shape[1]
    return pl.pallas_call(
        _ada_kernel,
        out_shape=jax.ShapeDtypeStruct((rows, n), F32),
        grid=(n // D_MODEL,),
        in_specs=[
            pl.BlockSpec((rows, D_MODEL), lambda j: (0, 0)),
            pl.BlockSpec((D_MODEL, D_MODEL), lambda j: (0, j)),
            pl.BlockSpec((1, D_MODEL), lambda j: (0, j)),
        ],
        out_specs=pl.BlockSpec((rows, D_MODEL), lambda j: (0, j)),
        compiler_params=pltpu.CompilerParams(dimension_semantics=("arbitrary",)),
        name="ada",
    )(c_all, w, b.reshape(1, n))


def _mixer_kernel(xp_ref, xsm_ref, mp_ref, ms_ref, g1_ref, win_ref, wt_ref, bsp_ref, gv_ref, wconv_ref,
                  goa_ref, gob_ref, wout_ref, g2_ref, wrt_ref, wshgu_ref, wshd_ref, csp_ref, bsps_ref, st_ref,
                  xs_out, h2_out, lt_out, cztail_out, vs_out, cz23_out, carry_ref, *, n_ptiles, tiles_per_seq):
    i = pl.program_id(0)
    d = D_MODEL

    def proj(x, sh1, sc1):
        h = _rms(x, g1_ref[...]) * (1.0 + sc1) + sh1
        p = jnp.dot(h.astype(BF16), win_ref[...], preferred_element_type=F32)
        u = jax.nn.gelu(p[:, :MIX_A])
        v = jax.nn.gelu(p[:, MIX_A:2 * MIX_A])
        vn = jnp.concatenate(
            [_rms(v[:, h * LANES:(h + 1) * LANES], gv_ref[:, h * LANES:(h + 1) * LANES]) for h in range(A_HEADS)],
            axis=1)
        o = 2 * MIX_A
        return u, vn, p[:, o:o + CONV_DIM], p[:, o + CONV_DIM:o + 2 * CONV_DIM], p[:, o + 2 * CONV_DIM:]

    def tail(x, ya, yb, gate1, sh2, sc2, gate2):
        cat = jnp.concatenate([_rms(ya, goa_ref[...]), _rms(yb, gob_ref[...])], axis=1).astype(BF16)
        x1 = x + gate1 * jnp.dot(cat, wout_ref[...], preferred_element_type=F32)
        h2 = _rms(x1, g2_ref[...]) * (1.0 + sc2) + sh2
        h2b = h2.astype(BF16)
        lt_out[...] = lax.dot_general(wrt_ref[...], h2b, (((1,), (1,)), ((), ())), preferred_element_type=F32)
        gu = jnp.dot(h2b, wshgu_ref[...], preferred_element_type=F32)
        hs = (jax.nn.silu(gu[:, :D_SHARED]) * gu[:, D_SHARED:]).astype(BF16)
        xs_out[...] = x1 + gate2 * jnp.dot(hs, wshd_ref[...], preferred_element_type=F32)
        _store_packed(h2_out, _pack_bf16_pairs(h2), TM)

    @pl.when(i < n_ptiles)
    def _prompt():
        b = i // tiles_per_seq
        m = mp_ref[pl.ds(b, 1), :]
        mod = [m[:, k * d:(k + 1) * d] for k in range(N_MOD)]
        x = xp_ref[...]
        u, vn, bg, cg, z = proj(x, mod[0], mod[1])
        n_chunks = TM // CHUNK
        ya_cols = []
        for h in range(A_HEADS):
            vh = vn[:, h * LANES:(h + 1) * LANES].astype(BF16)
            rhs = jnp.concatenate([vh[c * CHUNK:(c + 1) * CHUNK] for c in range(n_chunks)], axis=1)
            mix = jnp.dot(wt_ref[h], rhs, preferred_element_type=F32)
            bias = bsp_ref[:, h * LANES:(h + 1) * LANES]
            mix = jnp.concatenate([mix[:, c * LANES:(c + 1) * LANES] + bias for c in range(n_chunks)], axis=0)
            ya_cols.append(u[:, h * LANES:(h + 1) * LANES] * mix)
        ya = jnp.concatenate(ya_cols, axis=1)
        cz = cg * z

        @pl.when(i % tiles_per_seq == 0)
        def _():
            carry_ref[...] = jnp.zeros_like(carry_ref)

        c6 = carry_ref[SUBLANES - 2:SUBLANES - 1, :]
        c7 = carry_ref[SUBLANES - 1:SUBLANES, :]
        r = lax.broadcasted_iota(I32, (TM, 1), 0)
        p1 = jnp.where(r == 0, c7, pltpu.roll(cz, 1, 0))
        p2 = jnp.where(r == 0, c6, jnp.where(r == 1, c7, pltpu.roll(cz, 2, 0)))
        wc = wconv_ref[...]
        yb = bg * (p2 * wc[0:1] + p1 * wc[1:2] + cz * wc[2:3])
        carry_ref[...] = cz[TM - SUBLANES:]
        cztail_out[0] = cz[TM - SUBLANES:]
        tail(x, ya, yb, mod[2], mod[3], mod[4], mod[5])

    @pl.when(i == n_ptiles)
    def _sample():
        ms = ms_ref[...]
        nb = TM // DEC_SEQ

        def mod(k):
            return jnp.concatenate([ms[:, k * d:(k + 1) * d]] * DEC_SEQ, axis=0)

        x = xsm_ref[...]
        u, vn, bg, cg, z = proj(x, mod(0), mod(1))
        vt = [vn[t * nb:(t + 1) * nb] for t in range(DEC_SEQ)]
        mixes = []
        for t in range(DEC_SEQ):
            acc = csp_ref[DEC_SEQ * t:DEC_SEQ * t + 1, :] * vt[0]
            for s in range(1, t + 1):
                acc = acc + csp_ref[DEC_SEQ * t + s:DEC_SEQ * t + s + 1, :] * vt[s]
            mixes.append(acc + bsps_ref[t:t + 1, :])
        ya = u * jnp.concatenate(mixes, axis=0)
        cz = cg * z
        czt = [cz[t * nb:(t + 1) * nb] for t in range(DEC_SEQ)]
        full = [st_ref[0], st_ref[1]] + czt
        wc = wconv_ref[...]
        yc = jnp.concatenate(
            [full[t] * wc[0:1] + full[t + 1] * wc[1:2] + full[t + 2] * wc[2:3] for t in range(DEC_SEQ)], axis=0)
        yb = bg * yc
        vs_out[...] = vn
        cz23_out[0] = czt[DEC_SEQ - 2]
        cz23_out[1] = czt[DEC_SEQ - 1]
        cztail_out[0] = cz[TM - SUBLANES:]
        tail(x, ya, yb, mod(2), mod(3), mod(4), mod(5))


def _mixer(xp, xsm, m_all, g1, win, wt, bsp, gv, wconv, goa, gob, wout, g2, wrt, wshgu, wshd, csp, bsps, st,
           *, n_batch, seq):
    t_p = xp.shape[0]
    n_ptiles = t_p // TM
    n_tiles = n_ptiles + 1
    t_all = n_tiles * TM
    dec_batch = xsm.shape[0] // DEC_SEQ
    d = D_MODEL
    kern = functools.partial(_mixer_kernel, n_ptiles=n_ptiles, tiles_per_seq=seq // TM)
    in_specs = [
        pl.BlockSpec((TM, d), lambda i: (jnp.minimum(i, n_ptiles - 1), 0)),
        _const_spec((TM, d)),
        pl.BlockSpec((n_batch, N_MOD * d), lambda i: (dec_batch // n_batch, 0), pipeline_mode=pl.Buffered(1)),
        pl.BlockSpec((dec_batch, N_MOD * d), lambda i: (0, 0), pipeline_mode=pl.Buffered(1)),
        _const_spec(g1.shape), _const_spec(win.shape), _const_spec(wt.shape), _const_spec(bsp.shape),
        _const_spec(gv.shape), _const_spec(wconv.shape), _const_spec(goa.shape), _const_spec(gob.shape),
        _const_spec(wout.shape), _const_spec(g2.shape), _const_spec(wrt.shape), _const_spec(wshgu.shape),
        _const_spec(wshd.shape), _const_spec(csp.shape), _const_spec(bsps.shape), _const_spec(st.shape),
    ]
    out_shape = [
        jax.ShapeDtypeStruct((t_all, d), F32),
        jax.ShapeDtypeStruct((t_all * PACK_ROWS, LANES), U32),
        jax.ShapeDtypeStruct((N_EXPERTS, t_all), F32),
        jax.ShapeDtypeStruct((n_tiles, SUBLANES, CONV_DIM), F32),
        jax.ShapeDtypeStruct((TM, MIX_A), F32),
        jax.ShapeDtypeStruct((2, dec_batch, CONV_DIM), F32),
    ]
    out_specs = [
        pl.BlockSpec((TM, d), lambda i: (i, 0)),
        pl.BlockSpec((TM * PACK_ROWS, LANES), lambda i: (i, 0)),
        pl.BlockSpec((N_EXPERTS, TM), lambda i: (0, i)),
        pl.BlockSpec((1, SUBLANES, CONV_DIM), lambda i: (i, 0, 0)),
        pl.BlockSpec((TM, MIX_A), lambda i: (0, 0)),
        pl.BlockSpec((2, dec_batch, CONV_DIM), lambda i: (0, 0, 0)),
    ]
    return pl.pallas_call(
        kern,
        out_shape=out_shape,
        grid=(n_tiles,),
        in_specs=in_specs,
        out_specs=out_specs,
        scratch_shapes=[pltpu.VMEM((SUBLANES, CONV_DIM), F32)],
        compiler_params=pltpu.CompilerParams(dimension_semantics=("arbitrary",),
                                             vmem_limit_bytes=VMEM_LIMIT_BYTES),
        name="mixer",
    )(xp, xsm, m_all, m_all, g1, win, wt, bsp, gv, wconv, goa, gob, wout, g2, wrt, wshgu, wshd, csp, bsps, st)


def _route_kernel(lt_ref, br_ref, e_out, w_out, r_out, cnt_out, carry_ref, tri_ref):
    i = pl.program_id(0)
    neg = -jnp.inf

    @pl.when(i == 0)
    def _():
        carry_ref[...] = jnp.zeros_like(carry_ref)
        a = lax.broadcasted_iota(I32, (TM, TM), 0)
        b = lax.broadcasted_iota(I32, (TM, TM), 1)
        tri_ref[...] = jnp.where(a < b, 1.0, 0.0).astype(BF16)

    scores = jax.nn.sigmoid(lt_ref[...])
    sel = scores + br_ref[...]
    iog = lax.broadcasted_iota(I32, (GROUP_SIZE, TM), 0)
    gs_rows = []
    for g in range(N_GROUPS):
        sg = sel[g * GROUP_SIZE:(g + 1) * GROUP_SIZE]
        m1 = jnp.max(sg, axis=0, keepdims=True)
        i1 = jnp.min(jnp.where(sg == m1, iog, GROUP_SIZE), axis=0, keepdims=True)
        m2 = jnp.max(jnp.where(iog == i1, neg, sg), axis=0, keepdims=True)
        gs_rows.append(m1 + m2)
    gs = jnp.concatenate(gs_rows, axis=0)
    io8 = lax.broadcasted_iota(I32, (N_GROUPS, TM), 0)
    keep = jnp.zeros((N_GROUPS, TM), F32)
    for _ in range(TOPK_GROUPS):
        m = jnp.max(gs, axis=0, keepdims=True)
        idx = jnp.min(jnp.where(gs == m, io8, N_GROUPS), axis=0, keepdims=True)
        hit = io8 == idx
        keep = jnp.where(hit, 1.0, keep)
        gs = jnp.where(hit, neg, gs)
    sel = jnp.concatenate(
        [jnp.where(keep[g:g + 1] > 0.0, sel[g * GROUP_SIZE:(g + 1) * GROUP_SIZE], neg) for g in range(N_GROUPS)],
        axis=0)
    ioe = lax.broadcasted_iota(I32, (N_EXPERTS, TM), 0)
    picked = jnp.zeros((N_EXPERTS, TM), F32)
    e_rows, w_rows = [], []
    for _ in range(TOP_K):
        m = jnp.max(sel, axis=0, keepdims=True)
        idx = jnp.min(jnp.where(sel == m, ioe, N_EXPERTS), axis=0, keepdims=True)
        hit = ioe == idx
        e_rows.append(idx)
        w_rows.append(jnp.sum(jnp.where(hit, scores, 0.0), axis=0, keepdims=True))
        picked = jnp.where(hit, 1.0, picked)
        sel = jnp.where(hit, neg, sel)
    wk = jnp.concatenate(w_rows, axis=0)
    w_out[...] = wk / jnp.sum(wk, axis=0, keepdims=True) * ROUTED_SCALE
    e_out[...] = jnp.concatenate(e_rows, axis=0)
    before = jnp.dot(picked.astype(BF16), tri_ref[...], preferred_element_type=F32) + carry_ref[...]
    r_rows = [jnp.sum(jnp.where(ioe == e_rows[k], before, 0.0), axis=0, keepdims=True) for k in range(TOP_K)]
    r_out[...] = jnp.concatenate(r_rows, axis=0).astype(I32)
    carry_ref[...] = carry_ref[...] + jnp.sum(picked, axis=1, keepdims=True)
    cnt_out[...] = carry_ref[...]


def _route(lt, br):
    t_all = lt.shape[1]
    n_tiles = t_all // TM
    return pl.pallas_call(
        _route_kernel,
        out_shape=[
            jax.ShapeDtypeStruct((TOP_K, t_all), I32),
            jax.ShapeDtypeStruct((TOP_K, t_all), F32),
            jax.ShapeDtypeStruct((TOP_K, t_all), I32),
            jax.ShapeDtypeStruct((N_EXPERTS, 1), F32),
        ],
        grid=(n_tiles,),
        in_specs=[pl.BlockSpec((N_EXPERTS, TM), lambda i: (0, i)), _const_spec((N_EXPERTS, 1))],
        out_specs=[
            pl.BlockSpec((TOP_K, TM), lambda i: (0, i)),
            pl.BlockSpec((TOP_K, TM), lambda i: (0, i)),
            pl.BlockSpec((TOP_K, TM), lambda i: (0, i)),
            pl.BlockSpec((N_EXPERTS, 1), lambda i: (0, 0)),
        ],
        scratch_shapes=[pltpu.VMEM((N_EXPERTS, 1), F32), pltpu.VMEM((TM, TM), BF16)],
        compiler_params=pltpu.CompilerParams(dimension_semantics=("arbitrary",)),
        name="route",
    )(lt, br)


def _plan_kernel(cnt_ref, e_ref, r_ref, dest_out, blk_e_out, nxt_e_out, nblk_out, cnt_row_out, pstart_row_out,
                 pstart_ref):
    i = pl.program_id(0)

    @pl.when(i == 0)
    def _():
        nb = jnp.floor((cnt_ref[...] + (BM - 1)) * (1.0 / BM))
        a = lax.broadcasted_iota(I32, (N_EXPERTS, N_EXPERTS), 0)
        b = lax.broadcasted_iota(I32, (N_EXPERTS, N_EXPERTS), 1)
        lower = jnp.where(b < a, 1.0, 0.0).astype(BF16)
        nb_l = jnp.broadcast_to(nb, (N_EXPERTS, LANES)).astype(BF16)
        first_blk = jnp.dot(lower, nb_l, preferred_element_type=F32)[:, 0:1]
        end_blk = first_blk + nb
        pstart_ref[...] = first_blk * BM
        cnt_row_out[...] = jnp.sum(jnp.where(a == b, cnt_ref[...], 0.0), axis=0, keepdims=True).astype(I32)
        pstart_row_out[...] = jnp.sum(jnp.where(a == b, first_blk * BM, 0.0), axis=0, keepdims=True).astype(I32)
        n_lanes = blk_e_out.shape[1]
        blk = lax.broadcasted_iota(I32, (N_EXPERTS, n_lanes), 1).astype(F32)
        owner = jnp.sum(jnp.where(end_blk <= blk, 1.0, 0.0), axis=0, keepdims=True)
        blk_e_out[...] = jnp.minimum(owner, N_EXPERTS - 1.0).astype(I32)
        total = jnp.max(end_blk, axis=0, keepdims=True)
        nblk_out[...] = jnp.broadcast_to(total, nblk_out.shape).astype(I32)
        group_end = blk
        for h in range(W_AHEAD):
            group_end = jnp.min(jnp.where(end_blk > group_end, end_blk, 2.0 * n_lanes), axis=0, keepdims=True)
            nxt = jnp.sum(jnp.where(end_blk <= group_end, 1.0, 0.0), axis=0, keepdims=True)
            nxt_e_out[h:h + 1, :] = jnp.where(group_end < total, nxt, -1.0).astype(I32)

    ioe = lax.broadcasted_iota(I32, (N_EXPERTS, TM), 0)
    e = e_ref[...]
    rows = [jnp.sum(jnp.where(ioe == e[k:k + 1], pstart_ref[...], 0.0), axis=0, keepdims=True)
            for k in range(TOP_K)]
    dest_out[...] = jnp.concatenate(rows, axis=0).astype(I32) + r_ref[...]


def _plan(cnt, eidx, rank, n_blocks):
    t_all = eidx.shape[1]
    n_lanes = pl.cdiv(n_blocks, LANES) * LANES
    return pl.pallas_call(
        _plan_kernel,
        out_shape=[
            jax.ShapeDtypeStruct((TOP_K, t_all), I32),
            jax.ShapeDtypeStruct((1, n_lanes), I32),
            jax.ShapeDtypeStruct((W_AHEAD, n_lanes), I32),
            jax.ShapeDtypeStruct((1, LANES), I32),
            jax.ShapeDtypeStruct((1, N_EXPERTS), I32),
            jax.ShapeDtypeStruct((1, N_EXPERTS), I32),
        ],
        grid=(t_all // TM,),
        in_specs=[
            _const_spec((N_EXPERTS, 1)),
            pl.BlockSpec((TOP_K, TM), lambda i: (0, i)),
            pl.BlockSpec((TOP_K, TM), lambda i: (0, i)),
        ],
        out_specs=[
            pl.BlockSpec((TOP_K, TM), lambda i: (0, i)),
            pl.BlockSpec((1, n_lanes), lambda i: (0, 0)),
            pl.BlockSpec((W_AHEAD, n_lanes), lambda i: (0, 0)),
            pl.BlockSpec((1, LANES), lambda i: (0, 0)),
            pl.BlockSpec((1, N_EXPERTS), lambda i: (0, 0)),
            pl.BlockSpec((1, N_EXPERTS), lambda i: (0, 0)),
        ],
        scratch_shapes=[pltpu.VMEM((N_EXPERTS, 1), F32)],
        compiler_params=pltpu.CompilerParams(dimension_semantics=("arbitrary",)),
        name="plan",
    )(cnt, eidx, rank)


def _dispatch_kernel(cnt_ref, pst_ref, nblk_ref, dest_ref, h2_ref, xs_hbm, zeros, sem, fill_sem,
                     *, n_steps, n_blocks):
    i = pl.program_id(0)
    per_step = pl.cdiv(N_EXPERTS, n_steps)

    @pl.when(i == 0)
    def _():
        zeros[...] = jnp.zeros_like(zeros)

    def issue(t, carry):
        for k in range(TOP_K):
            pltpu.make_async_copy(_packed_rows(h2_ref, t, 1), _packed_rows(xs_hbm, dest_ref[t * TOP_K + k], 1),
                                  sem).start(priority=k % DMA_THREADS)
        return carry

    lax.fori_loop(0, TM, issue, 0, unroll=ISSUE_UNROLL)

    def fills(do):
        def per_expert(j, carry):
            e = i * per_step + j

            @pl.when(e < N_EXPERTS)
            def _():
                cnt = cnt_ref[e]
                base = pst_ref[e]
                padded = (cnt + (BM - 1)) // BM * BM
                mid = jnp.minimum((cnt + (SUBLANES - 1)) // SUBLANES * SUBLANES, padded)

                def one(r, c):
                    do(pltpu.make_async_copy(_packed_rows(zeros, 0, 1), _packed_rows(xs_hbm, base + r, 1), fill_sem))
                    return c

                lax.fori_loop(cnt, mid, one, 0)

                def eight(q, c):
                    r = base + mid + q * SUBLANES
                    do(pltpu.make_async_copy(_packed_rows(zeros, 0, SUBLANES), _packed_rows(xs_hbm, r, SUBLANES),
                                             fill_sem))
                    return c

                lax.fori_loop(0, (padded - mid) // SUBLANES, eight, 0)

            return carry

        lax.fori_loop(0, per_step, per_expert, 0)

        def per_block(j, carry):
            b = nblk_ref[0] + i * per_step + j

            @pl.when(b < n_blocks)
            def _():
                do(pltpu.make_async_copy(zeros, _packed_rows(xs_hbm, b * BM, BM), fill_sem))

            return carry

        lax.fori_loop(0, per_step, per_block, 0)

    fills(lambda c: c.start())
    for k in range(TOP_K):
        pltpu.make_async_copy(h2_ref, _packed_rows(xs_hbm, 0, TM), sem).wait()
    fills(lambda c: c.wait())


def _dispatch(cnt_row, pstart_row, nblk, dest, h2, n_blocks):
    t_all = h2.shape[0] // PACK_ROWS
    n_steps = t_all // TM
    kern = functools.partial(_dispatch_kernel, n_steps=n_steps, n_blocks=n_blocks)
    grid_spec = pltpu.PrefetchScalarGridSpec(
        num_scalar_prefetch=3,
        grid=(n_steps,),
        in_specs=[
            pl.BlockSpec((TM * TOP_K,), lambda i, *_: (i,), memory_space=pltpu.SMEM),
            pl.BlockSpec((TM * PACK_ROWS, LANES), lambda i, *_: (i, 0)),
        ],
        out_specs=pl.BlockSpec(memory_space=pl.ANY),
        scratch_shapes=[pltpu.VMEM((BM * PACK_ROWS, LANES), U32), pltpu.SemaphoreType.DMA(()),
                        pltpu.SemaphoreType.DMA(())],
    )
    return pl.pallas_call(
        kern,
        out_shape=jax.ShapeDtypeStruct((n_blocks * BM * PACK_ROWS, LANES), U32),
        grid_spec=grid_spec,
        compiler_params=pltpu.CompilerParams(dimension_semantics=("arbitrary",)),
        name="dispatch",
    )(cnt_row, pstart_row, nblk, dest, h2)


def _moe_kernel(blk_e_ref, nxt_e_ref, nblk_ref, x_hbm, wg_hbm, wu_hbm, wd_hbm, y_ref,
                x_ring, wg_l, wu_l, wd_l, wgu_s, wd_s, xsems, sems, cur_ref):
    nblk = nblk_ref[0]
    first = pl.program_id(0) * MOE_SUB

    def weight_copies(e, slot):
        return (pltpu.make_async_copy(wg_hbm.at[e], wg_l.at[slot], sems.at[slot, 0]),
                pltpu.make_async_copy(wu_hbm.at[e], wu_l.at[slot], sems.at[slot, 1]),
                pltpu.make_async_copy(wd_hbm.at[e], wd_l.at[slot], sems.at[slot, 2]))

    def row_copy(blk):
        slot = blk % X_RING
        return pltpu.make_async_copy(_packed_rows(x_hbm, blk * BM, BM), x_ring.at[slot], xsems.at[slot])

    def prepare(b):
        @pl.when(b < nblk)
        def _():
            e = blk_e_ref[b]

            @pl.when(b == 0)
            def _():
                cur_ref[0] = 0
                for c in weight_copies(e, 0):
                    c.start()
                for h in range(W_AHEAD - 1):
                    ahead = nxt_e_ref[h, 0]

                    @pl.when(ahead >= 0)
                    def _():
                        for c in weight_copies(ahead, h + 1):
                            c.start()

                for j in range(X_AHEAD):
                    @pl.when(j < nblk)
                    def _():
                        row_copy(j).start()

            @pl.when(b + X_AHEAD < nblk)
            def _():
                row_copy(b + X_AHEAD).start()

            @pl.when((b == 0) | (e != blk_e_ref[jnp.maximum(b - 1, 0)]))
            def _():
                @pl.when(b > 0)
                def _():
                    cur_ref[0] = cur_ref[0] + 1

                group = cur_ref[0]
                landed = group % W_RING
                for c in weight_copies(e, landed):
                    c.wait()
                ahead = nxt_e_ref[W_AHEAD - 1, b]

                @pl.when(ahead >= 0)
                def _():
                    for c in weight_copies(ahead, (group + W_AHEAD) % W_RING):
                        c.start()

                slot = group % MOE_SUB
                wgu_s[slot, :, :D_EXPERT] = wg_l[landed].astype(BF16)
                wgu_s[slot, :, D_EXPERT:] = wu_l[landed].astype(BF16)
                wd_s[slot] = wd_l[landed].astype(BF16)

            row_copy(b).wait()

        return cur_ref[0] % MOE_SUB

    slots = [prepare(first + j) for j in range(MOE_SUB)]

    @pl.when(first < nblk)
    def _():
        for j in range(MOE_SUB):
            b = first + j
            x = _unpack_bf16_pairs(x_ring.at[b % X_RING], BM).astype(BF16)
            gu = jnp.dot(x, wgu_s[slots[j]], preferred_element_type=F32)
            h = (jax.nn.silu(gu[:, :D_EXPERT]) * gu[:, D_EXPERT:]).astype(BF16)
            y = jnp.dot(h, wd_s[slots[j]], preferred_element_type=F32)
            y = jnp.where(b < nblk, y, 0.0)
            _store_packed(y_ref.at[pl.ds(j * BM * PACK_ROWS, BM * PACK_ROWS)], _pack_bf16_pairs(y), BM)

    @pl.when(first >= nblk)
    def _():
        y_ref[...] = jnp.zeros_like(y_ref)


def _moe(blk_e, nxt_e, nblk, xs, wg, wu, wd):
    d = D_MODEL
    n_blocks = xs.shape[0] // (BM * PACK_ROWS)
    assert n_blocks % MOE_SUB == 0
    grid_spec = pltpu.PrefetchScalarGridSpec(
        num_scalar_prefetch=3,
        grid=(n_blocks // MOE_SUB,),
        in_specs=[
            pl.BlockSpec(memory_space=pl.ANY),
            pl.BlockSpec(memory_space=pl.ANY),
            pl.BlockSpec(memory_space=pl.ANY),
            pl.BlockSpec(memory_space=pl.ANY),
        ],
        out_specs=pl.BlockSpec((MOE_SUB * BM * PACK_ROWS, LANES), lambda s, *_: (s, 0)),
        scratch_shapes=[
            pltpu.VMEM((X_RING, BM * PACK_ROWS, LANES), U32),
            pltpu.VMEM((W_RING, d, D_EXPERT), F32), pltpu.VMEM((W_RING, d, D_EXPERT), F32),
            pltpu.VMEM((W_RING, D_EXPERT, d), F32),
            pltpu.VMEM((MOE_SUB, d, 2 * D_EXPERT), BF16), pltpu.VMEM((MOE_SUB, D_EXPERT, d), BF16),
            pltpu.SemaphoreType.DMA((X_RING,)), pltpu.SemaphoreType.DMA((W_RING, 3)), pltpu.SMEM((1,), I32),
        ],
    )
    return pl.pallas_call(
        _moe_kernel,
        out_shape=jax.ShapeDtypeStruct((n_blocks * BM * PACK_ROWS, LANES), U32),
        grid_spec=grid_spec,
        compiler_params=pltpu.CompilerParams(dimension_semantics=("arbitrary",),
                                             vmem_limit_bytes=VMEM_LIMIT_BYTES),
        name="moe",
    )(blk_e, nxt_e, nblk, xs, wg, wu, wd)


def _final_kernel(dest_ref, dnext_ref, xs_ref, w_ref, gp_ref, gs_ref, fp_ref, fs_ref, gf_ref, y_hbm,
                  op_ref, os_ref, buf, sems, *, n_tiles, n_ptiles, tiles_per_seq):
    i = pl.program_id(0)
    d = D_MODEL
    slot = i % 2

    def gather(d_ref, s):
        def issue(t, carry):
            for k in range(TOP_K):
                pltpu.make_async_copy(_packed_rows(y_hbm, d_ref[t * TOP_K + k], 1), _packed_rows(buf.at[s, k], t, 1),
                                      sems.at[s]).start(priority=k % DMA_THREADS)
            return carry

        lax.fori_loop(0, TC, issue, 0, unroll=ISSUE_UNROLL)

    @pl.when(i == 0)
    def _():
        gather(dest_ref, 0)

    @pl.when(i + 1 < n_tiles)
    def _():
        gather(dnext_ref, 1 - slot)

    for k in range(TOP_K):
        pltpu.make_async_copy(_packed_rows(y_hbm, 0, TC), buf.at[slot, k], sems.at[slot]).wait()

    w = w_ref[...]
    acc = w[:, 0:1] * _unpack_bf16_pairs(buf.at[slot, 0], TC)
    for k in range(1, TOP_K):
        acc = acc + w[:, k:k + 1] * _unpack_bf16_pairs(buf.at[slot, k], TC)

    def finish(gate2, shift, scale, o_ref):
        x2 = xs_ref[...] + gate2 * acc
        o_ref[...] = _rms(x2, gf_ref[...]) * (1.0 + scale) + shift

    @pl.when(i < n_ptiles)
    def _():
        b = i // tiles_per_seq
        f = fp_ref[pl.ds(b, 1), :]
        finish(gp_ref[pl.ds(b, 1), :], f[:, :d], f[:, d:], op_ref)

    @pl.when(i >= n_ptiles)
    def _():
        f = fs_ref[...]
        finish(gs_ref[...], f[:, :d], f[:, d:], os_ref)


def _final(dest, xs, wts, m_all, mf_all, gf, y_sorted, *, n_batch, seq, t_p):
    t_all, d = xs.shape
    n_tiles = t_all // TC
    n_ptiles = t_p // TC
    dec_batch = m_all.shape[0] - n_batch
    kern = functools.partial(_final_kernel, n_tiles=n_tiles, n_ptiles=n_ptiles, tiles_per_seq=seq // TC)
    pb = dec_batch // n_batch
    return pl.pallas_call(
        kern,
        out_shape=[jax.ShapeDtypeStruct((t_p, d), F32), jax.ShapeDtypeStruct((t_all - t_p, d), F32)],
        grid=(n_tiles,),
        in_specs=[
            pl.BlockSpec((TC * TOP_K,), lambda i: (i,), memory_space=pltpu.SMEM),
            pl.BlockSpec((TC * TOP_K,), lambda i: (jnp.minimum(i + 1, n_tiles - 1),), memory_space=pltpu.SMEM),
            pl.BlockSpec((TC, d), lambda i: (i, 0)),
            pl.BlockSpec((TC, TOP_K), lambda i: (i, 0)),
            pl.BlockSpec((n_batch, d), lambda i: (pb, N_MOD - 1)),
            pl.BlockSpec((dec_batch, d), lambda i: (0, N_MOD - 1)),
            pl.BlockSpec((n_batch, 2 * d), lambda i: (pb, 0)),
            pl.BlockSpec((dec_batch, 2 * d), lambda i: (0, 0)),
            _const_spec(gf.shape),
            pl.BlockSpec(memory_space=pl.ANY),
        ],
        out_specs=[
            pl.BlockSpec((TC, d), lambda i: (jnp.minimum(i, n_ptiles - 1), 0)),
            pl.BlockSpec((TC, d), lambda i: (jnp.maximum(i - n_ptiles, 0), 0)),
        ],
        scratch_shapes=[pltpu.VMEM((2, TOP_K, TC * PACK_ROWS, LANES), U32), pltpu.SemaphoreType.DMA((2,))],
        compiler_params=pltpu.CompilerParams(dimension_semantics=("arbitrary",),
                                             vmem_limit_bytes=VMEM_LIMIT_BYTES),
        name="final",
    )(dest, dest, xs, wts, m_all, m_all, mf_all, mf_all, gf, y_sorted)


def kernel(x_prompt, x_sample, state_conv, c_prompt, c_sample, w_ada, b_ada, g_norm1, w_in, w_spatial, b_spatial, g_v, w_conv, g_out_a, g_out_b, w_out, g_norm2, w_router, b_router, w_exp_gate, w_exp_up, w_exp_down, w_sh_gate, w_sh_up, w_sh_down, w_ada_final, b_ada_final, g_final):
    n_batch, seq, d = x_prompt.shape
    dec_batch, dec_seq, _ = x_sample.shape
    assert w_ada.shape[0] == 1, "one layer only"
    assert d == D_MODEL and dec_seq == DEC_SEQ and dec_batch * dec_seq == TM and seq % TM == 0
    assert n_batch % SUBLANES == 0 and dec_batch % n_batch == 0
    t_p = n_batch * seq
    t_s = dec_batch * dec_seq
    t_all = t_p + t_s

    c_all = jnp.concatenate([c_sample, c_prompt], axis=0)
    m_all = _ada(c_all, w_ada[0], b_ada[0])
    mf_all = _ada(c_all, w_ada_final, b_ada_final)

    rep = functools.partial(jnp.repeat, repeats=A_HEAD_DIM, axis=1)
    tril = jnp.tril(jnp.ones((CHUNK, CHUNK), dtype=bool))
    wt = jnp.where(tril, w_spatial[0], 0.0).astype(BF16)
    bsp = rep(b_spatial[0].T)
    csp = rep(jnp.transpose(w_spatial[0][:, :DEC_SEQ, :DEC_SEQ], (1, 2, 0)).reshape(DEC_SEQ * DEC_SEQ, A_HEADS))
    bsps = rep(b_spatial[0][:, :DEC_SEQ].T)
    st = jnp.transpose(state_conv[0], (1, 0, 2))
    wshgu = jnp.concatenate([w_sh_gate[0], w_sh_up[0]], axis=1).astype(BF16)
    xsm = jnp.transpose(x_sample, (1, 0, 2)).reshape(t_s, d)

    xs, h2, lt, cztail, vs, cz23 = _mixer(
        x_prompt.reshape(t_p, d), xsm, m_all, g_norm1, w_in[0].astype(BF16), wt, bsp,
        g_v.reshape(1, MIX_A), w_conv[0], g_out_a, g_out_b, w_out[0].astype(BF16), g_norm2,
        w_router[0].T.astype(BF16), wshgu, w_sh_down[0].astype(BF16), csp, bsps, st,
        n_batch=n_batch, seq=seq)

    eidx, wts, rank, cnt = _route(lt, b_router[0].reshape(N_EXPERTS, 1))

    n_blocks = (t_all * TOP_K) // BM + N_EXPERTS
    dest, blk_e, nxt_e, nblk, cnt_row, pstart_row = _plan(cnt, eidx, rank, n_blocks)

    dest_tok = dest.T.reshape(t_all * TOP_K)
    x_sorted = _dispatch(cnt_row[0], pstart_row[0], nblk[0, :1], dest_tok, h2, n_blocks)
    y_sorted = _moe(blk_e[0, :n_blocks], nxt_e[:, :n_blocks], nblk[0, :1], x_sorted,
                    w_exp_gate[0], w_exp_up[0], w_exp_down[0])
    y_p, y_s = _final(dest_tok, xs, wts.T, m_all, mf_all, g_final.reshape(1, d), y_sorted,
                      n_batch=n_batch, seq=seq, t_p=t_p)

    tiles_per_seq = seq // TM
    y_prompt = y_p.reshape(n_batch, seq, d)
    y_sample = jnp.transpose(y_s.reshape(dec_seq, dec_batch, d), (1, 0, 2))
    conv_p = cztail[tiles_per_seq - 1:n_batch * tiles_per_seq:tiles_per_seq, SUBLANES - (CONV_W - 1):, :][None]
    conv_s = jnp.transpose(cz23, (1, 0, 2))[None]
    v_s = jnp.transpose(vs.reshape(dec_seq, dec_batch, A_HEADS, A_HEAD_DIM), (1, 0, 2, 3))[None]
    return (y_prompt, y_sample, conv_p, conv_s, v_s)
```

```python
import functools

import jax
import jax.numpy as jnp
from jax import lax
from jax.experimental import pallas as pl
from jax.experimental.pallas import tpu as pltpu

F32 = jnp.float32
BF16 = jnp.bfloat16
I32 = jnp.int32
U32 = jnp.uint32

D_MODEL = 1024
MIX_A = 512
A_HEADS = 4
A_HEAD_DIM = 128
CHUNK = 128
CONV_DIM = 512
CONV_W = 3
PROJ_DIM = 2 * MIX_A + 3 * CONV_DIM
N_EXPERTS = 256
TOP_K = 8
N_GROUPS = 8
TOPK_GROUPS = 4
GROUP_SIZE = N_EXPERTS // N_GROUPS
D_EXPERT = 256
D_SHARED = 256
ROUTED_SCALE = 2.5
N_MOD = 6
RMS_EPS = 1e-6
DEC_SEQ = 4

LANES = 128
SUBLANES = 8
PACK_ROWS = D_MODEL // 2 // LANES
TM = 512
BM = 256
TC = 128
DMA_THREADS = 2
ISSUE_UNROLL = 8
MOE_SUB = 2
W_AHEAD = 2
W_RING = W_AHEAD + 1
VMEM_LIMIT_BYTES = 56 * 1024 * 1024


def _rms(x, g):
    return x * lax.rsqrt(jnp.mean(x * x, axis=-1, keepdims=True) + RMS_EPS) * g


def _packed_rows(ref, row, n):
    return ref.at[pl.ds(pl.multiple_of(row * PACK_ROWS, PACK_ROWS), n * PACK_ROWS)]


def _pack_bf16_pairs(x):
    half = D_MODEL // 2
    xb = x.astype(BF16).astype(F32)
    lo = lax.shift_right_logical(lax.bitcast_convert_type(xb[:, :half], U32), jnp.uint32(16))
    return lo | lax.bitcast_convert_type(xb[:, half:], U32)


def _store_packed(ref, packed, n):
    for j in range(PACK_ROWS):
        ref[pl.ds(j, n, stride=PACK_ROWS), :] = packed[:, j * LANES:(j + 1) * LANES]


def _unpack_bf16_pairs(ref, n):
    words = [ref[pl.ds(j, n, stride=PACK_ROWS), :] for j in range(PACK_ROWS)]
    lo = [lax.bitcast_convert_type(lax.shift_left(w, jnp.uint32(16)), F32) for w in words]
    hi = [lax.bitcast_convert_type(w & jnp.uint32(0xFFFF0000), F32) for w in words]
    return jnp.concatenate(lo + hi, axis=1)


def _const_spec(shape):
    nd = len(shape)
    return pl.BlockSpec(shape, lambda *_: (0,) * nd, pipeline_mode=pl.Buffered(1))


def _ada_kernel(c_ref, w_ref, b_ref, o_ref):
    a = jax.nn.silu(c_ref[...]).astype(BF16)
    o_ref[...] = jnp.dot(a, w_ref[...].astype(BF16), preferred_element_type=F32) + b_ref[...]


def _ada(c_all, w, b):
    rows, n = c_all.shape[0], w.shape[1]
    return pl.pallas_call(
        _ada_kernel,
        out_shape=jax.ShapeDtypeStruct((rows, n), F32),
        grid=(n // D_MODEL,),
        in_specs=[
            pl.BlockSpec((rows, D_MODEL), lambda j: (0, 0)),
            pl.BlockSpec((D_MODEL, D_MODEL), lambda j: (0, j)),
            pl.BlockSpec((1, D_MODEL), lambda j: (0, j)),
        ],
        out_specs=pl.BlockSpec((rows, D_MODEL), lambda j: (0, j)),
        compiler_params=pltpu.CompilerParams(dimension_semantics=("arbitrary",)),
        name="ada",
    )(c_all, w, b.reshape(1, n))


def _mixer_kernel(xp_ref, xsm_ref, mp_ref, ms_ref, g1_ref, win_ref, wt_ref, bsp_ref, gv_ref, wconv_ref,
                  goa_ref, gob_ref, wout_ref, g2_ref, wrt_ref, wshgu_ref, wshd_ref, csp_ref, bsps_ref, st_ref,
                  xs_out, h2_out, lt_out, cztail_out, vs_out, cz23_out, carry_ref, *, n_ptiles, tiles_per_seq):
    i = pl.program_id(0)
    d = D_MODEL

    def proj(x, sh1, sc1):
        h = _rms(x, g1_ref[...]) * (1.0 + sc1) + sh1
        p = jnp.dot(h.astype(BF16), win_ref[...], preferred_element_type=F32)
        u = jax.nn.gelu(p[:, :MIX_A])
        v = jax.nn.gelu(p[:, MIX_A:2 * MIX_A])
        vn = jnp.concatenate(
            [_rms(v[:, h * LANES:(h + 1) * LANES], gv_ref[:, h * LANES:(h + 1) * LANES]) for h in range(A_HEADS)],
            axis=1)
        o = 2 * MIX_A
        return u, vn, p[:, o:o + CONV_DIM], p[:, o + CONV_DIM:o + 2 * CONV_DIM], p[:, o + 2 * CONV_DIM:]

    def tail(x, ya, yb, gate1, sh2, sc2, gate2):
        cat = jnp.concatenate([_rms(ya, goa_ref[...]), _rms(yb, gob_ref[...])], axis=1).astype(BF16)
        x1 = x + gate1 * jnp.dot(cat, wout_ref[...], preferred_element_type=F32)
        h2 = _rms(x1, g2_ref[...]) * (1.0 + sc2) + sh2
        h2b = h2.astype(BF16)
        lt_out[...] = lax.dot_general(wrt_ref[...], h2b, (((1,), (1,)), ((), ())), preferred_element_type=F32)
        gu = jnp.dot(h2b, wshgu_ref[...], preferred_element_type=F32)
        hs = (jax.nn.silu(gu[:, :D_SHARED]) * gu[:, D_SHARED:]).astype(BF16)
        xs_out[...] = x1 + gate2 * jnp.dot(hs, wshd_ref[...], preferred_element_type=F32)
        _store_packed(h2_out, _pack_bf16_pairs(h2), TM)

    @pl.when(i < n_ptiles)
    def _prompt():
        b = i // tiles_per_seq
        m = mp_ref[pl.ds(b, 1), :]
        mod = [m[:, k * d:(k + 1) * d] for k in range(N_MOD)]
        x = xp_ref[...]
        u, vn, bg, cg, z = proj(x, mod[0], mod[1])
        n_chunks = TM // CHUNK
        ya_cols = []
        for h in range(A_HEADS):
            vh = vn[:, h * LANES:(h + 1) * LANES].astype(BF16)
            rhs = jnp.concatenate([vh[c * CHUNK:(c + 1) * CHUNK] for c in range(n_chunks)], axis=1)
            mix = jnp.dot(wt_ref[h], rhs, preferred_element_type=F32)
            bias = bsp_ref[:, h * LANES:(h + 1) * LANES]
            mix = jnp.concatenate([mix[:, c * LANES:(c + 1) * LANES] + bias for c in range(n_chunks)], axis=0)
            ya_cols.append(u[:, h * LANES:(h + 1) * LANES] * mix)
        ya = jnp.concatenate(ya_cols, axis=1)
        cz = cg * z

        @pl.when(i % tiles_per_seq == 0)
        def _():
            carry_ref[...] = jnp.zeros_like(carry_ref)

        c6 = carry_ref[SUBLANES - 2:SUBLANES - 1, :]
        c7 = carry_ref[SUBLANES - 1:SUBLANES, :]
        r = lax.broadcasted_iota(I32, (TM, 1), 0)
        p1 = jnp.where(r == 0, c7, pltpu.roll(cz, 1, 0))
        p2 = jnp.where(r == 0, c6, jnp.where(r == 1, c7, pltpu.roll(cz, 2, 0)))
        wc = wconv_ref[...]
        yb = bg * (p2 * wc[0:1] + p1 * wc[1:2] + cz * wc[2:3])
        carry_ref[...] = cz[TM - SUBLANES:]
        cztail_out[0] = cz[TM - SUBLANES:]
        tail(x, ya, yb, mod[2], mod[3], mod[4], mod[5])

    @pl.when(i == n_ptiles)
    def _sample():
        ms = ms_ref[...]
        nb = TM // DEC_SEQ

        def mod(k):
            return jnp.concatenate([ms[:, k * d:(k + 1) * d]] * DEC_SEQ, axis=0)

        x = xsm_ref[...]
        u, vn, bg, cg, z = proj(x, mod(0), mod(1))
        vt = [vn[t * nb:(t + 1) * nb] for t in range(DEC_SEQ)]
        mixes = []
        for t in range(DEC_SEQ):
            acc = csp_ref[DEC_SEQ * t:DEC_SEQ * t + 1, :] * vt[0]
            for s in range(1, t + 1):
                acc = acc + csp_ref[DEC_SEQ * t + s:DEC_SEQ * t + s + 1, :] * vt[s]
            mixes.append(acc + bsps_ref[t:t + 1, :])
        ya = u * jnp.concatenate(mixes, axis=0)
        cz = cg * z
        czt = [cz[t * nb:(t + 1) * nb] for t in range(DEC_SEQ)]
        full = [st_ref[0], st_ref[1]] + czt
        wc = wconv_ref[...]
        yc = jnp.concatenate(
            [full[t] * wc[0:1] + full[t + 1] * wc[1:2] + full[t + 2] * wc[2:3] for t in range(DEC_SEQ)], axis=0)
        yb = bg * yc
        vs_out[...] = vn
        cz23_out[0] = czt[DEC_SEQ - 2]
        cz23_out[1] = czt[DEC_SEQ - 1]
        cztail_out[0] = cz[TM - SUBLANES:]
        tail(x, ya, yb, mod(2), mod(3), mod(4), mod(5))


def _mixer(xp, xsm, m_all, g1, win, wt, bsp, gv, wconv, goa, gob, wout, g2, wrt, wshgu, wshd, csp, bsps, st,
           *, n_batch, seq):
    t_p = xp.shape[0]
    n_ptiles = t_p // TM
    n_tiles = n_ptiles + 1
    t_all = n_tiles * TM
    dec_batch = xsm.shape[0] // DEC_SEQ
    d = D_MODEL
    kern = functools.partial(_mixer_kernel, n_ptiles=n_ptiles, tiles_per_seq=seq // TM)
    in_specs = [
        pl.BlockSpec((TM, d), lambda i: (jnp.minimum(i, n_ptiles - 1), 0)),
        _const_spec((TM, d)),
        pl.BlockSpec((n_batch, N_MOD * d), lambda i: (dec_batch // n_batch, 0), pipeline_mode=pl.Buffered(1)),
        pl.BlockSpec((dec_batch, N_MOD * d), lambda i: (0, 0), pipeline_mode=pl.Buffered(1)),
        _const_spec(g1.shape), _const_spec(win.shape), _const_spec(wt.shape), _const_spec(bsp.shape),
        _const_spec(gv.shape), _const_spec(wconv.shape), _const_spec(goa.shape), _const_spec(gob.shape),
        _const_spec(wout.shape), _const_spec(g2.shape), _const_spec(wrt.shape), _const_spec(wshgu.shape),
        _const_spec(wshd.shape), _const_spec(csp.shape), _const_spec(bsps.shape), _const_spec(st.shape),
    ]
    out_shape = [
        jax.ShapeDtypeStruct((t_all, d), F32),
        jax.ShapeDtypeStruct((t_all * PACK_ROWS, LANES), U32),
        jax.ShapeDtypeStruct((N_EXPERTS, t_all), F32),
        jax.ShapeDtypeStruct((n_tiles, SUBLANES, CONV_DIM), F32),
        jax.ShapeDtypeStruct((TM, MIX_A), F32),
        jax.ShapeDtypeStruct((2, dec_batch, CONV_DIM), F32),
    ]
    out_specs = [
        pl.BlockSpec((TM, d), lambda i: (i, 0)),
        pl.BlockSpec((TM * PACK_ROWS, LANES), lambda i: (i, 0)),
        pl.BlockSpec((N_EXPERTS, TM), lambda i: (0, i)),
        pl.BlockSpec((1, SUBLANES, CONV_DIM), lambda i: (i, 0, 0)),
        pl.BlockSpec((TM, MIX_A), lambda i: (0, 0)),
        pl.BlockSpec((2, dec_batch, CONV_DIM), lambda i: (0, 0, 0)),
    ]
    return pl.pallas_call(
        kern,
        out_shape=out_shape,
        grid=(n_tiles,),
        in_specs=in_specs,
        out_specs=out_specs,
        scratch_shapes=[pltpu.VMEM((SUBLANES, CONV_DIM), F32)],
        compiler_params=pltpu.CompilerParams(dimension_semantics=("arbitrary",),
                                             vmem_limit_bytes=VMEM_LIMIT_BYTES),
        name="mixer",
    )(xp, xsm, m_all, m_all, g1, win, wt, bsp, gv, wconv, goa, gob, wout, g2, wrt, wshgu, wshd, csp, bsps, st)


def _route_kernel(lt_ref, br_ref, e_out, w_out, r_out, cnt_out, carry_ref, tri_ref):
    i = pl.program_id(0)
    neg = -jnp.inf

    @pl.when(i == 0)
    def _():
        carry_ref[...] = jnp.zeros_like(carry_ref)
        a = lax.broadcasted_iota(I32, (TM, TM), 0)
        b = lax.broadcasted_iota(I32, (TM, TM), 1)
        tri_ref[...] = jnp.where(a < b, 1.0, 0.0).astype(BF16)

    scores = jax.nn.sigmoid(lt_ref[...])
    sel = scores + br_ref[...]
    iog = lax.broadcasted_iota(I32, (GROUP_SIZE, TM), 0)
    gs_rows = []
    for g in range(N_GROUPS):
        sg = sel[g * GROUP_SIZE:(g + 1) * GROUP_SIZE]
        m1 = jnp.max(sg, axis=0, keepdims=True)
        i1 = jnp.min(jnp.where(sg == m1, iog, GROUP_SIZE), axis=0, keepdims=True)
        m2 = jnp.max(jnp.where(iog == i1, neg, sg), axis=0, keepdims=True)
        gs_rows.append(m1 + m2)
    gs = jnp.concatenate(gs_rows, axis=0)
    io8 = lax.broadcasted_iota(I32, (N_GROUPS, TM), 0)
    keep = jnp.zeros((N_GROUPS, TM), F32)
    for _ in range(TOPK_GROUPS):
        m = jnp.max(gs, axis=0, keepdims=True)
        idx = jnp.min(jnp.where(gs == m, io8, N_GROUPS), axis=0, keepdims=True)
        hit = io8 == idx
        keep = jnp.where(hit, 1.0, keep)
        gs = jnp.where(hit, neg, gs)
    sel = jnp.concatenate(
        [jnp.where(keep[g:g + 1] > 0.0, sel[g * GROUP_SIZE:(g + 1) * GROUP_SIZE], neg) for g in range(N_GROUPS)],
        axis=0)
    ioe = lax.broadcasted_iota(I32, (N_EXPERTS, TM), 0)
    picked = jnp.zeros((N_EXPERTS, TM), F32)
    e_rows, w_rows = [], []
    for _ in range(TOP_K):
        m = jnp.max(sel, axis=0, keepdims=True)
        idx = jnp.min(jnp.where(sel == m, ioe, N_EXPERTS), axis=0, keepdims=True)
        hit = ioe == idx
        e_rows.append(idx)
        w_rows.append(jnp.sum(jnp.where(hit, scores, 0.0), axis=0, keepdims=True))
        picked = jnp.where(hit, 1.0, picked)
        sel = jnp.where(hit, neg, sel)
    wk = jnp.concatenate(w_rows, axis=0)
    w_out[...] = wk / jnp.sum(wk, axis=0, keepdims=True) * ROUTED_SCALE
    e_out[...] = jnp.concatenate(e_rows, axis=0)
    before = jnp.dot(picked.astype(BF16), tri_ref[...], preferred_element_type=F32) + carry_ref[...]
    r_rows = [jnp.sum(jnp.where(ioe == e_rows[k], before, 0.0), axis=0, keepdims=True) for k in range(TOP_K)]
    r_out[...] = jnp.concatenate(r_rows, axis=0).astype(I32)
    carry_ref[...] = carry_ref[...] + jnp.sum(picked, axis=1, keepdims=True)
    cnt_out[...] = carry_ref[...]


def _route(lt, br):
    t_all = lt.shape[1]
    n_tiles = t_all // TM
    return pl.pallas_call(
        _route_kernel,
        out_shape=[
            jax.ShapeDtypeStruct((TOP_K, t_all), I32),
            jax.ShapeDtypeStruct((TOP_K, t_all), F32),
            jax.ShapeDtypeStruct((TOP_K, t_all), I32),
            jax.ShapeDtypeStruct((N_EXPERTS, 1), F32),
        ],
        grid=(n_tiles,),
        in_specs=[pl.BlockSpec((N_EXPERTS, TM), lambda i: (0, i)), _const_spec((N_EXPERTS, 1))],
        out_specs=[
            pl.BlockSpec((TOP_K, TM), lambda i: (0, i)),
            pl.BlockSpec((TOP_K, TM), lambda i: (0, i)),
            pl.BlockSpec((TOP_K, TM), lambda i: (0, i)),
            pl.BlockSpec((N_EXPERTS, 1), lambda i: (0, 0)),
        ],
        scratch_shapes=[pltpu.VMEM((N_EXPERTS, 1), F32), pltpu.VMEM((TM, TM), BF16)],
        compiler_params=pltpu.CompilerParams(dimension_semantics=("arbitrary",)),
        name="route",
    )(lt, br)


def _plan_kernel(cnt_ref, e_ref, r_ref, dest_out, blk_e_out, nxt_e_out, nblk_out, cnt_row_out, pstart_row_out,
                 pstart_ref):
    i = pl.program_id(0)

    @pl.when(i == 0)
    def _():
        nb = jnp.floor((cnt_ref[...] + (BM - 1)) * (1.0 / BM))
        a = lax.broadcasted_iota(I32, (N_EXPERTS, N_EXPERTS), 0)
        b = lax.broadcasted_iota(I32, (N_EXPERTS, N_EXPERTS), 1)
        lower = jnp.where(b < a, 1.0, 0.0).astype(BF16)
        nb_l = jnp.broadcast_to(nb, (N_EXPERTS, LANES)).astype(BF16)
        first_blk = jnp.dot(lower, nb_l, preferred_element_type=F32)[:, 0:1]
        end_blk = first_blk + nb
        pstart_ref[...] = first_blk * BM
        cnt_row_out[...] = jnp.sum(jnp.where(a == b, cnt_ref[...], 0.0), axis=0, keepdims=True).astype(I32)
        pstart_row_out[...] = jnp.sum(jnp.where(a == b, first_blk * BM, 0.0), axis=0, keepdims=True).astype(I32)
        n_lanes = blk_e_out.shape[1]
        blk = lax.broadcasted_iota(I32, (N_EXPERTS, n_lanes), 1).astype(F32)
        owner = jnp.sum(jnp.where(end_blk <= blk, 1.0, 0.0), axis=0, keepdims=True)
        blk_e_out[...] = jnp.minimum(owner, N_EXPERTS - 1.0).astype(I32)
        total = jnp.max(end_blk, axis=0, keepdims=True)
        nblk_out[...] = jnp.broadcast_to(total, nblk_out.shape).astype(I32)
        group_end = blk
        for h in range(W_AHEAD):
            group_end = jnp.min(jnp.where(end_blk > group_end, end_blk, 2.0 * n_lanes), axis=0, keepdims=True)
            nxt = jnp.sum(jnp.where(end_blk <= group_end, 1.0, 0.0), axis=0, keepdims=True)
            nxt_e_out[h:h + 1, :] = jnp.where(group_end < total, nxt, -1.0).astype(I32)

    ioe = lax.broadcasted_iota(I32, (N_EXPERTS, TM), 0)
    e = e_ref[...]
    rows = [jnp.sum(jnp.where(ioe == e[k:k + 1], pstart_ref[...], 0.0), axis=0, keepdims=True)
            for k in range(TOP_K)]
    dest_out[...] = jnp.concatenate(rows, axis=0).astype(I32) + r_ref[...]


def _plan(cnt, eidx, rank, n_blocks):
    t_all = eidx.shape[1]
    n_lanes = pl.cdiv(n_blocks, LANES) * LANES
    return pl.pallas_call(
        _plan_kernel,
        out_shape=[
            jax.ShapeDtypeStruct((TOP_K, t_all), I32),
            jax.ShapeDtypeStruct((1, n_lanes), I32),
            jax.ShapeDtypeStruct((W_AHEAD, n_lanes), I32),
            jax.ShapeDtypeStruct((1, LANES), I32),
            jax.ShapeDtypeStruct((1, N_EXPERTS), I32),
            jax.ShapeDtypeStruct((1, N_EXPERTS), I32),
        ],
        grid=(t_all // TM,),
        in_specs=[
            _const_spec((N_EXPERTS, 1)),
            pl.BlockSpec((TOP_K, TM), lambda i: (0, i)),
            pl.BlockSpec((TOP_K, TM), lambda i: (0, i)),
        ],
        out_specs=[
            pl.BlockSpec((TOP_K, TM), lambda i: (0, i)),
            pl.BlockSpec((1, n_lanes), lambda i: (0, 0)),
            pl.BlockSpec((W_AHEAD, n_lanes), lambda i: (0, 0)),
            pl.BlockSpec((1, LANES), lambda i: (0, 0)),
            pl.BlockSpec((1, N_EXPERTS), lambda i: (0, 0)),
            pl.BlockSpec((1, N_EXPERTS), lambda i: (0, 0)),
        ],
        scratch_shapes=[pltpu.VMEM((N_EXPERTS, 1), F32)],
        compiler_params=pltpu.CompilerParams(dimension_semantics=("arbitrary",)),
        name="plan",
    )(cnt, eidx, rank)


def _dispatch_kernel(cnt_ref, pst_ref, nblk_ref, dest_ref, h2_ref, xs_hbm, zeros, sem, fill_sem,
                     *, n_steps, n_blocks):
    i = pl.program_id(0)
    per_step = pl.cdiv(N_EXPERTS, n_steps)

    @pl.when(i == 0)
    def _():
        zeros[...] = jnp.zeros_like(zeros)

    def issue(t, carry):
        for k in range(TOP_K):
            pltpu.make_async_copy(_packed_rows(h2_ref, t, 1), _packed_rows(xs_hbm, dest_ref[t * TOP_K + k], 1),
                                  sem).start(priority=k % DMA_THREADS)
        return carry

    lax.fori_loop(0, TM, issue, 0, unroll=ISSUE_UNROLL)

    def fills(do):
        def per_expert(j, carry):
            e = i * per_step + j

            @pl.when(e < N_EXPERTS)
            def _():
                cnt = cnt_ref[e]
                base = pst_ref[e]
                padded = (cnt + (BM - 1)) // BM * BM
                mid = jnp.minimum((cnt + (SUBLANES - 1)) // SUBLANES * SUBLANES, padded)

                def one(r, c):
                    do(pltpu.make_async_copy(_packed_rows(zeros, 0, 1), _packed_rows(xs_hbm, base + r, 1), fill_sem))
                    return c

                lax.fori_loop(cnt, mid, one, 0)

                def eight(q, c):
                    r = base + mid + q * SUBLANES
                    do(pltpu.make_async_copy(_packed_rows(zeros, 0, SUBLANES), _packed_rows(xs_hbm, r, SUBLANES),
                                             fill_sem))
                    return c

                lax.fori_loop(0, (padded - mid) // SUBLANES, eight, 0)

            return carry

        lax.fori_loop(0, per_step, per_expert, 0)

        def per_block(j, carry):
            b = nblk_ref[0] + i * per_step + j

            @pl.when(b < n_blocks)
            def _():
                do(pltpu.make_async_copy(zeros, _packed_rows(xs_hbm, b * BM, BM), fill_sem))

            return carry

        lax.fori_loop(0, per_step, per_block, 0)

    fills(lambda c: c.start())
    for k in range(TOP_K):
        pltpu.make_async_copy(h2_ref, _packed_rows(xs_hbm, 0, TM), sem).wait()
    fills(lambda c: c.wait())


def _dispatch(cnt_row, pstart_row, nblk, dest, h2, n_blocks):
    t_all = h2.shape[0] // PACK_ROWS
    n_steps = t_all // TM
    kern = functools.partial(_dispatch_kernel, n_steps=n_steps, n_blocks=n_blocks)
    grid_spec = pltpu.PrefetchScalarGridSpec(
        num_scalar_prefetch=3,
        grid=(n_steps,),
        in_specs=[
            pl.BlockSpec((TM * TOP_K,), lambda i, *_: (i,), memory_space=pltpu.SMEM),
            pl.BlockSpec((TM * PACK_ROWS, LANES), lambda i, *_: (i, 0)),
        ],
        out_specs=pl.BlockSpec(memory_space=pl.ANY),
        scratch_shapes=[pltpu.VMEM((BM * PACK_ROWS, LANES), U32), pltpu.SemaphoreType.DMA(()),
                        pltpu.SemaphoreType.DMA(())],
    )
    return pl.pallas_call(
        kern,
        out_shape=jax.ShapeDtypeStruct((n_blocks * BM * PACK_ROWS, LANES), U32),
        grid_spec=grid_spec,
        compiler_params=pltpu.CompilerParams(dimension_semantics=("arbitrary",)),
        name="dispatch",
    )(cnt_row, pstart_row, nblk, dest, h2)


def _moe_kernel(blk_e_ref, nxt_e_ref, nblk_ref, x_ref, wg_hbm, wu_hbm, wd_hbm, y_ref,
                wg_l, wu_l, wd_l, wgu_s, wd_s, sems, cur_ref):
    nblk = nblk_ref[0]
    first = pl.program_id(0) * MOE_SUB

    def weight_copies(e, slot):
        return (pltpu.make_async_copy(wg_hbm.at[e], wg_l.at[slot], sems.at[slot, 0]),
                pltpu.make_async_copy(wu_hbm.at[e], wu_l.at[slot], sems.at[slot, 1]),
                pltpu.make_async_copy(wd_hbm.at[e], wd_l.at[slot], sems.at[slot, 2]))

    def prepare(b):
        @pl.when(b < nblk)
        def _():
            e = blk_e_ref[b]

            @pl.when(b == 0)
            def _():
                cur_ref[0] = 0
                for c in weight_copies(e, 0):
                    c.start()
                for h in range(W_AHEAD - 1):
                    ahead = nxt_e_ref[h, 0]

                    @pl.when(ahead >= 0)
                    def _():
                        for c in weight_copies(ahead, h + 1):
                            c.start()

            @pl.when((b == 0) | (e != blk_e_ref[jnp.maximum(b - 1, 0)]))
            def _():
                @pl.when(b > 0)
                def _():
                    cur_ref[0] = cur_ref[0] + 1

                group = cur_ref[0]
                landed = group % W_RING
                for c in weight_copies(e, landed):
                    c.wait()
                ahead = nxt_e_ref[W_AHEAD - 1, b]

                @pl.when(ahead >= 0)
                def _():
                    for c in weight_copies(ahead, (group + W_AHEAD) % W_RING):
                        c.start()

                slot = group % MOE_SUB
                wgu_s[slot, :, :D_EXPERT] = wg_l[landed].astype(BF16)
                wgu_s[slot, :, D_EXPERT:] = wu_l[landed].astype(BF16)
                wd_s[slot] = wd_l[landed].astype(BF16)

        return cur_ref[0] % MOE_SUB

    slots = [prepare(first + j) for j in range(MOE_SUB)]

    @pl.when(first < nblk)
    def _():
        for j in range(MOE_SUB):
            b = first + j
            rows = pl.ds(j * BM * PACK_ROWS, BM * PACK_ROWS)
            x = _unpack_bf16_pairs(x_ref.at[rows], BM).astype(BF16)
            gu = jnp.dot(x, wgu_s[slots[j]], preferred_element_type=F32)
            h = (jax.nn.silu(gu[:, :D_EXPERT]) * gu[:, D_EXPERT:]).astype(BF16)
            y = jnp.dot(h, wd_s[slots[j]], preferred_element_type=F32)
            y = jnp.where(b < nblk, y, 0.0)
            _store_packed(y_ref.at[rows], _pack_bf16_pairs(y), BM)


def _moe(blk_e, nxt_e, nblk, xs, wg, wu, wd):
    d = D_MODEL
    n_blocks = xs.shape[0] // (BM * PACK_ROWS)
    assert n_blocks % MOE_SUB == 0

    def step_map(s, be, nx, nb):
        return (jnp.minimum(s, (nb[0] - 1) // MOE_SUB), 0)

    grid_spec = pltpu.PrefetchScalarGridSpec(
        num_scalar_prefetch=3,
        grid=(n_blocks // MOE_SUB,),
        in_specs=[
            pl.BlockSpec((MOE_SUB * BM * PACK_ROWS, LANES), step_map),
            pl.BlockSpec(memory_space=pl.ANY),
            pl.BlockSpec(memory_space=pl.ANY),
            pl.BlockSpec(memory_space=pl.ANY),
        ],
        out_specs=pl.BlockSpec((MOE_SUB * BM * PACK_ROWS, LANES), step_map),
        scratch_shapes=[
            pltpu.VMEM((W_RING, d, D_EXPERT), F32), pltpu.VMEM((W_RING, d, D_EXPERT), F32),
            pltpu.VMEM((W_RING, D_EXPERT, d), F32),
            pltpu.VMEM((MOE_SUB, d, 2 * D_EXPERT), BF16), pltpu.VMEM((MOE_SUB, D_EXPERT, d), BF16),
            pltpu.SemaphoreType.DMA((W_RING, 3)), pltpu.SMEM((1,), I32),
        ],
    )
    return pl.pallas_call(
        _moe_kernel,
        out_shape=jax.ShapeDtypeStruct((n_blocks * BM * PACK_ROWS, LANES), U32),
        grid_spec=grid_spec,
        input_output_aliases={3: 0},
        compiler_params=pltpu.CompilerParams(dimension_semantics=("arbitrary",),
                                             vmem_limit_bytes=VMEM_LIMIT_BYTES),
        name="moe",
    )(blk_e, nxt_e, nblk, xs, wg, wu, wd)


def _final_kernel(dest_ref, dnext_ref, xs_ref, w_ref, gp_ref, gs_ref, fp_ref, fs_ref, gf_ref, y_hbm,
                  op_ref, os_ref, buf, sems, *, n_tiles, n_ptiles, tiles_per_seq):
    i = pl.program_id(0)
    d = D_MODEL
    slot = i % 2

    def gather(d_ref, s):
        def issue(t, carry):
            for k in range(TOP_K):
                pltpu.make_async_copy(_packed_rows(y_hbm, d_ref[t * TOP_K + k], 1), _packed_rows(buf.at[s, k], t, 1),
                                      sems.at[s]).start(priority=k % DMA_THREADS)
            return carry

        lax.fori_loop(0, TC, issue, 0, unroll=ISSUE_UNROLL)

    @pl.when(i == 0)
    def _():
        gather(dest_ref, 0)

    @pl.when(i + 1 < n_tiles)
    def _():
        gather(dnext_ref, 1 - slot)

    for k in range(TOP_K):
        pltpu.make_async_copy(_packed_rows(y_hbm, 0, TC), buf.at[slot, k], sems.at[slot]).wait()

    w = w_ref[...]
    acc = w[:, 0:1] * _unpack_bf16_pairs(buf.at[slot, 0], TC)
    for k in range(1, TOP_K):
        acc = acc + w[:, k:k + 1] * _unpack_bf16_pairs(buf.at[slot, k], TC)

    def finish(gate2, shift, scale, o_ref):
        x2 = xs_ref[...] + gate2 * acc
        o_ref[...] = _rms(x2, gf_ref[...]) * (1.0 + scale) + shift

    @pl.when(i < n_ptiles)
    def _():
        b = i // tiles_per_seq
        f = fp_ref[pl.ds(b, 1), :]
        finish(gp_ref[pl.ds(b, 1), :], f[:, :d], f[:, d:], op_ref)

    @pl.when(i >= n_ptiles)
    def _():
        f = fs_ref[...]
        finish(gs_ref[...], f[:, :d], f[:, d:], os_ref)


def _final(dest, xs, wts, m_all, mf_all, gf, y_sorted, *, n_batch, seq, t_p):
    t_all, d = xs.shape
    n_tiles = t_all // TC
    n_ptiles = t_p // TC
    dec_batch = m_all.shape[0] - n_batch
    kern = functools.partial(_final_kernel, n_tiles=n_tiles, n_ptiles=n_ptiles, tiles_per_seq=seq // TC)
    pb = dec_batch // n_batch
    return pl.pallas_call(
        kern,
        out_shape=[jax.ShapeDtypeStruct((t_p, d), F32), jax.ShapeDtypeStruct((t_all - t_p, d), F32)],
        grid=(n_tiles,),
        in_specs=[
            pl.BlockSpec((TC * TOP_K,), lambda i: (i,), memory_space=pltpu.SMEM),
            pl.BlockSpec((TC * TOP_K,), lambda i: (jnp.minimum(i + 1, n_tiles - 1),), memory_space=pltpu.SMEM),
            pl.BlockSpec((TC, d), lambda i: (i, 0)),
            pl.BlockSpec((TC, TOP_K), lambda i: (i, 0)),
            pl.BlockSpec((n_batch, d), lambda i: (pb, N_MOD - 1)),
            pl.BlockSpec((dec_batch, d), lambda i: (0, N_MOD - 1)),
            pl.BlockSpec((n_batch, 2 * d), lambda i: (pb, 0)),
            pl.BlockSpec((dec_batch, 2 * d), lambda i: (0, 0)),
            _const_spec(gf.shape),
            pl.BlockSpec(memory_space=pl.ANY),
        ],
        out_specs=[
            pl.BlockSpec((TC, d), lambda i: (jnp.minimum(i, n_ptiles - 1), 0)),
            pl.BlockSpec((TC, d), lambda i: (jnp.maximum(i - n_ptiles, 0), 0)),
        ],
        scratch_shapes=[pltpu.VMEM((2, TOP_K, TC * PACK_ROWS, LANES), U32), pltpu.SemaphoreType.DMA((2,))],
        compiler_params=pltpu.CompilerParams(dimension_semantics=("arbitrary",),
                                             vmem_limit_bytes=VMEM_LIMIT_BYTES),
        name="final",
    )(dest, dest, xs, wts, m_all, m_all, mf_all, mf_all, gf, y_sorted)


def kernel(x_prompt, x_sample, state_conv, c_prompt, c_sample, w_ada, b_ada, g_norm1, w_in, w_spatial, b_spatial, g_v, w_conv, g_out_a, g_out_b, w_out, g_norm2, w_router, b_router, w_exp_gate, w_exp_up, w_exp_down, w_sh_gate, w_sh_up, w_sh_down, w_ada_final, b_ada_final, g_final):
    n_batch, seq, d = x_prompt.shape
    dec_batch, dec_seq, _ = x_sample.shape
    assert w_ada.shape[0] == 1, "one layer only"
    assert d == D_MODEL and dec_seq == DEC_SEQ and dec_batch * dec_seq == TM and seq % TM == 0
    assert n_batch % SUBLANES == 0 and dec_batch % n_batch == 0
    t_p = n_batch * seq
    t_s = dec_batch * dec_seq
    t_all = t_p + t_s

    c_all = jnp.concatenate([c_sample, c_prompt], axis=0)
    m_all = _ada(c_all, w_ada[0], b_ada[0])
    mf_all = _ada(c_all, w_ada_final, b_ada_final)

    rep = functools.partial(jnp.repeat, repeats=A_HEAD_DIM, axis=1)
    tril = jnp.tril(jnp.ones((CHUNK, CHUNK), dtype=bool))
    wt = jnp.where(tril, w_spatial[0], 0.0).astype(BF16)
    bsp = rep(b_spatial[0].T)
    csp = rep(jnp.transpose(w_spatial[0][:, :DEC_SEQ, :DEC_SEQ], (1, 2, 0)).reshape(DEC_SEQ * DEC_SEQ, A_HEADS))
    bsps = rep(b_spatial[0][:, :DEC_SEQ].T)
    st = jnp.transpose(state_conv[0], (1, 0, 2))
    wshgu = jnp.concatenate([w_sh_gate[0], w_sh_up[0]], axis=1).astype(BF16)
    xsm = jnp.transpose(x_sample, (1, 0, 2)).reshape(t_s, d)

    xs, h2, lt, cztail, vs, cz23 = _mixer(
        x_prompt.reshape(t_p, d), xsm, m_all, g_norm1, w_in[0].astype(BF16), wt, bsp,
        g_v.reshape(1, MIX_A), w_conv[0], g_out_a, g_out_b, w_out[0].astype(BF16), g_norm2,
        w_router[0].T.astype(BF16), wshgu, w_sh_down[0].astype(BF16), csp, bsps, st,
        n_batch=n_batch, seq=seq)

    eidx, wts, rank, cnt = _route(lt, b_router[0].reshape(N_EXPERTS, 1))

    n_blocks = (t_all * TOP_K) // BM + N_EXPERTS
    dest, blk_e, nxt_e, nblk, cnt_row, pstart_row = _plan(cnt, eidx, rank, n_blocks)

    dest_tok = dest.T.reshape(t_all * TOP_K)
    x_sorted = _dispatch(cnt_row[0], pstart_row[0], nblk[0, :1], dest_tok, h2, n_blocks)
    y_sorted = _moe(blk_e[0, :n_blocks], nxt_e[:, :n_blocks], nblk[0, :1], x_sorted,
                    w_exp_gate[0], w_exp_up[0], w_exp_down[0])
    y_p, y_s = _final(dest_tok, xs, wts.T, m_all, mf_all, g_final.reshape(1, d), y_sorted,
                      n_batch=n_batch, seq=seq, t_p=t_p)

    tiles_per_seq = seq // TM
    y_prompt = y_p.reshape(n_batch, seq, d)
    y_sample = jnp.transpose(y_s.reshape(dec_seq, dec_batch, d), (1, 0, 2))
    conv_p = cztail[tiles_per_seq - 1:n_batch * tiles_per_seq:tiles_per_seq, SUBLANES - (CONV_W - 1):, :][None]
    conv_s = jnp.transpose(cz23, (1, 0, 2))[None]
    v_s = jnp.transpose(vs.reshape(dec_seq, dec_batch, A_HEADS, A_HEAD_DIM), (1, 0, 2, 3))[None]
    return (y_prompt, y_sample, conv_p, conv_s, v_s)
```

```python
import functools

import jax
import jax.numpy as jnp
from jax import lax
from jax.experimental import pallas as pl
from jax.experimental.pallas import tpu as pltpu

F32 = jnp.float32
BF16 = jnp.bfloat16
I32 = jnp.int32
U32 = jnp.uint32

D_MODEL = 1024
MIX_A = 512
A_HEADS = 4
A_HEAD_DIM = 128
CHUNK = 128
CONV_DIM = 512
CONV_W = 3
PROJ_DIM = 2 * MIX_A + 3 * CONV_DIM
N_EXPERTS = 256
TOP_K = 8
N_GROUPS = 8
TOPK_GROUPS = 4
GROUP_SIZE = N_EXPERTS // N_GROUPS
D_EXPERT = 256
D_SHARED = 256
ROUTED_SCALE = 2.5
N_MOD = 6
RMS_EPS = 1e-6
DEC_SEQ = 4

LANES = 128
SUBLANES = 8
PACK_ROWS = D_MODEL // 2 // LANES
TM = 512
BM = 256
TC = 128
DMA_THREADS = 2
ISSUE_UNROLL = 8
MOE_SUB = 4
W_AHEAD = 3
W_RING = W_AHEAD + 1
VMEM_LIMIT_BYTES = 56 * 1024 * 1024


def _rms(x, g):
    return x * lax.rsqrt(jnp.mean(x * x, axis=-1, keepdims=True) + RMS_EPS) * g


def _packed_rows(ref, row, n):
    return ref.at[pl.ds(pl.multiple_of(row * PACK_ROWS, PACK_ROWS), n * PACK_ROWS)]


def _pack_bf16_pairs(x):
    half = D_MODEL // 2
    xb = x.astype(BF16).astype(F32)
    lo = lax.shift_right_logical(lax.bitcast_convert_type(xb[:, :half], U32), jnp.uint32(16))
    return lo | lax.bitcast_convert_type(xb[:, half:], U32)


def _store_packed(ref, packed, n):
    for j in range(PACK_ROWS):
        ref[pl.ds(j, n, stride=PACK_ROWS), :] = packed[:, j * LANES:(j + 1) * LANES]


def _unpack_bf16_pairs(ref, n):
    words = [ref[pl.ds(j, n, stride=PACK_ROWS), :] for j in range(PACK_ROWS)]
    lo = [lax.bitcast_convert_type(lax.shift_left(w, jnp.uint32(16)), F32) for w in words]
    hi = [lax.bitcast_convert_type(w & jnp.uint32(0xFFFF0000), F32) for w in words]
    return jnp.concatenate(lo + hi, axis=1)


def _const_spec(shape):
    nd = len(shape)
    return pl.BlockSpec(shape, lambda *_: (0,) * nd, pipeline_mode=pl.Buffered(1))


def _ada_kernel(c_ref, w_ref, b_ref, o_ref):
    a = jax.nn.silu(c_ref[...]).astype(BF16)
    o_ref[...] = jnp.dot(a, w_ref[...].astype(BF16), preferred_element_type=F32) + b_ref[...]


def _ada(c_all, w, b):
    rows, n = c_all.shape[0], w.shape[1]
    return pl.pallas_call(
        _ada_kernel,
        out_shape=jax.ShapeDtypeStruct((rows, n), F32),
        grid=(n // D_MODEL,),
        in_specs=[
            pl.BlockSpec((rows, D_MODEL), lambda j: (0, 0)),
            pl.BlockSpec((D_MODEL, D_MODEL), lambda j: (0, j)),
            pl.BlockSpec((1, D_MODEL), lambda j: (0, j)),
        ],
        out_specs=pl.BlockSpec((rows, D_MODEL), lambda j: (0, j)),
        compiler_params=pltpu.CompilerParams(dimension_semantics=("arbitrary",)),
        name="ada",
    )(c_all, w, b.reshape(1, n))


def _mixer_kernel(xp_ref, xsm_ref, mp_ref, ms_ref, g1_ref, win_ref, wt_ref, bsp_ref, gv_ref, wconv_ref,
                  goa_ref, gob_ref, wout_ref, g2_ref, wrt_ref, wshgu_ref, wshd_ref, csp_ref, bsps_ref, st_ref,
                  xs_out, h2_out, lt_out, cztail_out, vs_out, cz23_out, carry_ref, *, n_ptiles, tiles_per_seq):
    i = pl.program_id(0)
    d = D_MODEL

    def proj(x, sh1, sc1):
        h = _rms(x, g1_ref[...]) * (1.0 + sc1) + sh1
        p = jnp.dot(h.astype(BF16), win_ref[...], preferred_element_type=F32)
        u = jax.nn.gelu(p[:, :MIX_A])
        v = jax.nn.gelu(p[:, MIX_A:2 * MIX_A])
        vn = jnp.concatenate(
            [_rms(v[:, h * LANES:(h + 1) * LANES], gv_ref[:, h * LANES:(h + 1) * LANES]) for h in range(A_HEADS)],
            axis=1)
        o = 2 * MIX_A
        return u, vn, p[:, o:o + CONV_DIM], p[:, o + CONV_DIM:o + 2 * CONV_DIM], p[:, o + 2 * CONV_DIM:]

    def tail(x, ya, yb, gate1, sh2, sc2, gate2):
        cat = jnp.concatenate([_rms(ya, goa_ref[...]), _rms(yb, gob_ref[...])], axis=1).astype(BF16)
        x1 = x + gate1 * jnp.dot(cat, wout_ref[...], preferred_element_type=F32)
        h2 = _rms(x1, g2_ref[...]) * (1.0 + sc2) + sh2
        h2b = h2.astype(BF16)
        lt_out[...] = lax.dot_general(wrt_ref[...], h2b, (((1,), (1,)), ((), ())), preferred_element_type=F32)
        gu = jnp.dot(h2b, wshgu_ref[...], preferred_element_type=F32)
        hs = (jax.nn.silu(gu[:, :D_SHARED]) * gu[:, D_SHARED:]).astype(BF16)
        xs_out[...] = x1 + gate2 * jnp.dot(hs, wshd_ref[...], preferred_element_type=F32)
        _store_packed(h2_out, _pack_bf16_pairs(h2), TM)

    @pl.when(i < n_ptiles)
    def _prompt():
        b = i // tiles_per_seq
        m = mp_ref[pl.ds(b, 1), :]
        mod = [m[:, k * d:(k + 1) * d] for k in range(N_MOD)]
        x = xp_ref[...]
        u, vn, bg, cg, z = proj(x, mod[0], mod[1])
        n_chunks = TM // CHUNK
        ya_cols = []
        for h in range(A_HEADS):
            vh = vn[:, h * LANES:(h + 1) * LANES].astype(BF16)
            rhs = jnp.concatenate([vh[c * CHUNK:(c + 1) * CHUNK] for c in range(n_chunks)], axis=1)
            mix = jnp.dot(wt_ref[h], rhs, preferred_element_type=F32)
            bias = bsp_ref[:, h * LANES:(h + 1) * LANES]
            mix = jnp.concatenate([mix[:, c * LANES:(c + 1) * LANES] + bias for c in range(n_chunks)], axis=0)
            ya_cols.append(u[:, h * LANES:(h + 1) * LANES] * mix)
        ya = jnp.concatenate(ya_cols, axis=1)
        cz = cg * z

        @pl.when(i % tiles_per_seq == 0)
        def _():
            carry_ref[...] = jnp.zeros_like(carry_ref)

        c6 = carry_ref[SUBLANES - 2:SUBLANES - 1, :]
        c7 = carry_ref[SUBLANES - 1:SUBLANES, :]
        r = lax.broadcasted_iota(I32, (TM, 1), 0)
        p1 = jnp.where(r == 0, c7, pltpu.roll(cz, 1, 0))
        p2 = jnp.where(r == 0, c6, jnp.where(r == 1, c7, pltpu.roll(cz, 2, 0)))
        wc = wconv_ref[...]
        yb = bg * (p2 * wc[0:1] + p1 * wc[1:2] + cz * wc[2:3])
        carry_ref[...] = cz[TM - SUBLANES:]
        cztail_out[0] = cz[TM - SUBLANES:]
        tail(x, ya, yb, mod[2], mod[3], mod[4], mod[5])

    @pl.when(i == n_ptiles)
    def _sample():
        ms = ms_ref[...]
        nb = TM // DEC_SEQ

        def mod(k):
            return jnp.concatenate([ms[:, k * d:(k + 1) * d]] * DEC_SEQ, axis=0)

        x = xsm_ref[...]
        u, vn, bg, cg, z = proj(x, mod(0), mod(1))
        vt = [vn[t * nb:(t + 1) * nb] for t in range(DEC_SEQ)]
        mixes = []
        for t in range(DEC_SEQ):
            acc = csp_ref[DEC_SEQ * t:DEC_SEQ * t + 1, :] * vt[0]
            for s in range(1, t + 1):
                acc = acc + csp_ref[DEC_SEQ * t + s:DEC_SEQ * t + s + 1, :] * vt[s]
            mixes.append(acc + bsps_ref[t:t + 1, :])
        ya = u * jnp.concatenate(mixes, axis=0)
        cz = cg * z
        czt = [cz[t * nb:(t + 1) * nb] for t in range(DEC_SEQ)]
        full = [st_ref[0], st_ref[1]] + czt
        wc = wconv_ref[...]
        yc = jnp.concatenate(
            [full[t] * wc[0:1] + full[t + 1] * wc[1:2] + full[t + 2] * wc[2:3] for t in range(DEC_SEQ)], axis=0)
        yb = bg * yc
        vs_out[...] = vn
        cz23_out[0] = czt[DEC_SEQ - 2]
        cz23_out[1] = czt[DEC_SEQ - 1]
        cztail_out[0] = cz[TM - SUBLANES:]
        tail(x, ya, yb, mod(2), mod(3), mod(4), mod(5))


def _mixer(xp, xsm, m_all, g1, win, wt, bsp, gv, wconv, goa, gob, wout, g2, wrt, wshgu, wshd, csp, bsps, st,
           *, n_batch, seq):
    t_p = xp.shape[0]
    n_ptiles = t_p // TM
    n_tiles = n_ptiles + 1
    t_all = n_tiles * TM
    dec_batch = xsm.shape[0] // DEC_SEQ
    d = D_MODEL
    kern = functools.partial(_mixer_kernel, n_ptiles=n_ptiles, tiles_per_seq=seq // TM)
    in_specs = [
        pl.BlockSpec((TM, d), lambda i: (jnp.minimum(i, n_ptiles - 1), 0)),
        _const_spec((TM, d)),
        pl.BlockSpec((n_batch, N_MOD * d), lambda i: (dec_batch // n_batch, 0), pipeline_mode=pl.Buffered(1)),
        pl.BlockSpec((dec_batch, N_MOD * d), lambda i: (0, 0), pipeline_mode=pl.Buffered(1)),
        _const_spec(g1.shape), _const_spec(win.shape), _const_spec(wt.shape), _const_spec(bsp.shape),
        _const_spec(gv.shape), _const_spec(wconv.shape), _const_spec(goa.shape), _const_spec(gob.shape),
        _const_spec(wout.shape), _const_spec(g2.shape), _const_spec(wrt.shape), _const_spec(wshgu.shape),
        _const_spec(wshd.shape), _const_spec(csp.shape), _const_spec(bsps.shape), _const_spec(st.shape),
    ]
    out_shape = [
        jax.ShapeDtypeStruct((t_all, d), F32),
        jax.ShapeDtypeStruct((t_all * PACK_ROWS, LANES), U32),
        jax.ShapeDtypeStruct((N_EXPERTS, t_all), F32),
        jax.ShapeDtypeStruct((n_tiles, SUBLANES, CONV_DIM), F32),
        jax.ShapeDtypeStruct((TM, MIX_A), F32),
        jax.ShapeDtypeStruct((2, dec_batch, CONV_DIM), F32),
    ]
    out_specs = [
        pl.BlockSpec((TM, d), lambda i: (i, 0)),
        pl.BlockSpec((TM * PACK_ROWS, LANES), lambda i: (i, 0)),
        pl.BlockSpec((N_EXPERTS, TM), lambda i: (0, i)),
        pl.BlockSpec((1, SUBLANES, CONV_DIM), lambda i: (i, 0, 0)),
        pl.BlockSpec((TM, MIX_A), lambda i: (0, 0)),
        pl.BlockSpec((2, dec_batch, CONV_DIM), lambda i: (0, 0, 0)),
    ]
    return pl.pallas_call(
        kern,
        out_shape=out_shape,
        grid=(n_tiles,),
        in_specs=in_specs,
        out_specs=out_specs,
        scratch_shapes=[pltpu.VMEM((SUBLANES, CONV_DIM), F32)],
        compiler_params=pltpu.CompilerParams(dimension_semantics=("arbitrary",),
                                             vmem_limit_bytes=VMEM_LIMIT_BYTES),
        name="mixer",
    )(xp, xsm, m_all, m_all, g1, win, wt, bsp, gv, wconv, goa, gob, wout, g2, wrt, wshgu, wshd, csp, bsps, st)


def _route_kernel(lt_ref, br_ref, e_out, w_out, r_out, cnt_out, carry_ref, tri_ref):
    i = pl.program_id(0)
    neg = -jnp.inf

    @pl.when(i == 0)
    def _():
        carry_ref[...] = jnp.zeros_like(carry_ref)
        a = lax.broadcasted_iota(I32, (TM, TM), 0)
        b = lax.broadcasted_iota(I32, (TM, TM), 1)
        tri_ref[...] = jnp.where(a < b, 1.0, 0.0).astype(BF16)

    scores = jax.nn.sigmoid(lt_ref[...])
    sel = scores + br_ref[...]
    iog = lax.broadcasted_iota(I32, (GROUP_SIZE, TM), 0)
    gs_rows = []
    for g in range(N_GROUPS):
        sg = sel[g * GROUP_SIZE:(g + 1) * GROUP_SIZE]
        m1 = jnp.max(sg, axis=0, keepdims=True)
        i1 = jnp.min(jnp.where(sg == m1, iog, GROUP_SIZE), axis=0, keepdims=True)
        m2 = jnp.max(jnp.where(iog == i1, neg, sg), axis=0, keepdims=True)
        gs_rows.append(m1 + m2)
    gs = jnp.concatenate(gs_rows, axis=0)
    io8 = lax.broadcasted_iota(I32, (N_GROUPS, TM), 0)
    keep = jnp.zeros((N_GROUPS, TM), F32)
    for _ in range(TOPK_GROUPS):
        m = jnp.max(gs, axis=0, keepdims=True)
        idx = jnp.min(jnp.where(gs == m, io8, N_GROUPS), axis=0, keepdims=True)
        hit = io8 == idx
        keep = jnp.where(hit, 1.0, keep)
        gs = jnp.where(hit, neg, gs)
    sel = jnp.concatenate(
        [jnp.where(keep[g:g + 1] > 0.0, sel[g * GROUP_SIZE:(g + 1) * GROUP_SIZE], neg) for g in range(N_GROUPS)],
        axis=0)
    ioe = lax.broadcasted_iota(I32, (N_EXPERTS, TM), 0)
    picked = jnp.zeros((N_EXPERTS, TM), F32)
    e_rows, w_rows = [], []
    for _ in range(TOP_K):
        m = jnp.max(sel, axis=0, keepdims=True)
        idx = jnp.min(jnp.where(sel == m, ioe, N_EXPERTS), axis=0, keepdims=True)
        hit = ioe == idx
        e_rows.append(idx)
        w_rows.append(jnp.sum(jnp.where(hit, scores, 0.0), axis=0, keepdims=True))
        picked = jnp.where(hit, 1.0, picked)
        sel = jnp.where(hit, neg, sel)
    wk = jnp.concatenate(w_rows, axis=0)
    w_out[...] = wk / jnp.sum(wk, axis=0, keepdims=True) * ROUTED_SCALE
    e_out[...] = jnp.concatenate(e_rows, axis=0)
    before = jnp.dot(picked.astype(BF16), tri_ref[...], preferred_element_type=F32) + carry_ref[...]
    r_rows = [jnp.sum(jnp.where(ioe == e_rows[k], before, 0.0), axis=0, keepdims=True) for k in range(TOP_K)]
    r_out[...] = jnp.concatenate(r_rows, axis=0).astype(I32)
    carry_ref[...] = carry_ref[...] + jnp.sum(picked, axis=1, keepdims=True)
    cnt_out[...] = carry_ref[...]


def _route(lt, br):
    t_all = lt.shape[1]
    n_tiles = t_all // TM
    return pl.pallas_call(
        _route_kernel,
        out_shape=[
            jax.ShapeDtypeStruct((TOP_K, t_all), I32),
            jax.ShapeDtypeStruct((TOP_K, t_all), F32),
            jax.ShapeDtypeStruct((TOP_K, t_all), I32),
            jax.ShapeDtypeStruct((N_EXPERTS, 1), F32),
        ],
        grid=(n_tiles,),
        in_specs=[pl.BlockSpec((N_EXPERTS, TM), lambda i: (0, i)), _const_spec((N_EXPERTS, 1))],
        out_specs=[
            pl.BlockSpec((TOP_K, TM), lambda i: (0, i)),
            pl.BlockSpec((TOP_K, TM), lambda i: (0, i)),
            pl.BlockSpec((TOP_K, TM), lambda i: (0, i)),
            pl.BlockSpec((N_EXPERTS, 1), lambda i: (0, 0)),
        ],
        scratch_shapes=[pltpu.VMEM((N_EXPERTS, 1), F32), pltpu.VMEM((TM, TM), BF16)],
        compiler_params=pltpu.CompilerParams(dimension_semantics=("arbitrary",)),
        name="route",
    )(lt, br)


def _plan_kernel(cnt_ref, e_ref, r_ref, dest_out, blk_e_out, nxt_e_out, nblk_out, cnt_row_out, pstart_row_out,
                 pstart_ref):
    i = pl.program_id(0)

    @pl.when(i == 0)
    def _():
        nb = jnp.floor((cnt_ref[...] + (BM - 1)) * (1.0 / BM))
        a = lax.broadcasted_iota(I32, (N_EXPERTS, N_EXPERTS), 0)
        b = lax.broadcasted_iota(I32, (N_EXPERTS, N_EXPERTS), 1)
        lower = jnp.where(b < a, 1.0, 0.0).astype(BF16)
        nb_l = jnp.broadcast_to(nb, (N_EXPERTS, LANES)).astype(BF16)
        first_blk = jnp.dot(lower, nb_l, preferred_element_type=F32)[:, 0:1]
        end_blk = first_blk + nb
        pstart_ref[...] = first_blk * BM
        cnt_row_out[...] = jnp.sum(jnp.where(a == b, cnt_ref[...], 0.0), axis=0, keepdims=True).astype(I32)
        pstart_row_out[...] = jnp.sum(jnp.where(a == b, first_blk * BM, 0.0), axis=0, keepdims=True).astype(I32)
        n_lanes = blk_e_out.shape[1]
        blk = lax.broadcasted_iota(I32, (N_EXPERTS, n_lanes), 1).astype(F32)
        owner = jnp.sum(jnp.where(end_blk <= blk, 1.0, 0.0), axis=0, keepdims=True)
        blk_e_out[...] = jnp.minimum(owner, N_EXPERTS - 1.0).astype(I32)
        total = jnp.max(end_blk, axis=0, keepdims=True)
        nblk_out[...] = jnp.broadcast_to(total, nblk_out.shape).astype(I32)
        group_end = blk
        for h in range(W_AHEAD):
            group_end = jnp.min(jnp.where(end_blk > group_end, end_blk, 2.0 * n_lanes), axis=0, keepdims=True)
            nxt = jnp.sum(jnp.where(end_blk <= group_end, 1.0, 0.0), axis=0, keepdims=True)
            nxt_e_out[h:h + 1, :] = jnp.where(group_end < total, nxt, -1.0).astype(I32)

    ioe = lax.broadcasted_iota(I32, (N_EXPERTS, TM), 0)
    e = e_ref[...]
    rows = [jnp.sum(jnp.where(ioe == e[k:k + 1], pstart_ref[...], 0.0), axis=0, keepdims=True)
            for k in range(TOP_K)]
    dest_out[...] = jnp.concatenate(rows, axis=0).astype(I32) + r_ref[...]


def _plan(cnt, eidx, rank, n_blocks):
    t_all = eidx.shape[1]
    n_lanes = pl.cdiv(n_blocks, LANES) * LANES
    return pl.pallas_call(
        _plan_kernel,
        out_shape=[
            jax.ShapeDtypeStruct((TOP_K, t_all), I32),
            jax.ShapeDtypeStruct((1, n_lanes), I32),
            jax.ShapeDtypeStruct((W_AHEAD, n_lanes), I32),
            jax.ShapeDtypeStruct((1, LANES), I32),
            jax.ShapeDtypeStruct((1, N_EXPERTS), I32),
            jax.ShapeDtypeStruct((1, N_EXPERTS), I32),
        ],
        grid=(t_all // TM,),
        in_specs=[
            _const_spec((N_EXPERTS, 1)),
            pl.BlockSpec((TOP_K, TM), lambda i: (0, i)),
            pl.BlockSpec((TOP_K, TM), lambda i: (0, i)),
        ],
        out_specs=[
            pl.BlockSpec((TOP_K, TM), lambda i: (0, i)),
            pl.BlockSpec((1, n_lanes), lambda i: (0, 0)),
            pl.BlockSpec((W_AHEAD, n_lanes), lambda i: (0, 0)),
            pl.BlockSpec((1, LANES), lambda i: (0, 0)),
            pl.BlockSpec((1, N_EXPERTS), lambda i: (0, 0)),
            pl.BlockSpec((1, N_EXPERTS), lambda i: (0, 0)),
        ],
        scratch_shapes=[pltpu.VMEM((N_EXPERTS, 1), F32)],
        compiler_params=pltpu.CompilerParams(dimension_semantics=("arbitrary",)),
        name="plan",
    )(cnt, eidx, rank)


def _dispatch_kernel(cnt_ref, pst_ref, nblk_ref, dest_ref, h2_ref, xs_hbm, zeros, sem, fill_sem,
                     *, n_steps, n_blocks):
    i = pl.program_id(0)
    per_step = pl.cdiv(N_EXPERTS, n_steps)

    @pl.when(i == 0)
    def _():
        zeros[...] = jnp.zeros_like(zeros)

    def issue(t, carry):
        for k in range(TOP_K):
            pltpu.make_async_copy(_packed_rows(h2_ref, t, 1), _packed_rows(xs_hbm, dest_ref[t * TOP_K + k], 1),
                                  sem).start(priority=k % DMA_THREADS)
        return carry

    lax.fori_loop(0, TM, issue, 0, unroll=ISSUE_UNROLL)

    def fills(do):
        def per_expert(j, carry):
            e = i * per_step + j

            @pl.when(e < N_EXPERTS)
            def _():
                cnt = cnt_ref[e]
                base = pst_ref[e]
                padded = (cnt + (BM - 1)) // BM * BM
                mid = jnp.minimum((cnt + (SUBLANES - 1)) // SUBLANES * SUBLANES, padded)

                def one(r, c):
                    do(pltpu.make_async_copy(_packed_rows(zeros, 0, 1), _packed_rows(xs_hbm, base + r, 1), fill_sem))
                    return c

                lax.fori_loop(cnt, mid, one, 0)

                def eight(q, c):
                    r = base + mid + q * SUBLANES
                    do(pltpu.make_async_copy(_packed_rows(zeros, 0, SUBLANES), _packed_rows(xs_hbm, r, SUBLANES),
                                             fill_sem))
                    return c

                lax.fori_loop(0, (padded - mid) // SUBLANES, eight, 0)

            return carry

        lax.fori_loop(0, per_step, per_expert, 0)

        def per_block(j, carry):
            b = nblk_ref[0] + i * per_step + j

            @pl.when(b < n_blocks)
            def _():
                do(pltpu.make_async_copy(zeros, _packed_rows(xs_hbm, b * BM, BM), fill_sem))

            return carry

        lax.fori_loop(0, per_step, per_block, 0)

    fills(lambda c: c.start())
    for k in range(TOP_K):
        pltpu.make_async_copy(h2_ref, _packed_rows(xs_hbm, 0, TM), sem).wait()
    fills(lambda c: c.wait())


def _dispatch(cnt_row, pstart_row, nblk, dest, h2, n_blocks):
    t_all = h2.shape[0] // PACK_ROWS
    n_steps = t_all // TM
    kern = functools.partial(_dispatch_kernel, n_steps=n_steps, n_blocks=n_blocks)
    grid_spec = pltpu.PrefetchScalarGridSpec(
        num_scalar_prefetch=3,
        grid=(n_steps,),
        in_specs=[
            pl.BlockSpec((TM * TOP_K,), lambda i, *_: (i,), memory_space=pltpu.SMEM),
            pl.BlockSpec((TM * PACK_ROWS, LANES), lambda i, *_: (i, 0)),
        ],
        out_specs=pl.BlockSpec(memory_space=pl.ANY),
        scratch_shapes=[pltpu.VMEM((BM * PACK_ROWS, LANES), U32), pltpu.SemaphoreType.DMA(()),
                        pltpu.SemaphoreType.DMA(())],
    )
    return pl.pallas_call(
        kern,
        out_shape=jax.ShapeDtypeStruct((n_blocks * BM * PACK_ROWS, LANES), U32),
        grid_spec=grid_spec,
        compiler_params=pltpu.CompilerParams(dimension_semantics=("arbitrary",)),
        name="dispatch",
    )(cnt_row, pstart_row, nblk, dest, h2)


def _moe_kernel(blk_e_ref, nxt_e_ref, nblk_ref, x_ref, wg_hbm, wu_hbm, wd_hbm, y_ref,
                wg_l, wu_l, wd_l, wgu_s, wd_s, sems, cur_ref):
    nblk = nblk_ref[0]
    first = pl.program_id(0) * MOE_SUB

    def weight_copies(e, slot):
        return (pltpu.make_async_copy(wg_hbm.at[e], wg_l.at[slot], sems.at[slot, 0]),
                pltpu.make_async_copy(wu_hbm.at[e], wu_l.at[slot], sems.at[slot, 1]),
                pltpu.make_async_copy(wd_hbm.at[e], wd_l.at[slot], sems.at[slot, 2]))

    def prepare(b):
        @pl.when(b < nblk)
        def _():
            e = blk_e_ref[b]

            @pl.when(b == 0)
            def _():
                cur_ref[0] = 0
                for c in weight_copies(e, 0):
                    c.start()
                for h in range(W_AHEAD - 1):
                    ahead = nxt_e_ref[h, 0]

                    @pl.when(ahead >= 0)
                    def _():
                        for c in weight_copies(ahead, h + 1):
                            c.start()

            @pl.when((b == 0) | (e != blk_e_ref[jnp.maximum(b - 1, 0)]))
            def _():
                @pl.when(b > 0)
                def _():
                    cur_ref[0] = cur_ref[0] + 1

                group = cur_ref[0]
                landed = group % W_RING
                for c in weight_copies(e, landed):
                    c.wait()
                ahead = nxt_e_ref[W_AHEAD - 1, b]

                @pl.when(ahead >= 0)
                def _():
                    for c in weight_copies(ahead, (group + W_AHEAD) % W_RING):
                        c.start()

                slot = group % MOE_SUB
                wgu_s[slot, :, :D_EXPERT] = wg_l[landed].astype(BF16)
                wgu_s[slot, :, D_EXPERT:] = wu_l[landed].astype(BF16)
                wd_s[slot] = wd_l[landed].astype(BF16)

        return cur_ref[0] % MOE_SUB

    slots = [prepare(first + j) for j in range(MOE_SUB)]

    @pl.when(first < nblk)
    def _():
        for j in range(MOE_SUB):
            b = first + j
            rows = pl.ds(j * BM * PACK_ROWS, BM * PACK_ROWS)
            x = _unpack_bf16_pairs(x_ref.at[rows], BM).astype(BF16)
            gu = jnp.dot(x, wgu_s[slots[j]], preferred_element_type=F32)
            h = (jax.nn.silu(gu[:, :D_EXPERT]) * gu[:, D_EXPERT:]).astype(BF16)
            y = jnp.dot(h, wd_s[slots[j]], preferred_element_type=F32)
            y = jnp.where(b < nblk, y, 0.0)
            _store_packed(y_ref.at[rows], _pack_bf16_pairs(y), BM)


def _moe(blk_e, nxt_e, nblk, xs, wg, wu, wd):
    d = D_MODEL
    n_blocks = xs.shape[0] // (BM * PACK_ROWS)
    assert n_blocks % MOE_SUB == 0

    def step_map(s, be, nx, nb):
        return (jnp.minimum(s, (nb[0] - 1) // MOE_SUB), 0)

    grid_spec = pltpu.PrefetchScalarGridSpec(
        num_scalar_prefetch=3,
        grid=(n_blocks // MOE_SUB,),
        in_specs=[
            pl.BlockSpec((MOE_SUB * BM * PACK_ROWS, LANES), step_map),
            pl.BlockSpec(memory_space=pl.ANY),
            pl.BlockSpec(memory_space=pl.ANY),
            pl.BlockSpec(memory_space=pl.ANY),
        ],
        out_specs=pl.BlockSpec((MOE_SUB * BM * PACK_ROWS, LANES), step_map),
        scratch_shapes=[
            pltpu.VMEM((W_RING, d, D_EXPERT), F32), pltpu.VMEM((W_RING, d, D_EXPERT), F32),
            pltpu.VMEM((W_RING, D_EXPERT, d), F32),
            pltpu.VMEM((MOE_SUB, d, 2 * D_EXPERT), BF16), pltpu.VMEM((MOE_SUB, D_EXPERT, d), BF16),
            pltpu.SemaphoreType.DMA((W_RING, 3)), pltpu.SMEM((1,), I32),
        ],
    )
    return pl.pallas_call(
        _moe_kernel,
        out_shape=jax.ShapeDtypeStruct((n_blocks * BM * PACK_ROWS, LANES), U32),
        grid_spec=grid_spec,
        input_output_aliases={3: 0},
        compiler_params=pltpu.CompilerParams(dimension_semantics=("arbitrary",),
                                             vmem_limit_bytes=VMEM_LIMIT_BYTES),
        name="moe",
    )(blk_e, nxt_e, nblk, xs, wg, wu, wd)


def _final_kernel(dest_ref, dnext_ref, xs_ref, w_ref, gp_ref, gs_ref, fp_ref, fs_ref, gf_ref, y_hbm,
                  op_ref, os_ref, buf, sems, *, n_tiles, n_ptiles, tiles_per_seq):
    i = pl.program_id(0)
    d = D_MODEL
    slot = i % 2

    def gather(d_ref, s):
        def issue(t, carry):
            for k in range(TOP_K):
                pltpu.make_async_copy(_packed_rows(y_hbm, d_ref[t * TOP_K + k], 1), _packed_rows(buf.at[s, k], t, 1),
                                      sems.at[s]).start(priority=k % DMA_THREADS)
            return carry

        lax.fori_loop(0, TC, issue, 0, unroll=ISSUE_UNROLL)

    @pl.when(i == 0)
    def _():
        gather(dest_ref, 0)

    @pl.when(i + 1 < n_tiles)
    def _():
        gather(dnext_ref, 1 - slot)

    for k in range(TOP_K):
        pltpu.make_async_copy(_packed_rows(y_hbm, 0, TC), buf.at[slot, k], sems.at[slot]).wait()

    w = w_ref[...]
    acc = w[:, 0:1] * _unpack_bf16_pairs(buf.at[slot, 0], TC)
    for k in range(1, TOP_K):
        acc = acc + w[:, k:k + 1] * _unpack_bf16_pairs(buf.at[slot, k], TC)

    def finish(gate2, shift, scale, o_ref):
        x2 = xs_ref[...] + gate2 * acc
        o_ref[...] = _rms(x2, gf_ref[...]) * (1.0 + scale) + shift

    @pl.when(i < n_ptiles)
    def _():
        b = i // tiles_per_seq
        f = fp_ref[pl.ds(b, 1), :]
        finish(gp_ref[pl.ds(b, 1), :], f[:, :d], f[:, d:], op_ref)

    @pl.when(i >= n_ptiles)
    def _():
        f = fs_ref[...]
        finish(gs_ref[...], f[:, :d], f[:, d:], os_ref)


def _final(dest, xs, wts, m_all, mf_all, gf, y_sorted, *, n_batch, seq, t_p):
    t_all, d = xs.shape
    n_tiles = t_all // TC
    n_ptiles = t_p // TC
    dec_batch = m_all.shape[0] - n_batch
    kern = functools.partial(_final_kernel, n_tiles=n_tiles, n_ptiles=n_ptiles, tiles_per_seq=seq // TC)
    pb = dec_batch // n_batch
    return pl.pallas_call(
        kern,
        out_shape=[jax.ShapeDtypeStruct((t_p, d), F32), jax.ShapeDtypeStruct((t_all - t_p, d), F32)],
        grid=(n_tiles,),
        in_specs=[
            pl.BlockSpec((TC * TOP_K,), lambda i: (i,), memory_space=pltpu.SMEM),
            pl.BlockSpec((TC * TOP_K,), lambda i: (jnp.minimum(i + 1, n_tiles - 1),), memory_space=pltpu.SMEM),
            pl.BlockSpec((TC, d), lambda i: (i, 0)),
            pl.BlockSpec((TC, TOP_K), lambda i: (i, 0)),
            pl.BlockSpec((n_batch, d), lambda i: (pb, N_MOD - 1)),
            pl.BlockSpec((dec_batch, d), lambda i: (0, N_MOD - 1)),
            pl.BlockSpec((n_batch, 2 * d), lambda i: (pb, 0)),
            pl.BlockSpec((dec_batch, 2 * d), lambda i: (0, 0)),
            _const_spec(gf.shape),
            pl.BlockSpec(memory_space=pl.ANY),
        ],
        out_specs=[
            pl.BlockSpec((TC, d), lambda i: (jnp.minimum(i, n_ptiles - 1), 0)),
            pl.BlockSpec((TC, d), lambda i: (jnp.maximum(i - n_ptiles, 0), 0)),
        ],
        scratch_shapes=[pltpu.VMEM((2, TOP_K, TC * PACK_ROWS, LANES), U32), pltpu.SemaphoreType.DMA((2,))],
        compiler_params=pltpu.CompilerParams(dimension_semantics=("arbitrary",),
                                             vmem_limit_bytes=VMEM_LIMIT_BYTES),
        name="final",
    )(dest, dest, xs, wts, m_all, m_all, mf_all, mf_all, gf, y_sorted)


def kernel(x_prompt, x_sample, state_conv, c_prompt, c_sample, w_ada, b_ada, g_norm1, w_in, w_spatial, b_spatial, g_v, w_conv, g_out_a, g_out_b, w_out, g_norm2, w_router, b_router, w_exp_gate, w_exp_up, w_exp_down, w_sh_gate, w_sh_up, w_sh_down, w_ada_final, b_ada_final, g_final):
    n_batch, seq, d = x_prompt.shape
    dec_batch, dec_seq, _ = x_sample.shape
    assert w_ada.shape[0] == 1, "one layer only"
    assert d == D_MODEL and dec_seq == DEC_SEQ and dec_batch * dec_seq == TM and seq % TM == 0
    assert n_batch % SUBLANES == 0 and dec_batch % n_batch == 0
    t_p = n_batch * seq
    t_s = dec_batch * dec_seq
    t_all = t_p + t_s

    c_all = jnp.concatenate([c_sample, c_prompt], axis=0)
    m_all = _ada(c_all, w_ada[0], b_ada[0])
    mf_all = _ada(c_all, w_ada_final, b_ada_final)

    rep = functools.partial(jnp.repeat, repeats=A_HEAD_DIM, axis=1)
    tril = jnp.tril(jnp.ones((CHUNK, CHUNK), dtype=bool))
    wt = jnp.where(tril, w_spatial[0], 0.0).astype(BF16)
    bsp = rep(b_spatial[0].T)
    csp = rep(jnp.transpose(w_spatial[0][:, :DEC_SEQ, :DEC_SEQ], (1, 2, 0)).reshape(DEC_SEQ * DEC_SEQ, A_HEADS))
    bsps = rep(b_spatial[0][:, :DEC_SEQ].T)
    st = jnp.transpose(state_conv[0], (1, 0, 2))
    wshgu = jnp.concatenate([w_sh_gate[0], w_sh_up[0]], axis=1).astype(BF16)
    xsm = jnp.transpose(x_sample, (1, 0, 2)).reshape(t_s, d)

    xs, h2, lt, cztail, vs, cz23 = _mixer(
        x_prompt.reshape(t_p, d), xsm, m_all, g_norm1, w_in[0].astype(BF16), wt, bsp,
        g_v.reshape(1, MIX_A), w_conv[0], g_out_a, g_out_b, w_out[0].astype(BF16), g_norm2,
        w_router[0].T.astype(BF16), wshgu, w_sh_down[0].astype(BF16), csp, bsps, st,
        n_batch=n_batch, seq=seq)

    eidx, wts, rank, cnt = _route(lt, b_router[0].reshape(N_EXPERTS, 1))

    n_blocks = (t_all * TOP_K) // BM + N_EXPERTS
    dest, blk_e, nxt_e, nblk, cnt_row, pstart_row = _plan(cnt, eidx, rank, n_blocks)

    dest_tok = dest.T.reshape(t_all * TOP_K)
    x_sorted = _dispatch(cnt_row[0], pstart_row[0], nblk[0, :1], dest_tok, h2, n_blocks)
    y_sorted = _moe(blk_e[0, :n_blocks], nxt_e[:, :n_blocks], nblk[0, :1], x_sorted,
                    w_exp_gate[0], w_exp_up[0], w_exp_down[0])
    y_p, y_s = _final(dest_tok, xs, wts.T, m_all, mf_all, g_final.reshape(1, d), y_sorted,
                      n_batch=n_batch, seq=seq, t_p=t_p)

    tiles_per_seq = seq // TM
    y_prompt = y_p.reshape(n_batch, seq, d)
    y_sample = jnp.transpose(y_s.reshape(dec_seq, dec_batch, d), (1, 0, 2))
    conv_p = cztail[tiles_per_seq - 1:n_batch * tiles_per_seq:tiles_per_seq, SUBLANES - (CONV_W - 1):, :][None]
    conv_s = jnp.transpose(cz23, (1, 0, 2))[None]
    v_s = jnp.transpose(vs.reshape(dec_seq, dec_batch, A_HEADS, A_HEAD_DIM), (1, 0, 2, 3))[None]
    return (y_prompt, y_sample, conv_p, conv_s, v_s)
```

```python
import functools

import jax
import jax.numpy as jnp
from jax import lax
from jax.experimental import pallas as pl
from jax.experimental.pallas import tpu as pltpu

F32 = jnp.float32
BF16 = jnp.bfloat16
I32 = jnp.int32
U32 = jnp.uint32

D_MODEL = 1024
MIX_A = 512
A_HEADS = 4
A_HEAD_DIM = 128
CHUNK = 128
CONV_DIM = 512
CONV_W = 3
PROJ_DIM = 2 * MIX_A + 3 * CONV_DIM
N_EXPERTS = 256
TOP_K = 8
N_GROUPS = 8
TOPK_GROUPS = 4
GROUP_SIZE = N_EXPERTS // N_GROUPS
D_EXPERT = 256
D_SHARED = 256
ROUTED_SCALE = 2.5
N_MOD = 6
RMS_EPS = 1e-6
DEC_SEQ = 4

LANES = 128
SUBLANES = 8
PACK_ROWS = D_MODEL // 2 // LANES
TM = 512
BM = 256
TC = 128
DMA_THREADS = 2
ISSUE_UNROLL = 16
MOE_SUB = 4
W_AHEAD = 3
W_RING = W_AHEAD + 1
VMEM_LIMIT_BYTES = 56 * 1024 * 1024


def _rms(x, g):
    return x * lax.rsqrt(jnp.mean(x * x, axis=-1, keepdims=True) + RMS_EPS) * g


def _packed_rows(ref, row, n):
    return ref.at[pl.ds(pl.multiple_of(row * PACK_ROWS, PACK_ROWS), n * PACK_ROWS)]


def _pack_bf16_pairs(x):
    half = D_MODEL // 2
    xb = x.astype(BF16).astype(F32)
    lo = lax.shift_right_logical(lax.bitcast_convert_type(xb[:, :half], U32), jnp.uint32(16))
    return lo | lax.bitcast_convert_type(xb[:, half:], U32)


def _store_packed(ref, packed, n):
    for j in range(PACK_ROWS):
        ref[pl.ds(j, n, stride=PACK_ROWS), :] = packed[:, j * LANES:(j + 1) * LANES]


def _unpack_bf16_pairs(ref, n):
    words = [ref[pl.ds(j, n, stride=PACK_ROWS), :] for j in range(PACK_ROWS)]
    lo = [lax.bitcast_convert_type(lax.shift_left(w, jnp.uint32(16)), F32) for w in words]
    hi = [lax.bitcast_convert_type(w & jnp.uint32(0xFFFF0000), F32) for w in words]
    return jnp.concatenate(lo + hi, axis=1)


def _const_spec(shape):
    nd = len(shape)
    return pl.BlockSpec(shape, lambda *_: (0,) * nd, pipeline_mode=pl.Buffered(1))


def _ada_kernel(c_ref, w_ref, b_ref, o_ref):
    a = jax.nn.silu(c_ref[...]).astype(BF16)
    o_ref[...] = jnp.dot(a, w_ref[...].astype(BF16), preferred_element_type=F32) + b_ref[...]


def _ada(c_all, w, b):
    rows, n = c_all.shape[0], w.shape[1]
    return pl.pallas_call(
        _ada_kernel,
        out_shape=jax.ShapeDtypeStruct((rows, n), F32),
        grid=(n // D_MODEL,),
        in_specs=[
            pl.BlockSpec((rows, D_MODEL), lambda j: (0, 0)),
            pl.BlockSpec((D_MODEL, D_MODEL), lambda j: (0, j)),
            pl.BlockSpec((1, D_MODEL), lambda j: (0, j)),
        ],
        out_specs=pl.BlockSpec((rows, D_MODEL), lambda j: (0, j)),
        compiler_params=pltpu.CompilerParams(dimension_semantics=("arbitrary",)),
        name="ada",
    )(c_all, w, b.reshape(1, n))


def _mixer_kernel(xp_ref, xsm_ref, mp_ref, ms_ref, g1_ref, win_ref, wt_ref, bsp_ref, gv_ref, wconv_ref,
                  goa_ref, gob_ref, wout_ref, g2_ref, wrt_ref, wshgu_ref, wshd_ref, csp_ref, bsps_ref, st_ref,
                  xs_out, h2_out, lt_out, cztail_out, vs_out, cz23_out, carry_ref, *, n_ptiles, tiles_per_seq):
    i = pl.program_id(0)
    d = D_MODEL

    def proj(x, sh1, sc1):
        h = _rms(x, g1_ref[...]) * (1.0 + sc1) + sh1
        p = jnp.dot(h.astype(BF16), win_ref[...], preferred_element_type=F32)
        u = jax.nn.gelu(p[:, :MIX_A])
        v = jax.nn.gelu(p[:, MIX_A:2 * MIX_A])
        vn = jnp.concatenate(
            [_rms(v[:, h * LANES:(h + 1) * LANES], gv_ref[:, h * LANES:(h + 1) * LANES]) for h in range(A_HEADS)],
            axis=1)
        o = 2 * MIX_A
        return u, vn, p[:, o:o + CONV_DIM], p[:, o + CONV_DIM:o + 2 * CONV_DIM], p[:, o + 2 * CONV_DIM:]

    def tail(x, ya, yb, gate1, sh2, sc2, gate2):
        cat = jnp.concatenate([_rms(ya, goa_ref[...]), _rms(yb, gob_ref[...])], axis=1).astype(BF16)
        x1 = x + gate1 * jnp.dot(cat, wout_ref[...], preferred_element_type=F32)
        h2 = _rms(x1, g2_ref[...]) * (1.0 + sc2) + sh2
        h2b = h2.astype(BF16)
        lt_out[...] = lax.dot_general(wrt_ref[...], h2b, (((1,), (1,)), ((), ())), preferred_element_type=F32)
        gu = jnp.dot(h2b, wshgu_ref[...], preferred_element_type=F32)
        hs = (jax.nn.silu(gu[:, :D_SHARED]) * gu[:, D_SHARED:]).astype(BF16)
        xs_out[...] = x1 + gate2 * jnp.dot(hs, wshd_ref[...], preferred_element_type=F32)
        _store_packed(h2_out, _pack_bf16_pairs(h2), TM)

    @pl.when(i < n_ptiles)
    def _prompt():
        b = i // tiles_per_seq
        m = mp_ref[pl.ds(b, 1), :]
        mod = [m[:, k * d:(k + 1) * d] for k in range(N_MOD)]
        x = xp_ref[...]
        u, vn, bg, cg, z = proj(x, mod[0], mod[1])
        n_chunks = TM // CHUNK
        ya_cols = []
        for h in range(A_HEADS):
            vh = vn[:, h * LANES:(h + 1) * LANES].astype(BF16)
            rhs = jnp.concatenate([vh[c * CHUNK:(c + 1) * CHUNK] for c in range(n_chunks)], axis=1)
            mix = jnp.dot(wt_ref[h], rhs, preferred_element_type=F32)
            bias = bsp_ref[:, h * LANES:(h + 1) * LANES]
            mix = jnp.concatenate([mix[:, c * LANES:(c + 1) * LANES] + bias for c in range(n_chunks)], axis=0)
            ya_cols.append(u[:, h * LANES:(h + 1) * LANES] * mix)
        ya = jnp.concatenate(ya_cols, axis=1)
        cz = cg * z

        @pl.when(i % tiles_per_seq == 0)
        def _():
            carry_ref[...] = jnp.zeros_like(carry_ref)

        c6 = carry_ref[SUBLANES - 2:SUBLANES - 1, :]
        c7 = carry_ref[SUBLANES - 1:SUBLANES, :]
        r = lax.broadcasted_iota(I32, (TM, 1), 0)
        p1 = jnp.where(r == 0, c7, pltpu.roll(cz, 1, 0))
        p2 = jnp.where(r == 0, c6, jnp.where(r == 1, c7, pltpu.roll(cz, 2, 0)))
        wc = wconv_ref[...]
        yb = bg * (p2 * wc[0:1] + p1 * wc[1:2] + cz * wc[2:3])
        carry_ref[...] = cz[TM - SUBLANES:]
        cztail_out[0] = cz[TM - SUBLANES:]
        tail(x, ya, yb, mod[2], mod[3], mod[4], mod[5])

    @pl.when(i == n_ptiles)
    def _sample():
        ms = ms_ref[...]
        nb = TM // DEC_SEQ

        def mod(k):
            return jnp.concatenate([ms[:, k * d:(k + 1) * d]] * DEC_SEQ, axis=0)

        x = xsm_ref[...]
        u, vn, bg, cg, z = proj(x, mod(0), mod(1))
        vt = [vn[t * nb:(t + 1) * nb] for t in range(DEC_SEQ)]
        mixes = []
        for t in range(DEC_SEQ):
            acc = csp_ref[DEC_SEQ * t:DEC_SEQ * t + 1, :] * vt[0]
            for s in range(1, t + 1):
                acc = acc + csp_ref[DEC_SEQ * t + s:DEC_SEQ * t + s + 1, :] * vt[s]
            mixes.append(acc + bsps_ref[t:t + 1, :])
        ya = u * jnp.concatenate(mixes, axis=0)
        cz = cg * z
        czt = [cz[t * nb:(t + 1) * nb] for t in range(DEC_SEQ)]
        full = [st_ref[0], st_ref[1]] + czt
        wc = wconv_ref[...]
        yc = jnp.concatenate(
            [full[t] * wc[0:1] + full[t + 1] * wc[1:2] + full[t + 2] * wc[2:3] for t in range(DEC_SEQ)], axis=0)
        yb = bg * yc
        vs_out[...] = vn
        cz23_out[0] = czt[DEC_SEQ - 2]
        cz23_out[1] = czt[DEC_SEQ - 1]
        cztail_out[0] = cz[TM - SUBLANES:]
        tail(x, ya, yb, mod(2), mod(3), mod(4), mod(5))


def _mixer(xp, xsm, m_all, g1, win, wt, bsp, gv, wconv, goa, gob, wout, g2, wrt, wshgu, wshd, csp, bsps, st,
           *, n_batch, seq):
    t_p = xp.shape[0]
    n_ptiles = t_p // TM
    n_tiles = n_ptiles + 1
    t_all = n_tiles * TM
    dec_batch = xsm.shape[0] // DEC_SEQ
    d = D_MODEL
    kern = functools.partial(_mixer_kernel, n_ptiles=n_ptiles, tiles_per_seq=seq // TM)
    in_specs = [
        pl.BlockSpec((TM, d), lambda i: (jnp.minimum(i, n_ptiles - 1), 0)),
        _const_spec((TM, d)),
        pl.BlockSpec((n_batch, N_MOD * d), lambda i: (dec_batch // n_batch, 0), pipeline_mode=pl.Buffered(1)),
        pl.BlockSpec((dec_batch, N_MOD * d), lambda i: (0, 0), pipeline_mode=pl.Buffered(1)),
        _const_spec(g1.shape), _const_spec(win.shape), _const_spec(wt.shape), _const_spec(bsp.shape),
        _const_spec(gv.shape), _const_spec(wconv.shape), _const_spec(goa.shape), _const_spec(gob.shape),
        _const_spec(wout.shape), _const_spec(g2.shape), _const_spec(wrt.shape), _const_spec(wshgu.shape),
        _const_spec(wshd.shape), _const_spec(csp.shape), _const_spec(bsps.shape), _const_spec(st.shape),
    ]
    out_shape = [
        jax.ShapeDtypeStruct((t_all, d), F32),
        jax.ShapeDtypeStruct((t_all * PACK_ROWS, LANES), U32),
        jax.ShapeDtypeStruct((N_EXPERTS, t_all), F32),
        jax.ShapeDtypeStruct((n_tiles, SUBLANES, CONV_DIM), F32),
        jax.ShapeDtypeStruct((TM, MIX_A), F32),
        jax.ShapeDtypeStruct((2, dec_batch, CONV_DIM), F32),
    ]
    out_specs = [
        pl.BlockSpec((TM, d), lambda i: (i, 0)),
        pl.BlockSpec((TM * PACK_ROWS, LANES), lambda i: (i, 0)),
        pl.BlockSpec((N_EXPERTS, TM), lambda i: (0, i)),
        pl.BlockSpec((1, SUBLANES, CONV_DIM), lambda i: (i, 0, 0)),
        pl.BlockSpec((TM, MIX_A), lambda i: (0, 0)),
        pl.BlockSpec((2, dec_batch, CONV_DIM), lambda i: (0, 0, 0)),
    ]
    return pl.pallas_call(
        kern,
        out_shape=out_shape,
        grid=(n_tiles,),
        in_specs=in_specs,
        out_specs=out_specs,
        scratch_shapes=[pltpu.VMEM((SUBLANES, CONV_DIM), F32)],
        compiler_params=pltpu.CompilerParams(dimension_semantics=("arbitrary",),
                                             vmem_limit_bytes=VMEM_LIMIT_BYTES),
        name="mixer",
    )(xp, xsm, m_all, m_all, g1, win, wt, bsp, gv, wconv, goa, gob, wout, g2, wrt, wshgu, wshd, csp, bsps, st)


def _route_kernel(lt_ref, br_ref, e_out, w_out, r_out, cnt_out, carry_ref, tri_ref):
    i = pl.program_id(0)
    neg = -jnp.inf

    @pl.when(i == 0)
    def _():
        carry_ref[...] = jnp.zeros_like(carry_ref)
        a = lax.broadcasted_iota(I32, (TM, TM), 0)
        b = lax.broadcasted_iota(I32, (TM, TM), 1)
        tri_ref[...] = jnp.where(a < b, 1.0, 0.0).astype(BF16)

    scores = jax.nn.sigmoid(lt_ref[...])
    sel = scores + br_ref[...]
    iog = lax.broadcasted_iota(I32, (GROUP_SIZE, TM), 0)
    gs_rows = []
    for g in range(N_GROUPS):
        sg = sel[g * GROUP_SIZE:(g + 1) * GROUP_SIZE]
        m1 = jnp.max(sg, axis=0, keepdims=True)
        i1 = jnp.min(jnp.where(sg == m1, iog, GROUP_SIZE), axis=0, keepdims=True)
        m2 = jnp.max(jnp.where(iog == i1, neg, sg), axis=0, keepdims=True)
        gs_rows.append(m1 + m2)
    gs = jnp.concatenate(gs_rows, axis=0)
    io8 = lax.broadcasted_iota(I32, (N_GROUPS, TM), 0)
    keep = jnp.zeros((N_GROUPS, TM), F32)
    for _ in range(TOPK_GROUPS):
        m = jnp.max(gs, axis=0, keepdims=True)
        idx = jnp.min(jnp.where(gs == m, io8, N_GROUPS), axis=0, keepdims=True)
        hit = io8 == idx
        keep = jnp.where(hit, 1.0, keep)
        gs = jnp.where(hit, neg, gs)
    sel = jnp.concatenate(
        [jnp.where(keep[g:g + 1] > 0.0, sel[g * GROUP_SIZE:(g + 1) * GROUP_SIZE], neg) for g in range(N_GROUPS)],
        axis=0)
    ioe = lax.broadcasted_iota(I32, (N_EXPERTS, TM), 0)
    picked = jnp.zeros((N_EXPERTS, TM), F32)
    e_rows, w_rows = [], []
    for _ in range(TOP_K):
        m = jnp.max(sel, axis=0, keepdims=True)
        idx = jnp.min(jnp.where(sel == m, ioe, N_EXPERTS), axis=0, keepdims=True)
        hit = ioe == idx
        e_rows.append(idx)
        w_rows.append(jnp.sum(jnp.where(hit, scores, 0.0), axis=0, keepdims=True))
        picked = jnp.where(hit, 1.0, picked)
        sel = jnp.where(hit, neg, sel)
    wk = jnp.concatenate(w_rows, axis=0)
    w_out[...] = wk / jnp.sum(wk, axis=0, keepdims=True) * ROUTED_SCALE
    e_out[...] = jnp.concatenate(e_rows, axis=0)
    before = jnp.dot(picked.astype(BF16), tri_ref[...], preferred_element_type=F32) + carry_ref[...]
    r_rows = [jnp.sum(jnp.where(ioe == e_rows[k], before, 0.0), axis=0, keepdims=True) for k in range(TOP_K)]
    r_out[...] = jnp.concatenate(r_rows, axis=0).astype(I32)
    carry_ref[...] = carry_ref[...] + jnp.sum(picked, axis=1, keepdims=True)
    cnt_out[...] = carry_ref[...]


def _route(lt, br):
    t_all = lt.shape[1]
    n_tiles = t_all // TM
    return pl.pallas_call(
        _route_kernel,
        out_shape=[
            jax.ShapeDtypeStruct((TOP_K, t_all), I32),
            jax.ShapeDtypeStruct((TOP_K, t_all), F32),
            jax.ShapeDtypeStruct((TOP_K, t_all), I32),
            jax.ShapeDtypeStruct((N_EXPERTS, 1), F32),
        ],
        grid=(n_tiles,),
        in_specs=[pl.BlockSpec((N_EXPERTS, TM), lambda i: (0, i)), _const_spec((N_EXPERTS, 1))],
        out_specs=[
            pl.BlockSpec((TOP_K, TM), lambda i: (0, i)),
            pl.BlockSpec((TOP_K, TM), lambda i: (0, i)),
            pl.BlockSpec((TOP_K, TM), lambda i: (0, i)),
            pl.BlockSpec((N_EXPERTS, 1), lambda i: (0, 0)),
        ],
        scratch_shapes=[pltpu.VMEM((N_EXPERTS, 1), F32), pltpu.VMEM((TM, TM), BF16)],
        compiler_params=pltpu.CompilerParams(dimension_semantics=("arbitrary",)),
        name="route",
    )(lt, br)


def _plan_kernel(cnt_ref, e_ref, r_ref, dest_out, blk_e_out, nxt_e_out, nblk_out, cnt_row_out, pstart_row_out,
                 pstart_ref):
    i = pl.program_id(0)

    @pl.when(i == 0)
    def _():
        nb = jnp.floor((cnt_ref[...] + (BM - 1)) * (1.0 / BM))
        a = lax.broadcasted_iota(I32, (N_EXPERTS, N_EXPERTS), 0)
        b = lax.broadcasted_iota(I32, (N_EXPERTS, N_EXPERTS), 1)
        lower = jnp.where(b < a, 1.0, 0.0).astype(BF16)
        nb_l = jnp.broadcast_to(nb, (N_EXPERTS, LANES)).astype(BF16)
        first_blk = jnp.dot(lower, nb_l, preferred_element_type=F32)[:, 0:1]
        end_blk = first_blk + nb
        pstart_ref[...] = first_blk * BM
        cnt_row_out[...] = jnp.sum(jnp.where(a == b, cnt_ref[...], 0.0), axis=0, keepdims=True).astype(I32)
        pstart_row_out[...] = jnp.sum(jnp.where(a == b, first_blk * BM, 0.0), axis=0, keepdims=True).astype(I32)
        n_lanes = blk_e_out.shape[1]
        blk = lax.broadcasted_iota(I32, (N_EXPERTS, n_lanes), 1).astype(F32)
        owner = jnp.sum(jnp.where(end_blk <= blk, 1.0, 0.0), axis=0, keepdims=True)
        blk_e_out[...] = jnp.minimum(owner, N_EXPERTS - 1.0).astype(I32)
        total = jnp.max(end_blk, axis=0, keepdims=True)
        nblk_out[...] = jnp.broadcast_to(total, nblk_out.shape).astype(I32)
        group_end = blk
        for h in range(W_AHEAD):
            group_end = jnp.min(jnp.where(end_blk > group_end, end_blk, 2.0 * n_lanes), axis=0, keepdims=True)
            nxt = jnp.sum(jnp.where(end_blk <= group_end, 1.0, 0.0), axis=0, keepdims=True)
            nxt_e_out[h:h + 1, :] = jnp.where(group_end < total, nxt, -1.0).astype(I32)

    ioe = lax.broadcasted_iota(I32, (N_EXPERTS, TM), 0)
    e = e_ref[...]
    rows = [jnp.sum(jnp.where(ioe == e[k:k + 1], pstart_ref[...], 0.0), axis=0, keepdims=True)
            for k in range(TOP_K)]
    dest_out[...] = jnp.concatenate(rows, axis=0).astype(I32) + r_ref[...]


def _plan(cnt, eidx, rank, n_blocks):
    t_all = eidx.shape[1]
    n_lanes = pl.cdiv(n_blocks, LANES) * LANES
    return pl.pallas_call(
        _plan_kernel,
        out_shape=[
            jax.ShapeDtypeStruct((TOP_K, t_all), I32),
            jax.ShapeDtypeStruct((1, n_lanes), I32),
            jax.ShapeDtypeStruct((W_AHEAD, n_lanes), I32),
            jax.ShapeDtypeStruct((1, LANES), I32),
            jax.ShapeDtypeStruct((1, N_EXPERTS), I32),
            jax.ShapeDtypeStruct((1, N_EXPERTS), I32),
        ],
        grid=(t_all // TM,),
        in_specs=[
            _const_spec((N_EXPERTS, 1)),
            pl.BlockSpec((TOP_K, TM), lambda i: (0, i)),
            pl.BlockSpec((TOP_K, TM), lambda i: (0, i)),
        ],
        out_specs=[
            pl.BlockSpec((TOP_K, TM), lambda i: (0, i)),
            pl.BlockSpec((1, n_lanes), lambda i: (0, 0)),
            pl.BlockSpec((W_AHEAD, n_lanes), lambda i: (0, 0)),
            pl.BlockSpec((1, LANES), lambda i: (0, 0)),
            pl.BlockSpec((1, N_EXPERTS), lambda i: (0, 0)),
            pl.BlockSpec((1, N_EXPERTS), lambda i: (0, 0)),
        ],
        scratch_shapes=[pltpu.VMEM((N_EXPERTS, 1), F32)],
        compiler_params=pltpu.CompilerParams(dimension_semantics=("arbitrary",)),
        name="plan",
    )(cnt, eidx, rank)


def _dispatch_kernel(cnt_ref, pst_ref, nblk_ref, dest_ref, h2_ref, xs_hbm, zeros, sem, fill_sem,
                     *, n_steps, n_blocks):
    i = pl.program_id(0)
    per_step = pl.cdiv(N_EXPERTS, n_steps)

    @pl.when(i == 0)
    def _():
        zeros[...] = jnp.zeros_like(zeros)

    def issue(t, carry):
        for k in range(TOP_K):
            pltpu.make_async_copy(_packed_rows(h2_ref, t, 1), _packed_rows(xs_hbm, dest_ref[t * TOP_K + k], 1),
                                  sem).start(priority=k % DMA_THREADS)
        return carry

    lax.fori_loop(0, TM, issue, 0, unroll=ISSUE_UNROLL)

    def fills(do):
        def per_expert(j, carry):
            e = i * per_step + j

            @pl.when(e < N_EXPERTS)
            def _():
                cnt = cnt_ref[e]
                base = pst_ref[e]
                padded = (cnt + (BM - 1)) // BM * BM
                mid = jnp.minimum((cnt + (SUBLANES - 1)) // SUBLANES * SUBLANES, padded)

                def one(r, c):
                    do(pltpu.make_async_copy(_packed_rows(zeros, 0, 1), _packed_rows(xs_hbm, base + r, 1), fill_sem))
                    return c

                lax.fori_loop(cnt, mid, one, 0)

                def eight(q, c):
                    r = base + mid + q * SUBLANES
                    do(pltpu.make_async_copy(_packed_rows(zeros, 0, SUBLANES), _packed_rows(xs_hbm, r, SUBLANES),
                                             fill_sem))
                    return c

                lax.fori_loop(0, (padded - mid) // SUBLANES, eight, 0)

            return carry

        lax.fori_loop(0, per_step, per_expert, 0)

        def per_block(j, carry):
            b = nblk_ref[0] + i * per_step + j

            @pl.when(b < n_blocks)
            def _():
                do(pltpu.make_async_copy(zeros, _packed_rows(xs_hbm, b * BM, BM), fill_sem))

            return carry

        lax.fori_loop(0, per_step, per_block, 0)

    fills(lambda c: c.start())
    for k in range(TOP_K):
        pltpu.make_async_copy(h2_ref, _packed_rows(xs_hbm, 0, TM), sem).wait()
    fills(lambda c: c.wait())


def _dispatch(cnt_row, pstart_row, nblk, dest, h2, n_blocks):
    t_all = h2.shape[0] // PACK_ROWS
    n_steps = t_all // TM
    kern = functools.partial(_dispatch_kernel, n_steps=n_steps, n_blocks=n_blocks)
    grid_spec = pltpu.PrefetchScalarGridSpec(
        num_scalar_prefetch=3,
        grid=(n_steps,),
        in_specs=[
            pl.BlockSpec((TM * TOP_K,), lambda i, *_: (i,), memory_space=pltpu.SMEM),
            pl.BlockSpec((TM * PACK_ROWS, LANES), lambda i, *_: (i, 0)),
        ],
        out_specs=pl.BlockSpec(memory_space=pl.ANY),
        scratch_shapes=[pltpu.VMEM((BM * PACK_ROWS, LANES), U32), pltpu.SemaphoreType.DMA(()),
                        pltpu.SemaphoreType.DMA(())],
    )
    return pl.pallas_call(
        kern,
        out_shape=jax.ShapeDtypeStruct((n_blocks * BM * PACK_ROWS, LANES), U32),
        grid_spec=grid_spec,
        compiler_params=pltpu.CompilerParams(dimension_semantics=("arbitrary",)),
        name="dispatch",
    )(cnt_row, pstart_row, nblk, dest, h2)


def _moe_kernel(blk_e_ref, nxt_e_ref, nblk_ref, x_ref, wg_hbm, wu_hbm, wd_hbm, y_ref,
                wg_l, wu_l, wd_l, wgu_s, wd_s, sems, cur_ref):
    nblk = nblk_ref[0]
    first = pl.program_id(0) * MOE_SUB

    def weight_copies(e, slot):
        return (pltpu.make_async_copy(wg_hbm.at[e], wg_l.at[slot], sems.at[slot, 0]),
                pltpu.make_async_copy(wu_hbm.at[e], wu_l.at[slot], sems.at[slot, 1]),
                pltpu.make_async_copy(wd_hbm.at[e], wd_l.at[slot], sems.at[slot, 2]))

    def prepare(b):
        @pl.when(b < nblk)
        def _():
            e = blk_e_ref[b]

            @pl.when(b == 0)
            def _():
                cur_ref[0] = 0
                for c in weight_copies(e, 0):
                    c.start()
                for h in range(W_AHEAD - 1):
                    ahead = nxt_e_ref[h, 0]

                    @pl.when(ahead >= 0)
                    def _():
                        for c in weight_copies(ahead, h + 1):
                            c.start()

            @pl.when((b == 0) | (e != blk_e_ref[jnp.maximum(b - 1, 0)]))
            def _():
                @pl.when(b > 0)
                def _():
                    cur_ref[0] = cur_ref[0] + 1

                group = cur_ref[0]
                landed = group % W_RING
                for c in weight_copies(e, landed):
                    c.wait()
                ahead = nxt_e_ref[W_AHEAD - 1, b]

                @pl.when(ahead >= 0)
                def _():
                    for c in weight_copies(ahead, (group + W_AHEAD) % W_RING):
                        c.start()

                slot = group % MOE_SUB
                wgu_s[slot, :, :D_EXPERT] = wg_l[landed].astype(BF16)
                wgu_s[slot, :, D_EXPERT:] = wu_l[landed].astype(BF16)
                wd_s[slot] = wd_l[landed].astype(BF16)

        return cur_ref[0] % MOE_SUB

    slots = [prepare(first + j) for j in range(MOE_SUB)]

    @pl.when(first < nblk)
    def _():
        for j in range(MOE_SUB):
            b = first + j
            rows = pl.ds(j * BM * PACK_ROWS, BM * PACK_ROWS)
            x = _unpack_bf16_pairs(x_ref.at[rows], BM).astype(BF16)
            gu = jnp.dot(x, wgu_s[slots[j]], preferred_element_type=F32)
            h = (jax.nn.silu(gu[:, :D_EXPERT]) * gu[:, D_EXPERT:]).astype(BF16)
            y = jnp.dot(h, wd_s[slots[j]], preferred_element_type=F32)
            y = jnp.where(b < nblk, y, 0.0)
            _store_packed(y_ref.at[rows], _pack_bf16_pairs(y), BM)


def _moe(blk_e, nxt_e, nblk, xs, wg, wu, wd):
    d = D_MODEL
    n_blocks = xs.shape[0] // (BM * PACK_ROWS)
    assert n_blocks % MOE_SUB == 0

    def step_map(s, be, nx, nb):
        return (jnp.minimum(s, (nb[0] - 1) // MOE_SUB), 0)

    grid_spec = pltpu.PrefetchScalarGridSpec(
        num_scalar_prefetch=3,
        grid=(n_blocks // MOE_SUB,),
        in_specs=[
            pl.BlockSpec((MOE_SUB * BM * PACK_ROWS, LANES), step_map),
            pl.BlockSpec(memory_space=pl.ANY),
            pl.BlockSpec(memory_space=pl.ANY),
            pl.BlockSpec(memory_space=pl.ANY),
        ],
        out_specs=pl.BlockSpec((MOE_SUB * BM * PACK_ROWS, LANES), step_map),
        scratch_shapes=[
            pltpu.VMEM((W_RING, d, D_EXPERT), F32), pltpu.VMEM((W_RING, d, D_EXPERT), F32),
            pltpu.VMEM((W_RING, D_EXPERT, d), F32),
            pltpu.VMEM((MOE_SUB, d, 2 * D_EXPERT), BF16), pltpu.VMEM((MOE_SUB, D_EXPERT, d), BF16),
            pltpu.SemaphoreType.DMA((W_RING, 3)), pltpu.SMEM((1,), I32),
        ],
    )
    return pl.pallas_call(
        _moe_kernel,
        out_shape=jax.ShapeDtypeStruct((n_blocks * BM * PACK_ROWS, LANES), U32),
        grid_spec=grid_spec,
        input_output_aliases={3: 0},
        compiler_params=pltpu.CompilerParams(dimension_semantics=("arbitrary",),
                                             vmem_limit_bytes=VMEM_LIMIT_BYTES),
        name="moe",
    )(blk_e, nxt_e, nblk, xs, wg, wu, wd)


def _final_kernel(dest_ref, dnext_ref, xs_ref, w_ref, gp_ref, gs_ref, fp_ref, fs_ref, gf_ref, y_hbm,
                  op_ref, os_ref, buf, sems, *, n_tiles, n_ptiles, tiles_per_seq):
    i = pl.program_id(0)
    d = D_MODEL
    slot = i % 2

    def gather(d_ref, s):
        def issue(t, carry):
            for k in range(TOP_K):
                pltpu.make_async_copy(_packed_rows(y_hbm, d_ref[t * TOP_K + k], 1), _packed_rows(buf.at[s, k], t, 1),
                                      sems.at[s]).start(priority=k % DMA_THREADS)
            return carry

        lax.fori_loop(0, TC, issue, 0, unroll=ISSUE_UNROLL)

    @pl.when(i == 0)
    def _():
        gather(dest_ref, 0)

    @pl.when(i + 1 < n_tiles)
    def _():
        gather(dnext_ref, 1 - slot)

    for k in range(TOP_K):
        pltpu.make_async_copy(_packed_rows(y_hbm, 0, TC), buf.at[slot, k], sems.at[slot]).wait()

    w = w_ref[...]
    acc = w[:, 0:1] * _unpack_bf16_pairs(buf.at[slot, 0], TC)
    for k in range(1, TOP_K):
        acc = acc + w[:, k:k + 1] * _unpack_bf16_pairs(buf.at[slot, k], TC)

    def finish(gate2, shift, scale, o_ref):
        x2 = xs_ref[...] + gate2 * acc
        o_ref[...] = _rms(x2, gf_ref[...]) * (1.0 + scale) + shift

    @pl.when(i < n_ptiles)
    def _():
        b = i // tiles_per_seq
        f = fp_ref[pl.ds(b, 1), :]
        finish(gp_ref[pl.ds(b, 1), :], f[:, :d], f[:, d:], op_ref)

    @pl.when(i >= n_ptiles)
    def _():
        f = fs_ref[...]
        finish(gs_ref[...], f[:, :d], f[:, d:], os_ref)


def _final(dest, xs, wts, m_all, mf_all, gf, y_sorted, *, n_batch, seq, t_p):
    t_all, d = xs.shape
    n_tiles = t_all // TC
    n_ptiles = t_p // TC
    dec_batch = m_all.shape[0] - n_batch
    kern = functools.partial(_final_kernel, n_tiles=n_tiles, n_ptiles=n_ptiles, tiles_per_seq=seq // TC)
    pb = dec_batch // n_batch
    return pl.pallas_call(
        kern,
        out_shape=[jax.ShapeDtypeStruct((t_p, d), F32), jax.ShapeDtypeStruct((t_all - t_p, d), F32)],
        grid=(n_tiles,),
        in_specs=[
            pl.BlockSpec((TC * TOP_K,), lambda i: (i,), memory_space=pltpu.SMEM),
            pl.BlockSpec((TC * TOP_K,), lambda i: (jnp.minimum(i + 1, n_tiles - 1),), memory_space=pltpu.SMEM),
            pl.BlockSpec((TC, d), lambda i: (i, 0)),
            pl.BlockSpec((TC, TOP_K), lambda i: (i, 0)),
            pl.BlockSpec((n_batch, d), lambda i: (pb, N_MOD - 1)),
            pl.BlockSpec((dec_batch, d), lambda i: (0, N_MOD - 1)),
            pl.BlockSpec((n_batch, 2 * d), lambda i: (pb, 0)),
            pl.BlockSpec((dec_batch, 2 * d), lambda i: (0, 0)),
            _const_spec(gf.shape),
            pl.BlockSpec(memory_space=pl.ANY),
        ],
        out_specs=[
            pl.BlockSpec((TC, d), lambda i: (jnp.minimum(i, n_ptiles - 1), 0)),
            pl.BlockSpec((TC, d), lambda i: (jnp.maximum(i - n_ptiles, 0), 0)),
        ],
        scratch_shapes=[pltpu.VMEM((2, TOP_K, TC * PACK_ROWS, LANES), U32), pltpu.SemaphoreType.DMA((2,))],
        compiler_params=pltpu.CompilerParams(dimension_semantics=("arbitrary",),
                                             vmem_limit_bytes=VMEM_LIMIT_BYTES),
        name="final",
    )(dest, dest, xs, wts, m_all, m_all, mf_all, mf_all, gf, y_sorted)


def kernel(x_prompt, x_sample, state_conv, c_prompt, c_sample, w_ada, b_ada, g_norm1, w_in, w_spatial, b_spatial, g_v, w_conv, g_out_a, g_out_b, w_out, g_norm2, w_router, b_router, w_exp_gate, w_exp_up, w_exp_down, w_sh_gate, w_sh_up, w_sh_down, w_ada_final, b_ada_final, g_final):
    n_batch, seq, d = x_prompt.shape
    dec_batch, dec_seq, _ = x_sample.shape
    assert w_ada.shape[0] == 1, "one layer only"
    assert d == D_MODEL and dec_seq == DEC_SEQ and dec_batch * dec_seq == TM and seq % TM == 0
    assert n_batch % SUBLANES == 0 and dec_batch % n_batch == 0
    t_p = n_batch * seq
    t_s = dec_batch * dec_seq
    t_all = t_p + t_s

    c_all = jnp.concatenate([c_sample, c_prompt], axis=0)
    m_all = _ada(c_all, w_ada[0], b_ada[0])
    mf_all = _ada(c_all, w_ada_final, b_ada_final)

    rep = functools.partial(jnp.repeat, repeats=A_HEAD_DIM, axis=1)
    tril = jnp.tril(jnp.ones((CHUNK, CHUNK), dtype=bool))
    wt = jnp.where(tril, w_spatial[0], 0.0).astype(BF16)
    bsp = rep(b_spatial[0].T)
    csp = rep(jnp.transpose(w_spatial[0][:, :DEC_SEQ, :DEC_SEQ], (1, 2, 0)).reshape(DEC_SEQ * DEC_SEQ, A_HEADS))
    bsps = rep(b_spatial[0][:, :DEC_SEQ].T)
    st = jnp.transpose(state_conv[0], (1, 0, 2))
    wshgu = jnp.concatenate([w_sh_gate[0], w_sh_up[0]], axis=1).astype(BF16)
    xsm = jnp.transpose(x_sample, (1, 0, 2)).reshape(t_s, d)

    xs, h2, lt, cztail, vs, cz23 = _mixer(
        x_prompt.reshape(t_p, d), xsm, m_all, g_norm1, w_in[0].astype(BF16), wt, bsp,
        g_v.reshape(1, MIX_A), w_conv[0], g_out_a, g_out_b, w_out[0].astype(BF16), g_norm2,
        w_router[0].T.astype(BF16), wshgu, w_sh_down[0].astype(BF16), csp, bsps, st,
        n_batch=n_batch, seq=seq)

    eidx, wts, rank, cnt = _route(lt, b_router[0].reshape(N_EXPERTS, 1))

    n_blocks = (t_all * TOP_K) // BM + N_EXPERTS
    dest, blk_e, nxt_e, nblk, cnt_row, pstart_row = _plan(cnt, eidx, rank, n_blocks)

    dest_tok = dest.T.reshape(t_all * TOP_K)
    x_sorted = _dispatch(cnt_row[0], pstart_row[0], nblk[0, :1], dest_tok, h2, n_blocks)
    y_sorted = _moe(blk_e[0, :n_blocks], nxt_e[:, :n_blocks], nblk[0, :1], x_sorted,
                    w_exp_gate[0], w_exp_up[0], w_exp_down[0])
    y_p, y_s = _final(dest_tok, xs, wts.T, m_all, mf_all, g_final.reshape(1, d), y_sorted,
                      n_batch=n_batch, seq=seq, t_p=t_p)

    tiles_per_seq = seq // TM
    y_prompt = y_p.reshape(n_batch, seq, d)
    y_sample = jnp.transpose(y_s.reshape(dec_seq, dec_batch, d), (1, 0, 2))
    conv_p = cztail[tiles_per_seq - 1:n_batch * tiles_per_seq:tiles_per_seq, SUBLANES - (CONV_W - 1):, :][None]
    conv_s = jnp.transpose(cz23, (1, 0, 2))[None]
    v_s = jnp.transpose(vs.reshape(dec_seq, dec_batch, A_HEADS, A_HEAD_DIM), (1, 0, 2, 3))[None]
    return (y_prompt, y_sample, conv_p, conv_s, v_s)
```
